```python
import jax, jax.numpy as jnp
from jax import lax
import numpy as np

D_MODEL = 1024
BATCH = 8
SEQ = 8192
DEPTH = 2

CHUNK = 64
SB_BLOCK = 128
EPS = 1e-6
GLA_HEADS = 4
GLA_HEAD_K = 64
GLA_HEAD_V = 128
GLA_DK = GLA_HEADS * GLA_HEAD_K
GLA_DV = GLA_HEADS * GLA_HEAD_V
GLA_GATE_RANK = 16
GLA_GATE_NORMALIZER = 16.0
SB_HEADS = 8
SB_HEAD_DIM = 64
SB_D = SB_HEADS * SB_HEAD_DIM
MIX_WIDTH = GLA_DV + SB_D
IN_WIDTH = 2 * GLA_DK + 2 * GLA_DV + GLA_GATE_RANK + 3 * SB_D
CONV_WIDTH = 31
D_FF = -(-8 * D_MODEL // (3 * 256)) * 256
N_EVEN = (DEPTH + 1) // 2
N_ODD = DEPTH // 2

kernel_name = "hybrid_gla_stickbreaking_conformer_trunk"


def _rms_f32(xf, g):
    return xf * lax.rsqrt(jnp.mean(xf * xf, axis=-1, keepdims=True) + EPS) * g.astype(jnp.float32)


def rms_norm(x, g):
    return _rms_f32(x.astype(jnp.float32), g).astype(x.dtype)


def gla_mixer(q, k, v, r, gate_lr, w_gate2, b_gate, g_out):
    f32 = jnp.float32
    B, T, _ = q.shape
    nc = T // CHUNK
    log_a = jax.nn.log_sigmoid((gate_lr @ w_gate2 + b_gate).astype(f32)) / GLA_GATE_NORMALIZER

    def split(t, hd):
        return t.astype(f32).reshape(B, nc, CHUNK, -1, hd).transpose(0, 3, 1, 2, 4)

    qc = split(q, GLA_HEAD_K) * (GLA_HEAD_K ** -0.5)
    kc = split(k, GLA_HEAD_K)
    vc = split(v, GLA_HEAD_V)
    bc = jnp.cumsum(split(log_a, GLA_HEAD_K), axis=3)
    b_end = bc[:, :, :, -1:, :]
    k_end = kc * jnp.exp(b_end - bc)
    scores = jnp.einsum('bhnik,bhnjk->bhnij', qc, k_end)
    intra = jnp.einsum('bhnij,bhnjv->bhniv', scores, vc)
    chunk_update = jnp.einsum('bhnjk,bhnjv->bhnkv', k_end, vc)
    chunk_decay = jnp.exp(b_end[:, :, :, 0, :])

    def step(s, inp):
        a, u = inp
        return a[..., None] * s + u, s

    s0 = jnp.zeros((B, GLA_HEADS, GLA_HEAD_K, GLA_HEAD_V), f32)
    _, s_prev = lax.scan(step, s0, (jnp.moveaxis(chunk_decay, 2, 0), jnp.moveaxis(chunk_update, 2, 0)))
    s_prev = jnp.moveaxis(s_prev, 0, 2)
    inter = jnp.einsum('bhnik,bhnkv->bhniv', qc * jnp.exp(b_end), s_prev)
    o = _rms_f32(intra + inter, g_out)
    o = o.transpose(0, 2, 3, 1, 4).reshape(B, T, GLA_DV)
    return o * jax.nn.silu(r.astype(f32))


def stick_breaking_mixer(q, k, v, g_q, g_k):
    f32 = jnp.float32
    B, T, _ = q.shape

    def heads(t):
        return t.astype(f32).reshape(B, T, SB_HEADS, SB_HEAD_DIM).transpose(0, 2, 1, 3)

    qh = _rms_f32(heads(q), g_q) * (SB_HEAD_DIM ** -0.5)
    kh = _rms_f32(heads(k), g_k)
    vh = heads(v)
    outs = []
    for blk in range(T // SB_BLOCK):
        q0 = blk * SB_BLOCK
        kv_len = q0 + SB_BLOCK
        past = np.arange(kv_len)[None, :] < np.arange(q0, kv_len)[:, None]
        z = jnp.einsum('bhtd,bhsd->bhts', qh[:, :, q0:kv_len], kh[:, :, :kv_len])
        log_keep = jnp.where(past, jax.nn.log_sigmoid(-z), 0.0)
        between = lax.cumsum(log_keep, axis=3, reverse=True) - log_keep
        w = jnp.where(past, jnp.exp(jax.nn.log_sigmoid(z) + between), 0.0)
        outs.append(jnp.einsum('bhts,bhsd->bhtd', w, vh[:, :, :kv_len]))
    o = jnp.concatenate(outs, axis=2)
    return o.transpose(0, 2, 1, 3).reshape(B, T, SB_D)


def hybrid_mixer(h, w_in, w_gate2, b_gate, g_gla, g_q, g_k, w_out):
    proj = h @ w_in
    cuts = np.cumsum([GLA_DK, GLA_DK, GLA_DV, GLA_DV, GLA_GATE_RANK, SB_D, SB_D])
    gq, gk, gv, gr, glr, sq, sk, sv = jnp.split(proj, cuts, axis=-1)
    o_gla = gla_mixer(gq, gk, gv, gr, glr, w_gate2, b_gate, g_gla)
    o_sb = stick_breaking_mixer(sq, sk, sv, g_q, g_k)
    o = jnp.concatenate([o_gla, o_sb], axis=-1).astype(h.dtype)
    return o @ w_out


def conformer_conv(h, w_pw1, b_pw1, w_dw, b_dw, ln_g, ln_b, w_pw2, b_pw2):
    a = h @ w_pw1 + b_pw1
    u = a[..., :D_MODEL] * jax.nn.sigmoid(a[..., D_MODEL:])
    u = lax.conv_general_dilated(u, w_dw[:, None, :].astype(u.dtype), window_strides=(1,),
                                 padding=[(CONV_WIDTH - 1, 0)],
                                 dimension_numbers=('NWC', 'WIO', 'NWC'),
                                 feature_group_count=D_MODEL) + b_dw
    uf = u.astype(jnp.float32)
    mu = jnp.mean(uf, axis=-1, keepdims=True)
    var = jnp.mean(jnp.square(uf - mu), axis=-1, keepdims=True)
    u = ((uf - mu) * lax.rsqrt(var + EPS) * ln_g + ln_b).astype(h.dtype)
    return jax.nn.silu(u) @ w_pw2 + b_pw2


def swiglu(h, wg, wu, wd):
    return (jax.nn.silu(h @ wg) * (h @ wu)) @ wd


def _fwd_setup_inputs(seed: int = 0) -> dict:
    key = jax.random.key(seed)
    ks = iter(jax.random.split(key, 32))

    def nrm(shape, scale):
        return jax.random.normal(next(ks), shape, jnp.float32) * scale

    def gain(shape):
        return 1.0 + nrm(shape, 0.05)

    return {
        "x": nrm((BATCH, SEQ, D_MODEL), 1.0),
        "mix_norm": gain((DEPTH, D_MODEL)),
        "ffn_norm": gain((DEPTH, D_MODEL)),
        "hy_w_in": nrm((N_EVEN, D_MODEL, IN_WIDTH), D_MODEL ** -0.5),
        "hy_w_gate2": nrm((N_EVEN, GLA_GATE_RANK, GLA_DK), GLA_GATE_RANK ** -0.5),
        "hy_b_gate": nrm((N_EVEN, GLA_DK), 0.1),
        "hy_gla_norm": gain((N_EVEN, GLA_HEAD_V)),
        "hy_sb_q_norm": gain((N_EVEN, SB_HEAD_DIM)),
        "hy_sb_k_norm": gain((N_EVEN, SB_HEAD_DIM)),
        "hy_w_out": nrm((N_EVEN, MIX_WIDTH, D_MODEL), MIX_WIDTH ** -0.5),
        "cv_w_pw1": nrm((N_ODD, D_MODEL, 2 * D_MODEL), D_MODEL ** -0.5),
        "cv_b_pw1": nrm((N_ODD, 2 * D_MODEL), 0.02),
        "cv_w_dw": nrm((N_ODD, CONV_WIDTH, D_MODEL), CONV_WIDTH ** -0.5),
        "cv_b_dw": nrm((N_ODD, D_MODEL), 0.02),
        "cv_ln_g": gain((N_ODD, D_MODEL)),
        "cv_ln_b": nrm((N_ODD, D_MODEL), 0.02),
        "cv_w_pw2": nrm((N_ODD, D_MODEL, D_MODEL), D_MODEL ** -0.5),
        "cv_b_pw2": nrm((N_ODD, D_MODEL), 0.02),
        "ffn_w_gate": nrm((DEPTH, D_MODEL, D_FF), D_MODEL ** -0.5),
        "ffn_w_up": nrm((DEPTH, D_MODEL, D_FF), D_MODEL ** -0.5),
        "ffn_w_down": nrm((DEPTH, D_FF, D_MODEL), D_FF ** -0.5),
    }


def _fwd_reference(x, mix_norm, ffn_norm, hy_w_in, hy_w_gate2, hy_b_gate, hy_gla_norm,
              hy_sb_q_norm, hy_sb_k_norm, hy_w_out, cv_w_pw1, cv_b_pw1, cv_w_dw, cv_b_dw,
              cv_ln_g, cv_ln_b, cv_w_pw2, cv_b_pw2, ffn_w_gate, ffn_w_up, ffn_w_down):
    h = x
    for layer in range(DEPTH):
        hn = rms_norm(h, mix_norm[layer])
        if layer % 2 == 0:
            i = layer // 2
            mix = hybrid_mixer(hn, hy_w_in[i], hy_w_gate2[i], hy_b_gate[i], hy_gla_norm[i],
                               hy_sb_q_norm[i], hy_sb_k_norm[i], hy_w_out[i])
        else:
            i = layer // 2
            mix = conformer_conv(hn, cv_w_pw1[i], cv_b_pw1[i], cv_w_dw[i], cv_b_dw[i],
                                 cv_ln_g[i], cv_ln_b[i], cv_w_pw2[i], cv_b_pw2[i])
        h = h + mix.astype(h.dtype)
        h = h + swiglu(rms_norm(h, ffn_norm[layer]), ffn_w_gate[layer], ffn_w_up[layer],
                       ffn_w_down[layer]).astype(h.dtype)
    return h


import jax as _jax
import jax.numpy as _jnp

TWIN_FORMAT = 'train_step'
FWD_PARAMS = ['x', 'mix_norm', 'ffn_norm', 'hy_w_in', 'hy_w_gate2', 'hy_b_gate', 'hy_gla_norm', 'hy_sb_q_norm', 'hy_sb_k_norm', 'hy_w_out', 'cv_w_pw1', 'cv_b_pw1', 'cv_w_dw', 'cv_b_dw', 'cv_ln_g', 'cv_ln_b', 'cv_w_pw2', 'cv_b_pw2', 'ffn_w_gate', 'ffn_w_up', 'ffn_w_down']
TWIN_WEIGHTS = ['mix_norm', 'ffn_norm', 'hy_w_in', 'hy_w_gate2', 'hy_b_gate', 'hy_gla_norm', 'hy_sb_q_norm', 'hy_sb_k_norm', 'hy_w_out', 'cv_w_pw1', 'cv_b_pw1', 'cv_w_dw', 'cv_b_dw', 'cv_ln_g', 'cv_ln_b', 'cv_w_pw2', 'cv_b_pw2', 'ffn_w_gate', 'ffn_w_up', 'ffn_w_down']
TWIN_DIFF_INPUT = 'x'
TWIN_INPUTS = ['x', 'mix_norm', 'ffn_norm', 'hy_w_in', 'hy_w_gate2', 'hy_b_gate', 'hy_gla_norm', 'hy_sb_q_norm', 'hy_sb_k_norm', 'hy_w_out', 'cv_w_pw1', 'cv_b_pw1', 'cv_w_dw', 'cv_b_dw', 'cv_ln_g', 'cv_ln_b', 'cv_w_pw2', 'cv_b_pw2', 'ffn_w_gate', 'ffn_w_up', 'ffn_w_down', 'loss_target', 'm_mix_norm', 'm_ffn_norm', 'm_hy_w_in', 'm_hy_w_gate2', 'm_hy_b_gate', 'm_hy_gla_norm', 'm_hy_sb_q_norm', 'm_hy_sb_k_norm', 'm_hy_w_out', 'm_cv_w_pw1', 'm_cv_b_pw1', 'm_cv_w_dw', 'm_cv_b_dw', 'm_cv_ln_g', 'm_cv_ln_b', 'm_cv_w_pw2', 'm_cv_b_pw2', 'm_ffn_w_gate', 'm_ffn_w_up', 'm_ffn_w_down', 'v_mix_norm', 'v_ffn_norm', 'v_hy_w_in', 'v_hy_w_gate2', 'v_hy_b_gate', 'v_hy_gla_norm', 'v_hy_sb_q_norm', 'v_hy_sb_k_norm', 'v_hy_w_out', 'v_cv_w_pw1', 'v_cv_b_pw1', 'v_cv_w_dw', 'v_cv_b_dw', 'v_cv_ln_g', 'v_cv_ln_b', 'v_cv_w_pw2', 'v_cv_b_pw2', 'v_ffn_w_gate', 'v_ffn_w_up', 'v_ffn_w_down']
TWIN_OUTPUTS = ['loss', 'grad_x', 'grad_mix_norm', 'grad_ffn_norm', 'grad_hy_w_in', 'grad_hy_w_gate2', 'grad_hy_b_gate', 'grad_hy_gla_norm', 'grad_hy_sb_q_norm', 'grad_hy_sb_k_norm', 'grad_hy_w_out', 'grad_cv_w_pw1', 'grad_cv_b_pw1', 'grad_cv_w_dw', 'grad_cv_b_dw', 'grad_cv_ln_g', 'grad_cv_ln_b', 'grad_cv_w_pw2', 'grad_cv_b_pw2', 'grad_ffn_w_gate', 'grad_ffn_w_up', 'grad_ffn_w_down', 'delta_mix_norm', 'delta_ffn_norm', 'delta_hy_w_in', 'delta_hy_w_gate2', 'delta_hy_b_gate', 'delta_hy_gla_norm', 'delta_hy_sb_q_norm', 'delta_hy_sb_k_norm', 'delta_hy_w_out', 'delta_cv_w_pw1', 'delta_cv_b_pw1', 'delta_cv_w_dw', 'delta_cv_b_dw', 'delta_cv_ln_g', 'delta_cv_ln_b', 'delta_cv_w_pw2', 'delta_cv_b_pw2', 'delta_ffn_w_gate', 'delta_ffn_w_up', 'delta_ffn_w_down', 'new_m_mix_norm', 'new_m_ffn_norm', 'new_m_hy_w_in', 'new_m_hy_w_gate2', 'new_m_hy_b_gate', 'new_m_hy_gla_norm', 'new_m_hy_sb_q_norm', 'new_m_hy_sb_k_norm', 'new_m_hy_w_out', 'new_m_cv_w_pw1', 'new_m_cv_b_pw1', 'new_m_cv_w_dw', 'new_m_cv_b_dw', 'new_m_cv_ln_g', 'new_m_cv_ln_b', 'new_m_cv_w_pw2', 'new_m_cv_b_pw2', 'new_m_ffn_w_gate', 'new_m_ffn_w_up', 'new_m_ffn_w_down', 'new_v_mix_norm', 'new_v_ffn_norm', 'new_v_hy_w_in', 'new_v_hy_w_gate2', 'new_v_hy_b_gate', 'new_v_hy_gla_norm', 'new_v_hy_sb_q_norm', 'new_v_hy_sb_k_norm', 'new_v_hy_w_out', 'new_v_cv_w_pw1', 'new_v_cv_b_pw1', 'new_v_cv_w_dw', 'new_v_cv_b_dw', 'new_v_cv_ln_g', 'new_v_cv_ln_b', 'new_v_cv_w_pw2', 'new_v_cv_b_pw2', 'new_v_ffn_w_gate', 'new_v_ffn_w_up', 'new_v_ffn_w_down']
TWIN_LEAF_KINDS = {'loss': 'loss', 'grad_x': 'grad_x', 'grad_mix_norm': 'grad_w', 'grad_ffn_norm': 'grad_w', 'grad_hy_w_in': 'grad_w', 'grad_hy_w_gate2': 'grad_w', 'grad_hy_b_gate': 'grad_w', 'grad_hy_gla_norm': 'grad_w', 'grad_hy_sb_q_norm': 'grad_w', 'grad_hy_sb_k_norm': 'grad_w', 'grad_hy_w_out': 'grad_w', 'grad_cv_w_pw1': 'grad_w', 'grad_cv_b_pw1': 'grad_w', 'grad_cv_w_dw': 'grad_w', 'grad_cv_b_dw': 'grad_w', 'grad_cv_ln_g': 'grad_w', 'grad_cv_ln_b': 'grad_w', 'grad_cv_w_pw2': 'grad_w', 'grad_cv_b_pw2': 'grad_w', 'grad_ffn_w_gate': 'grad_w', 'grad_ffn_w_up': 'grad_w', 'grad_ffn_w_down': 'grad_w', 'delta_mix_norm': 'delta_w', 'delta_ffn_norm': 'delta_w', 'delta_hy_w_in': 'delta_w', 'delta_hy_w_gate2': 'delta_w', 'delta_hy_b_gate': 'delta_w', 'delta_hy_gla_norm': 'delta_w', 'delta_hy_sb_q_norm': 'delta_w', 'delta_hy_sb_k_norm': 'delta_w', 'delta_hy_w_out': 'delta_w', 'delta_cv_w_pw1': 'delta_w', 'delta_cv_b_pw1': 'delta_w', 'delta_cv_w_dw': 'delta_w', 'delta_cv_b_dw': 'delta_w', 'delta_cv_ln_g': 'delta_w', 'delta_cv_ln_b': 'delta_w', 'delta_cv_w_pw2': 'delta_w', 'delta_cv_b_pw2': 'delta_w', 'delta_ffn_w_gate': 'delta_w', 'delta_ffn_w_up': 'delta_w', 'delta_ffn_w_down': 'delta_w', 'new_m_mix_norm': 'new_m', 'new_m_ffn_norm': 'new_m', 'new_m_hy_w_in': 'new_m', 'new_m_hy_w_gate2': 'new_m', 'new_m_hy_b_gate': 'new_m', 'new_m_hy_gla_norm': 'new_m', 'new_m_hy_sb_q_norm': 'new_m', 'new_m_hy_sb_k_norm': 'new_m', 'new_m_hy_w_out': 'new_m', 'new_m_cv_w_pw1': 'new_m', 'new_m_cv_b_pw1': 'new_m', 'new_m_cv_w_dw': 'new_m', 'new_m_cv_b_dw': 'new_m', 'new_m_cv_ln_g': 'new_m', 'new_m_cv_ln_b': 'new_m', 'new_m_cv_w_pw2': 'new_m', 'new_m_cv_b_pw2': 'new_m', 'new_m_ffn_w_gate': 'new_m', 'new_m_ffn_w_up': 'new_m', 'new_m_ffn_w_down': 'new_m', 'new_v_mix_norm': 'new_v', 'new_v_ffn_norm': 'new_v', 'new_v_hy_w_in': 'new_v', 'new_v_hy_w_gate2': 'new_v', 'new_v_hy_b_gate': 'new_v', 'new_v_hy_gla_norm': 'new_v', 'new_v_hy_sb_q_norm': 'new_v', 'new_v_hy_sb_k_norm': 'new_v', 'new_v_hy_w_out': 'new_v', 'new_v_cv_w_pw1': 'new_v', 'new_v_cv_b_pw1': 'new_v', 'new_v_cv_w_dw': 'new_v', 'new_v_cv_b_dw': 'new_v', 'new_v_cv_ln_g': 'new_v', 'new_v_cv_ln_b': 'new_v', 'new_v_cv_w_pw2': 'new_v', 'new_v_cv_b_pw2': 'new_v', 'new_v_ffn_w_gate': 'new_v', 'new_v_ffn_w_up': 'new_v', 'new_v_ffn_w_down': 'new_v'}


def _forward(args):
    return _fwd_reference(*[args[k] for k in FWD_PARAMS])


def _output_shape():
    def fwd():
        inp = _fwd_setup_inputs(0)
        return _fwd_reference(*[inp[k] for k in FWD_PARAMS])
    out = _jax.eval_shape(fwd)
    return out.shape, out.dtype

N_MICROBATCH = 1
ADAM_LR = 0.001
ADAM_B1 = 0.9
ADAM_B2 = 0.999
ADAM_EPS = 1e-08
ADAM_WD = 0.01
ADAM_STEP = 10
PER_EXAMPLE_BATCH_AXIS = {'x': 0, 'loss_target': 0}
SHARED_INPUTS = []
_WEIGHT_DTYPES = {'mix_norm': _jnp.float32, 'ffn_norm': _jnp.float32, 'hy_w_in': _jnp.float32, 'hy_w_gate2': _jnp.float32, 'hy_b_gate': _jnp.float32, 'hy_gla_norm': _jnp.float32, 'hy_sb_q_norm': _jnp.float32, 'hy_sb_k_norm': _jnp.float32, 'hy_w_out': _jnp.float32, 'cv_w_pw1': _jnp.float32, 'cv_b_pw1': _jnp.float32, 'cv_w_dw': _jnp.float32, 'cv_b_dw': _jnp.float32, 'cv_ln_g': _jnp.float32, 'cv_ln_b': _jnp.float32, 'cv_w_pw2': _jnp.float32, 'cv_b_pw2': _jnp.float32, 'ffn_w_gate': _jnp.float32, 'ffn_w_up': _jnp.float32, 'ffn_w_down': _jnp.float32}
MOMENT_SCALE = {'mix_norm': 1.851608e+01, 'ffn_norm': 4.987285e+01, 'hy_w_in': 6.050860e-01, 'hy_w_gate2': 9.065538e-02, 'hy_b_gate': 3.674328e-01, 'hy_gla_norm': 9.062039e+01, 'hy_sb_q_norm': 3.180642e+01, 'hy_sb_k_norm': 3.180602e+01, 'hy_w_out': 7.444468e-01, 'cv_w_pw1': 3.181351e-01, 'cv_b_pw1': 6.423610e+00, 'cv_w_dw': 9.094046e-01, 'cv_b_dw': 1.521644e+01, 'cv_ln_g': 2.893967e+01, 'cv_ln_b': 2.002399e+01, 'cv_w_pw2': 3.231908e+00, 'cv_b_pw2': 1.628786e+01, 'ffn_w_gate': 5.785153e-01, 'ffn_w_up': 4.062720e-01, 'ffn_w_down': 6.352979e-01}


def _to_microbatches(a, axis):
    t = _jnp.moveaxis(a, axis, 0)
    t = t.reshape((N_MICROBATCH, t.shape[0] // N_MICROBATCH) + t.shape[1:])
    return _jnp.moveaxis(t, 1, axis + 1)


def setup_inputs(seed: int = 0) -> dict:
    inp = _fwd_setup_inputs(seed)
    key = _jax.random.fold_in(_jax.random.key(seed), 7919)
    shape, _ = _output_shape()
    out = dict(inp)
    out["loss_target"] = _jax.random.normal(_jax.random.fold_in(key, 0), shape, _jnp.float32)
    for i, name in enumerate(TWIN_WEIGHTS):
        w = inp[name].astype(_jnp.float32)
        if MOMENT_SCALE is None:
            s = _jnp.sqrt(_jnp.mean(_jnp.square(w)) + 1e-30)
        else:
            s = MOMENT_SCALE[name]
        km, kv = _jax.random.split(_jax.random.fold_in(key, i + 1))
        out[name] = w
        out["m_" + name] = s * _jax.random.normal(km, w.shape, _jnp.float32)
        out["v_" + name] = (s * s) * _jax.random.uniform(kv, w.shape, _jnp.float32, 0.5, 1.5)
    if N_MICROBATCH > 1:
        for name, axis in PER_EXAMPLE_BATCH_AXIS.items():
            out[name] = _to_microbatches(out[name], axis)
    return {'x': out['x'], 'mix_norm': out['mix_norm'], 'ffn_norm': out['ffn_norm'], 'hy_w_in': out['hy_w_in'], 'hy_w_gate2': out['hy_w_gate2'], 'hy_b_gate': out['hy_b_gate'], 'hy_gla_norm': out['hy_gla_norm'], 'hy_sb_q_norm': out['hy_sb_q_norm'], 'hy_sb_k_norm': out['hy_sb_k_norm'], 'hy_w_out': out['hy_w_out'], 'cv_w_pw1': out['cv_w_pw1'], 'cv_b_pw1': out['cv_b_pw1'], 'cv_w_dw': out['cv_w_dw'], 'cv_b_dw': out['cv_b_dw'], 'cv_ln_g': out['cv_ln_g'], 'cv_ln_b': out['cv_ln_b'], 'cv_w_pw2': out['cv_w_pw2'], 'cv_b_pw2': out['cv_b_pw2'], 'ffn_w_gate': out['ffn_w_gate'], 'ffn_w_up': out['ffn_w_up'], 'ffn_w_down': out['ffn_w_down'], 'loss_target': out['loss_target'], 'm_mix_norm': out['m_mix_norm'], 'm_ffn_norm': out['m_ffn_norm'], 'm_hy_w_in': out['m_hy_w_in'], 'm_hy_w_gate2': out['m_hy_w_gate2'], 'm_hy_b_gate': out['m_hy_b_gate'], 'm_hy_gla_norm': out['m_hy_gla_norm'], 'm_hy_sb_q_norm': out['m_hy_sb_q_norm'], 'm_hy_sb_k_norm': out['m_hy_sb_k_norm'], 'm_hy_w_out': out['m_hy_w_out'], 'm_cv_w_pw1': out['m_cv_w_pw1'], 'm_cv_b_pw1': out['m_cv_b_pw1'], 'm_cv_w_dw': out['m_cv_w_dw'], 'm_cv_b_dw': out['m_cv_b_dw'], 'm_cv_ln_g': out['m_cv_ln_g'], 'm_cv_ln_b': out['m_cv_ln_b'], 'm_cv_w_pw2': out['m_cv_w_pw2'], 'm_cv_b_pw2': out['m_cv_b_pw2'], 'm_ffn_w_gate': out['m_ffn_w_gate'], 'm_ffn_w_up': out['m_ffn_w_up'], 'm_ffn_w_down': out['m_ffn_w_down'], 'v_mix_norm': out['v_mix_norm'], 'v_ffn_norm': out['v_ffn_norm'], 'v_hy_w_in': out['v_hy_w_in'], 'v_hy_w_gate2': out['v_hy_w_gate2'], 'v_hy_b_gate': out['v_hy_b_gate'], 'v_hy_gla_norm': out['v_hy_gla_norm'], 'v_hy_sb_q_norm': out['v_hy_sb_q_norm'], 'v_hy_sb_k_norm': out['v_hy_sb_k_norm'], 'v_hy_w_out': out['v_hy_w_out'], 'v_cv_w_pw1': out['v_cv_w_pw1'], 'v_cv_b_pw1': out['v_cv_b_pw1'], 'v_cv_w_dw': out['v_cv_w_dw'], 'v_cv_b_dw': out['v_cv_b_dw'], 'v_cv_ln_g': out['v_cv_ln_g'], 'v_cv_ln_b': out['v_cv_ln_b'], 'v_cv_w_pw2': out['v_cv_w_pw2'], 'v_cv_b_pw2': out['v_cv_b_pw2'], 'v_ffn_w_gate': out['v_ffn_w_gate'], 'v_ffn_w_up': out['v_ffn_w_up'], 'v_ffn_w_down': out['v_ffn_w_down']}


def _loss(weights, diff, rest, loss_target):
    with _jax.named_scope("forward"):
        args = {**rest, TWIN_DIFF_INPUT: diff, **{k: w.astype(_WEIGHT_DTYPES[k]) for k, w in weights.items()}}
        y = _forward(args)
    with _jax.named_scope("loss_head"):
        err = _jnp.square(y.astype(_jnp.float32) - loss_target)
        return 0.5 * _jnp.sum(_jnp.mean(err, axis=-1)) if err.ndim else 0.5 * err


def _adamw(w, g, m, v):
    m = ADAM_B1 * m + (1.0 - ADAM_B1) * g
    v = ADAM_B2 * v + (1.0 - ADAM_B2) * _jnp.square(g)
    m_hat = m / (1.0 - ADAM_B1 ** ADAM_STEP)
    v_hat = v / (1.0 - ADAM_B2 ** ADAM_STEP)
    delta = -ADAM_LR * (m_hat / (_jnp.sqrt(v_hat) + ADAM_EPS) + ADAM_WD * w)
    return delta, m, v


def reference(x, mix_norm, ffn_norm, hy_w_in, hy_w_gate2, hy_b_gate, hy_gla_norm, hy_sb_q_norm, hy_sb_k_norm, hy_w_out, cv_w_pw1, cv_b_pw1, cv_w_dw, cv_b_dw, cv_ln_g, cv_ln_b, cv_w_pw2, cv_b_pw2, ffn_w_gate, ffn_w_up, ffn_w_down, loss_target, m_mix_norm, m_ffn_norm, m_hy_w_in, m_hy_w_gate2, m_hy_b_gate, m_hy_gla_norm, m_hy_sb_q_norm, m_hy_sb_k_norm, m_hy_w_out, m_cv_w_pw1, m_cv_b_pw1, m_cv_w_dw, m_cv_b_dw, m_cv_ln_g, m_cv_ln_b, m_cv_w_pw2, m_cv_b_pw2, m_ffn_w_gate, m_ffn_w_up, m_ffn_w_down, v_mix_norm, v_ffn_norm, v_hy_w_in, v_hy_w_gate2, v_hy_b_gate, v_hy_gla_norm, v_hy_sb_q_norm, v_hy_sb_k_norm, v_hy_w_out, v_cv_w_pw1, v_cv_b_pw1, v_cv_w_dw, v_cv_b_dw, v_cv_ln_g, v_cv_ln_b, v_cv_w_pw2, v_cv_b_pw2, v_ffn_w_gate, v_ffn_w_up, v_ffn_w_down):
    given = dict(x=x, mix_norm=mix_norm, ffn_norm=ffn_norm, hy_w_in=hy_w_in, hy_w_gate2=hy_w_gate2, hy_b_gate=hy_b_gate, hy_gla_norm=hy_gla_norm, hy_sb_q_norm=hy_sb_q_norm, hy_sb_k_norm=hy_sb_k_norm, hy_w_out=hy_w_out, cv_w_pw1=cv_w_pw1, cv_b_pw1=cv_b_pw1, cv_w_dw=cv_w_dw, cv_b_dw=cv_b_dw, cv_ln_g=cv_ln_g, cv_ln_b=cv_ln_b, cv_w_pw2=cv_w_pw2, cv_b_pw2=cv_b_pw2, ffn_w_gate=ffn_w_gate, ffn_w_up=ffn_w_up, ffn_w_down=ffn_w_down, loss_target=loss_target, m_mix_norm=m_mix_norm, m_ffn_norm=m_ffn_norm, m_hy_w_in=m_hy_w_in, m_hy_w_gate2=m_hy_w_gate2, m_hy_b_gate=m_hy_b_gate, m_hy_gla_norm=m_hy_gla_norm, m_hy_sb_q_norm=m_hy_sb_q_norm, m_hy_sb_k_norm=m_hy_sb_k_norm, m_hy_w_out=m_hy_w_out, m_cv_w_pw1=m_cv_w_pw1, m_cv_b_pw1=m_cv_b_pw1, m_cv_w_dw=m_cv_w_dw, m_cv_b_dw=m_cv_b_dw, m_cv_ln_g=m_cv_ln_g, m_cv_ln_b=m_cv_ln_b, m_cv_w_pw2=m_cv_w_pw2, m_cv_b_pw2=m_cv_b_pw2, m_ffn_w_gate=m_ffn_w_gate, m_ffn_w_up=m_ffn_w_up, m_ffn_w_down=m_ffn_w_down, v_mix_norm=v_mix_norm, v_ffn_norm=v_ffn_norm, v_hy_w_in=v_hy_w_in, v_hy_w_gate2=v_hy_w_gate2, v_hy_b_gate=v_hy_b_gate, v_hy_gla_norm=v_hy_gla_norm, v_hy_sb_q_norm=v_hy_sb_q_norm, v_hy_sb_k_norm=v_hy_sb_k_norm, v_hy_w_out=v_hy_w_out, v_cv_w_pw1=v_cv_w_pw1, v_cv_b_pw1=v_cv_b_pw1, v_cv_w_dw=v_cv_w_dw, v_cv_b_dw=v_cv_b_dw, v_cv_ln_g=v_cv_ln_g, v_cv_ln_b=v_cv_ln_b, v_cv_w_pw2=v_cv_w_pw2, v_cv_b_pw2=v_cv_b_pw2, v_ffn_w_gate=v_ffn_w_gate, v_ffn_w_up=v_ffn_w_up, v_ffn_w_down=v_ffn_w_down)
    weights = {n: given[n] for n in TWIN_WEIGHTS}
    shared = {n: given[n] for n in SHARED_INPUTS}
    per_example = {n: given[n] for n in ['x']}
    grad_fn = _jax.value_and_grad(_loss, argnums=(0, 1))

    def one_microbatch(ex, loss_target):
        ex = dict(ex)
        diff = ex.pop(TWIN_DIFF_INPUT)
        return grad_fn(weights, diff, {**shared, **ex}, loss_target)

    if N_MICROBATCH == 1:
        loss, (grad_w, grad_x) = one_microbatch(per_example, given["loss_target"])
    else:
        def body(carry, xs):
            loss_sum, grad_sum = carry
            l_k, (gw_k, gx_k) = one_microbatch(xs[0], xs[1])
            with _jax.named_scope("update"):
                return (loss_sum + l_k, _jax.tree.map(_jnp.add, grad_sum, gw_k)), gx_k

        init = (_jnp.zeros((), _jnp.float32), _jax.tree.map(_jnp.zeros_like, weights))
        (loss, grad_w), grad_x = _jax.lax.scan(body, init, (per_example, given["loss_target"]))
    with _jax.named_scope("update"):
        delta_w, new_m, new_v = {}, {}, {}
        for n in TWIN_WEIGHTS:
            delta_w[n], new_m[n], new_v[n] = _adamw(weights[n], grad_w[n], given["m_" + n], given["v_" + n])
    return (loss, grad_x, *[grad_w[n] for n in TWIN_WEIGHTS], *[delta_w[n] for n in TWIN_WEIGHTS],
            *[new_m[n] for n in TWIN_WEIGHTS], *[new_v[n] for n in TWIN_WEIGHTS])
```

```python
import functools

import numpy as np
import jax
import jax.numpy as jnp
from jax import lax
from jax.experimental import pallas as pl
from jax.experimental.pallas import tpu as pltpu

F32 = jnp.float32
BF16 = jnp.bfloat16
CDT = jnp.bfloat16

D_MODEL = 1024
EPS = 1e-6
CHUNK = 64
GLA_HEADS = 4
GLA_HEAD_K = 64
GLA_HEAD_V = 128
GLA_DK = GLA_HEADS * GLA_HEAD_K
GLA_DV = GLA_HEADS * GLA_HEAD_V
GLA_GATE_RANK = 16
GLA_GATE_NORMALIZER = 16.0
SB_HEAD_DIM = 64
SB_D = 512
SB_BLOCK = 128
IN_WIDTH = 3088
IN_PAD = 3200
CONV_WIDTH = 31
CONV_HALO = 32
D_FF = 2816
N_DEV = 8
PACK_W = 512
PACK_ALIGN = 16 * PACK_W

ADAM_LR = 0.001
ADAM_B1 = 0.9
ADAM_B2 = 0.999
ADAM_EPS = 1e-08
ADAM_WD = 0.01
ADAM_STEP = 10

VMEM_LIMIT = 56 * 1024 * 1024

WEIGHT_NAMES = ['mix_norm', 'ffn_norm', 'hy_w_in', 'hy_w_gate2', 'hy_b_gate', 'hy_gla_norm', 'hy_sb_q_norm',
                'hy_sb_k_norm', 'hy_w_out', 'cv_w_pw1', 'cv_b_pw1', 'cv_w_dw', 'cv_b_dw', 'cv_ln_g', 'cv_ln_b',
                'cv_w_pw2', 'cv_b_pw2', 'ffn_w_gate', 'ffn_w_up', 'ffn_w_down']
SHARD_AXIS = {'mix_norm': None, 'ffn_norm': None, 'hy_w_in': 2, 'hy_w_gate2': 2, 'hy_b_gate': None,
              'hy_gla_norm': None, 'hy_sb_q_norm': None, 'hy_sb_k_norm': None, 'hy_w_out': 1, 'cv_w_pw1': 2,
              'cv_b_pw1': 1, 'cv_w_dw': 2, 'cv_b_dw': 1, 'cv_ln_g': 1, 'cv_ln_b': 1, 'cv_w_pw2': 1, 'cv_b_pw2': 1,
              'ffn_w_gate': 2, 'ffn_w_up': 2, 'ffn_w_down': 1}
GATHER_BF16 = ('hy_w_in', 'hy_w_gate2', 'hy_w_out', 'cv_w_pw1', 'cv_w_pw2', 'ffn_w_gate', 'ffn_w_up', 'ffn_w_down')
GATHER_F32 = ('cv_b_pw1', 'cv_w_dw', 'cv_b_dw', 'cv_ln_g', 'cv_ln_b', 'cv_b_pw2')


def _cparams(sem=None, vmem=VMEM_LIMIT):
    return pltpu.CompilerParams(dimension_semantics=sem, vmem_limit_bytes=vmem)


def _log_sigmoid(x):
    return jnp.minimum(x, 0.0) - jnp.log1p(jnp.exp(-jnp.abs(x)))


def _sigmoid(x):
    return 1.0 / (1.0 + jnp.exp(-x))


def _split_bf16(x, n):
    parts = []
    rem = x
    for _ in range(n):
        p = rem.astype(BF16)
        parts.append(p)
        rem = rem - p.astype(F32)
    return parts


def _dot_exact_rhs(x, m, n):
    return sum(jnp.dot(p, m, preferred_element_type=F32) for p in _split_bf16(x, n))


def _dot_exact_lhs(m, x, n):
    return sum(jnp.dot(m, p, preferred_element_type=F32) for p in _split_bf16(x, n))


_NN = (((1,), (0,)), ((), ()))
_NT = (((1,), (1,)), ((), ()))
_TN = (((0,), (0,)), ((), ()))


def _dg(a, b, dn):
    return lax.dot_general(a.astype(CDT), b.astype(CDT), dn, preferred_element_type=F32)


def _matmul(a, b, *, mode, out_dtype, name, tm, tn, tk, bias=None, residual=None):
    if mode == 'nn':
        (M, K), (K2, N) = a.shape, b.shape
    elif mode == 'nt':
        (M, K), (N, K2) = a.shape, b.shape
    else:
        (K, M), (K2, N) = a.shape, b.shape
    assert K == K2 and M % tm == 0 and N % tn == 0 and K % tk == 0, (name, a.shape, b.shape, tm, tn, tk)
    nk = K // tk
    a_spec = pl.BlockSpec((tk, tm), lambda i, j, k: (k, i)) if mode == 'tn' else pl.BlockSpec((tm, tk), lambda i, j, k: (i, k))
    b_spec = pl.BlockSpec((tn, tk), lambda i, j, k: (j, k)) if mode == 'nt' else pl.BlockSpec((tk, tn), lambda i, j, k: (k, j))
    dn = {'nn': _NN, 'nt': _NT, 'tn': _TN}[mode]
    has_bias, has_res = bias is not None, residual is not None

    def body(*refs):
        a_ref, b_ref = refs[0], refs[1]
        pos = 2
        bias_ref = res_ref = None
        if has_bias:
            bias_ref = refs[pos]
            pos += 1
        if has_res:
            res_ref = refs[pos]
            pos += 1
        o_ref = refs[pos]
        acc_ref = refs[pos + 1] if nk > 1 else None
        p = _dg(a_ref[...], b_ref[...], dn)

        def finish(acc):
            if has_bias:
                acc = acc + bias_ref[...]
            if has_res:
                acc = res_ref[...] + acc
            o_ref[...] = acc.astype(o_ref.dtype)

        if nk == 1:
            finish(p)
        else:
            k = pl.program_id(2)

            @pl.when(k == 0)
            def _():
                acc_ref[...] = p

            @pl.when(k > 0)
            def _():
                acc_ref[...] += p

            @pl.when(k == nk - 1)
            def _():
                finish(acc_ref[...])

    in_specs = [a_spec, b_spec]
    args = [a, b]
    if has_bias:
        in_specs.append(pl.BlockSpec((1, tn), lambda i, j, k: (0, j)))
        args.append(bias)
    if has_res:
        in_specs.append(pl.BlockSpec((tm, tn), lambda i, j, k: (i, j)))
        args.append(residual)
    return pl.pallas_call(
        body, name=name, grid=(M // tm, N // tn, nk),
        in_specs=in_specs, out_specs=pl.BlockSpec((tm, tn), lambda i, j, k: (i, j)),
        out_shape=jax.ShapeDtypeStruct((M, N), out_dtype),
        scratch_shapes=[pltpu.VMEM((tm, tn), F32)] if nk > 1 else [],
        compiler_params=_cparams(("parallel", "parallel", "arbitrary")),
    )(*args)


def _rms_fwd(x, g, *, name, tm=512):
    T, Dm = x.shape

    def body(x_ref, g_ref, o_ref):
        xv = x_ref[...]
        r = lax.rsqrt(jnp.mean(xv * xv, axis=-1, keepdims=True) + EPS)
        o_ref[...] = (xv * r * g_ref[...]).astype(o_ref.dtype)

    return pl.pallas_call(
        body, name=name, grid=(T // tm,),
        in_specs=[pl.BlockSpec((tm, Dm), lambda i: (i, 0)), pl.BlockSpec((1, Dm), lambda i: (0, 0))],
        out_specs=pl.BlockSpec((tm, Dm), lambda i: (i, 0)),
        out_shape=jax.ShapeDtypeStruct((T, Dm), CDT),
        compiler_params=_cparams(("parallel",)),
    )(x, g)


def _rms_bwd(dy, x, g, resid, *, name, tm=512):
    T, Dm = x.shape

    def body(dy_ref, x_ref, g_ref, res_ref, dx_ref, dxb_ref, dg_ref):
        i = pl.program_id(0)
        xv, dyv = x_ref[...], dy_ref[...]
        r = lax.rsqrt(jnp.mean(xv * xv, axis=-1, keepdims=True) + EPS)
        u = dyv * g_ref[...]
        dot = jnp.mean(u * xv, axis=-1, keepdims=True)
        dx = res_ref[...] + (r * u - xv * (r * r * r * dot))
        dx_ref[...] = dx
        dxb_ref[...] = dx.astype(dxb_ref.dtype)
        part = jnp.sum(dyv * xv * r, axis=0, keepdims=True)

        @pl.when(i == 0)
        def _():
            dg_ref[...] = part

        @pl.when(i > 0)
        def _():
            dg_ref[...] += part

    row = pl.BlockSpec((tm, Dm), lambda i: (i, 0))
    vec = pl.BlockSpec((1, Dm), lambda i: (0, 0))
    return pl.pallas_call(
        body, name=name, grid=(T // tm,),
        in_specs=[row, row, vec, row], out_specs=[row, row, vec],
        out_shape=[jax.ShapeDtypeStruct((T, Dm), F32), jax.ShapeDtypeStruct((T, Dm), CDT),
                   jax.ShapeDtypeStruct((1, Dm), F32)],
        compiler_params=_cparams(("arbitrary",)),
    )(dy, x, g, resid)


def _loss_head(y, tgt, *, tm=512):
    T, Dm = y.shape

    def body(y_ref, t_ref, s_ref, dy_ref, dyb_ref):
        i = pl.program_id(0)
        e = y_ref[...] - t_ref[...]
        dy = e * (1.0 / Dm)
        dy_ref[...] = dy
        dyb_ref[...] = dy.astype(dyb_ref.dtype)
        part = jnp.sum(jnp.sum(e * e, axis=1, keepdims=True), axis=0, keepdims=True)

        @pl.when(i == 0)
        def _():
            s_ref[...] = part

        @pl.when(i > 0)
        def _():
            s_ref[...] += part

    row = pl.BlockSpec((tm, Dm), lambda i: (i, 0))
    return pl.pallas_call(
        body, name="loss_head", grid=(T // tm,),
        in_specs=[row, row], out_specs=[pl.BlockSpec((1, 1), lambda i: (0, 0)), row, row],
        out_shape=[jax.ShapeDtypeStruct((1, 1), F32), jax.ShapeDtypeStruct((T, Dm), F32),
                   jax.ShapeDtypeStruct((T, Dm), CDT)],
        compiler_params=_cparams(("arbitrary",)),
    )(y, tgt)


def _swiglu_fwd(ab, *, name, tm=256):
    T, F2 = ab.shape
    F = F2 // 2

    def body(ab_ref, s_ref):
        a = ab_ref[:, :F]
        b = ab_ref[:, F:]
        s_ref[...] = (a * _sigmoid(a) * b).astype(s_ref.dtype)

    return pl.pallas_call(
        body, name=name, grid=(T // tm,),
        in_specs=[pl.BlockSpec((tm, F2), lambda i: (i, 0))], out_specs=pl.BlockSpec((tm, F), lambda i: (i, 0)),
        out_shape=jax.ShapeDtypeStruct((T, F), CDT),
        compiler_params=_cparams(("parallel",)),
    )(ab)


def _swiglu_bwd(ds, ab, *, name, tm=256):
    T, F2 = ab.shape
    F = F2 // 2

    def body(ds_ref, ab_ref, o_ref):
        a = ab_ref[:, :F]
        b = ab_ref[:, F:]
        dsv = ds_ref[...]
        sg = _sigmoid(a)
        o_ref[:, :F] = (dsv * b * (sg * (1.0 + a * (1.0 - sg)))).astype(o_ref.dtype)
        o_ref[:, F:] = (dsv * (a * sg)).astype(o_ref.dtype)

    return pl.pallas_call(
        body, name=name, grid=(T // tm,),
        in_specs=[pl.BlockSpec((tm, F), lambda i: (i, 0)), pl.BlockSpec((tm, F2), lambda i: (i, 0))],
        out_specs=pl.BlockSpec((tm, F2), lambda i: (i, 0)),
        out_shape=jax.ShapeDtypeStruct((T, F2), CDT),
        compiler_params=_cparams(("parallel",)),
    )(ds, ab)


def _conv_fwd(a, w_dw, b_dw, ln_g, ln_b, *, tm=256):
    T = a.shape[0]
    Dm = D_MODEL

    def body(a_ref, w_ref, bdw_ref, g_ref, b_ref, s_ref, u_ref, c_ref, ubuf):
        i = pl.program_id(0)

        @pl.when(i == 0)
        def _():
            ubuf[0:CONV_HALO, :] = jnp.zeros((CONV_HALO, Dm), F32)

        @pl.when(i > 0)
        def _():
            ubuf[0:CONV_HALO, :] = ubuf[tm:tm + CONV_HALO, :]

        u = a_ref[:, :Dm] * _sigmoid(a_ref[:, Dm:])
        ubuf[CONV_HALO:CONV_HALO + tm, :] = u
        u_ref[...] = u
        acc = jnp.zeros((tm, Dm), F32) + bdw_ref[...]
        for k in range(CONV_WIDTH):
            acc = acc + w_ref[k:k + 1, :] * ubuf[pl.ds(CONV_HALO - (CONV_WIDTH - 1) + k, tm), :]
        c_ref[...] = acc
        mu = jnp.mean(acc, axis=-1, keepdims=True)
        cen = acc - mu
        var = jnp.mean(cen * cen, axis=-1, keepdims=True)
        l = cen * lax.rsqrt(var + EPS) * g_ref[...] + b_ref[...]
        s_ref[...] = (l * _sigmoid(l)).astype(s_ref.dtype)

    row = pl.BlockSpec((tm, Dm), lambda i: (i, 0))
    vec = pl.BlockSpec((1, Dm), lambda i: (0, 0))
    return pl.pallas_call(
        body, name="conv_fwd", grid=(T // tm,),
        in_specs=[pl.BlockSpec((tm, 2 * Dm), lambda i: (i, 0)), pl.BlockSpec((CONV_HALO, Dm), lambda i: (0, 0)), vec, vec, vec],
        out_specs=[row, row, row],
        out_shape=[jax.ShapeDtypeStruct((T, Dm), CDT), jax.ShapeDtypeStruct((T, Dm), F32), jax.ShapeDtypeStruct((T, Dm), F32)],
        scratch_shapes=[pltpu.VMEM((tm + CONV_HALO, Dm), F32)],
        compiler_params=_cparams(("arbitrary",)),
    )(a, w_dw, b_dw, ln_g, ln_b)


def _conv_bwd(ds, c, u, a, w_dw, ln_g, ln_b, *, tm=256):
    T = a.shape[0]
    Dm = D_MODEL
    nt = T // tm

    def body(ds_ref, c_ref, u_ref, a_ref, w_ref, g_ref, b_ref,
             da_ref, db1_ref, dw_ref, dbdw_ref, dg_ref, dbln_ref, dcbuf):
        i = pl.program_id(0)

        @pl.when(i == 0)
        def _():
            dcbuf[tm:tm + CONV_HALO, :] = jnp.zeros((CONV_HALO, Dm), F32)
            db1_ref[...] = jnp.zeros_like(db1_ref)
            dw_ref[...] = jnp.zeros_like(dw_ref)
            dbdw_ref[...] = jnp.zeros_like(dbdw_ref)
            dg_ref[...] = jnp.zeros_like(dg_ref)
            dbln_ref[...] = jnp.zeros_like(dbln_ref)

        @pl.when(i > 0)
        def _():
            dcbuf[tm:tm + CONV_HALO, :] = dcbuf[0:CONV_HALO, :]

        cv = c_ref[...]
        mu = jnp.mean(cv, axis=-1, keepdims=True)
        cen = cv - mu
        var = jnp.mean(cen * cen, axis=-1, keepdims=True)
        rstd = lax.rsqrt(var + EPS)
        n = cen * rstd
        l = n * g_ref[...] + b_ref[...]
        sg = _sigmoid(l)
        dl = ds_ref[...] * (sg * (1.0 + l * (1.0 - sg)))
        dg_ref[...] += jnp.sum(dl * n, axis=0, keepdims=True)
        dbln_ref[...] += jnp.sum(dl, axis=0, keepdims=True)
        dn = dl * g_ref[...]
        dc = rstd * (dn - jnp.mean(dn, axis=-1, keepdims=True) - n * jnp.mean(dn * n, axis=-1, keepdims=True))
        dbdw_ref[...] += jnp.sum(dc, axis=0, keepdims=True)
        dcbuf[0:tm, :] = dc
        uv = u_ref[...]
        du = jnp.zeros((tm, Dm), F32)
        for k in range(CONV_WIDTH):
            slab = dcbuf[pl.ds(CONV_WIDTH - 1 - k, tm), :]
            du = du + w_ref[k:k + 1, :] * slab
            dw_ref[k:k + 1, :] += jnp.sum(slab * uv, axis=0, keepdims=True)
        a1 = a_ref[:, :Dm]
        s2 = _sigmoid(a_ref[:, Dm:])
        da1 = du * s2
        da2 = du * a1 * (s2 * (1.0 - s2))
        da_ref[:, :Dm] = da1.astype(da_ref.dtype)
        da_ref[:, Dm:] = da2.astype(da_ref.dtype)
        db1_ref[:, :Dm] += jnp.sum(da1, axis=0, keepdims=True)
        db1_ref[:, Dm:] += jnp.sum(da2, axis=0, keepdims=True)

    rev = lambda i: (nt - 1 - i, 0)
    row = pl.BlockSpec((tm, Dm), rev)
    row2 = pl.BlockSpec((tm, 2 * Dm), rev)
    vec = pl.BlockSpec((1, Dm), lambda i: (0, 0))
    vec2 = pl.BlockSpec((1, 2 * Dm), lambda i: (0, 0))
    taps = pl.BlockSpec((CONV_HALO, Dm), lambda i: (0, 0))
    return pl.pallas_call(
        body, name="conv_bwd", grid=(nt,),
        in_specs=[row, row, row, row2, taps, vec, vec],
        out_specs=[row2, vec2, taps, vec, vec, vec],
        out_shape=[jax.ShapeDtypeStruct((T, 2 * Dm), CDT), jax.ShapeDtypeStruct((1, 2 * Dm), F32),
                   jax.ShapeDtypeStruct((CONV_HALO, Dm), F32), jax.ShapeDtypeStruct((1, Dm), F32),
                   jax.ShapeDtypeStruct((1, Dm), F32), jax.ShapeDtypeStruct((1, Dm), F32)],
        scratch_shapes=[pltpu.VMEM((tm + CONV_HALO, Dm), F32)],
        compiler_params=_cparams(("arbitrary",)),
    )(ds, c, u, a, w_dw, ln_g, ln_b)


def _gla_head_masks(width, per_head):
    lane = lax.broadcasted_iota(jnp.int32, (1, width), 1)
    return [((lane >= h * per_head) & (lane < (h + 1) * per_head)).astype(F32) for h in range(GLA_HEADS)]


def _gla_specs(tm, order):
    return [pl.BlockSpec((tm, GLA_DK), lambda i: (order(i), 0)),
            pl.BlockSpec((tm, GLA_DK), lambda i: (order(i), 1)),
            pl.BlockSpec((tm, GLA_DV), lambda i: (order(i), 1)),
            pl.BlockSpec((tm, GLA_DV), lambda i: (order(i), 2)),
            pl.BlockSpec((tm, 128), lambda i: (order(i), 3072 // 128))]


def _gla_chunk_decay(la_c, tri):
    bc = _dot_exact_lhs(tri, la_c, 3)
    b_end = bc[CHUNK - 1:CHUNK, :]
    return b_end, jnp.exp(b_end - bc)


def _gla_fwd(proj, wg2p, b_gate, g_gla, *, tm=256):
    T = proj.shape[0]
    ncs = tm // CHUNK
    scale = GLA_HEAD_K ** -0.5

    def body(q_ref, k_ref, v_ref, r_ref, glr_ref, wg_ref, bg_ref, gg_ref, o_ref, oraw_ref, st_ref, s_scr):
        i = pl.program_id(0)

        @pl.when(i == 0)
        def _():
            s_scr[...] = jnp.zeros_like(s_scr)

        mk = _gla_head_masks(GLA_DK, GLA_HEAD_K)
        rr = lax.broadcasted_iota(jnp.int32, (CHUNK, CHUNK), 0)
        cc = lax.broadcasted_iota(jnp.int32, (CHUNK, CHUNK), 1)
        tri = (cc <= rr).astype(BF16)
        y = _dg(glr_ref[...], wg_ref[...], _NN) + bg_ref[...]
        la = _log_sigmoid(y) / GLA_GATE_NORMALIZER
        qs = q_ref[...] * scale
        for ci in range(ncs):
            rows = slice(ci * CHUNK, (ci + 1) * CHUNK)
            b_end, dec = _gla_chunk_decay(la[rows], tri)
            kend = (k_ref[rows, :] * dec).astype(CDT)
            upd = jnp.zeros((GLA_HEAD_V, GLA_DK), F32)
            for h in range(GLA_HEADS):
                vh = v_ref[rows, h * GLA_HEAD_V:(h + 1) * GLA_HEAD_V]
                upd = upd + mk[h] * _dg(vh, kend, _TN)
            s_new = jnp.exp(b_end) * s_scr[...] + upd
            s_scr[...] = s_new
            st_ref[ci] = s_new
            s_c = s_new.astype(CDT)
            for h in range(GLA_HEADS):
                o_h = _dg(qs[rows] * mk[h], s_c, _NT)
                oraw_ref[rows, h * GLA_HEAD_V:(h + 1) * GLA_HEAD_V] = o_h
        for h in range(GLA_HEADS):
            cols = slice(h * GLA_HEAD_V, (h + 1) * GLA_HEAD_V)
            o_h = oraw_ref[:, cols]
            rs = lax.rsqrt(jnp.mean(o_h * o_h, axis=-1, keepdims=True) + EPS)
            rg = r_ref[:, cols]
            o_ref[:, cols] = (o_h * rs * gg_ref[...] * (rg * _sigmoid(rg))).astype(o_ref.dtype)

    full = lambda shape: pl.BlockSpec(shape, lambda i: tuple(0 for _ in shape))
    return pl.pallas_call(
        body, name="gla_fwd", grid=(T // tm,),
        in_specs=_gla_specs(tm, lambda i: i) + [full((128, GLA_DK)), full((1, GLA_DK)), full((1, GLA_HEAD_V))],
        out_specs=[pl.BlockSpec((tm, GLA_DV), lambda i: (i, 0)), pl.BlockSpec((tm, GLA_DV), lambda i: (i, 0)),
                   pl.BlockSpec((ncs, GLA_HEAD_V, GLA_DK), lambda i: (i, 0, 0))],
        out_shape=[jax.ShapeDtypeStruct((T, GLA_DV), CDT), jax.ShapeDtypeStruct((T, GLA_DV), F32),
                   jax.ShapeDtypeStruct((T // CHUNK, GLA_HEAD_V, GLA_DK), F32)],
        scratch_shapes=[pltpu.VMEM((GLA_HEAD_V, GLA_DK), F32)],
        compiler_params=_cparams(("arbitrary",)),
    )(proj, proj, proj, proj, proj, wg2p, b_gate, g_gla)


def _gla_bwd(d_o, proj, oraw, states, wg2p, b_gate, g_gla, *, tm=256):
    T = proj.shape[0]
    nt = T // tm
    ncs = tm // CHUNK
    scale = GLA_HEAD_K ** -0.5

    def body(do_ref, q_ref, k_ref, v_ref, r_ref, glr_ref, oraw_ref, st_ref, stp_ref, wg_ref, bg_ref, gg_ref,
             dgla_ref, dglr_ref, dwg_ref, dbg_ref, dgg_ref, ds_scr, dy_scr, dor_scr):
        i = pl.program_id(0)
        tile = nt - 1 - i

        @pl.when(i == 0)
        def _():
            ds_scr[...] = jnp.zeros_like(ds_scr)
            dwg_ref[...] = jnp.zeros_like(dwg_ref)
            dbg_ref[...] = jnp.zeros_like(dbg_ref)
            dgg_ref[...] = jnp.zeros_like(dgg_ref)

        mk = _gla_head_masks(GLA_DK, GLA_HEAD_K)
        rr = lax.broadcasted_iota(jnp.int32, (CHUNK, CHUNK), 0)
        cc = lax.broadcasted_iota(jnp.int32, (CHUNK, CHUNK), 1)
        tri = (cc <= rr).astype(BF16)
        tri_t = (cc >= rr).astype(BF16)
        last_row = (lax.broadcasted_iota(jnp.int32, (CHUNK, 1), 0) == CHUNK - 1).astype(F32)

        dgg = jnp.zeros((1, GLA_HEAD_V), F32)
        for h in range(GLA_HEADS):
            cols = slice(h * GLA_HEAD_V, (h + 1) * GLA_HEAD_V)
            o_h = oraw_ref[:, cols]
            rs = lax.rsqrt(jnp.mean(o_h * o_h, axis=-1, keepdims=True) + EPS)
            rg = r_ref[:, cols]
            sg = _sigmoid(rg)
            dov = do_ref[:, cols]
            on = o_h * rs * gg_ref[...]
            d_on = dov * (rg * sg)
            dgla_ref[:, 2 * GLA_DK + GLA_DV + h * GLA_HEAD_V:2 * GLA_DK + GLA_DV + (h + 1) * GLA_HEAD_V] = (
                dov * on * (sg * (1.0 + rg * (1.0 - sg)))).astype(dgla_ref.dtype)
            dgg = dgg + jnp.sum(d_on * o_h * rs, axis=0, keepdims=True)
            uu = d_on * gg_ref[...]
            dor_scr[:, cols] = rs * uu - o_h * (rs * rs * rs * jnp.mean(uu * o_h, axis=-1, keepdims=True))
        dgg_ref[...] += dgg

        y = _dg(glr_ref[...], wg_ref[...], _NN) + bg_ref[...]
        la = _log_sigmoid(y) / GLA_GATE_NORMALIZER
        qs = q_ref[...] * scale
        for ci in reversed(range(ncs)):
            rows = slice(ci * CHUNK, (ci + 1) * CHUNK)
            b_end, dec = _gla_chunk_decay(la[rows], tri)
            decay = jnp.exp(b_end)
            kend = k_ref[rows, :] * dec
            kend_c = kend.astype(CDT)
            s_c = st_ref[ci].astype(CDT)
            if ci > 0:
                s_prev = st_ref[ci - 1]
            else:
                s_prev = jnp.where(tile > 0, stp_ref[0], 0.0)
            dqs = jnp.zeros((CHUNK, GLA_DK), F32)
            dst = ds_scr[...]
            for h in range(GLA_HEADS):
                do_h = dor_scr[rows, h * GLA_HEAD_V:(h + 1) * GLA_HEAD_V].astype(CDT)
                dqs = dqs + mk[h] * _dg(do_h, s_c, _NN)
                dst = dst + mk[h] * _dg(do_h, qs[rows], _TN)
            d_decay = jnp.sum(dst * s_prev, axis=0, keepdims=True)
            ds_scr[...] = decay * dst
            dst_c = dst.astype(CDT)
            dkend = jnp.zeros((CHUNK, GLA_DK), F32)
            for h in range(GLA_HEADS):
                cols = slice(h * GLA_HEAD_V, (h + 1) * GLA_HEAD_V)
                dv_h = _dg(kend * mk[h], dst_c, _NT)
                dgla_ref[rows, 2 * GLA_DK + h * GLA_HEAD_V:2 * GLA_DK + (h + 1) * GLA_HEAD_V] = dv_h.astype(dgla_ref.dtype)
                dkend = dkend + mk[h] * _dg(v_ref[rows, cols], dst_c, _NN)
            dgla_ref[rows, 0:GLA_DK] = (dqs * scale).astype(dgla_ref.dtype)
            dgla_ref[rows, GLA_DK:2 * GLA_DK] = (dkend * dec).astype(dgla_ref.dtype)
            mm = dkend * kend
            db_end = jnp.sum(mm, axis=0, keepdims=True) + d_decay * decay
            dbc = last_row * db_end - mm
            dla = _dot_exact_lhs(tri_t, dbc, 3)
            dy_scr[rows, :] = dla * (1.0 / GLA_GATE_NORMALIZER) * _sigmoid(-y[rows])
        dy = dy_scr[...]
        dbg_ref[...] += jnp.sum(dy, axis=0, keepdims=True)
        dwg_ref[...] += _dg(glr_ref[...], dy, _TN)
        dglr_ref[...] = _dg(dy, wg_ref[...], _NT).astype(dglr_ref.dtype)

    rev = lambda i: nt - 1 - i
    full = lambda shape: pl.BlockSpec(shape, lambda i: tuple(0 for _ in shape))
    st_spec = pl.BlockSpec((ncs, GLA_HEAD_V, GLA_DK), lambda i: (rev(i), 0, 0))
    stp_spec = pl.BlockSpec((1, GLA_HEAD_V, GLA_DK), lambda i: (jnp.maximum(rev(i) * ncs - 1, 0), 0, 0))
    return pl.pallas_call(
        body, name="gla_bwd", grid=(nt,),
        in_specs=[pl.BlockSpec((tm, GLA_DV), lambda i: (rev(i), 0))] + _gla_specs(tm, rev)
        + [pl.BlockSpec((tm, GLA_DV), lambda i: (rev(i), 0)), st_spec, stp_spec,
           full((128, GLA_DK)), full((1, GLA_DK)), full((1, GLA_HEAD_V))],
        out_specs=[pl.BlockSpec((tm, 2 * GLA_DK + 2 * GLA_DV), lambda i: (rev(i), 0)),
                   pl.BlockSpec((tm, 128), lambda i: (rev(i), 0)),
                   full((128, GLA_DK)), full((1, GLA_DK)), full((1, GLA_HEAD_V))],
        out_shape=[jax.ShapeDtypeStruct((T, 2 * GLA_DK + 2 * GLA_DV), CDT), jax.ShapeDtypeStruct((T, 128), CDT),
                   jax.ShapeDtypeStruct((128, GLA_DK), F32), jax.ShapeDtypeStruct((1, GLA_DK), F32),
                   jax.ShapeDtypeStruct((1, GLA_HEAD_V), F32)],
        scratch_shapes=[pltpu.VMEM((GLA_HEAD_V, GLA_DK), F32), pltpu.VMEM((tm, GLA_DK), F32),
                        pltpu.VMEM((tm, GLA_DV), F32)],
        compiler_params=_cparams(("arbitrary",)),
    )(d_o, proj, proj, proj, proj, proj, oraw, states, states, wg2p, b_gate, g_gla)


def _head_mean_matrix():
    r = lax.broadcasted_iota(jnp.int32, (SB_D, SB_D), 0) // SB_HEAD_DIM
    c = lax.broadcasted_iota(jnp.int32, (SB_D, SB_D), 1) // SB_HEAD_DIM
    return jnp.where(r == c, 1.0 / SB_HEAD_DIM, 0.0).astype(BF16)


def _sb_prep(proj, gq, gk, *, tm=256):
    T = proj.shape[0]
    scale = SB_HEAD_DIM ** -0.5

    def body(q_ref, k_ref, v_ref, gq_ref, gk_ref, qn_ref, kn_ref, vb_ref):
        hm = _head_mean_matrix()
        qv, kv = q_ref[...], k_ref[...]
        rq = lax.rsqrt(_dot_exact_rhs(qv * qv, hm, 3) + EPS)
        rk = lax.rsqrt(_dot_exact_rhs(kv * kv, hm, 3) + EPS)
        qn_ref[...] = (qv * rq * gq_ref[...] * scale).astype(qn_ref.dtype)
        kn_ref[...] = (kv * rk * gk_ref[...]).astype(kn_ref.dtype)
        vb_ref[...] = v_ref[...].astype(vb_ref.dtype)

    col = lambda j: pl.BlockSpec((tm, SB_D), lambda i: (i, j))
    vec = pl.BlockSpec((1, SB_D), lambda i: (0, 0))
    out = pl.BlockSpec((tm, SB_D), lambda i: (i, 0))
    return pl.pallas_call(
        body, name="sb_prep", grid=(T // tm,),
        in_specs=[col(3), col(4), col(5), vec, vec], out_specs=[out, out, out],
        out_shape=[jax.ShapeDtypeStruct((T, SB_D), CDT)] * 3,
        compiler_params=_cparams(("parallel",)),
    )(proj, proj, proj, gq, gk)


def _sb_prep_bwd(dqn, dkn, dv, proj, gq, gk, *, tm=256):
    T = proj.shape[0]
    scale = SB_HEAD_DIM ** -0.5

    def body(dqn_ref, dkn_ref, dv_ref, q_ref, k_ref, gq_ref, gk_ref, dsb_ref, dgq_ref, dgk_ref):
        i = pl.program_id(0)

        @pl.when(i == 0)
        def _():
            dgq_ref[...] = jnp.zeros_like(dgq_ref)
            dgk_ref[...] = jnp.zeros_like(dgk_ref)

        hm = _head_mean_matrix()

        def one(dn_ref, x_ref, g_ref, dg_ref, sc, lo):
            xv = x_ref[...]
            dnv = dn_ref[...] * sc
            r = lax.rsqrt(_dot_exact_rhs(xv * xv, hm, 3) + EPS)
            u = dnv * g_ref[...]
            dot = _dot_exact_rhs(u * xv, hm, 3)
            dsb_ref[:, lo:lo + SB_D] = (r * u - xv * (r * r * r * dot)).astype(dsb_ref.dtype)
            dg_ref[...] += jnp.sum(dnv * xv * r, axis=0, keepdims=True)

        one(dqn_ref, q_ref, gq_ref, dgq_ref, scale, 0)
        one(dkn_ref, k_ref, gk_ref, dgk_ref, 1.0, SB_D)
        dsb_ref[:, 2 * SB_D:3 * SB_D] = dv_ref[...].astype(dsb_ref.dtype)

    col = lambda j: pl.BlockSpec((tm, SB_D), lambda i: (i, j))
    vec = pl.BlockSpec((1, SB_D), lambda i: (0, 0))
    row = pl.BlockSpec((tm, SB_D), lambda i: (i, 0))
    return pl.pallas_call(
        body, name="sb_prep_bwd", grid=(T // tm,),
        in_specs=[row, row, row, col(3), col(4), vec, vec],
        out_specs=[pl.BlockSpec((tm, 3 * SB_D), lambda i: (i, 0)), vec, vec],
        out_shape=[jax.ShapeDtypeStruct((T, 3 * SB_D), CDT), jax.ShapeDtypeStruct((1, SB_D), F32),
                   jax.ShapeDtypeStruct((1, SB_D), F32)],
        compiler_params=_cparams(("arbitrary",)),
    )(dqn, dkn, dv, proj, proj, gq, gk)


def _sb_masks():
    lane = lax.broadcasted_iota(jnp.int32, (1, 128), 1)
    m = [lane < SB_HEAD_DIM, lane >= SB_HEAD_DIM]
    return m, [x.astype(F32) for x in m]


def _sb_fwd(qn, kn, vb):
    T = qn.shape[0]
    nq = T // SB_BLOCK
    B = SB_BLOCK

    def body(q_ref, k_ref, v_ref, o_ref, l_ref, acc_ref):
        qb = pl.program_id(1)
        m, mf = _sb_masks()
        row = lax.broadcasted_iota(jnp.int32, (B, B), 0)
        col = lax.broadcasted_iota(jnp.int32, (B, B), 1)
        later = (row > col).astype(BF16)
        past = col < row
        q2 = q_ref[...]
        qm = [jnp.where(m[h], q2, jnp.zeros_like(q2)) for h in range(2)]
        acc_ref[...] = jnp.zeros_like(acc_ref)

        def tile(kb, R, diag):
            koff = pl.multiple_of(kb * B, B)
            k2 = k_ref[pl.ds(koff, B), :]
            v2 = v_ref[pl.ds(koff, B), :]
            out = []
            for h in range(2):
                z = _dg(qm[h], k2, _NT)
                lsn = _log_sigmoid(-z)
                lk = jnp.where(past, lsn, 0.0) if diag else lsn
                e = (z + lsn) + (_dot_exact_rhs(lk, later, 2) + R[h])
                w = jnp.exp(e)
                if diag:
                    w = jnp.where(past, w, 0.0)
                acc_ref[h] += _dg(w, v2, _NN)
                out.append(R[h] + jnp.sum(lk, axis=1, keepdims=True))
            return tuple(out)

        zero = jnp.zeros((B, 1), F32)
        R = tile(qb, (zero, zero), True)
        R = lax.fori_loop(0, qb, lambda i, c: tile(qb - 1 - i, c, False), R)
        o_ref[...] = (acc_ref[0] * mf[0] + acc_ref[1] * mf[1]).astype(o_ref.dtype)
        l_ref[0] = R[0] * mf[0] + R[1] * mf[1]

    slab = pl.BlockSpec((T, B), lambda hp, qb: (0, hp))
    blk = pl.BlockSpec((B, B), lambda hp, qb: (qb, hp))
    return pl.pallas_call(
        body, name="sb_fwd", grid=(SB_D // B, nq),
        in_specs=[blk, slab, slab],
        out_specs=[blk, pl.BlockSpec((1, B, B), lambda hp, qb: (hp, qb, 0))],
        out_shape=[jax.ShapeDtypeStruct((T, SB_D), CDT), jax.ShapeDtypeStruct((SB_D // B, T, B), F32)],
        scratch_shapes=[pltpu.VMEM((2, B, B), F32)],
        compiler_params=_cparams(("arbitrary", "arbitrary")),
    )(qn, kn, vb)


def _sb_bwd(d_o, qn, kn, vb, lsum):
    T = qn.shape[0]
    nq = T // SB_BLOCK
    B = SB_BLOCK

    def body(do_ref, q_ref, k_ref, v_ref, l_ref, dq_ref, dk_ref, dv_ref, dqacc_ref):
        qb = pl.program_id(1)

        @pl.when(qb == 0)
        def _():
            dk_ref[...] = jnp.zeros_like(dk_ref)
            dv_ref[...] = jnp.zeros_like(dv_ref)

        m, mf = _sb_masks()
        row = lax.broadcasted_iota(jnp.int32, (B, B), 0)
        col = lax.broadcasted_iota(jnp.int32, (B, B), 1)
        upto = (row <= col).astype(BF16)
        before = (row < col).astype(BF16)
        past = col < row
        q2 = q_ref[...]
        qm = [jnp.where(m[h], q2, jnp.zeros_like(q2)) for h in range(2)]
        do2 = do_ref[...]
        dom = [jnp.where(m[h], do2, 0.0).astype(CDT) for h in range(2)]
        lb = l_ref[0]
        ltot = [lb[:, 0:1], lb[:, SB_HEAD_DIM:SB_HEAD_DIM + 1]]
        dqacc_ref[...] = jnp.zeros_like(dqacc_ref)

        def tile(kb, carry, diag):
            P, Pg = carry
            koff = pl.multiple_of(kb * B, B)
            k2 = k_ref[pl.ds(koff, B), :]
            v2 = v_ref[pl.ds(koff, B), :]
            newP, newPg = [], []
            for h in range(2):
                z = _dg(qm[h], k2, _NT)
                lsn = _log_sigmoid(-z)
                lk = jnp.where(past, lsn, 0.0) if diag else lsn
                between = (ltot[h] - P[h]) - _dot_exact_rhs(lk, upto, 2)
                w = jnp.exp((z + lsn) + between)
                if diag:
                    w = jnp.where(past, w, 0.0)
                dw = _dg(dom[h], v2, _NT)
                g = w * dw
                gx = _dot_exact_rhs(g, before, 2) + Pg[h]
                sneg = jnp.exp(lsn)
                dz = g * sneg - (1.0 - sneg) * gx
                if diag:
                    dz = jnp.where(past, dz, 0.0)
                dz_c = dz.astype(CDT)
                dv_ref[pl.ds(koff, B), :] += _dg(w, dom[h], _TN)
                dk_ref[pl.ds(koff, B), :] += _dg(dz_c, qm[h], _TN)
                dqacc_ref[h] += _dg(dz_c, k2, _NN)
                newP.append(P[h] + jnp.sum(lk, axis=1, keepdims=True))
                newPg.append(Pg[h] + jnp.sum(g, axis=1, keepdims=True))
            return tuple(newP), tuple(newPg)

        zero = jnp.zeros((B, 1), F32)
        carry = lax.fori_loop(0, qb, lambda i, c: tile(i, c, False), ((zero, zero), (zero, zero)))
        tile(qb, carry, True)
        dq_ref[...] = dqacc_ref[0] * mf[0] + dqacc_ref[1] * mf[1]

    slab = pl.BlockSpec((T, B), lambda hp, qb: (0, hp))
    blk = pl.BlockSpec((B, B), lambda hp, qb: (qb, hp))
    return pl.pallas_call(
        body, name="sb_bwd", grid=(SB_D // B, nq),
        in_specs=[blk, blk, slab, slab, pl.BlockSpec((1, B, B), lambda hp, qb: (hp, qb, 0))],
        out_specs=[blk, slab, slab],
        out_shape=[jax.ShapeDtypeStruct((T, SB_D), F32)] * 3,
        scratch_shapes=[pltpu.VMEM((2, B, B), F32)],
        compiler_params=_cparams(("arbitrary", "arbitrary")),
    )(d_o, qn, kn, vb, lsum)


def _regroup_in_cols(w):
    cut = 2 * GLA_DK + 2 * GLA_DV
    pad = jnp.zeros((w.shape[0], IN_PAD - IN_WIDTH), w.dtype)
    return jnp.concatenate([w[:, :cut], w[:, cut + GLA_GATE_RANK:], w[:, cut:cut + GLA_GATE_RANK], pad], axis=1)


def _ungroup_in_cols(g):
    cut = 2 * GLA_DK + 2 * GLA_DV
    return jnp.concatenate([g[:, :cut], g[:, 3072:3072 + GLA_GATE_RANK], g[:, cut:3072]], axis=1)


def _colsum(v, *, name, tm=512):
    T, C = v.shape

    def body(v_ref, o_ref):
        i = pl.program_id(0)
        part = jnp.sum(v_ref[...], axis=0, keepdims=True)

        @pl.when(i == 0)
        def _():
            o_ref[...] = part

        @pl.when(i > 0)
        def _():
            o_ref[...] += part

    return pl.pallas_call(
        body, name=name, grid=(T // tm,),
        in_specs=[pl.BlockSpec((tm, C), lambda i: (i, 0))], out_specs=pl.BlockSpec((1, C), lambda i: (0, 0)),
        out_shape=jax.ShapeDtypeStruct((1, C), F32),
        compiler_params=_cparams(("arbitrary",)),
    )(v)


def _ffn_fwd(h, g_norm, wgu, wd, tag):
    hf = _rms_fwd(h, g_norm, name=f"ffn{tag}_norm")
    ab = _matmul(hf, wgu, mode='nn', out_dtype=F32, name=f"ffn{tag}_up", tm=256, tn=D_FF, tk=D_MODEL)
    s = _swiglu_fwd(ab, name=f"ffn{tag}_act")
    h_out = _matmul(s, wd, mode='nn', out_dtype=F32, name=f"ffn{tag}_down", tm=512, tn=D_MODEL, tk=D_FF, residual=h)
    return h_out, (hf, ab, s)


def _ffn_bwd(dh, dh_c, h_in, g_norm, wgu, wd, saved, tag):
    hf, ab, s = saved
    ds = _matmul(dh_c, wd, mode='nt', out_dtype=F32, name=f"ffn{tag}_dact", tm=512, tn=D_FF // 2, tk=D_MODEL)
    dwd = _matmul(s, dh_c, mode='tn', out_dtype=F32, name=f"ffn{tag}_dwd", tm=D_FF // 2, tn=D_MODEL, tk=512)
    dab = _swiglu_bwd(ds, ab, name=f"ffn{tag}_dab")
    dwgu = _matmul(hf, dab, mode='tn', out_dtype=F32, name=f"ffn{tag}_dwgu", tm=D_MODEL, tn=D_FF // 2, tk=512)
    dhf = _matmul(dab, wgu, mode='nt', out_dtype=F32, name=f"ffn{tag}_dhf", tm=256, tn=D_MODEL, tk=2 * D_FF)
    dh_in, dh_in_c, dg = _rms_bwd(dhf, h_in, g_norm, dh, name=f"ffn{tag}_dnorm")
    return dh_in, dh_in_c, dwgu, dwd, dg


def _local_step(x, tgt, W):
    row = lambda v: v.reshape(1, -1)
    win_p = _regroup_in_cols(W['hy_w_in'][0])
    wg2p = jnp.concatenate([W['hy_w_gate2'][0], jnp.zeros((128 - GLA_GATE_RANK, GLA_DK), W['hy_w_gate2'].dtype)], axis=0)
    b_gate = row(W['hy_b_gate'][0])
    g_gla = row(W['hy_gla_norm'][0])
    gq = jnp.tile(W['hy_sb_q_norm'][0], SB_D // SB_HEAD_DIM).reshape(1, SB_D)
    gk = jnp.tile(W['hy_sb_k_norm'][0], SB_D // SB_HEAD_DIM).reshape(1, SB_D)
    w_out = W['hy_w_out'][0]
    wgu = [jnp.concatenate([W['ffn_w_gate'][l], W['ffn_w_up'][l]], axis=1) for l in range(2)]
    wd = [W['ffn_w_down'][l] for l in range(2)]
    w_dw = jnp.concatenate([W['cv_w_dw'][0], jnp.zeros((1, D_MODEL), F32)], axis=0)
    mixn = [row(W['mix_norm'][l]) for l in range(2)]
    ffnn = [row(W['ffn_norm'][l]) for l in range(2)]

    hn0 = _rms_fwd(x, mixn[0], name="mix0_norm")
    proj = _matmul(hn0, win_p, mode='nn', out_dtype=F32, name="hy_in", tm=256, tn=IN_PAD, tk=D_MODEL)
    o_gla, o_raw, states = _gla_fwd(proj, wg2p, b_gate, g_gla)
    qn, kn, vb = _sb_prep(proj, gq, gk)
    o_sb, lsum = _sb_fwd(qn, kn, vb)
    o_mix = jnp.concatenate([o_gla, o_sb], axis=1)
    h1 = _matmul(o_mix, w_out, mode='nn', out_dtype=F32, name="hy_out", tm=512, tn=D_MODEL, tk=D_MODEL, residual=x)
    h2, ffn0_saved = _ffn_fwd(h1, ffnn[0], wgu[0], wd[0], 0)
    hn1 = _rms_fwd(h2, mixn[1], name="mix1_norm")
    a_cv = _matmul(hn1, W['cv_w_pw1'][0], mode='nn', out_dtype=F32, name="cv_pw1", tm=512, tn=2 * D_MODEL, tk=D_MODEL,
                   bias=row(W['cv_b_pw1'][0]))
    s_cv, u_cv, c_cv = _conv_fwd(a_cv, w_dw, row(W['cv_b_dw'][0]), row(W['cv_ln_g'][0]), row(W['cv_ln_b'][0]))
    h3 = _matmul(s_cv, W['cv_w_pw2'][0], mode='nn', out_dtype=F32, name="cv_pw2", tm=512, tn=D_MODEL, tk=D_MODEL,
                 bias=row(W['cv_b_pw2'][0]), residual=h2)
    h4, ffn1_saved = _ffn_fwd(h3, ffnn[1], wgu[1], wd[1], 1)
    sq_err, dy, dy_c = _loss_head(h4, tgt)

    G = {}
    dh3, dh3_c, dwgu1, dwd1, dg_ffn1 = _ffn_bwd(dy, dy_c, h3, ffnn[1], wgu[1], wd[1], ffn1_saved, 1)
    G['cv_b_pw2'] = _colsum(dh3, name="cv_db2")
    G['cv_w_pw2'] = _matmul(s_cv, dh3_c, mode='tn', out_dtype=F32, name="cv_dw2", tm=D_MODEL, tn=D_MODEL, tk=512)[None]
    ds_cv = _matmul(dh3_c, W['cv_w_pw2'][0], mode='nt', out_dtype=F32, name="cv_ds", tm=512, tn=D_MODEL, tk=D_MODEL)
    da_cv, db1, dwdw, dbdw, dlng, dlnb = _conv_bwd(ds_cv, c_cv, u_cv, a_cv, w_dw, row(W['cv_ln_g'][0]), row(W['cv_ln_b'][0]))
    G['cv_b_pw1'] = db1
    G['cv_w_dw'] = dwdw[None, :CONV_WIDTH]
    G['cv_b_dw'], G['cv_ln_g'], G['cv_ln_b'] = dbdw, dlng, dlnb
    G['cv_w_pw1'] = _matmul(hn1, da_cv, mode='tn', out_dtype=F32, name="cv_dw1", tm=D_MODEL, tn=D_MODEL, tk=512)[None]
    dhn1 = _matmul(da_cv, W['cv_w_pw1'][0], mode='nt', out_dtype=F32, name="cv_dhn", tm=512, tn=D_MODEL, tk=2 * D_MODEL)
    dh2, dh2_c, dg_mix1 = _rms_bwd(dhn1, h2, mixn[1], dh3, name="mix1_dnorm")
    dh1, dh1_c, dwgu0, dwd0, dg_ffn0 = _ffn_bwd(dh2, dh2_c, h1, ffnn[0], wgu[0], wd[0], ffn0_saved, 0)
    G['hy_w_out'] = _matmul(o_mix, dh1_c, mode='tn', out_dtype=F32, name="hy_dwout", tm=D_MODEL, tn=D_MODEL, tk=512)[None]
    d_omix = _matmul(dh1_c, w_out, mode='nt', out_dtype=F32, name="hy_domix", tm=512, tn=D_MODEL, tk=D_MODEL)
    dgla, dglr, dwg2, dbg, dgg = _gla_bwd(d_omix[:, :GLA_DV], proj, o_raw, states, wg2p, b_gate, g_gla)
    dqn, dkn, dvs = _sb_bwd(d_omix[:, GLA_DV:], qn, kn, vb, lsum)
    dsb, dgq, dgk = _sb_prep_bwd(dqn, dkn, dvs, proj, gq, gk)
    dproj = jnp.concatenate([dgla, dsb, dglr], axis=1)
    dwin_p = _matmul(hn0, dproj, mode='tn', out_dtype=F32, name="hy_dwin", tm=D_MODEL, tn=IN_PAD // 5, tk=512)
    dhn0 = _matmul(dproj, win_p, mode='nt', out_dtype=F32, name="hy_dhn", tm=256, tn=D_MODEL, tk=IN_PAD)
    dx, _, dg_mix0 = _rms_bwd(dhn0, x, mixn[0], dh1, name="mix0_dnorm")

    G['hy_w_in'] = _ungroup_in_cols(dwin_p)[None]
    G['hy_w_gate2'] = dwg2[None, :GLA_GATE_RANK]
    G['hy_b_gate'] = dbg
    G['hy_gla_norm'] = dgg
    G['hy_sb_q_norm'] = dgq.reshape(SB_D // SB_HEAD_DIM, SB_HEAD_DIM).sum(axis=0)[None]
    G['hy_sb_k_norm'] = dgk.reshape(SB_D // SB_HEAD_DIM, SB_HEAD_DIM).sum(axis=0)[None]
    G['mix_norm'] = jnp.concatenate([dg_mix0, dg_mix1], axis=0)
    G['ffn_norm'] = jnp.concatenate([dg_ffn0, dg_ffn1], axis=0)
    G['ffn_w_gate'] = jnp.stack([dwgu0[:, :D_FF], dwgu1[:, :D_FF]])
    G['ffn_w_up'] = jnp.stack([dwgu0[:, D_FF:], dwgu1[:, D_FF:]])
    G['ffn_w_down'] = jnp.stack([dwd0, dwd1])
    return sq_err, dx, G


MESH_IDS = pl.DeviceIdType.MESH


def _all_gather(block):
    R, Wd = block.shape

    def body(x_ref, out_ref, send_sems, recv_sems, local_sem):
        x, y, c = lax.axis_index("x"), lax.axis_index("y"), lax.axis_index("c")
        me, sibling = (x, y, c), (x, y, 1 - c)
        chips = [(1 - x, y), (x, 1 - y), (1 - x, 1 - y)]

        def slot(px, py, pc):
            return out_ref.at[4 * px + 2 * py + pc]

        def copy(k, blk, to, src=None):
            return pltpu.make_async_remote_copy(
                src_ref=slot(*blk) if src is None else src, dst_ref=slot(*blk),
                send_sem=send_sems.at[k], recv_sem=recv_sems.at[k], device_id=to, device_id_type=MESH_IDS)

        mine = pltpu.make_async_copy(x_ref, slot(*me), local_sem)
        mine.start()
        first = [copy(0, me, sibling, src=x_ref)]
        first += [copy(1 + j, me, (*chip, c), src=x_ref) for j, chip in enumerate(chips)]
        for cp in first:
            cp.start()
        passed = [copy(4 + j, (*chip, c), sibling) for j, chip in enumerate(chips)]
        for j, chip in enumerate(chips):
            copy(1 + j, (*chip, c), me).wait_recv()
            passed[j].start()
        copy(0, sibling, me).wait_recv()
        for j, chip in enumerate(chips):
            copy(4 + j, (*chip, 1 - c), me).wait_recv()
        for cp in first + passed:
            cp.wait_send()
        mine.wait()

    return pl.pallas_call(
        body, name="fsdp_all_gather",
        out_shape=jax.ShapeDtypeStruct((N_DEV, R, Wd), block.dtype),
        in_specs=[pl.BlockSpec(memory_space=pl.ANY)], out_specs=pl.BlockSpec(memory_space=pl.ANY),
        scratch_shapes=[pltpu.SemaphoreType.DMA((7,)), pltpu.SemaphoreType.DMA((7,)), pltpu.SemaphoreType.DMA],
    )(block)


def _scatter_exchange(send):
    _, R, Wd = send.shape

    def body(s_ref, r_ref, send_sems, recv_sems, local_sem):
        x, y, c = lax.axis_index("x"), lax.axis_index("y"), lax.axis_index("c")
        me = 4 * x + 2 * y + c
        mine = pltpu.make_async_copy(s_ref.at[me], r_ref.at[me], local_sem)
        mine.start()
        copies = []
        for k in range(1, N_DEV):
            px, py, pc = x ^ ((k >> 2) & 1), y ^ ((k >> 1) & 1), c ^ (k & 1)
            cp = pltpu.make_async_remote_copy(
                src_ref=s_ref.at[4 * px + 2 * py + pc], dst_ref=r_ref.at[me],
                send_sem=send_sems.at[k - 1], recv_sem=recv_sems.at[k - 1],
                device_id=(px, py, pc), device_id_type=MESH_IDS)
            cp.start()
            copies.append(cp)
        for cp in copies:
            cp.wait()
        mine.wait()

    return pl.pallas_call(
        body, name="fsdp_scatter_exchange",
        out_shape=jax.ShapeDtypeStruct(send.shape, send.dtype),
        in_specs=[pl.BlockSpec(memory_space=pl.ANY)], out_specs=pl.BlockSpec(memory_space=pl.ANY),
        scratch_shapes=[pltpu.SemaphoreType.DMA((7,)), pltpu.SemaphoreType.DMA((7,)), pltpu.SemaphoreType.DMA],
    )(send)


def _sum_adamw(recv, w, m, v, *, tr=256):
    _, R, Wd = recv.shape
    assert R % tr == 0

    def body(r_ref, w_ref, m_ref, v_ref, g_ref, d_ref, mo_ref, vo_ref):
        g = r_ref[0]
        for s in range(1, N_DEV):
            g = g + r_ref[s]
        g_ref[...] = g
        mn = ADAM_B1 * m_ref[...] + (1.0 - ADAM_B1) * g
        vn = ADAM_B2 * v_ref[...] + (1.0 - ADAM_B2) * (g * g)
        m_hat = mn / (1.0 - ADAM_B1 ** ADAM_STEP)
        v_hat = vn / (1.0 - ADAM_B2 ** ADAM_STEP)
        d_ref[...] = -ADAM_LR * (m_hat / (jnp.sqrt(v_hat) + ADAM_EPS) + ADAM_WD * w_ref[...])
        mo_ref[...] = mn
        vo_ref[...] = vn

    row = pl.BlockSpec((tr, Wd), lambda i: (i, 0))
    return pl.pallas_call(
        body, name="sum_adamw", grid=(R // tr,),
        in_specs=[pl.BlockSpec((N_DEV, tr, Wd), lambda i: (0, i, 0)), row, row, row],
        out_specs=[row, row, row, row],
        out_shape=[jax.ShapeDtypeStruct((R, Wd), F32)] * 4,
        compiler_params=_cparams(("parallel",)),
    )(recv, w, m, v)


def _padded(n):
    return -(-n // PACK_ALIGN) * PACK_ALIGN


def _pack_rows(parts, lead=()):
    out = []
    for p in parts:
        n = p.shape[-1]
        out.append(jnp.pad(p, [(0, 0)] * len(lead) + [(0, _padded(n) - n)]))
    flat = jnp.concatenate(out, axis=-1)
    return flat.reshape(*lead, flat.shape[-1] // PACK_W, PACK_W)


def _unpack_rows(packed, sizes, lead=()):
    flat = packed.reshape(*lead, -1)
    out, off = [], 0
    for n in sizes:
        out.append(flat[..., off:off + n])
        off += _padded(n)
    return out


def _to_blocks(full, axis):
    shp = full.shape
    t = full.reshape(shp[:axis] + (N_DEV, shp[axis] // N_DEV) + shp[axis + 1:])
    return jnp.moveaxis(t, axis, 0)


def _from_blocks(blocks, axis):
    t = jnp.moveaxis(blocks, 0, axis)
    shp = t.shape
    return t.reshape(shp[:axis] + (shp[axis] * shp[axis + 1],) + shp[axis + 2:])


def _round_up(n, k):
    return -(-n // k) * k


def kernel(x, mix_norm, ffn_norm, hy_w_in, hy_w_gate2, hy_b_gate, hy_gla_norm, hy_sb_q_norm, hy_sb_k_norm, hy_w_out, cv_w_pw1, cv_b_pw1, cv_w_dw, cv_b_dw, cv_ln_g, cv_ln_b, cv_w_pw2, cv_b_pw2, ffn_w_gate, ffn_w_up, ffn_w_down, loss_target, m_mix_norm, m_ffn_norm, m_hy_w_in, m_hy_w_gate2, m_hy_b_gate, m_hy_gla_norm, m_hy_sb_q_norm, m_hy_sb_k_norm, m_hy_w_out, m_cv_w_pw1, m_cv_b_pw1, m_cv_w_dw, m_cv_b_dw, m_cv_ln_g, m_cv_ln_b, m_cv_w_pw2, m_cv_b_pw2, m_ffn_w_gate, m_ffn_w_up, m_ffn_w_down, v_mix_norm, v_ffn_norm, v_hy_w_in, v_hy_w_gate2, v_hy_b_gate, v_hy_gla_norm, v_hy_sb_q_norm, v_hy_sb_k_norm, v_hy_w_out, v_cv_w_pw1, v_cv_b_pw1, v_cv_w_dw, v_cv_b_dw, v_cv_ln_g, v_cv_ln_b, v_cv_w_pw2, v_cv_b_pw2, v_ffn_w_gate, v_ffn_w_up, v_ffn_w_down):
    w_loc = dict(zip(WEIGHT_NAMES, (mix_norm, ffn_norm, hy_w_in, hy_w_gate2, hy_b_gate, hy_gla_norm, hy_sb_q_norm, hy_sb_k_norm, hy_w_out, cv_w_pw1, cv_b_pw1, cv_w_dw, cv_b_dw, cv_ln_g, cv_ln_b, cv_w_pw2, cv_b_pw2, ffn_w_gate, ffn_w_up, ffn_w_down)))
    m_loc = dict(zip(WEIGHT_NAMES, (m_mix_norm, m_ffn_norm, m_hy_w_in, m_hy_w_gate2, m_hy_b_gate, m_hy_gla_norm, m_hy_sb_q_norm, m_hy_sb_k_norm, m_hy_w_out, m_cv_w_pw1, m_cv_b_pw1, m_cv_w_dw, m_cv_b_dw, m_cv_ln_g, m_cv_ln_b, m_cv_w_pw2, m_cv_b_pw2, m_ffn_w_gate, m_ffn_w_up, m_ffn_w_down)))
    v_loc = dict(zip(WEIGHT_NAMES, (v_mix_norm, v_ffn_norm, v_hy_w_in, v_hy_w_gate2, v_hy_b_gate, v_hy_gla_norm, v_hy_sb_q_norm, v_hy_sb_k_norm, v_hy_w_out, v_cv_w_pw1, v_cv_b_pw1, v_cv_w_dw, v_cv_b_dw, v_cv_ln_g, v_cv_ln_b, v_cv_w_pw2, v_cv_b_pw2, v_ffn_w_gate, v_ffn_w_up, v_ffn_w_down)))

    parts = [w_loc[n].astype(BF16).reshape(-1) for n in GATHER_BF16]
    parts += [lax.bitcast_convert_type(w_loc[n], BF16).reshape(-1) for n in GATHER_F32]
    gathered = _all_gather(_pack_rows(parts))
    sizes = [w_loc[n].size for n in GATHER_BF16] + [2 * w_loc[n].size for n in GATHER_F32]
    pieces = _unpack_rows(gathered, sizes, lead=(N_DEV,))
    W = {n: w_loc[n] for n in WEIGHT_NAMES if SHARD_AXIS[n] is None}
    for n, p in zip(GATHER_BF16 + GATHER_F32, pieces):
        shard_shape = w_loc[n].shape
        if n in GATHER_F32:
            p = lax.bitcast_convert_type(p.reshape(N_DEV, -1, 2), F32)
        W[n] = _from_blocks(p.reshape((N_DEV,) + shard_shape), SHARD_AXIS[n])

    sq_err, dx, G = _local_step(x[0], loss_target[0], W)
    loss = lax.psum(0.5 / D_MODEL * sq_err[0, 0], ("x", "y", "c"))

    send_parts, shard_sizes = [], []
    for n in WEIGHT_NAMES:
        g = G[n].reshape(W[n].shape)
        if SHARD_AXIS[n] is None:
            send_parts.append(jnp.broadcast_to(g.reshape(1, -1), (N_DEV, g.size)))
        else:
            send_parts.append(_to_blocks(g, SHARD_AXIS[n]).reshape(N_DEV, -1))
        shard_sizes.append(w_loc[n].size)
    tr = 256
    send = _pack_rows(send_parts, lead=(N_DEV,))
    rows = send.shape[1]
    rows_p = _round_up(rows, tr)
    send = jnp.pad(send, ((0, 0), (0, rows_p - rows), (0, 0)))
    recv = _scatter_exchange(send)

    def packed_local(d):
        p = _pack_rows([d[n].reshape(-1) for n in WEIGHT_NAMES])
        return jnp.pad(p, ((0, rows_p - rows), (0, 0)))

    g_p, d_p, m_p, v_p = _sum_adamw(recv, packed_local(w_loc), packed_local(m_loc), packed_local(v_loc), tr=tr)

    def unpacked(p):
        return [a.reshape(w_loc[n].shape) for n, a in zip(WEIGHT_NAMES, _unpack_rows(p, shard_sizes))]

    return (loss, dx[None], *unpacked(g_p), *unpacked(d_p), *unpacked(m_p), *unpacked(v_p))
```

```python
import jax
import jax.numpy as jnp
from jax import lax
from jax.experimental import pallas as pl
from jax.experimental.pallas import tpu as pltpu

F32 = jnp.float32
BF16 = jnp.bfloat16
CDT = jnp.bfloat16

D_MODEL = 1024
EPS = 1e-6
CHUNK = 64
GLA_HEADS = 4
GLA_HEAD_K = 64
GLA_HEAD_V = 128
GLA_DK = GLA_HEADS * GLA_HEAD_K
GLA_DV = GLA_HEADS * GLA_HEAD_V
GLA_GATE_RANK = 16
GLA_GATE_NORMALIZER = 16.0
SB_HEAD_DIM = 64
SB_D = 512
SB_TILE = 256
SB_PAIR = 128
SB_SPLIT_LK = 2
SB_SPLIT_G = 2
IN_WIDTH = 3088
IN_PAD = 3200
CONV_WIDTH = 31
CONV_HALO = 32
D_FF = 2816
N_DEV = 8

ADAM_LR = 0.001
ADAM_B1 = 0.9
ADAM_B2 = 0.999
ADAM_EPS = 1e-08
ADAM_WD = 0.01
ADAM_STEP = 10

VMEM_LIMIT = 56 * 1024 * 1024

WEIGHT_NAMES = ['mix_norm', 'ffn_norm', 'hy_w_in', 'hy_w_gate2', 'hy_b_gate', 'hy_gla_norm', 'hy_sb_q_norm',
                'hy_sb_k_norm', 'hy_w_out', 'cv_w_pw1', 'cv_b_pw1', 'cv_w_dw', 'cv_b_dw', 'cv_ln_g', 'cv_ln_b',
                'cv_w_pw2', 'cv_b_pw2', 'ffn_w_gate', 'ffn_w_up', 'ffn_w_down']
SHARD_AXIS = {'mix_norm': None, 'ffn_norm': None, 'hy_w_in': 2, 'hy_w_gate2': 2, 'hy_b_gate': None,
              'hy_gla_norm': None, 'hy_sb_q_norm': None, 'hy_sb_k_norm': None, 'hy_w_out': 1, 'cv_w_pw1': 2,
              'cv_b_pw1': 1, 'cv_w_dw': 2, 'cv_b_dw': 1, 'cv_ln_g': 1, 'cv_ln_b': 1, 'cv_w_pw2': 1, 'cv_b_pw2': 1,
              'ffn_w_gate': 2, 'ffn_w_up': 2, 'ffn_w_down': 1}


def _cparams(sem=None, vmem=VMEM_LIMIT):
    return pltpu.CompilerParams(dimension_semantics=sem, vmem_limit_bytes=vmem)


def _log_sigmoid(x):
    return jnp.minimum(x, 0.0) - jnp.log1p(jnp.exp(-jnp.abs(x)))


def _sigmoid(x):
    return 1.0 / (1.0 + jnp.exp(-x))


def _softplus(x):
    return jnp.maximum(x, 0.0) + jnp.log(1.0 + jnp.exp(-jnp.abs(x)))


def _split_bf16(x, n):
    parts = []
    rem = x
    for _ in range(n):
        p = rem.astype(BF16)
        parts.append(p)
        rem = rem - p.astype(F32)
    return parts


def _dot_exact_rhs(x, m, n):
    return sum(jnp.dot(p, m, preferred_element_type=F32) for p in _split_bf16(x, n))


def _dot_exact_lhs(m, x, n):
    return sum(jnp.dot(m, p, preferred_element_type=F32) for p in _split_bf16(x, n))


_NN = (((1,), (0,)), ((), ()))
_NT = (((1,), (1,)), ((), ()))
_TN = (((0,), (0,)), ((), ()))


def _dg(a, b, dn):
    return lax.dot_general(a.astype(CDT), b.astype(CDT), dn, preferred_element_type=F32)


def _matmul(a, b, *, mode, out_dtype, name, tm, tn, tk, bias=None, residual=None):
    if mode == 'nn':
        (M, K), (K2, N) = a.shape, b.shape
    elif mode == 'nt':
        (M, K), (N, K2) = a.shape, b.shape
    else:
        (K, M), (K2, N) = a.shape, b.shape
    assert K == K2 and M % tm == 0 and N % tn == 0 and K % tk == 0, (name, a.shape, b.shape, tm, tn, tk)
    nk = K // tk
    a_spec = pl.BlockSpec((tk, tm), lambda i, j, k: (k, i)) if mode == 'tn' else pl.BlockSpec((tm, tk), lambda i, j, k: (i, k))
    b_spec = pl.BlockSpec((tn, tk), lambda i, j, k: (j, k)) if mode == 'nt' else pl.BlockSpec((tk, tn), lambda i, j, k: (k, j))
    dn = {'nn': _NN, 'nt': _NT, 'tn': _TN}[mode]
    has_bias, has_res = bias is not None, residual is not None

    def body(*refs):
        a_ref, b_ref = refs[0], refs[1]
        pos = 2
        bias_ref = res_ref = None
        if has_bias:
            bias_ref = refs[pos]
            pos += 1
        if has_res:
            res_ref = refs[pos]
            pos += 1
        o_ref = refs[pos]
        acc_ref = refs[pos + 1] if nk > 1 else None
        p = _dg(a_ref[...], b_ref[...], dn)

        def finish(acc):
            if has_bias:
                acc = acc + bias_ref[...]
            if has_res:
                acc = res_ref[...] + acc
            o_ref[...] = acc.astype(o_ref.dtype)

        if nk == 1:
            finish(p)
        else:
            k = pl.program_id(2)

            @pl.when(k == 0)
            def _():
                acc_ref[...] = p

            @pl.when(k > 0)
            def _():
                acc_ref[...] += p

            @pl.when(k == nk - 1)
            def _():
                finish(acc_ref[...])

    in_specs = [a_spec, b_spec]
    args = [a, b]
    if has_bias:
        in_specs.append(pl.BlockSpec((1, tn), lambda i, j, k: (0, j)))
        args.append(bias)
    if has_res:
        in_specs.append(pl.BlockSpec((tm, tn), lambda i, j, k: (i, j)))
        args.append(residual)
    return pl.pallas_call(
        body, name=name, grid=(M // tm, N // tn, nk),
        in_specs=in_specs, out_specs=pl.BlockSpec((tm, tn), lambda i, j, k: (i, j)),
        out_shape=jax.ShapeDtypeStruct((M, N), out_dtype),
        scratch_shapes=[pltpu.VMEM((tm, tn), F32)] if nk > 1 else [],
        compiler_params=_cparams(("parallel", "parallel", "arbitrary")),
    )(*args)


def _rms_fwd(x, g, *, name, tm=512):
    T, Dm = x.shape

    def body(x_ref, g_ref, o_ref):
        xv = x_ref[...]
        r = lax.rsqrt(jnp.mean(xv * xv, axis=-1, keepdims=True) + EPS)
        o_ref[...] = (xv * r * g_ref[...]).astype(o_ref.dtype)

    return pl.pallas_call(
        body, name=name, grid=(T // tm,),
        in_specs=[pl.BlockSpec((tm, Dm), lambda i: (i, 0)), pl.BlockSpec((1, Dm), lambda i: (0, 0))],
        out_specs=pl.BlockSpec((tm, Dm), lambda i: (i, 0)),
        out_shape=jax.ShapeDtypeStruct((T, Dm), CDT),
        compiler_params=_cparams(("parallel",)),
    )(x, g)


def _rms_bwd(dy, x, g, resid, *, name, tm=512):
    T, Dm = x.shape

    def body(dy_ref, x_ref, g_ref, res_ref, dx_ref, dxb_ref, dg_ref):
        i = pl.program_id(0)
        xv, dyv = x_ref[...], dy_ref[...]
        r = lax.rsqrt(jnp.mean(xv * xv, axis=-1, keepdims=True) + EPS)
        u = dyv * g_ref[...]
        dot = jnp.mean(u * xv, axis=-1, keepdims=True)
        dx = res_ref[...] + (r * u - xv * (r * r * r * dot))
        dx_ref[...] = dx
        dxb_ref[...] = dx.astype(dxb_ref.dtype)
        part = jnp.sum(dyv * xv * r, axis=0, keepdims=True)

        @pl.when(i == 0)
        def _():
            dg_ref[...] = part

        @pl.when(i > 0)
        def _():
            dg_ref[...] += part

    row = pl.BlockSpec((tm, Dm), lambda i: (i, 0))
    vec = pl.BlockSpec((1, Dm), lambda i: (0, 0))
    return pl.pallas_call(
        body, name=name, grid=(T // tm,),
        in_specs=[row, row, vec, row], out_specs=[row, row, vec],
        out_shape=[jax.ShapeDtypeStruct((T, Dm), F32), jax.ShapeDtypeStruct((T, Dm), CDT),
                   jax.ShapeDtypeStruct((1, Dm), F32)],
        compiler_params=_cparams(("arbitrary",)),
    )(dy, x, g, resid)


def _loss_head(y, tgt, *, tm=512):
    T, Dm = y.shape

    def body(y_ref, t_ref, s_ref, dy_ref, dyb_ref):
        i = pl.program_id(0)
        e = y_ref[...] - t_ref[...]
        dy = e * (1.0 / Dm)
        dy_ref[...] = dy
        dyb_ref[...] = dy.astype(dyb_ref.dtype)
        part = jnp.sum(jnp.sum(e * e, axis=1, keepdims=True), axis=0, keepdims=True)

        @pl.when(i == 0)
        def _():
            s_ref[...] = part

        @pl.when(i > 0)
        def _():
            s_ref[...] += part

    row = pl.BlockSpec((tm, Dm), lambda i: (i, 0))
    return pl.pallas_call(
        body, name="loss_head", grid=(T // tm,),
        in_specs=[row, row], out_specs=[pl.BlockSpec((1, 1), lambda i: (0, 0)), row, row],
        out_shape=[jax.ShapeDtypeStruct((1, 1), F32), jax.ShapeDtypeStruct((T, Dm), F32),
                   jax.ShapeDtypeStruct((T, Dm), CDT)],
        compiler_params=_cparams(("arbitrary",)),
    )(y, tgt)


def _swiglu_fwd(ab, *, name, tm=256):
    T, F2 = ab.shape
    F = F2 // 2

    def body(ab_ref, s_ref):
        a = ab_ref[:, :F]
        b = ab_ref[:, F:]
        s_ref[...] = (a * _sigmoid(a) * b).astype(s_ref.dtype)

    return pl.pallas_call(
        body, name=name, grid=(T // tm,),
        in_specs=[pl.BlockSpec((tm, F2), lambda i: (i, 0))], out_specs=pl.BlockSpec((tm, F), lambda i: (i, 0)),
        out_shape=jax.ShapeDtypeStruct((T, F), CDT),
        compiler_params=_cparams(("parallel",)),
    )(ab)


def _swiglu_bwd(ds, ab, *, name, tm=256):
    T, F2 = ab.shape
    F = F2 // 2

    def body(ds_ref, ab_ref, o_ref):
        a = ab_ref[:, :F]
        b = ab_ref[:, F:]
        dsv = ds_ref[...]
        sg = _sigmoid(a)
        o_ref[:, :F] = (dsv * b * (sg * (1.0 + a * (1.0 - sg)))).astype(o_ref.dtype)
        o_ref[:, F:] = (dsv * (a * sg)).astype(o_ref.dtype)

    return pl.pallas_call(
        body, name=name, grid=(T // tm,),
        in_specs=[pl.BlockSpec((tm, F), lambda i: (i, 0)), pl.BlockSpec((tm, F2), lambda i: (i, 0))],
        out_specs=pl.BlockSpec((tm, F2), lambda i: (i, 0)),
        out_shape=jax.ShapeDtypeStruct((T, F2), CDT),
        compiler_params=_cparams(("parallel",)),
    )(ds, ab)


def _conv_fwd(a, w_dw, b_dw, ln_g, ln_b, *, tm=256):
    T = a.shape[0]
    Dm = D_MODEL

    def body(a_ref, w_ref, bdw_ref, g_ref, b_ref, s_ref, u_ref, c_ref, ubuf):
        i = pl.program_id(0)

        @pl.when(i == 0)
        def _():
            ubuf[0:CONV_HALO, :] = jnp.zeros((CONV_HALO, Dm), F32)

        @pl.when(i > 0)
        def _():
            ubuf[0:CONV_HALO, :] = ubuf[tm:tm + CONV_HALO, :]

        u = a_ref[:, :Dm] * _sigmoid(a_ref[:, Dm:])
        ubuf[CONV_HALO:CONV_HALO + tm, :] = u
        u_ref[...] = u
        acc = jnp.zeros((tm, Dm), F32) + bdw_ref[...]
        for k in range(CONV_WIDTH):
            acc = acc + w_ref[k:k + 1, :] * ubuf[pl.ds(CONV_HALO - (CONV_WIDTH - 1) + k, tm), :]
        c_ref[...] = acc
        mu = jnp.mean(acc, axis=-1, keepdims=True)
        cen = acc - mu
        var = jnp.mean(cen * cen, axis=-1, keepdims=True)
        l = cen * lax.rsqrt(var + EPS) * g_ref[...] + b_ref[...]
        s_ref[...] = (l * _sigmoid(l)).astype(s_ref.dtype)

    row = pl.BlockSpec((tm, Dm), lambda i: (i, 0))
    vec = pl.BlockSpec((1, Dm), lambda i: (0, 0))
    return pl.pallas_call(
        body, name="conv_fwd", grid=(T // tm,),
        in_specs=[pl.BlockSpec((tm, 2 * Dm), lambda i: (i, 0)), pl.BlockSpec((CONV_HALO, Dm), lambda i: (0, 0)), vec, vec, vec],
        out_specs=[row, row, row],
        out_shape=[jax.ShapeDtypeStruct((T, Dm), CDT), jax.ShapeDtypeStruct((T, Dm), F32), jax.ShapeDtypeStruct((T, Dm), F32)],
        scratch_shapes=[pltpu.VMEM((tm + CONV_HALO, Dm), F32)],
        compiler_params=_cparams(("arbitrary",)),
    )(a, w_dw, b_dw, ln_g, ln_b)


def _conv_bwd(ds, c, u, a, w_dw, ln_g, ln_b, *, tm=256):
    T = a.shape[0]
    Dm = D_MODEL
    nt = T // tm

    def body(ds_ref, c_ref, u_ref, a_ref, w_ref, g_ref, b_ref,
             da_ref, db1_ref, dw_ref, dbdw_ref, dg_ref, dbln_ref, dcbuf):
        i = pl.program_id(0)

        @pl.when(i == 0)
        def _():
            dcbuf[tm:tm + CONV_HALO, :] = jnp.zeros((CONV_HALO, Dm), F32)
            db1_ref[...] = jnp.zeros_like(db1_ref)
            dw_ref[...] = jnp.zeros_like(dw_ref)
            dbdw_ref[...] = jnp.zeros_like(dbdw_ref)
            dg_ref[...] = jnp.zeros_like(dg_ref)
            dbln_ref[...] = jnp.zeros_like(dbln_ref)

        @pl.when(i > 0)
        def _():
            dcbuf[tm:tm + CONV_HALO, :] = dcbuf[0:CONV_HALO, :]

        cv = c_ref[...]
        mu = jnp.mean(cv, axis=-1, keepdims=True)
        cen = cv - mu
        var = jnp.mean(cen * cen, axis=-1, keepdims=True)
        rstd = lax.rsqrt(var + EPS)
        n = cen * rstd
        l = n * g_ref[...] + b_ref[...]
        sg = _sigmoid(l)
        dl = ds_ref[...] * (sg * (1.0 + l * (1.0 - sg)))
        dg_ref[...] += jnp.sum(dl * n, axis=0, keepdims=True)
        dbln_ref[...] += jnp.sum(dl, axis=0, keepdims=True)
        dn = dl * g_ref[...]
        dc = rstd * (dn - jnp.mean(dn, axis=-1, keepdims=True) - n * jnp.mean(dn * n, axis=-1, keepdims=True))
        dbdw_ref[...] += jnp.sum(dc, axis=0, keepdims=True)
        dcbuf[0:tm, :] = dc
        uv = u_ref[...]
        du = jnp.zeros((tm, Dm), F32)
        for k in range(CONV_WIDTH):
            slab = dcbuf[pl.ds(CONV_WIDTH - 1 - k, tm), :]
            du = du + w_ref[k:k + 1, :] * slab
            dw_ref[k:k + 1, :] += jnp.sum(slab * uv, axis=0, keepdims=True)
        a1 = a_ref[:, :Dm]
        s2 = _sigmoid(a_ref[:, Dm:])
        da1 = du * s2
        da2 = du * a1 * (s2 * (1.0 - s2))
        da_ref[:, :Dm] = da1.astype(da_ref.dtype)
        da_ref[:, Dm:] = da2.astype(da_ref.dtype)
        db1_ref[:, :Dm] += jnp.sum(da1, axis=0, keepdims=True)
        db1_ref[:, Dm:] += jnp.sum(da2, axis=0, keepdims=True)

    rev = lambda i: (nt - 1 - i, 0)
    row = pl.BlockSpec((tm, Dm), rev)
    row2 = pl.BlockSpec((tm, 2 * Dm), rev)
    vec = pl.BlockSpec((1, Dm), lambda i: (0, 0))
    vec2 = pl.BlockSpec((1, 2 * Dm), lambda i: (0, 0))
    taps = pl.BlockSpec((CONV_HALO, Dm), lambda i: (0, 0))
    return pl.pallas_call(
        body, name="conv_bwd", grid=(nt,),
        in_specs=[row, row, row, row2, taps, vec, vec],
        out_specs=[row2, vec2, taps, vec, vec, vec],
        out_shape=[jax.ShapeDtypeStruct((T, 2 * Dm), CDT), jax.ShapeDtypeStruct((1, 2 * Dm), F32),
                   jax.ShapeDtypeStruct((CONV_HALO, Dm), F32), jax.ShapeDtypeStruct((1, Dm), F32),
                   jax.ShapeDtypeStruct((1, Dm), F32), jax.ShapeDtypeStruct((1, Dm), F32)],
        scratch_shapes=[pltpu.VMEM((tm + CONV_HALO, Dm), F32)],
        compiler_params=_cparams(("arbitrary",)),
    )(ds, c, u, a, w_dw, ln_g, ln_b)


def _gla_head_masks(width, per_head):
    lane = lax.broadcasted_iota(jnp.int32, (1, width), 1)
    return [((lane >= h * per_head) & (lane < (h + 1) * per_head)).astype(F32) for h in range(GLA_HEADS)]


def _gla_specs(tm, order):
    return [pl.BlockSpec((tm, GLA_DK), lambda i: (order(i), 0)),
            pl.BlockSpec((tm, GLA_DK), lambda i: (order(i), 1)),
            pl.BlockSpec((tm, GLA_DV), lambda i: (order(i), 1)),
            pl.BlockSpec((tm, GLA_DV), lambda i: (order(i), 2)),
            pl.BlockSpec((tm, 128), lambda i: (order(i), 3072 // 128))]


def _gla_chunk_decay(la_c, tri):
    bc = _dot_exact_lhs(tri, la_c, 3)
    b_end = bc[CHUNK - 1:CHUNK, :]
    return b_end, jnp.exp(b_end - bc)


def _gla_fwd(proj, wg2p, b_gate, g_gla, *, tm=256):
    T = proj.shape[0]
    ncs = tm // CHUNK
    scale = GLA_HEAD_K ** -0.5

    def body(q_ref, k_ref, v_ref, r_ref, glr_ref, wg_ref, bg_ref, gg_ref, o_ref, oraw_ref, st_ref, s_scr):
        i = pl.program_id(0)

        @pl.when(i == 0)
        def _():
            s_scr[...] = jnp.zeros_like(s_scr)

        mk = _gla_head_masks(GLA_DK, GLA_HEAD_K)
        rr = lax.broadcasted_iota(jnp.int32, (CHUNK, CHUNK), 0)
        cc = lax.broadcasted_iota(jnp.int32, (CHUNK, CHUNK), 1)
        tri = (cc <= rr).astype(BF16)
        y = _dg(glr_ref[...], wg_ref[...], _NN) + bg_ref[...]
        la = _log_sigmoid(y) / GLA_GATE_NORMALIZER
        qs = q_ref[...] * scale
        for ci in range(ncs):
            rows = slice(ci * CHUNK, (ci + 1) * CHUNK)
            b_end, dec = _gla_chunk_decay(la[rows], tri)
            kend = (k_ref[rows, :] * dec).astype(CDT)
            upd = jnp.zeros((GLA_HEAD_V, GLA_DK), F32)
            for h in range(GLA_HEADS):
                vh = v_ref[rows, h * GLA_HEAD_V:(h + 1) * GLA_HEAD_V]
                upd = upd + mk[h] * _dg(vh, kend, _TN)
            s_new = jnp.exp(b_end) * s_scr[...] + upd
            s_scr[...] = s_new
            st_ref[ci] = s_new
            s_c = s_new.astype(CDT)
            for h in range(GLA_HEADS):
                o_h = _dg(qs[rows] * mk[h], s_c, _NT)
                oraw_ref[rows, h * GLA_HEAD_V:(h + 1) * GLA_HEAD_V] = o_h
        for h in range(GLA_HEADS):
            cols = slice(h * GLA_HEAD_V, (h + 1) * GLA_HEAD_V)
            o_h = oraw_ref[:, cols]
            rs = lax.rsqrt(jnp.mean(o_h * o_h, axis=-1, keepdims=True) + EPS)
            rg = r_ref[:, cols]
            o_ref[:, cols] = (o_h * rs * gg_ref[...] * (rg * _sigmoid(rg))).astype(o_ref.dtype)

    full = lambda shape: pl.BlockSpec(shape, lambda i: tuple(0 for _ in shape))
    return pl.pallas_call(
        body, name="gla_fwd", grid=(T // tm,),
        in_specs=_gla_specs(tm, lambda i: i) + [full((128, GLA_DK)), full((1, GLA_DK)), full((1, GLA_HEAD_V))],
        out_specs=[pl.BlockSpec((tm, GLA_DV), lambda i: (i, 0)), pl.BlockSpec((tm, GLA_DV), lambda i: (i, 0)),
                   pl.BlockSpec((ncs, GLA_HEAD_V, GLA_DK), lambda i: (i, 0, 0))],
        out_shape=[jax.ShapeDtypeStruct((T, GLA_DV), CDT), jax.ShapeDtypeStruct((T, GLA_DV), F32),
                   jax.ShapeDtypeStruct((T // CHUNK, GLA_HEAD_V, GLA_DK), F32)],
        scratch_shapes=[pltpu.VMEM((GLA_HEAD_V, GLA_DK), F32)],
        compiler_params=_cparams(("arbitrary",)),
    )(proj, proj, proj, proj, proj, wg2p, b_gate, g_gla)


def _gla_bwd(d_o, proj, oraw, states, wg2p, b_gate, g_gla, *, tm=256):
    T = proj.shape[0]
    nt = T // tm
    ncs = tm // CHUNK
    scale = GLA_HEAD_K ** -0.5

    def body(do_ref, q_ref, k_ref, v_ref, r_ref, glr_ref, oraw_ref, st_ref, stp_ref, wg_ref, bg_ref, gg_ref,
             dgla_ref, dglr_ref, dwg_ref, dbg_ref, dgg_ref, ds_scr, dy_scr, dor_scr):
        i = pl.program_id(0)
        tile = nt - 1 - i

        @pl.when(i == 0)
        def _():
            ds_scr[...] = jnp.zeros_like(ds_scr)
            dwg_ref[...] = jnp.zeros_like(dwg_ref)
            dbg_ref[...] = jnp.zeros_like(dbg_ref)
            dgg_ref[...] = jnp.zeros_like(dgg_ref)

        mk = _gla_head_masks(GLA_DK, GLA_HEAD_K)
        rr = lax.broadcasted_iota(jnp.int32, (CHUNK, CHUNK), 0)
        cc = lax.broadcasted_iota(jnp.int32, (CHUNK, CHUNK), 1)
        tri = (cc <= rr).astype(BF16)
        tri_t = (cc >= rr).astype(BF16)
        last_row = (lax.broadcasted_iota(jnp.int32, (CHUNK, 1), 0) == CHUNK - 1).astype(F32)

        dgg = jnp.zeros((1, GLA_HEAD_V), F32)
        for h in range(GLA_HEADS):
            cols = slice(h * GLA_HEAD_V, (h + 1) * GLA_HEAD_V)
            o_h = oraw_ref[:, cols]
            rs = lax.rsqrt(jnp.mean(o_h * o_h, axis=-1, keepdims=True) + EPS)
            rg = r_ref[:, cols]
            sg = _sigmoid(rg)
            dov = do_ref[:, cols]
            on = o_h * rs * gg_ref[...]
            d_on = dov * (rg * sg)
            dgla_ref[:, 2 * GLA_DK + GLA_DV + h * GLA_HEAD_V:2 * GLA_DK + GLA_DV + (h + 1) * GLA_HEAD_V] = (
                dov * on * (sg * (1.0 + rg * (1.0 - sg)))).astype(dgla_ref.dtype)
            dgg = dgg + jnp.sum(d_on * o_h * rs, axis=0, keepdims=True)
            uu = d_on * gg_ref[...]
            dor_scr[:, cols] = rs * uu - o_h * (rs * rs * rs * jnp.mean(uu * o_h, axis=-1, keepdims=True))
        dgg_ref[...] += dgg

        y = _dg(glr_ref[...], wg_ref[...], _NN) + bg_ref[...]
        la = _log_sigmoid(y) / GLA_GATE_NORMALIZER
        qs = q_ref[...] * scale
        for ci in reversed(range(ncs)):
            rows = slice(ci * CHUNK, (ci + 1) * CHUNK)
            b_end, dec = _gla_chunk_decay(la[rows], tri)
            decay = jnp.exp(b_end)
            kend = k_ref[rows, :] * dec
            kend_c = kend.astype(CDT)
            s_c = st_ref[ci].astype(CDT)
            if ci > 0:
                s_prev = st_ref[ci - 1]
            else:
                s_prev = jnp.where(tile > 0, stp_ref[0], 0.0)
            dqs = jnp.zeros((CHUNK, GLA_DK), F32)
            dst = ds_scr[...]
            for h in range(GLA_HEADS):
                do_h = dor_scr[rows, h * GLA_HEAD_V:(h + 1) * GLA_HEAD_V].astype(CDT)
                dqs = dqs + mk[h] * _dg(do_h, s_c, _NN)
                dst = dst + mk[h] * _dg(do_h, qs[rows], _TN)
            d_decay = jnp.sum(dst * s_prev, axis=0, keepdims=True)
            ds_scr[...] = decay * dst
            dst_c = dst.astype(CDT)
            dkend = jnp.zeros((CHUNK, GLA_DK), F32)
            for h in range(GLA_HEADS):
                cols = slice(h * GLA_HEAD_V, (h + 1) * GLA_HEAD_V)
                dv_h = _dg(kend * mk[h], dst_c, _NT)
                dgla_ref[rows, 2 * GLA_DK + h * GLA_HEAD_V:2 * GLA_DK + (h + 1) * GLA_HEAD_V] = dv_h.astype(dgla_ref.dtype)
                dkend = dkend + mk[h] * _dg(v_ref[rows, cols], dst_c, _NN)
            dgla_ref[rows, 0:GLA_DK] = (dqs * scale).astype(dgla_ref.dtype)
            dgla_ref[rows, GLA_DK:2 * GLA_DK] = (dkend * dec).astype(dgla_ref.dtype)
            mm = dkend * kend
            db_end = jnp.sum(mm, axis=0, keepdims=True) + d_decay * decay
            dbc = last_row * db_end - mm
            dla = _dot_exact_lhs(tri_t, dbc, 3)
            dy_scr[rows, :] = dla * (1.0 / GLA_GATE_NORMALIZER) * _sigmoid(-y[rows])
        dy = dy_scr[...]
        dbg_ref[...] += jnp.sum(dy, axis=0, keepdims=True)
        dwg_ref[...] += _dg(glr_ref[...], dy, _TN)
        dglr_ref[...] = _dg(dy, wg_ref[...], _NT).astype(dglr_ref.dtype)

    rev = lambda i: nt - 1 - i
    full = lambda shape: pl.BlockSpec(shape, lambda i: tuple(0 for _ in shape))
    st_spec = pl.BlockSpec((ncs, GLA_HEAD_V, GLA_DK), lambda i: (rev(i), 0, 0))
    stp_spec = pl.BlockSpec((1, GLA_HEAD_V, GLA_DK), lambda i: (jnp.maximum(rev(i) * ncs - 1, 0), 0, 0))
    return pl.pallas_call(
        body, name="gla_bwd", grid=(nt,),
        in_specs=[pl.BlockSpec((tm, GLA_DV), lambda i: (rev(i), 0))] + _gla_specs(tm, rev)
        + [pl.BlockSpec((tm, GLA_DV), lambda i: (rev(i), 0)), st_spec, stp_spec,
           full((128, GLA_DK)), full((1, GLA_DK)), full((1, GLA_HEAD_V))],
        out_specs=[pl.BlockSpec((tm, 2 * GLA_DK + 2 * GLA_DV), lambda i: (rev(i), 0)),
                   pl.BlockSpec((tm, 128), lambda i: (rev(i), 0)),
                   full((128, GLA_DK)), full((1, GLA_DK)), full((1, GLA_HEAD_V))],
        out_shape=[jax.ShapeDtypeStruct((T, 2 * GLA_DK + 2 * GLA_DV), CDT), jax.ShapeDtypeStruct((T, 128), CDT),
                   jax.ShapeDtypeStruct((128, GLA_DK), F32), jax.ShapeDtypeStruct((1, GLA_DK), F32),
                   jax.ShapeDtypeStruct((1, GLA_HEAD_V), F32)],
        scratch_shapes=[pltpu.VMEM((GLA_HEAD_V, GLA_DK), F32), pltpu.VMEM((tm, GLA_DK), F32),
                        pltpu.VMEM((tm, GLA_DV), F32)],
        compiler_params=_cparams(("arbitrary",)),
    )(d_o, proj, proj, proj, proj, proj, oraw, states, states, wg2p, b_gate, g_gla)


def _head_mean_matrix():
    r = lax.broadcasted_iota(jnp.int32, (SB_D, SB_D), 0) // SB_HEAD_DIM
    c = lax.broadcasted_iota(jnp.int32, (SB_D, SB_D), 1) // SB_HEAD_DIM
    return jnp.where(r == c, 1.0 / SB_HEAD_DIM, 0.0).astype(BF16)


def _sb_prep(proj, gq, gk, *, tm=256):
    T = proj.shape[0]
    scale = SB_HEAD_DIM ** -0.5

    def body(q_ref, k_ref, v_ref, gq_ref, gk_ref, qn_ref, kn_ref, vb_ref):
        hm = _head_mean_matrix()
        qv, kv = q_ref[...], k_ref[...]
        rq = lax.rsqrt(_dot_exact_rhs(qv * qv, hm, 3) + EPS)
        rk = lax.rsqrt(_dot_exact_rhs(kv * kv, hm, 3) + EPS)
        qn_ref[...] = (qv * rq * gq_ref[...] * scale).astype(qn_ref.dtype)
        kn_ref[...] = (kv * rk * gk_ref[...]).astype(kn_ref.dtype)
        vb_ref[...] = v_ref[...].astype(vb_ref.dtype)

    col = lambda j: pl.BlockSpec((tm, SB_D), lambda i: (i, j))
    vec = pl.BlockSpec((1, SB_D), lambda i: (0, 0))
    out = pl.BlockSpec((tm, SB_D), lambda i: (i, 0))
    return pl.pallas_call(
        body, name="sb_prep", grid=(T // tm,),
        in_specs=[col(3), col(4), col(5), vec, vec], out_specs=[out, out, out],
        out_shape=[jax.ShapeDtypeStruct((T, SB_D), CDT)] * 3,
        compiler_params=_cparams(("parallel",)),
    )(proj, proj, proj, gq, gk)


def _sb_prep_bwd(dqn, dkn, dv, proj, gq, gk, *, tm=256):
    T = proj.shape[0]
    scale = SB_HEAD_DIM ** -0.5

    def body(dqn_ref, dkn_ref, dv_ref, q_ref, k_ref, gq_ref, gk_ref, dsb_ref, dgq_ref, dgk_ref):
        i = pl.program_id(0)

        @pl.when(i == 0)
        def _():
            dgq_ref[...] = jnp.zeros_like(dgq_ref)
            dgk_ref[...] = jnp.zeros_like(dgk_ref)

        hm = _head_mean_matrix()

        def one(dn_ref, x_ref, g_ref, dg_ref, sc, lo):
            xv = x_ref[...]
            dnv = dn_ref[...] * sc
            r = lax.rsqrt(_dot_exact_rhs(xv * xv, hm, 3) + EPS)
            u = dnv * g_ref[...]
            dot = _dot_exact_rhs(u * xv, hm, 3)
            dsb_ref[:, lo:lo + SB_D] = (r * u - xv * (r * r * r * dot)).astype(dsb_ref.dtype)
            dg_ref[...] += jnp.sum(dnv * xv * r, axis=0, keepdims=True)

        one(dqn_ref, q_ref, gq_ref, dgq_ref, scale, 0)
        one(dkn_ref, k_ref, gk_ref, dgk_ref, 1.0, SB_D)
        dsb_ref[:, 2 * SB_D:3 * SB_D] = dv_ref[...].astype(dsb_ref.dtype)

    col = lambda j: pl.BlockSpec((tm, SB_D), lambda i: (i, j))
    vec = pl.BlockSpec((1, SB_D), lambda i: (0, 0))
    row = pl.BlockSpec((tm, SB_D), lambda i: (i, 0))
    return pl.pallas_call(
        body, name="sb_prep_bwd", grid=(T // tm,),
        in_specs=[row, row, row, col(3), col(4), vec, vec],
        out_specs=[pl.BlockSpec((tm, 3 * SB_D), lambda i: (i, 0)), vec, vec],
        out_shape=[jax.ShapeDtypeStruct((T, 3 * SB_D), CDT), jax.ShapeDtypeStruct((1, SB_D), F32),
                   jax.ShapeDtypeStruct((1, SB_D), F32)],
        compiler_params=_cparams(("arbitrary",)),
    )(dqn, dkn, dv, proj, proj, gq, gk)


def _sb_masks():
    lane = lax.broadcasted_iota(jnp.int32, (1, 128), 1)
    m = [lane < SB_HEAD_DIM, lane >= SB_HEAD_DIM]
    return m, [x.astype(F32) for x in m]


def _sb_fwd(qn, kn, vb):
    T = qn.shape[0]
    nq = T // SB_TILE
    B, P = SB_TILE, SB_PAIR

    def body(q_ref, k_ref, v_ref, o_ref, l_ref, acc_ref):
        qb = pl.program_id(1)
        m, mf = _sb_masks()
        row = lax.broadcasted_iota(jnp.int32, (B, B), 0)
        col = lax.broadcasted_iota(jnp.int32, (B, B), 1)
        later = (row > col).astype(BF16)
        past = col < row
        q2 = q_ref[...]
        qm = [jnp.where(m[h], q2, jnp.zeros_like(q2)) for h in range(2)]
        acc_ref[...] = jnp.zeros_like(acc_ref)

        def tile(kb, R, diag):
            koff = pl.multiple_of(kb * B, B)
            k2 = k_ref[pl.ds(koff, B), :]
            v2 = v_ref[pl.ds(koff, B), :]
            hs = range(2)
            z = [_dg(qm[h], k2, _NT) for h in hs]
            sp = [_softplus(z[h]) for h in hs]
            lk = [jnp.where(past, sp[h], 0.0) if diag else sp[h] for h in hs]
            cum = [_dot_exact_rhs(lk[h], later, SB_SPLIT_LK) for h in hs]
            w = [jnp.exp((z[h] - sp[h]) - (cum[h] + R[h])) for h in hs]
            if diag:
                w = [jnp.where(past, w[h], 0.0) for h in hs]
            for h in hs:
                acc_ref[h] += _dg(w[h], v2, _NN)
            return tuple(R[h] + jnp.sum(lk[h], axis=1, keepdims=True) for h in hs)

        zero = jnp.zeros((B, 1), F32)
        R = tile(qb, (zero, zero), True)
        R = lax.fori_loop(0, qb, lambda i, c: tile(qb - 1 - i, c, False), R)
        o_ref[...] = (acc_ref[0] * mf[0] + acc_ref[1] * mf[1]).astype(o_ref.dtype)
        l_ref[0] = R[0] * mf[0] + R[1] * mf[1]

    slab = pl.BlockSpec((T, P), lambda hp, qb: (0, hp))
    blk = pl.BlockSpec((B, P), lambda hp, qb: (qb, hp))
    return pl.pallas_call(
        body, name="sb_fwd", grid=(SB_D // P, nq),
        in_specs=[blk, slab, slab],
        out_specs=[blk, pl.BlockSpec((1, B, P), lambda hp, qb: (hp, qb, 0))],
        out_shape=[jax.ShapeDtypeStruct((T, SB_D), CDT), jax.ShapeDtypeStruct((SB_D // P, T, P), F32)],
        scratch_shapes=[pltpu.VMEM((2, B, P), F32)],
        compiler_params=_cparams(("arbitrary", "arbitrary")),
    )(qn, kn, vb)


def _sb_bwd(d_o, qn, kn, vb, lsum):
    T = qn.shape[0]
    nq = T // SB_TILE
    B, P = SB_TILE, SB_PAIR

    def body(do_ref, q_ref, k_ref, v_ref, l_ref, dq_ref, dk_ref, dv_ref, dqacc_ref):
        qb = pl.program_id(1)

        @pl.when(qb == 0)
        def _():
            dk_ref[...] = jnp.zeros_like(dk_ref)
            dv_ref[...] = jnp.zeros_like(dv_ref)

        m, mf = _sb_masks()
        row = lax.broadcasted_iota(jnp.int32, (B, B), 0)
        col = lax.broadcasted_iota(jnp.int32, (B, B), 1)
        upto = (row <= col).astype(BF16)
        before = (row < col).astype(BF16)
        past = col < row
        q2 = q_ref[...]
        qm = [jnp.where(m[h], q2, jnp.zeros_like(q2)) for h in range(2)]
        do2 = do_ref[...]
        dom = [jnp.where(m[h], do2, 0.0).astype(CDT) for h in range(2)]
        lb = l_ref[0]
        ltot = [lb[:, 0:1], lb[:, SB_HEAD_DIM:SB_HEAD_DIM + 1]]
        dqacc_ref[...] = jnp.zeros_like(dqacc_ref)

        def tile(kb, carry, diag):
            Ps, Pg = carry
            koff = pl.multiple_of(kb * B, B)
            k2 = k_ref[pl.ds(koff, B), :]
            v2 = v_ref[pl.ds(koff, B), :]
            hs = range(2)
            z = [_dg(qm[h], k2, _NT) for h in hs]
            dw = [_dg(dom[h], v2, _NT) for h in hs]
            sp = [_softplus(z[h]) for h in hs]
            lk = [jnp.where(past, sp[h], 0.0) if diag else sp[h] for h in hs]
            cum = [_dot_exact_rhs(lk[h], upto, SB_SPLIT_LK) for h in hs]
            w = [jnp.exp((z[h] - sp[h]) - ((ltot[h] - Ps[h]) - cum[h])) for h in hs]
            if diag:
                w = [jnp.where(past, w[h], 0.0) for h in hs]
            g = [w[h] * dw[h] for h in hs]
            gx = [_dot_exact_rhs(g[h], before, SB_SPLIT_G) + Pg[h] for h in hs]
            sneg = [jnp.exp(-sp[h]) for h in hs]
            dz = [g[h] * sneg[h] - (1.0 - sneg[h]) * gx[h] for h in hs]
            if diag:
                dz = [jnp.where(past, dz[h], 0.0) for h in hs]
            dz_c = [dz[h].astype(CDT) for h in hs]
            for h in hs:
                dv_ref[pl.ds(koff, B), :] += _dg(w[h], dom[h], _TN)
                dk_ref[pl.ds(koff, B), :] += _dg(dz_c[h], qm[h], _TN)
                dqacc_ref[h] += _dg(dz_c[h], k2, _NN)
            return (tuple(Ps[h] + jnp.sum(lk[h], axis=1, keepdims=True) for h in hs),
                    tuple(Pg[h] + jnp.sum(g[h], axis=1, keepdims=True) for h in hs))

        zero = jnp.zeros((B, 1), F32)
        carry = lax.fori_loop(0, qb, lambda i, c: tile(i, c, False), ((zero, zero), (zero, zero)))
        tile(qb, carry, True)
        dq_ref[...] = dqacc_ref[0] * mf[0] + dqacc_ref[1] * mf[1]

    slab = pl.BlockSpec((T, P), lambda hp, qb: (0, hp))
    blk = pl.BlockSpec((B, P), lambda hp, qb: (qb, hp))
    return pl.pallas_call(
        body, name="sb_bwd", grid=(SB_D // P, nq),
        in_specs=[blk, blk, slab, slab, pl.BlockSpec((1, B, P), lambda hp, qb: (hp, qb, 0))],
        out_specs=[blk, slab, slab],
        out_shape=[jax.ShapeDtypeStruct((T, SB_D), F32)] * 3,
        scratch_shapes=[pltpu.VMEM((2, B, P), F32)],
        compiler_params=_cparams(("arbitrary", "arbitrary")),
    )(d_o, qn, kn, vb, lsum)


def _regroup_in_rows(wt):
    cut = 2 * GLA_DK + 2 * GLA_DV
    pad = jnp.zeros((IN_PAD - IN_WIDTH, wt.shape[1]), wt.dtype)
    return jnp.concatenate([wt[:cut], wt[cut + GLA_GATE_RANK:], wt[cut:cut + GLA_GATE_RANK], pad], axis=0)


def _ungroup_in_rows(gt):
    cut = 2 * GLA_DK + 2 * GLA_DV
    return jnp.concatenate([gt[:cut], gt[3072:3072 + GLA_GATE_RANK], gt[cut:3072]], axis=0)


def _colsum(v, *, name, tm=512):
    T, C = v.shape

    def body(v_ref, o_ref):
        i = pl.program_id(0)
        part = jnp.sum(v_ref[...], axis=0, keepdims=True)

        @pl.when(i == 0)
        def _():
            o_ref[...] = part

        @pl.when(i > 0)
        def _():
            o_ref[...] += part

    return pl.pallas_call(
        body, name=name, grid=(T // tm,),
        in_specs=[pl.BlockSpec((tm, C), lambda i: (i, 0))], out_specs=pl.BlockSpec((1, C), lambda i: (0, 0)),
        out_shape=jax.ShapeDtypeStruct((1, C), F32),
        compiler_params=_cparams(("arbitrary",)),
    )(v)


def _ffn_fwd(h, g_norm, wgu_t, wd, tag):
    hf = _rms_fwd(h, g_norm, name=f"ffn{tag}_norm")
    ab = _matmul(hf, wgu_t, mode='nt', out_dtype=F32, name=f"ffn{tag}_up", tm=256, tn=D_FF, tk=D_MODEL)
    s = _swiglu_fwd(ab, name=f"ffn{tag}_act")
    h_out = _matmul(s, wd, mode='nn', out_dtype=F32, name=f"ffn{tag}_down", tm=512, tn=D_MODEL, tk=D_FF, residual=h)
    return h_out, (hf, ab, s)


def _ffn_bwd(dh, dh_c, h_in, g_norm, wgu_t, wd, saved, tag):
    hf, ab, s = saved
    ds = _matmul(dh_c, wd, mode='nt', out_dtype=F32, name=f"ffn{tag}_dact", tm=512, tn=D_FF // 2, tk=D_MODEL)
    dwd = _matmul(s, dh_c, mode='tn', out_dtype=F32, name=f"ffn{tag}_dwd", tm=D_FF // 2, tn=D_MODEL, tk=512)
    dab = _swiglu_bwd(ds, ab, name=f"ffn{tag}_dab")
    dwgu_t = _matmul(dab, hf, mode='tn', out_dtype=F32, name=f"ffn{tag}_dwgu", tm=D_FF // 2, tn=D_MODEL, tk=512)
    dhf = _matmul(dab, wgu_t, mode='nn', out_dtype=F32, name=f"ffn{tag}_dhf", tm=256, tn=D_MODEL, tk=2 * D_FF)
    dh_in, dh_in_c, dg = _rms_bwd(dhf, h_in, g_norm, dh, name=f"ffn{tag}_dnorm")
    return dh_in, dh_in_c, dwgu_t, dwd, dg


def _local_step(x, tgt, W):
    row = lambda v: v.reshape(1, -1)
    win_p = _regroup_in_rows(W['hy_w_in_t'])
    wg2p = jnp.pad(W['hy_w_gate2'], ((0, 128 - GLA_GATE_RANK), (0, 0)))
    b_gate = row(W['hy_b_gate'])
    g_gla = row(W['hy_gla_norm'])
    gq = jnp.tile(W['hy_sb_q_norm'].reshape(-1), SB_D // SB_HEAD_DIM).reshape(1, SB_D)
    gk = jnp.tile(W['hy_sb_k_norm'].reshape(-1), SB_D // SB_HEAD_DIM).reshape(1, SB_D)
    w_out = W['hy_w_out']
    wgu = W['ffn_wgu_t']
    wd = [W['ffn_w_down'][l] for l in range(2)]
    w_dw = jnp.pad(W['cv_w_dw'], ((0, CONV_HALO - CONV_WIDTH), (0, 0)))
    mixn = [row(W['mix_norm'][l]) for l in range(2)]
    ffnn = [row(W['ffn_norm'][l]) for l in range(2)]

    hn0 = _rms_fwd(x, mixn[0], name="mix0_norm")
    proj = _matmul(hn0, win_p, mode='nt', out_dtype=F32, name="hy_in", tm=256, tn=IN_PAD, tk=D_MODEL)
    o_gla, o_raw, states = _gla_fwd(proj, wg2p, b_gate, g_gla)
    qn, kn, vb = _sb_prep(proj, gq, gk)
    o_sb, lsum = _sb_fwd(qn, kn, vb)
    o_mix = jnp.concatenate([o_gla, o_sb], axis=1)
    h1 = _matmul(o_mix, w_out, mode='nn', out_dtype=F32, name="hy_out", tm=512, tn=D_MODEL, tk=D_MODEL, residual=x)
    h2, ffn0_saved = _ffn_fwd(h1, ffnn[0], wgu[0], wd[0], 0)
    hn1 = _rms_fwd(h2, mixn[1], name="mix1_norm")
    a_cv = _matmul(hn1, W['cv_w_pw1_t'], mode='nt', out_dtype=F32, name="cv_pw1", tm=512, tn=2 * D_MODEL, tk=D_MODEL,
                   bias=row(W['cv_b_pw1']))
    s_cv, u_cv, c_cv = _conv_fwd(a_cv, w_dw, row(W['cv_b_dw']), row(W['cv_ln_g']), row(W['cv_ln_b']))
    h3 = _matmul(s_cv, W['cv_w_pw2'], mode='nn', out_dtype=F32, name="cv_pw2", tm=512, tn=D_MODEL, tk=D_MODEL,
                 bias=row(W['cv_b_pw2']), residual=h2)
    h4, ffn1_saved = _ffn_fwd(h3, ffnn[1], wgu[1], wd[1], 1)
    sq_err, dy, dy_c = _loss_head(h4, tgt)

    G = {}
    dh3, dh3_c, dwgu1, dwd1, dg_ffn1 = _ffn_bwd(dy, dy_c, h3, ffnn[1], wgu[1], wd[1], ffn1_saved, 1)
    G['cv_b_pw2'] = _colsum(dh3, name="cv_db2")
    G['cv_w_pw2'] = _matmul(s_cv, dh3_c, mode='tn', out_dtype=F32, name="cv_dw2", tm=D_MODEL, tn=D_MODEL, tk=512)
    ds_cv = _matmul(dh3_c, W['cv_w_pw2'], mode='nt', out_dtype=F32, name="cv_ds", tm=512, tn=D_MODEL, tk=D_MODEL)
    da_cv, db1, dwdw, dbdw, dlng, dlnb = _conv_bwd(ds_cv, c_cv, u_cv, a_cv, w_dw, row(W['cv_ln_g']), row(W['cv_ln_b']))
    G['cv_b_pw1'] = db1
    G['cv_w_dw'] = dwdw[:CONV_WIDTH]
    G['cv_b_dw'], G['cv_ln_g'], G['cv_ln_b'] = dbdw, dlng, dlnb
    G['cv_w_pw1_t'] = _matmul(da_cv, hn1, mode='tn', out_dtype=F32, name="cv_dw1", tm=D_MODEL, tn=D_MODEL, tk=512)
    dhn1 = _matmul(da_cv, W['cv_w_pw1_t'], mode='nn', out_dtype=F32, name="cv_dhn", tm=512, tn=D_MODEL, tk=2 * D_MODEL)
    dh2, dh2_c, dg_mix1 = _rms_bwd(dhn1, h2, mixn[1], dh3, name="mix1_dnorm")
    dh1, dh1_c, dwgu0, dwd0, dg_ffn0 = _ffn_bwd(dh2, dh2_c, h1, ffnn[0], wgu[0], wd[0], ffn0_saved, 0)
    G['hy_w_out'] = _matmul(o_mix, dh1_c, mode='tn', out_dtype=F32, name="hy_dwout", tm=D_MODEL, tn=D_MODEL, tk=512)
    d_omix = _matmul(dh1_c, w_out, mode='nt', out_dtype=F32, name="hy_domix", tm=512, tn=D_MODEL, tk=D_MODEL)
    dgla, dglr, dwg2, dbg, dgg = _gla_bwd(d_omix[:, :GLA_DV], proj, o_raw, states, wg2p, b_gate, g_gla)
    dqn, dkn, dvs = _sb_bwd(d_omix[:, GLA_DV:], qn, kn, vb, lsum)
    dsb, dgq, dgk = _sb_prep_bwd(dqn, dkn, dvs, proj, gq, gk)
    dproj = jnp.concatenate([dgla, dsb, dglr], axis=1)
    dwin_p = _matmul(dproj, hn0, mode='tn', out_dtype=F32, name="hy_dwin", tm=IN_PAD // 5, tn=D_MODEL, tk=512)
    dhn0 = _matmul(dproj, win_p, mode='nn', out_dtype=F32, name="hy_dhn", tm=256, tn=D_MODEL, tk=IN_PAD)
    dx, _, dg_mix0 = _rms_bwd(dhn0, x, mixn[0], dh1, name="mix0_dnorm")

    G['hy_w_in_t'] = _ungroup_in_rows(dwin_p)
    G['hy_w_gate2'] = dwg2[:GLA_GATE_RANK]
    G['hy_b_gate'] = dbg
    G['hy_gla_norm'] = dgg
    G['hy_sb_q_norm'] = dgq.reshape(SB_D // SB_HEAD_DIM, SB_HEAD_DIM).sum(axis=0, keepdims=True)
    G['hy_sb_k_norm'] = dgk.reshape(SB_D // SB_HEAD_DIM, SB_HEAD_DIM).sum(axis=0, keepdims=True)
    G['mix_norm'] = jnp.concatenate([dg_mix0, dg_mix1], axis=0)
    G['ffn_norm'] = jnp.concatenate([dg_ffn0, dg_ffn1], axis=0)
    G['ffn_wgu_t'] = [dwgu0, dwgu1]
    G['ffn_w_down'] = [dwd0, dwd1]
    return sq_err, dx, G


MESH_IDS = pl.DeviceIdType.MESH
N_PEER = N_DEV - 1


def _all_gather(blocks):
    n = len(blocks)

    def body(*refs):
        x_refs, out_refs = refs[:n], refs[n:2 * n]
        send_sems, recv_sems, local_sems = refs[2 * n:]
        x, y, c = lax.axis_index("x"), lax.axis_index("y"), lax.axis_index("c")
        me, sibling = (x, y, c), (x, y, 1 - c)
        chips = [(1 - x, y), (x, 1 - y), (1 - x, 1 - y)]

        def slot(a, px, py, pc):
            return out_refs[a].at[4 * px + 2 * py + pc]

        def copy(a, k, blk, to, src=None):
            return pltpu.make_async_remote_copy(
                src_ref=slot(a, *blk) if src is None else src, dst_ref=slot(a, *blk),
                send_sem=send_sems.at[a * N_PEER + k], recv_sem=recv_sems.at[a * N_PEER + k],
                device_id=to, device_id_type=MESH_IDS)

        mine = [pltpu.make_async_copy(x_refs[a], slot(a, *me), local_sems.at[a]) for a in range(n)]
        for cp in mine:
            cp.start()
        first = []
        for a in range(n):
            first.append(copy(a, 0, me, sibling, src=x_refs[a]))
            first += [copy(a, 1 + j, me, (*chip, c), src=x_refs[a]) for j, chip in enumerate(chips)]
        for cp in first:
            cp.start()
        passed = []
        for a in range(n):
            for j, chip in enumerate(chips):
                copy(a, 1 + j, (*chip, c), me).wait_recv()
                fwd = copy(a, 4 + j, (*chip, c), sibling)
                fwd.start()
                passed.append(fwd)
        for a in range(n):
            copy(a, 0, sibling, me).wait_recv()
            for j, chip in enumerate(chips):
                copy(a, 4 + j, (*chip, 1 - c), me).wait_recv()
        for cp in first + passed:
            cp.wait_send()
        for cp in mine:
            cp.wait()

    anyspec = pl.BlockSpec(memory_space=pl.ANY)
    return pl.pallas_call(
        body, name="fsdp_all_gather",
        out_shape=[jax.ShapeDtypeStruct((N_DEV,) + b.shape, b.dtype) for b in blocks],
        in_specs=[anyspec] * n, out_specs=[anyspec] * n,
        scratch_shapes=[pltpu.SemaphoreType.DMA((n * N_PEER,)), pltpu.SemaphoreType.DMA((n * N_PEER,)),
                        pltpu.SemaphoreType.DMA((n,))],
    )(*blocks)


def _scatter_exchange(sends):
    n = len(sends)

    def body(*refs):
        s_refs, r_refs = refs[:n], refs[n:2 * n]
        send_sems, recv_sems, local_sems = refs[2 * n:]
        x, y, c = lax.axis_index("x"), lax.axis_index("y"), lax.axis_index("c")
        me = 4 * x + 2 * y + c
        mine = [pltpu.make_async_copy(s_refs[a].at[me], r_refs[a].at[me], local_sems.at[a]) for a in range(n)]
        for cp in mine:
            cp.start()
        copies = []
        for a in range(n):
            for k in range(1, N_DEV):
                px, py, pc = x ^ ((k >> 2) & 1), y ^ ((k >> 1) & 1), c ^ (k & 1)
                cp = pltpu.make_async_remote_copy(
                    src_ref=s_refs[a].at[4 * px + 2 * py + pc], dst_ref=r_refs[a].at[me],
                    send_sem=send_sems.at[a * N_PEER + k - 1], recv_sem=recv_sems.at[a * N_PEER + k - 1],
                    device_id=(px, py, pc), device_id_type=MESH_IDS)
                cp.start()
                copies.append(cp)
        for cp in copies:
            cp.wait()
        for cp in mine:
            cp.wait()

    anyspec = pl.BlockSpec(memory_space=pl.ANY)
    return pl.pallas_call(
        body, name="fsdp_scatter_exchange",
        out_shape=[jax.ShapeDtypeStruct(s.shape, s.dtype) for s in sends],
        in_specs=[anyspec] * n, out_specs=[anyspec] * n,
        scratch_shapes=[pltpu.SemaphoreType.DMA((n * N_PEER,)), pltpu.SemaphoreType.DMA((n * N_PEER,)),
                        pltpu.SemaphoreType.DMA((n,))],
    )(*sends)


def _sum_contrib(recv, own, *, name, tr):
    _, R, C = recv.shape
    assert R % tr == 0

    def body(r_ref, own_ref, g_ref):
        me = 4 * lax.axis_index("x") + 2 * lax.axis_index("y") + lax.axis_index("c")
        g = jnp.zeros((tr, C), F32)
        for s in range(N_DEV):
            g = g + jnp.where(me == s, own_ref[...], r_ref[s].astype(F32))
        g_ref[...] = g

    row = pl.BlockSpec((tr, C), lambda i: (i, 0))
    return pl.pallas_call(
        body, name=name, grid=(R // tr,),
        in_specs=[pl.BlockSpec((N_DEV, tr, C), lambda i: (0, i, 0)), row], out_specs=row,
        out_shape=jax.ShapeDtypeStruct((R, C), F32),
        compiler_params=_cparams(("parallel",)),
    )(recv, own)


def _adamw(g, w, m, v, *, name, tr):
    R, C = g.shape
    assert R % tr == 0

    def body(g_ref, w_ref, m_ref, v_ref, d_ref, mo_ref, vo_ref):
        gv = g_ref[...]
        mn = ADAM_B1 * m_ref[...] + (1.0 - ADAM_B1) * gv
        vn = ADAM_B2 * v_ref[...] + (1.0 - ADAM_B2) * (gv * gv)
        m_hat = mn / (1.0 - ADAM_B1 ** ADAM_STEP)
        v_hat = vn / (1.0 - ADAM_B2 ** ADAM_STEP)
        d_ref[...] = -ADAM_LR * (m_hat / (jnp.sqrt(v_hat) + ADAM_EPS) + ADAM_WD * w_ref[...])
        mo_ref[...] = mn
        vo_ref[...] = vn

    row = pl.BlockSpec((tr, C), lambda i: (i, 0))
    return pl.pallas_call(
        body, name=name, grid=(R // tr,),
        in_specs=[row] * 4, out_specs=[row] * 3,
        out_shape=[jax.ShapeDtypeStruct((R, C), F32)] * 3,
        compiler_params=_cparams(("parallel",)),
    )(g, w, m, v)


SMALL_SHARDED = ('hy_w_gate2', 'cv_b_pw1', 'cv_w_dw', 'cv_b_dw', 'cv_ln_g', 'cv_ln_b', 'cv_b_pw2')
SMALL_REPLICATED = ('mix_norm', 'ffn_norm', 'hy_b_gate', 'hy_gla_norm', 'hy_sb_q_norm', 'hy_sb_k_norm')
LANES = 128


def _small_rows(n):
    return -(-n // (8 * LANES)) * 8


def _pack_small(parts, lead=()):
    out = []
    for p in parts:
        n = p.shape[-1]
        p = jnp.pad(p, [(0, 0)] * len(lead) + [(0, _small_rows(n) * LANES - n)])
        out.append(p.reshape(*lead, _small_rows(n), LANES))
    return jnp.concatenate(out, axis=len(lead))


def _unpack_small(packed, sizes, lead=()):
    out, r0 = [], 0
    for n in sizes:
        r = _small_rows(n)
        out.append(packed[..., r0:r0 + r, :].reshape(*lead, r * LANES)[..., :n])
        r0 += r
    return out


def _to_blocks(full, axis):
    shp = full.shape
    t = full.reshape(shp[:axis] + (N_DEV, shp[axis] // N_DEV) + shp[axis + 1:])
    return jnp.moveaxis(t, axis, 0)


def _from_blocks(blocks, axis):
    t = jnp.moveaxis(blocks, 0, axis)
    shp = t.shape
    return t.reshape(shp[:axis] + (shp[axis] * shp[axis + 1],) + shp[axis + 2:])


def kernel(x, mix_norm, ffn_norm, hy_w_in, hy_w_gate2, hy_b_gate, hy_gla_norm, hy_sb_q_norm, hy_sb_k_norm, hy_w_out, cv_w_pw1, cv_b_pw1, cv_w_dw, cv_b_dw, cv_ln_g, cv_ln_b, cv_w_pw2, cv_b_pw2, ffn_w_gate, ffn_w_up, ffn_w_down, loss_target, m_mix_norm, m_ffn_norm, m_hy_w_in, m_hy_w_gate2, m_hy_b_gate, m_hy_gla_norm, m_hy_sb_q_norm, m_hy_sb_k_norm, m_hy_w_out, m_cv_w_pw1, m_cv_b_pw1, m_cv_w_dw, m_cv_b_dw, m_cv_ln_g, m_cv_ln_b, m_cv_w_pw2, m_cv_b_pw2, m_ffn_w_gate, m_ffn_w_up, m_ffn_w_down, v_mix_norm, v_ffn_norm, v_hy_w_in, v_hy_w_gate2, v_hy_b_gate, v_hy_gla_norm, v_hy_sb_q_norm, v_hy_sb_k_norm, v_hy_w_out, v_cv_w_pw1, v_cv_b_pw1, v_cv_w_dw, v_cv_b_dw, v_cv_ln_g, v_cv_ln_b, v_cv_w_pw2, v_cv_b_pw2, v_ffn_w_gate, v_ffn_w_up, v_ffn_w_down):
    w_loc = dict(zip(WEIGHT_NAMES, (mix_norm, ffn_norm, hy_w_in, hy_w_gate2, hy_b_gate, hy_gla_norm, hy_sb_q_norm, hy_sb_k_norm, hy_w_out, cv_w_pw1, cv_b_pw1, cv_w_dw, cv_b_dw, cv_ln_g, cv_ln_b, cv_w_pw2, cv_b_pw2, ffn_w_gate, ffn_w_up, ffn_w_down)))
    m_loc = dict(zip(WEIGHT_NAMES, (m_mix_norm, m_ffn_norm, m_hy_w_in, m_hy_w_gate2, m_hy_b_gate, m_hy_gla_norm, m_hy_sb_q_norm, m_hy_sb_k_norm, m_hy_w_out, m_cv_w_pw1, m_cv_b_pw1, m_cv_w_dw, m_cv_b_dw, m_cv_ln_g, m_cv_ln_b, m_cv_w_pw2, m_cv_b_pw2, m_ffn_w_gate, m_ffn_w_up, m_ffn_w_down)))
    v_loc = dict(zip(WEIGHT_NAMES, (v_mix_norm, v_ffn_norm, v_hy_w_in, v_hy_w_gate2, v_hy_b_gate, v_hy_gla_norm, v_hy_sb_q_norm, v_hy_sb_k_norm, v_hy_w_out, v_cv_w_pw1, v_cv_b_pw1, v_cv_w_dw, v_cv_b_dw, v_cv_ln_g, v_cv_ln_b, v_cv_w_pw2, v_cv_b_pw2, v_ffn_w_gate, v_ffn_w_up, v_ffn_w_down)))

    Dm, F8 = D_MODEL, D_FF // N_DEV
    tr_ = lambda a: jnp.swapaxes(a, -1, -2)

    small_local = _pack_small([w_loc[n].reshape(-1) for n in SMALL_SHARDED])
    g_in, g_out, g_pw1, g_pw2, g_gate, g_up, g_down, g_small = _all_gather([
        tr_(hy_w_in[0]).astype(BF16),
        hy_w_out[0].astype(BF16),
        tr_(cv_w_pw1[0]).astype(BF16),
        cv_w_pw2[0].astype(BF16),
        tr_(ffn_w_gate).astype(BF16),
        tr_(ffn_w_up).astype(BF16),
        ffn_w_down.astype(BF16),
        small_local])
    small_sizes = [w_loc[n].size for n in SMALL_SHARDED]
    small_full = dict(zip(SMALL_SHARDED, _unpack_small(g_small, small_sizes, lead=(N_DEV,))))
    W = {n: w_loc[n] for n in SMALL_REPLICATED}
    W['hy_w_in_t'] = g_in.reshape(IN_WIDTH, Dm)
    W['hy_w_out'] = g_out.reshape(Dm, Dm)
    W['cv_w_pw1_t'] = g_pw1.reshape(2 * Dm, Dm)
    W['cv_w_pw2'] = g_pw2.reshape(Dm, Dm)
    W['ffn_wgu_t'] = [jnp.concatenate([g_gate[:, l].reshape(D_FF, Dm), g_up[:, l].reshape(D_FF, Dm)], axis=0)
                      for l in range(2)]
    W['ffn_w_down'] = jnp.stack([g_down[:, l].reshape(D_FF, Dm) for l in range(2)])
    W['hy_w_gate2'] = _from_blocks(small_full['hy_w_gate2'].reshape(N_DEV, GLA_GATE_RANK, GLA_DK // N_DEV), 1).astype(BF16)
    W['cv_w_dw'] = _from_blocks(small_full['cv_w_dw'].reshape(N_DEV, CONV_WIDTH, Dm // N_DEV), 1)
    for n in ('cv_b_pw1', 'cv_b_dw', 'cv_ln_g', 'cv_ln_b', 'cv_b_pw2'):
        W[n] = small_full[n].reshape(-1)

    sq_err, dx, G = _local_step(x[0], loss_target[0], W)
    loss = lax.psum(0.5 / Dm * sq_err[0, 0], ("x", "y", "c"))

    own_f32 = [
        G['hy_w_in_t'].reshape(N_DEV, IN_WIDTH // N_DEV, Dm),
        G['hy_w_out'].reshape(N_DEV, Dm // N_DEV, Dm),
        G['cv_w_pw1_t'].reshape(N_DEV, 2 * Dm // N_DEV, Dm),
        G['cv_w_pw2'].reshape(N_DEV, Dm // N_DEV, Dm),
        jnp.stack([g.reshape(2, N_DEV, F8, Dm) for g in G['ffn_wgu_t']]).transpose(2, 0, 1, 3, 4).reshape(N_DEV, 4 * F8, Dm),
        jnp.stack([g.reshape(N_DEV, F8, Dm) for g in G['ffn_w_down']], axis=1).reshape(N_DEV, 2 * F8, Dm),
    ]
    small_parts = []
    for n in SMALL_SHARDED:
        axis = SHARD_AXIS[n] - 1
        shard = w_loc[n].shape[1:]
        full = shard[:axis] + (shard[axis] * N_DEV,) + shard[axis + 1:]
        small_parts.append(_to_blocks(G[n].reshape(full), axis).reshape(N_DEV, -1))
    for n in SMALL_REPLICATED:
        small_parts.append(jnp.broadcast_to(G[n].reshape(1, -1), (N_DEV, G[n].size)))
    send_small = _pack_small(small_parts, lead=(N_DEV,))
    recv = _scatter_exchange([a.astype(BF16) for a in own_f32] + [send_small])
    me = 4 * lax.axis_index("x") + 2 * lax.axis_index("y") + lax.axis_index("c")
    own = [lax.dynamic_index_in_dim(a, me, 0, keepdims=False) for a in own_f32 + [send_small]]
    tags = ['hy_w_in', 'hy_w_out', 'cv_w_pw1', 'cv_w_pw2', 'ffn_wgu', 'ffn_w_down', 'small']
    rows_tile = [IN_WIDTH // N_DEV, Dm // N_DEV, 2 * Dm // N_DEV, Dm // N_DEV, F8, F8, recv[-1].shape[1]]
    gsum = [_sum_contrib(r, o, name=f"sum_{t}", tr=tr) for r, o, t, tr in zip(recv, own, tags, rows_tile)]

    grad = {}
    grad['hy_w_in'] = tr_(gsum[0])[None]
    grad['hy_w_out'] = gsum[1][None]
    grad['cv_w_pw1'] = tr_(gsum[2])[None]
    grad['cv_w_pw2'] = gsum[3][None]
    gu = gsum[4].reshape(2, 2, F8, Dm)
    grad['ffn_w_gate'] = tr_(gu[:, 0])
    grad['ffn_w_up'] = tr_(gu[:, 1])
    grad['ffn_w_down'] = gsum[5].reshape(2, F8, Dm)
    small_names = SMALL_SHARDED + SMALL_REPLICATED
    small_all = [w_loc[n].size for n in small_names]
    for n, a in zip(small_names, _unpack_small(gsum[6], small_all)):
        grad[n] = a.reshape(w_loc[n].shape)

    delta, new_m, new_v = {}, {}, {}
    view = {'hy_w_in': (Dm, 256), 'hy_w_out': (Dm // N_DEV, Dm // N_DEV), 'cv_w_pw1': (Dm, 256),
            'cv_w_pw2': (Dm // N_DEV, Dm // N_DEV), 'ffn_w_gate': (2 * Dm, 256), 'ffn_w_up': (2 * Dm, 256),
            'ffn_w_down': (2 * F8, F8)}
    for n, (rows, tr) in view.items():
        shp = w_loc[n].shape
        outs = _adamw(grad[n].reshape(rows, -1), w_loc[n].reshape(rows, -1), m_loc[n].reshape(rows, -1),
                      v_loc[n].reshape(rows, -1), name=f"adamw_{n}", tr=tr)
        delta[n], new_m[n], new_v[n] = (o.reshape(shp) for o in outs)
    packed = [_pack_small([d[n].reshape(-1) for n in small_names]) for d in (w_loc, m_loc, v_loc)]
    outs = _adamw(gsum[6], *packed, name="adamw_small", tr=gsum[6].shape[0])
    for dst, o in zip((delta, new_m, new_v), outs):
        for n, a in zip(small_names, _unpack_small(o, small_all)):
            dst[n] = a.reshape(w_loc[n].shape)

    return (loss, dx[None], *[grad[n] for n in WEIGHT_NAMES], *[delta[n] for n in WEIGHT_NAMES],
            *[new_m[n] for n in WEIGHT_NAMES], *[new_v[n] for n in WEIGHT_NAMES])
```

```python
import jax
import jax.numpy as jnp
from jax import lax
from jax.experimental import pallas as pl
from jax.experimental.pallas import tpu as pltpu

F32 = jnp.float32
BF16 = jnp.bfloat16
CDT = jnp.bfloat16

D_MODEL = 1024
EPS = 1e-6
CHUNK = 64
GLA_HEADS = 4
GLA_HEAD_K = 64
GLA_HEAD_V = 128
GLA_DK = GLA_HEADS * GLA_HEAD_K
GLA_DV = GLA_HEADS * GLA_HEAD_V
GLA_GATE_RANK = 16
GLA_GATE_NORMALIZER = 16.0
SB_HEAD_DIM = 64
SB_D = 512
SB_TILE = 256
SB_PAIR = 128
SB_SPLIT_LK = 2
SB_SPLIT_G = 1
IN_WIDTH = 3088
IN_PAD = 3200
CONV_WIDTH = 31
CONV_HALO = 32
D_FF = 2816
N_DEV = 8

ADAM_LR = 0.001
ADAM_B1 = 0.9
ADAM_B2 = 0.999
ADAM_EPS = 1e-08
ADAM_WD = 0.01
ADAM_STEP = 10

VMEM_LIMIT = 56 * 1024 * 1024
TK_TOKENS = 2048

WEIGHT_NAMES = ['mix_norm', 'ffn_norm', 'hy_w_in', 'hy_w_gate2', 'hy_b_gate', 'hy_gla_norm', 'hy_sb_q_norm',
                'hy_sb_k_norm', 'hy_w_out', 'cv_w_pw1', 'cv_b_pw1', 'cv_w_dw', 'cv_b_dw', 'cv_ln_g', 'cv_ln_b',
                'cv_w_pw2', 'cv_b_pw2', 'ffn_w_gate', 'ffn_w_up', 'ffn_w_down']
SHARD_AXIS = {'mix_norm': None, 'ffn_norm': None, 'hy_w_in': 2, 'hy_w_gate2': 2, 'hy_b_gate': None,
              'hy_gla_norm': None, 'hy_sb_q_norm': None, 'hy_sb_k_norm': None, 'hy_w_out': 1, 'cv_w_pw1': 2,
              'cv_b_pw1': 1, 'cv_w_dw': 2, 'cv_b_dw': 1, 'cv_ln_g': 1, 'cv_ln_b': 1, 'cv_w_pw2': 1, 'cv_b_pw2': 1,
              'ffn_w_gate': 2, 'ffn_w_up': 2, 'ffn_w_down': 1}


def _cparams(sem=None, vmem=VMEM_LIMIT):
    return pltpu.CompilerParams(dimension_semantics=sem, vmem_limit_bytes=vmem)


def _log_sigmoid(x):
    return jnp.minimum(x, 0.0) - jnp.log1p(jnp.exp(-jnp.abs(x)))


def _sigmoid(x):
    return 1.0 / (1.0 + jnp.exp(-x))


def _softplus(x):
    return jnp.maximum(x, 0.0) + jnp.log(1.0 + jnp.exp(-jnp.abs(x)))


def _split_bf16(x, n):
    parts = []
    rem = x
    for _ in range(n):
        p = rem.astype(BF16)
        parts.append(p)
        rem = rem - p.astype(F32)
    return parts


def _dot_exact_rhs(x, m, n):
    return sum(jnp.dot(p, m, preferred_element_type=F32) for p in _split_bf16(x, n))


def _dot_exact_lhs(m, x, n):
    return sum(jnp.dot(m, p, preferred_element_type=F32) for p in _split_bf16(x, n))


_NN = (((1,), (0,)), ((), ()))
_NT = (((1,), (1,)), ((), ()))
_TN = (((0,), (0,)), ((), ()))


def _dg(a, b, dn):
    return lax.dot_general(a.astype(CDT), b.astype(CDT), dn, preferred_element_type=F32)


def _matmul(a, b, *, mode, out_dtype, name, tm, tn, tk, bias=None, residual=None):
    if mode == 'nn':
        (M, K), (K2, N) = a.shape, b.shape
    elif mode == 'nt':
        (M, K), (N, K2) = a.shape, b.shape
    else:
        (K, M), (K2, N) = a.shape, b.shape
    assert K == K2 and M % tm == 0 and N % tn == 0 and K % tk == 0, (name, a.shape, b.shape, tm, tn, tk)
    nk = K // tk
    a_spec = pl.BlockSpec((tk, tm), lambda i, j, k: (k, i)) if mode == 'tn' else pl.BlockSpec((tm, tk), lambda i, j, k: (i, k))
    b_spec = pl.BlockSpec((tn, tk), lambda i, j, k: (j, k)) if mode == 'nt' else pl.BlockSpec((tk, tn), lambda i, j, k: (k, j))
    dn = {'nn': _NN, 'nt': _NT, 'tn': _TN}[mode]
    has_bias, has_res = bias is not None, residual is not None

    def body(*refs):
        a_ref, b_ref = refs[0], refs[1]
        pos = 2
        bias_ref = res_ref = None
        if has_bias:
            bias_ref = refs[pos]
            pos += 1
        if has_res:
            res_ref = refs[pos]
            pos += 1
        o_ref = refs[pos]
        acc_ref = refs[pos + 1] if nk > 1 else None
        p = _dg(a_ref[...], b_ref[...], dn)

        def finish(acc):
            if has_bias:
                acc = acc + bias_ref[...]
            if has_res:
                acc = res_ref[...] + acc
            o_ref[...] = acc.astype(o_ref.dtype)

        if nk == 1:
            finish(p)
        else:
            k = pl.program_id(2)

            @pl.when(k == 0)
            def _():
                acc_ref[...] = p

            @pl.when(k > 0)
            def _():
                acc_ref[...] += p

            @pl.when(k == nk - 1)
            def _():
                finish(acc_ref[...])

    in_specs = [a_spec, b_spec]
    args = [a, b]
    if has_bias:
        in_specs.append(pl.BlockSpec((1, tn), lambda i, j, k: (0, j)))
        args.append(bias)
    if has_res:
        in_specs.append(pl.BlockSpec((tm, tn), lambda i, j, k: (i, j)))
        args.append(residual)
    return pl.pallas_call(
        body, name=name, grid=(M // tm, N // tn, nk),
        in_specs=in_specs, out_specs=pl.BlockSpec((tm, tn), lambda i, j, k: (i, j)),
        out_shape=jax.ShapeDtypeStruct((M, N), out_dtype),
        scratch_shapes=[pltpu.VMEM((tm, tn), F32)] if nk > 1 else [],
        compiler_params=_cparams(("parallel", "parallel", "arbitrary")),
    )(*args)


def _rms_fwd(x, g, *, name, tm=512):
    T, Dm = x.shape

    def body(x_ref, g_ref, o_ref):
        xv = x_ref[...]
        r = lax.rsqrt(jnp.mean(xv * xv, axis=-1, keepdims=True) + EPS)
        o_ref[...] = (xv * r * g_ref[...]).astype(o_ref.dtype)

    return pl.pallas_call(
        body, name=name, grid=(T // tm,),
        in_specs=[pl.BlockSpec((tm, Dm), lambda i: (i, 0)), pl.BlockSpec((1, Dm), lambda i: (0, 0))],
        out_specs=pl.BlockSpec((tm, Dm), lambda i: (i, 0)),
        out_shape=jax.ShapeDtypeStruct((T, Dm), CDT),
        compiler_params=_cparams(("parallel",)),
    )(x, g)


def _rms_bwd(dy, x, g, resid, *, name, tm=512):
    T, Dm = x.shape

    def body(dy_ref, x_ref, g_ref, res_ref, dx_ref, dxb_ref, dg_ref):
        i = pl.program_id(0)
        xv, dyv = x_ref[...], dy_ref[...]
        r = lax.rsqrt(jnp.mean(xv * xv, axis=-1, keepdims=True) + EPS)
        u = dyv * g_ref[...]
        dot = jnp.mean(u * xv, axis=-1, keepdims=True)
        dx = res_ref[...] + (r * u - xv * (r * r * r * dot))
        dx_ref[...] = dx
        dxb_ref[...] = dx.astype(dxb_ref.dtype)
        part = jnp.sum(dyv * xv * r, axis=0, keepdims=True)

        @pl.when(i == 0)
        def _():
            dg_ref[...] = part

        @pl.when(i > 0)
        def _():
            dg_ref[...] += part

    row = pl.BlockSpec((tm, Dm), lambda i: (i, 0))
    vec = pl.BlockSpec((1, Dm), lambda i: (0, 0))
    return pl.pallas_call(
        body, name=name, grid=(T // tm,),
        in_specs=[row, row, vec, row], out_specs=[row, row, vec],
        out_shape=[jax.ShapeDtypeStruct((T, Dm), F32), jax.ShapeDtypeStruct((T, Dm), CDT),
                   jax.ShapeDtypeStruct((1, Dm), F32)],
        compiler_params=_cparams(("arbitrary",)),
    )(dy, x, g, resid)


def _loss_head(y, tgt, *, tm=512):
    T, Dm = y.shape

    def body(y_ref, t_ref, s_ref, dy_ref, dyb_ref):
        i = pl.program_id(0)
        e = y_ref[...] - t_ref[...]
        dy = e * (1.0 / Dm)
        dy_ref[...] = dy
        dyb_ref[...] = dy.astype(dyb_ref.dtype)
        part = jnp.sum(jnp.sum(e * e, axis=1, keepdims=True), axis=0, keepdims=True)

        @pl.when(i == 0)
        def _():
            s_ref[...] = part

        @pl.when(i > 0)
        def _():
            s_ref[...] += part

    row = pl.BlockSpec((tm, Dm), lambda i: (i, 0))
    return pl.pallas_call(
        body, name="loss_head", grid=(T // tm,),
        in_specs=[row, row], out_specs=[pl.BlockSpec((1, 1), lambda i: (0, 0)), row, row],
        out_shape=[jax.ShapeDtypeStruct((1, 1), F32), jax.ShapeDtypeStruct((T, Dm), F32),
                   jax.ShapeDtypeStruct((T, Dm), CDT)],
        compiler_params=_cparams(("arbitrary",)),
    )(y, tgt)


def _swiglu_up(hf, wgu_t, *, name, tm=256):
    T, Dm = hf.shape
    F2 = wgu_t.shape[0]
    F = F2 // 2

    def body(h_ref, w_ref, ab_ref, s_ref):
        ab_ref[...] = _dg(h_ref[...], w_ref[...], _NT)
        a = ab_ref[:, :F]
        b = ab_ref[:, F:]
        s_ref[...] = (a * _sigmoid(a) * b).astype(s_ref.dtype)

    return pl.pallas_call(
        body, name=name, grid=(T // tm,),
        in_specs=[pl.BlockSpec((tm, Dm), lambda i: (i, 0)), pl.BlockSpec((F2, Dm), lambda i: (0, 0))],
        out_specs=[pl.BlockSpec((tm, F2), lambda i: (i, 0)), pl.BlockSpec((tm, F), lambda i: (i, 0))],
        out_shape=[jax.ShapeDtypeStruct((T, F2), F32), jax.ShapeDtypeStruct((T, F), CDT)],
        compiler_params=_cparams(("parallel",)),
    )(hf, wgu_t)


def _swiglu_dact(dh_c, wd, ab, *, name, tm=256):
    T, Dm = dh_c.shape
    F2 = ab.shape[1]
    F = F2 // 2

    def body(dh_ref, w_ref, ab_ref, o_ref):
        dsv = _dg(dh_ref[...], w_ref[...], _NT)
        a = ab_ref[:, :F]
        b = ab_ref[:, F:]
        sg = _sigmoid(a)
        o_ref[:, :F] = (dsv * b * (sg * (1.0 + a * (1.0 - sg)))).astype(o_ref.dtype)
        o_ref[:, F:] = (dsv * (a * sg)).astype(o_ref.dtype)

    return pl.pallas_call(
        body, name=name, grid=(T // tm,),
        in_specs=[pl.BlockSpec((tm, Dm), lambda i: (i, 0)), pl.BlockSpec((F, Dm), lambda i: (0, 0)),
                  pl.BlockSpec((tm, F2), lambda i: (i, 0))],
        out_specs=pl.BlockSpec((tm, F2), lambda i: (i, 0)),
        out_shape=jax.ShapeDtypeStruct((T, F2), CDT),
        compiler_params=_cparams(("parallel",)),
    )(dh_c, wd, ab)


SUBLANES = 8


def _shifted_copies(buf, shifted, tm):
    n = tm + CONV_HALO - SUBLANES
    for b in range(1, SUBLANES):
        shifted[b - 1] = buf[pl.ds(b, n), :]


def _rows_from(buf, shifted, offset, tm):
    a, b = divmod(offset, SUBLANES)
    if b == 0:
        return buf[pl.ds(SUBLANES * a, tm), :]
    return shifted[b - 1, pl.ds(SUBLANES * a, tm), :]


def _conv_fwd(a, w_dw, b_dw, ln_g, ln_b, *, tm=256):
    T = a.shape[0]
    Dm = D_MODEL

    def body(a_ref, w_ref, bdw_ref, g_ref, b_ref, s_ref, u_ref, c_ref, ubuf, shifted):
        i = pl.program_id(0)

        @pl.when(i == 0)
        def _():
            ubuf[0:CONV_HALO, :] = jnp.zeros((CONV_HALO, Dm), F32)

        @pl.when(i > 0)
        def _():
            ubuf[0:CONV_HALO, :] = ubuf[tm:tm + CONV_HALO, :]

        u = a_ref[:, :Dm] * _sigmoid(a_ref[:, Dm:])
        ubuf[CONV_HALO:CONV_HALO + tm, :] = u
        u_ref[...] = u
        _shifted_copies(ubuf, shifted, tm)
        acc = jnp.zeros((tm, Dm), F32) + bdw_ref[...]
        for k in range(CONV_WIDTH):
            acc = acc + w_ref[k:k + 1, :] * _rows_from(ubuf, shifted, CONV_HALO - (CONV_WIDTH - 1) + k, tm)
        c_ref[...] = acc
        mu = jnp.mean(acc, axis=-1, keepdims=True)
        cen = acc - mu
        var = jnp.mean(cen * cen, axis=-1, keepdims=True)
        l = cen * lax.rsqrt(var + EPS) * g_ref[...] + b_ref[...]
        s_ref[...] = (l * _sigmoid(l)).astype(s_ref.dtype)

    row = pl.BlockSpec((tm, Dm), lambda i: (i, 0))
    vec = pl.BlockSpec((1, Dm), lambda i: (0, 0))
    return pl.pallas_call(
        body, name="conv_fwd", grid=(T // tm,),
        in_specs=[pl.BlockSpec((tm, 2 * Dm), lambda i: (i, 0)), pl.BlockSpec((CONV_HALO, Dm), lambda i: (0, 0)), vec, vec, vec],
        out_specs=[row, row, row],
        out_shape=[jax.ShapeDtypeStruct((T, Dm), CDT), jax.ShapeDtypeStruct((T, Dm), F32), jax.ShapeDtypeStruct((T, Dm), F32)],
        scratch_shapes=[pltpu.VMEM((tm + CONV_HALO, Dm), F32), pltpu.VMEM((SUBLANES - 1, tm + CONV_HALO - SUBLANES, Dm), F32)],
        compiler_params=_cparams(("arbitrary",)),
    )(a, w_dw, b_dw, ln_g, ln_b)


def _conv_bwd(ds, c, u, a, w_dw, ln_g, ln_b, *, tm=256):
    T = a.shape[0]
    Dm = D_MODEL
    nt = T // tm

    def body(ds_ref, c_ref, u_ref, a_ref, w_ref, g_ref, b_ref,
             da_ref, db1_ref, dw_ref, dbdw_ref, dg_ref, dbln_ref, dcbuf, shifted):
        i = pl.program_id(0)

        @pl.when(i == 0)
        def _():
            dcbuf[tm:tm + CONV_HALO, :] = jnp.zeros((CONV_HALO, Dm), F32)
            db1_ref[...] = jnp.zeros_like(db1_ref)
            dw_ref[...] = jnp.zeros_like(dw_ref)
            dbdw_ref[...] = jnp.zeros_like(dbdw_ref)
            dg_ref[...] = jnp.zeros_like(dg_ref)
            dbln_ref[...] = jnp.zeros_like(dbln_ref)

        @pl.when(i > 0)
        def _():
            dcbuf[tm:tm + CONV_HALO, :] = dcbuf[0:CONV_HALO, :]

        cv = c_ref[...]
        mu = jnp.mean(cv, axis=-1, keepdims=True)
        cen = cv - mu
        var = jnp.mean(cen * cen, axis=-1, keepdims=True)
        rstd = lax.rsqrt(var + EPS)
        n = cen * rstd
        l = n * g_ref[...] + b_ref[...]
        sg = _sigmoid(l)
        dl = ds_ref[...] * (sg * (1.0 + l * (1.0 - sg)))
        dg_ref[...] += jnp.sum(dl * n, axis=0, keepdims=True)
        dbln_ref[...] += jnp.sum(dl, axis=0, keepdims=True)
        dn = dl * g_ref[...]
        dc = rstd * (dn - jnp.mean(dn, axis=-1, keepdims=True) - n * jnp.mean(dn * n, axis=-1, keepdims=True))
        dbdw_ref[...] += jnp.sum(dc, axis=0, keepdims=True)
        dcbuf[0:tm, :] = dc
        _shifted_copies(dcbuf, shifted, tm)
        uv = u_ref[...]
        du = jnp.zeros((tm, Dm), F32)
        for k in range(CONV_WIDTH):
            slab = _rows_from(dcbuf, shifted, CONV_WIDTH - 1 - k, tm)
            du = du + w_ref[k:k + 1, :] * slab
            dw_ref[k:k + 1, :] += jnp.sum(slab * uv, axis=0, keepdims=True)
        a1 = a_ref[:, :Dm]
        s2 = _sigmoid(a_ref[:, Dm:])
        da1 = du * s2
        da2 = du * a1 * (s2 * (1.0 - s2))
        da_ref[:, :Dm] = da1.astype(da_ref.dtype)
        da_ref[:, Dm:] = da2.astype(da_ref.dtype)
        db1_ref[:, :Dm] += jnp.sum(da1, axis=0, keepdims=True)
        db1_ref[:, Dm:] += jnp.sum(da2, axis=0, keepdims=True)

    rev = lambda i: (nt - 1 - i, 0)
    row = pl.BlockSpec((tm, Dm), rev)
    row2 = pl.BlockSpec((tm, 2 * Dm), rev)
    vec = pl.BlockSpec((1, Dm), lambda i: (0, 0))
    vec2 = pl.BlockSpec((1, 2 * Dm), lambda i: (0, 0))
    taps = pl.BlockSpec((CONV_HALO, Dm), lambda i: (0, 0))
    return pl.pallas_call(
        body, name="conv_bwd", grid=(nt,),
        in_specs=[row, row, row, row2, taps, vec, vec],
        out_specs=[row2, vec2, taps, vec, vec, vec],
        out_shape=[jax.ShapeDtypeStruct((T, 2 * Dm), CDT), jax.ShapeDtypeStruct((1, 2 * Dm), F32),
                   jax.ShapeDtypeStruct((CONV_HALO, Dm), F32), jax.ShapeDtypeStruct((1, Dm), F32),
                   jax.ShapeDtypeStruct((1, Dm), F32), jax.ShapeDtypeStruct((1, Dm), F32)],
        scratch_shapes=[pltpu.VMEM((tm + CONV_HALO, Dm), F32), pltpu.VMEM((SUBLANES - 1, tm + CONV_HALO - SUBLANES, Dm), F32)],
        compiler_params=_cparams(("arbitrary",)),
    )(ds, c, u, a, w_dw, ln_g, ln_b)


def _gla_head_masks(width, per_head):
    lane = lax.broadcasted_iota(jnp.int32, (1, width), 1)
    return [((lane >= h * per_head) & (lane < (h + 1) * per_head)).astype(F32) for h in range(GLA_HEADS)]


def _gla_specs(tm, order):
    return [pl.BlockSpec((tm, GLA_DK), lambda i: (order(i), 0)),
            pl.BlockSpec((tm, GLA_DK), lambda i: (order(i), 1)),
            pl.BlockSpec((tm, GLA_DV), lambda i: (order(i), 1)),
            pl.BlockSpec((tm, GLA_DV), lambda i: (order(i), 2)),
            pl.BlockSpec((tm, 128), lambda i: (order(i), 3072 // 128))]


def _gla_chunk_decay(la_c, tri):
    bc = _dot_exact_lhs(tri, la_c, 3)
    b_end = bc[CHUNK - 1:CHUNK, :]
    return b_end, jnp.exp(b_end - bc)


def _gla_fwd(proj, wg2p, b_gate, g_gla, *, tm=256):
    T = proj.shape[0]
    ncs = tm // CHUNK
    scale = GLA_HEAD_K ** -0.5

    def body(q_ref, k_ref, v_ref, r_ref, glr_ref, wg_ref, bg_ref, gg_ref, o_ref, oraw_ref, st_ref, s_scr):
        i = pl.program_id(0)

        @pl.when(i == 0)
        def _():
            s_scr[...] = jnp.zeros_like(s_scr)

        mk = _gla_head_masks(GLA_DK, GLA_HEAD_K)
        rr = lax.broadcasted_iota(jnp.int32, (CHUNK, CHUNK), 0)
        cc = lax.broadcasted_iota(jnp.int32, (CHUNK, CHUNK), 1)
        tri = (cc <= rr).astype(BF16)
        y = _dg(glr_ref[...], wg_ref[...], _NN) + bg_ref[...]
        la = _log_sigmoid(y) / GLA_GATE_NORMALIZER
        qs = q_ref[...] * scale
        for ci in range(ncs):
            rows = slice(ci * CHUNK, (ci + 1) * CHUNK)
            b_end, dec = _gla_chunk_decay(la[rows], tri)
            kend = (k_ref[rows, :] * dec).astype(CDT)
            upd = jnp.zeros((GLA_HEAD_V, GLA_DK), F32)
            for h in range(GLA_HEADS):
                vh = v_ref[rows, h * GLA_HEAD_V:(h + 1) * GLA_HEAD_V]
                upd = upd + mk[h] * _dg(vh, kend, _TN)
            s_new = jnp.exp(b_end) * s_scr[...] + upd
            s_scr[...] = s_new
            st_ref[ci] = s_new
            s_c = s_new.astype(CDT)
            for h in range(GLA_HEADS):
                o_h = _dg(qs[rows] * mk[h], s_c, _NT)
                oraw_ref[rows, h * GLA_HEAD_V:(h + 1) * GLA_HEAD_V] = o_h
        for h in range(GLA_HEADS):
            cols = slice(h * GLA_HEAD_V, (h + 1) * GLA_HEAD_V)
            o_h = oraw_ref[:, cols]
            rs = lax.rsqrt(jnp.mean(o_h * o_h, axis=-1, keepdims=True) + EPS)
            rg = r_ref[:, cols]
            o_ref[:, cols] = (o_h * rs * gg_ref[...] * (rg * _sigmoid(rg))).astype(o_ref.dtype)

    full = lambda shape: pl.BlockSpec(shape, lambda i: tuple(0 for _ in shape))
    return pl.pallas_call(
        body, name="gla_fwd", grid=(T // tm,),
        in_specs=_gla_specs(tm, lambda i: i) + [full((128, GLA_DK)), full((1, GLA_DK)), full((1, GLA_HEAD_V))],
        out_specs=[pl.BlockSpec((tm, GLA_DV), lambda i: (i, 0)), pl.BlockSpec((tm, GLA_DV), lambda i: (i, 0)),
                   pl.BlockSpec((ncs, GLA_HEAD_V, GLA_DK), lambda i: (i, 0, 0))],
        out_shape=[jax.ShapeDtypeStruct((T, GLA_DV), CDT), jax.ShapeDtypeStruct((T, GLA_DV), F32),
                   jax.ShapeDtypeStruct((T // CHUNK, GLA_HEAD_V, GLA_DK), F32)],
        scratch_shapes=[pltpu.VMEM((GLA_HEAD_V, GLA_DK), F32)],
        compiler_params=_cparams(("arbitrary",)),
    )(proj, proj, proj, proj, proj, wg2p, b_gate, g_gla)


def _gla_bwd(d_o, proj, oraw, states, wg2p, b_gate, g_gla, *, tm=256):
    T = proj.shape[0]
    nt = T // tm
    ncs = tm // CHUNK
    scale = GLA_HEAD_K ** -0.5

    def body(do_ref, q_ref, k_ref, v_ref, r_ref, glr_ref, oraw_ref, st_ref, stp_ref, wg_ref, bg_ref, gg_ref,
             dgla_ref, dglr_ref, dwg_ref, dbg_ref, dgg_ref, ds_scr, dy_scr, dor_scr):
        i = pl.program_id(0)
        tile = nt - 1 - i

        @pl.when(i == 0)
        def _():
            ds_scr[...] = jnp.zeros_like(ds_scr)
            dwg_ref[...] = jnp.zeros_like(dwg_ref)
            dbg_ref[...] = jnp.zeros_like(dbg_ref)
            dgg_ref[...] = jnp.zeros_like(dgg_ref)

        mk = _gla_head_masks(GLA_DK, GLA_HEAD_K)
        rr = lax.broadcasted_iota(jnp.int32, (CHUNK, CHUNK), 0)
        cc = lax.broadcasted_iota(jnp.int32, (CHUNK, CHUNK), 1)
        tri = (cc <= rr).astype(BF16)
        tri_t = (cc >= rr).astype(BF16)
        last_row = (lax.broadcasted_iota(jnp.int32, (CHUNK, 1), 0) == CHUNK - 1).astype(F32)

        dgg = jnp.zeros((1, GLA_HEAD_V), F32)
        for h in range(GLA_HEADS):
            cols = slice(h * GLA_HEAD_V, (h + 1) * GLA_HEAD_V)
            o_h = oraw_ref[:, cols]
            rs = lax.rsqrt(jnp.mean(o_h * o_h, axis=-1, keepdims=True) + EPS)
            rg = r_ref[:, cols]
            sg = _sigmoid(rg)
            dov = do_ref[:, cols]
            on = o_h * rs * gg_ref[...]
            d_on = dov * (rg * sg)
            dgla_ref[:, 2 * GLA_DK + GLA_DV + h * GLA_HEAD_V:2 * GLA_DK + GLA_DV + (h + 1) * GLA_HEAD_V] = (
                dov * on * (sg * (1.0 + rg * (1.0 - sg)))).astype(dgla_ref.dtype)
            dgg = dgg + jnp.sum(d_on * o_h * rs, axis=0, keepdims=True)
            uu = d_on * gg_ref[...]
            dor_scr[:, cols] = rs * uu - o_h * (rs * rs * rs * jnp.mean(uu * o_h, axis=-1, keepdims=True))
        dgg_ref[...] += dgg

        y = _dg(glr_ref[...], wg_ref[...], _NN) + bg_ref[...]
        la = _log_sigmoid(y) / GLA_GATE_NORMALIZER
        qs = q_ref[...] * scale
        for ci in reversed(range(ncs)):
            rows = slice(ci * CHUNK, (ci + 1) * CHUNK)
            b_end, dec = _gla_chunk_decay(la[rows], tri)
            decay = jnp.exp(b_end)
            kend = k_ref[rows, :] * dec
            kend_c = kend.astype(CDT)
            s_c = st_ref[ci].astype(CDT)
            if ci > 0:
                s_prev = st_ref[ci - 1]
            else:
                s_prev = jnp.where(tile > 0, stp_ref[0], 0.0)
            dqs = jnp.zeros((CHUNK, GLA_DK), F32)
            dst = ds_scr[...]
            for h in range(GLA_HEADS):
                do_h = dor_scr[rows, h * GLA_HEAD_V:(h + 1) * GLA_HEAD_V].astype(CDT)
                dqs = dqs + mk[h] * _dg(do_h, s_c, _NN)
                dst = dst + mk[h] * _dg(do_h, qs[rows], _TN)
            d_decay = jnp.sum(dst * s_prev, axis=0, keepdims=True)
            ds_scr[...] = decay * dst
            dst_c = dst.astype(CDT)
            dkend = jnp.zeros((CHUNK, GLA_DK), F32)
            for h in range(GLA_HEADS):
                cols = slice(h * GLA_HEAD_V, (h + 1) * GLA_HEAD_V)
                dv_h = _dg(kend * mk[h], dst_c, _NT)
                dgla_ref[rows, 2 * GLA_DK + h * GLA_HEAD_V:2 * GLA_DK + (h + 1) * GLA_HEAD_V] = dv_h.astype(dgla_ref.dtype)
                dkend = dkend + mk[h] * _dg(v_ref[rows, cols], dst_c, _NN)
            dgla_ref[rows, 0:GLA_DK] = (dqs * scale).astype(dgla_ref.dtype)
            dgla_ref[rows, GLA_DK:2 * GLA_DK] = (dkend * dec).astype(dgla_ref.dtype)
            mm = dkend * kend
            db_end = jnp.sum(mm, axis=0, keepdims=True) + d_decay * decay
            dbc = last_row * db_end - mm
            dla = _dot_exact_lhs(tri_t, dbc, 3)
            dy_scr[rows, :] = dla * (1.0 / GLA_GATE_NORMALIZER) * _sigmoid(-y[rows])
        dy = dy_scr[...]
        dbg_ref[...] += jnp.sum(dy, axis=0, keepdims=True)
        dwg_ref[...] += _dg(glr_ref[...], dy, _TN)
        dglr_ref[...] = _dg(dy, wg_ref[...], _NT).astype(dglr_ref.dtype)

    rev = lambda i: nt - 1 - i
    full = lambda shape: pl.BlockSpec(shape, lambda i: tuple(0 for _ in shape))
    st_spec = pl.BlockSpec((ncs, GLA_HEAD_V, GLA_DK), lambda i: (rev(i), 0, 0))
    stp_spec = pl.BlockSpec((1, GLA_HEAD_V, GLA_DK), lambda i: (jnp.maximum(rev(i) * ncs - 1, 0), 0, 0))
    return pl.pallas_call(
        body, name="gla_bwd", grid=(nt,),
        in_specs=[pl.BlockSpec((tm, GLA_DV), lambda i: (rev(i), 0))] + _gla_specs(tm, rev)
        + [pl.BlockSpec((tm, GLA_DV), lambda i: (rev(i), 0)), st_spec, stp_spec,
           full((128, GLA_DK)), full((1, GLA_DK)), full((1, GLA_HEAD_V))],
        out_specs=[pl.BlockSpec((tm, 2 * GLA_DK + 2 * GLA_DV), lambda i: (rev(i), 0)),
                   pl.BlockSpec((tm, 128), lambda i: (rev(i), 0)),
                   full((128, GLA_DK)), full((1, GLA_DK)), full((1, GLA_HEAD_V))],
        out_shape=[jax.ShapeDtypeStruct((T, 2 * GLA_DK + 2 * GLA_DV), CDT), jax.ShapeDtypeStruct((T, 128), CDT),
                   jax.ShapeDtypeStruct((128, GLA_DK), F32), jax.ShapeDtypeStruct((1, GLA_DK), F32),
                   jax.ShapeDtypeStruct((1, GLA_HEAD_V), F32)],
        scratch_shapes=[pltpu.VMEM((GLA_HEAD_V, GLA_DK), F32), pltpu.VMEM((tm, GLA_DK), F32),
                        pltpu.VMEM((tm, GLA_DV), F32)],
        compiler_params=_cparams(("arbitrary",)),
    )(d_o, proj, proj, proj, proj, proj, oraw, states, states, wg2p, b_gate, g_gla)


def _head_mean_matrix():
    r = lax.broadcasted_iota(jnp.int32, (SB_D, SB_D), 0) // SB_HEAD_DIM
    c = lax.broadcasted_iota(jnp.int32, (SB_D, SB_D), 1) // SB_HEAD_DIM
    return jnp.where(r == c, 1.0 / SB_HEAD_DIM, 0.0).astype(BF16)


def _sb_prep(proj, gq, gk, *, tm=256):
    T = proj.shape[0]
    scale = SB_HEAD_DIM ** -0.5

    def body(q_ref, k_ref, v_ref, gq_ref, gk_ref, qn_ref, kn_ref, vb_ref):
        hm = _head_mean_matrix()
        qv, kv = q_ref[...], k_ref[...]
        rq = lax.rsqrt(_dot_exact_rhs(qv * qv, hm, 3) + EPS)
        rk = lax.rsqrt(_dot_exact_rhs(kv * kv, hm, 3) + EPS)
        qn_ref[...] = (qv * rq * gq_ref[...] * scale).astype(qn_ref.dtype)
        kn_ref[...] = (kv * rk * gk_ref[...]).astype(kn_ref.dtype)
        vb_ref[...] = v_ref[...].astype(vb_ref.dtype)

    col = lambda j: pl.BlockSpec((tm, SB_D), lambda i: (i, j))
    vec = pl.BlockSpec((1, SB_D), lambda i: (0, 0))
    out = pl.BlockSpec((tm, SB_D), lambda i: (i, 0))
    return pl.pallas_call(
        body, name="sb_prep", grid=(T // tm,),
        in_specs=[col(3), col(4), col(5), vec, vec], out_specs=[out, out, out],
        out_shape=[jax.ShapeDtypeStruct((T, SB_D), CDT)] * 3,
        compiler_params=_cparams(("parallel",)),
    )(proj, proj, proj, gq, gk)


def _sb_prep_bwd(dqn, dkn, dv, proj, gq, gk, *, tm=256):
    T = proj.shape[0]
    scale = SB_HEAD_DIM ** -0.5

    def body(dqn_ref, dkn_ref, dv_ref, q_ref, k_ref, gq_ref, gk_ref, dsb_ref, dgq_ref, dgk_ref):
        i = pl.program_id(0)

        @pl.when(i == 0)
        def _():
            dgq_ref[...] = jnp.zeros_like(dgq_ref)
            dgk_ref[...] = jnp.zeros_like(dgk_ref)

        hm = _head_mean_matrix()

        def one(dn_ref, x_ref, g_ref, dg_ref, sc, lo):
            xv = x_ref[...]
            dnv = dn_ref[...] * sc
            r = lax.rsqrt(_dot_exact_rhs(xv * xv, hm, 3) + EPS)
            u = dnv * g_ref[...]
            dot = _dot_exact_rhs(u * xv, hm, 3)
            dsb_ref[:, lo:lo + SB_D] = (r * u - xv * (r * r * r * dot)).astype(dsb_ref.dtype)
            dg_ref[...] += jnp.sum(dnv * xv * r, axis=0, keepdims=True)

        one(dqn_ref, q_ref, gq_ref, dgq_ref, scale, 0)
        one(dkn_ref, k_ref, gk_ref, dgk_ref, 1.0, SB_D)
        dsb_ref[:, 2 * SB_D:3 * SB_D] = dv_ref[...].astype(dsb_ref.dtype)

    col = lambda j: pl.BlockSpec((tm, SB_D), lambda i: (i, j))
    vec = pl.BlockSpec((1, SB_D), lambda i: (0, 0))
    row = pl.BlockSpec((tm, SB_D), lambda i: (i, 0))
    return pl.pallas_call(
        body, name="sb_prep_bwd", grid=(T // tm,),
        in_specs=[row, row, row, col(3), col(4), vec, vec],
        out_specs=[pl.BlockSpec((tm, 3 * SB_D), lambda i: (i, 0)), vec, vec],
        out_shape=[jax.ShapeDtypeStruct((T, 3 * SB_D), CDT), jax.ShapeDtypeStruct((1, SB_D), F32),
                   jax.ShapeDtypeStruct((1, SB_D), F32)],
        compiler_params=_cparams(("arbitrary",)),
    )(dqn, dkn, dv, proj, proj, gq, gk)


def _sb_masks():
    lane = lax.broadcasted_iota(jnp.int32, (1, 128), 1)
    m = [lane < SB_HEAD_DIM, lane >= SB_HEAD_DIM]
    return m, [x.astype(F32) for x in m]


def _sb_fwd(qn, kn, vb):
    T = qn.shape[0]
    nq = T // SB_TILE
    B, P = SB_TILE, SB_PAIR
    hs = range(2)

    def body(q_ref, k_ref, v_ref, o_ref, l_ref, acc_ref):
        qb = pl.program_id(1)
        m, mf = _sb_masks()
        row = lax.broadcasted_iota(jnp.int32, (B, B), 0)
        col = lax.broadcasted_iota(jnp.int32, (B, B), 1)
        later = (row > col).astype(BF16)
        past = col < row
        q2 = q_ref[...]
        qm = [jnp.where(m[h], q2, jnp.zeros_like(q2)) for h in hs]
        acc_ref[...] = jnp.zeros_like(acc_ref)

        def keys(kb):
            return k_ref[pl.ds(pl.multiple_of(kb * B, B), B), :]

        def values(kb):
            return v_ref[pl.ds(pl.multiple_of(kb * B, B), B), :]

        def scores(kb):
            k2 = keys(kb)
            return [_dg(qm[h], k2, _NT) for h in hs]

        def run(tiles, R, diag):
            zs, cum, rsum = {}, {}, {}
            for t, (z, _) in enumerate(tiles):
                for h in hs:
                    sp = _softplus(z[h])
                    lk = jnp.where(past, sp, 0.0) if diag else sp
                    zs[t, h] = z[h] - sp
                    cum[t, h] = _dot_exact_rhs(lk, later, SB_SPLIT_LK)
                    rsum[t, h] = jnp.sum(lk, axis=1, keepdims=True)
            R = list(R)
            for t, (_, kb) in enumerate(tiles):
                v2 = values(kb)
                for h in hs:
                    w = jnp.exp(zs[t, h] - (cum[t, h] + R[h]))
                    if diag:
                        w = jnp.where(past, w, 0.0)
                    acc_ref[h] += _dg(w, v2, _NN)
                R = [R[h] + rsum[t, h] for h in hs]
            return tuple(R)

        zero = jnp.zeros((B, 1), F32)
        R = run([(scores(qb), qb)], (zero, zero), True)
        R = lax.cond(qb % 2 == 1, lambda r: run([(scores(qb - 1), qb - 1)], r, False), lambda r: r, R)
        npairs = qb // 2

        def pair(i, carry):
            r, za, zb = carry[:2], carry[2:4], carry[4:6]
            ka = 2 * (npairs - 1 - i) + 1
            nxt = (scores(jnp.maximum(ka - 2, 0)), scores(jnp.maximum(ka - 3, 0)))
            r = run([(za, ka), (zb, ka - 1)], r, False)
            return (*r, *nxt[0], *nxt[1])

        first = jnp.maximum(2 * npairs - 1, 0)
        out = lax.fori_loop(0, npairs, pair, (*R, *scores(first), *scores(jnp.maximum(first - 1, 0))))
        R = out[:2]
        o_ref[...] = (acc_ref[0] * mf[0] + acc_ref[1] * mf[1]).astype(o_ref.dtype)
        l_ref[0] = R[0] * mf[0] + R[1] * mf[1]

    slab = pl.BlockSpec((T, P), lambda hp, qb: (0, hp))
    blk = pl.BlockSpec((B, P), lambda hp, qb: (qb, hp))
    return pl.pallas_call(
        body, name="sb_fwd", grid=(SB_D // P, nq),
        in_specs=[blk, slab, slab],
        out_specs=[blk, pl.BlockSpec((1, B, P), lambda hp, qb: (hp, qb, 0))],
        out_shape=[jax.ShapeDtypeStruct((T, SB_D), CDT), jax.ShapeDtypeStruct((SB_D // P, T, P), F32)],
        scratch_shapes=[pltpu.VMEM((2, B, P), F32)],
        compiler_params=_cparams(("arbitrary", "arbitrary")),
    )(qn, kn, vb)


def _sb_bwd(d_o, qn, kn, vb, lsum):
    T = qn.shape[0]
    nq = T // SB_TILE
    B, P = SB_TILE, SB_PAIR
    hs = range(2)

    def body(do_ref, q_ref, k_ref, v_ref, l_ref, dq_ref, dk_ref, dv_ref, dqacc_ref):
        qb = pl.program_id(1)

        @pl.when(qb == 0)
        def _():
            dk_ref[...] = jnp.zeros_like(dk_ref)
            dv_ref[...] = jnp.zeros_like(dv_ref)

        m, mf = _sb_masks()
        row = lax.broadcasted_iota(jnp.int32, (B, B), 0)
        col = lax.broadcasted_iota(jnp.int32, (B, B), 1)
        upto = (row <= col).astype(BF16)
        before = (row < col).astype(BF16)
        past = col < row
        q2 = q_ref[...]
        qm = [jnp.where(m[h], q2, jnp.zeros_like(q2)) for h in hs]
        do2 = do_ref[...]
        dom = [jnp.where(m[h], do2, 0.0).astype(CDT) for h in hs]
        lb = l_ref[0]
        ltot = [lb[:, 0:1], lb[:, SB_HEAD_DIM:SB_HEAD_DIM + 1]]
        dqacc_ref[...] = jnp.zeros_like(dqacc_ref)

        def rows(kb):
            return pl.ds(pl.multiple_of(kb * B, B), B)

        def scores(kb):
            k2, v2 = k_ref[rows(kb), :], v_ref[rows(kb), :]
            return [_dg(qm[h], k2, _NT) for h in hs] + [_dg(dom[h], v2, _NT) for h in hs]

        def run(tiles, carry, diag):
            Ps, Pg = list(carry[0]), list(carry[1])
            zs, sp_, cum, rest = {}, {}, {}, {}
            for t, tl in enumerate(tiles):
                for h in hs:
                    sp = _softplus(tl[h])
                    lk = jnp.where(past, sp, 0.0) if diag else sp
                    zs[t, h], sp_[t, h] = tl[h] - sp, sp
                    cum[t, h] = _dot_exact_rhs(lk, upto, SB_SPLIT_LK)
                    rest[t, h] = ltot[h] - Ps[h]
                    Ps[h] = Ps[h] + jnp.sum(lk, axis=1, keepdims=True)
            w, g, gx = {}, {}, {}
            for t, tl in enumerate(tiles):
                for h in hs:
                    wt = jnp.exp(zs[t, h] - (rest[t, h] - cum[t, h]))
                    if diag:
                        wt = jnp.where(past, wt, 0.0)
                    w[t, h] = wt
                    g[t, h] = wt * tl[2 + h]
                    gx[t, h] = _dot_exact_rhs(g[t, h], before, SB_SPLIT_G) + Pg[h]
                    Pg[h] = Pg[h] + jnp.sum(g[t, h], axis=1, keepdims=True)
            for t, tl in enumerate(tiles):
                kb = tl[4]
                k2 = k_ref[rows(kb), :]
                for h in hs:
                    sneg = jnp.exp(-sp_[t, h])
                    dz = g[t, h] * sneg - (1.0 - sneg) * gx[t, h]
                    if diag:
                        dz = jnp.where(past, dz, 0.0)
                    dz_c = dz.astype(CDT)
                    dv_ref[rows(kb), :] += _dg(w[t, h], dom[h], _TN)
                    dk_ref[rows(kb), :] += _dg(dz_c, qm[h], _TN)
                    dqacc_ref[h] += _dg(dz_c, k2, _NN)
            return tuple(Ps), tuple(Pg)

        zero = jnp.zeros((B, 1), F32)
        npairs = qb // 2

        def pair(i, carry):
            sums, ta, tb = (carry[0:2], carry[2:4]), carry[4:8], carry[8:12]
            nxt = scores(jnp.minimum(2 * i + 2, qb)) + scores(jnp.minimum(2 * i + 3, qb))
            Ps, Pg = run([(*ta, 2 * i), (*tb, 2 * i + 1)], sums, False)
            return (*Ps, *Pg, *nxt)

        out = lax.fori_loop(0, npairs, pair, (zero, zero, zero, zero, *scores(0), *scores(jnp.minimum(1, qb))))
        sums = (out[0:2], out[2:4])
        sums = lax.cond(qb % 2 == 1, lambda s: run([(*scores(qb - 1), qb - 1)], s, False), lambda s: s, sums)
        run([(*scores(qb), qb)], sums, True)
        dq_ref[...] = dqacc_ref[0] * mf[0] + dqacc_ref[1] * mf[1]

    slab = pl.BlockSpec((T, P), lambda hp, qb: (0, hp))
    blk = pl.BlockSpec((B, P), lambda hp, qb: (qb, hp))
    return pl.pallas_call(
        body, name="sb_bwd", grid=(SB_D // P, nq),
        in_specs=[blk, blk, slab, slab, pl.BlockSpec((1, B, P), lambda hp, qb: (hp, qb, 0))],
        out_specs=[blk, slab, slab],
        out_shape=[jax.ShapeDtypeStruct((T, SB_D), F32)] * 3,
        scratch_shapes=[pltpu.VMEM((2, B, P), F32)],
        compiler_params=_cparams(("arbitrary", "arbitrary")),
    )(d_o, qn, kn, vb, lsum)


def _regroup_in_rows(wt):
    cut = 2 * GLA_DK + 2 * GLA_DV
    pad = jnp.zeros((IN_PAD - IN_WIDTH, wt.shape[1]), wt.dtype)
    return jnp.concatenate([wt[:cut], wt[cut + GLA_GATE_RANK:], wt[cut:cut + GLA_GATE_RANK], pad], axis=0)


def _ungroup_in_rows(gt):
    cut = 2 * GLA_DK + 2 * GLA_DV
    return jnp.concatenate([gt[:cut], gt[3072:3072 + GLA_GATE_RANK], gt[cut:3072]], axis=0)


def _colsum(v, *, name, tm=512):
    T, C = v.shape

    def body(v_ref, o_ref):
        i = pl.program_id(0)
        part = jnp.sum(v_ref[...], axis=0, keepdims=True)

        @pl.when(i == 0)
        def _():
            o_ref[...] = part

        @pl.when(i > 0)
        def _():
            o_ref[...] += part

    return pl.pallas_call(
        body, name=name, grid=(T // tm,),
        in_specs=[pl.BlockSpec((tm, C), lambda i: (i, 0))], out_specs=pl.BlockSpec((1, C), lambda i: (0, 0)),
        out_shape=jax.ShapeDtypeStruct((1, C), F32),
        compiler_params=_cparams(("arbitrary",)),
    )(v)


def _ffn_fwd(h, g_norm, wgu_t, wd, tag):
    hf = _rms_fwd(h, g_norm, name=f"ffn{tag}_norm")
    ab, s = _swiglu_up(hf, wgu_t, name=f"ffn{tag}_up")
    h_out = _matmul(s, wd, mode='nn', out_dtype=F32, name=f"ffn{tag}_down", tm=512, tn=D_MODEL, tk=D_FF, residual=h)
    return h_out, (hf, ab, s)


def _ffn_bwd(dh, dh_c, h_in, g_norm, wgu_t, wd, saved, tag):
    hf, ab, s = saved
    dwd = _matmul(s, dh_c, mode='tn', out_dtype=F32, name=f"ffn{tag}_dwd", tm=D_FF // 2, tn=D_MODEL, tk=TK_TOKENS)
    dab = _swiglu_dact(dh_c, wd, ab, name=f"ffn{tag}_dact")
    dwgu_t = _matmul(dab, hf, mode='tn', out_dtype=F32, name=f"ffn{tag}_dwgu", tm=D_FF // 2, tn=D_MODEL, tk=TK_TOKENS)
    dhf = _matmul(dab, wgu_t, mode='nn', out_dtype=F32, name=f"ffn{tag}_dhf", tm=256, tn=D_MODEL, tk=2 * D_FF)
    dh_in, dh_in_c, dg = _rms_bwd(dhf, h_in, g_norm, dh, name=f"ffn{tag}_dnorm")
    return dh_in, dh_in_c, dwgu_t, dwd, dg


def _local_step(x, tgt, W):
    row = lambda v: v.reshape(1, -1)
    win_p = _regroup_in_rows(W['hy_w_in_t'])
    wg2p = jnp.pad(W['hy_w_gate2'], ((0, 128 - GLA_GATE_RANK), (0, 0)))
    b_gate = row(W['hy_b_gate'])
    g_gla = row(W['hy_gla_norm'])
    gq = jnp.tile(W['hy_sb_q_norm'].reshape(-1), SB_D // SB_HEAD_DIM).reshape(1, SB_D)
    gk = jnp.tile(W['hy_sb_k_norm'].reshape(-1), SB_D // SB_HEAD_DIM).reshape(1, SB_D)
    w_out = W['hy_w_out']
    wgu = W['ffn_wgu_t']
    wd = [W['ffn_w_down'][l] for l in range(2)]
    w_dw = jnp.pad(W['cv_w_dw'], ((0, CONV_HALO - CONV_WIDTH), (0, 0)))
    mixn = [row(W['mix_norm'][l]) for l in range(2)]
    ffnn = [row(W['ffn_norm'][l]) for l in range(2)]

    hn0 = _rms_fwd(x, mixn[0], name="mix0_norm")
    proj = _matmul(hn0, win_p, mode='nt', out_dtype=F32, name="hy_in", tm=256, tn=IN_PAD, tk=D_MODEL)
    o_gla, o_raw, states = _gla_fwd(proj, wg2p, b_gate, g_gla)
    qn, kn, vb = _sb_prep(proj, gq, gk)
    o_sb, lsum = _sb_fwd(qn, kn, vb)
    o_mix = jnp.concatenate([o_gla, o_sb], axis=1)
    h1 = _matmul(o_mix, w_out, mode='nn', out_dtype=F32, name="hy_out", tm=512, tn=D_MODEL, tk=D_MODEL, residual=x)
    h2, ffn0_saved = _ffn_fwd(h1, ffnn[0], wgu[0], wd[0], 0)
    hn1 = _rms_fwd(h2, mixn[1], name="mix1_norm")
    a_cv = _matmul(hn1, W['cv_w_pw1_t'], mode='nt', out_dtype=F32, name="cv_pw1", tm=512, tn=2 * D_MODEL, tk=D_MODEL,
                   bias=row(W['cv_b_pw1']))
    s_cv, u_cv, c_cv = _conv_fwd(a_cv, w_dw, row(W['cv_b_dw']), row(W['cv_ln_g']), row(W['cv_ln_b']))
    h3 = _matmul(s_cv, W['cv_w_pw2'], mode='nn', out_dtype=F32, name="cv_pw2", tm=512, tn=D_MODEL, tk=D_MODEL,
                 bias=row(W['cv_b_pw2']), residual=h2)
    h4, ffn1_saved = _ffn_fwd(h3, ffnn[1], wgu[1], wd[1], 1)
    sq_err, dy, dy_c = _loss_head(h4, tgt)

    G = {}
    dh3, dh3_c, dwgu1, dwd1, dg_ffn1 = _ffn_bwd(dy, dy_c, h3, ffnn[1], wgu[1], wd[1], ffn1_saved, 1)
    G['cv_b_pw2'] = _colsum(dh3, name="cv_db2")
    G['cv_w_pw2'] = _matmul(s_cv, dh3_c, mode='tn', out_dtype=F32, name="cv_dw2", tm=D_MODEL, tn=D_MODEL, tk=TK_TOKENS)
    ds_cv = _matmul(dh3_c, W['cv_w_pw2'], mode='nt', out_dtype=F32, name="cv_ds", tm=512, tn=D_MODEL, tk=D_MODEL)
    da_cv, db1, dwdw, dbdw, dlng, dlnb = _conv_bwd(ds_cv, c_cv, u_cv, a_cv, w_dw, row(W['cv_ln_g']), row(W['cv_ln_b']))
    G['cv_b_pw1'] = db1
    G['cv_w_dw'] = dwdw[:CONV_WIDTH]
    G['cv_b_dw'], G['cv_ln_g'], G['cv_ln_b'] = dbdw, dlng, dlnb
    G['cv_w_pw1_t'] = _matmul(da_cv, hn1, mode='tn', out_dtype=F32, name="cv_dw1", tm=D_MODEL, tn=D_MODEL, tk=TK_TOKENS)
    dhn1 = _matmul(da_cv, W['cv_w_pw1_t'], mode='nn', out_dtype=F32, name="cv_dhn", tm=512, tn=D_MODEL, tk=2 * D_MODEL)
    dh2, dh2_c, dg_mix1 = _rms_bwd(dhn1, h2, mixn[1], dh3, name="mix1_dnorm")
    dh1, dh1_c, dwgu0, dwd0, dg_ffn0 = _ffn_bwd(dh2, dh2_c, h1, ffnn[0], wgu[0], wd[0], ffn0_saved, 0)
    G['hy_w_out'] = _matmul(o_mix, dh1_c, mode='tn', out_dtype=F32, name="hy_dwout", tm=D_MODEL, tn=D_MODEL, tk=TK_TOKENS)
    d_omix = _matmul(dh1_c, w_out, mode='nt', out_dtype=F32, name="hy_domix", tm=512, tn=D_MODEL, tk=D_MODEL)
    dgla, dglr, dwg2, dbg, dgg = _gla_bwd(d_omix[:, :GLA_DV], proj, o_raw, states, wg2p, b_gate, g_gla)
    dqn, dkn, dvs = _sb_bwd(d_omix[:, GLA_DV:], qn, kn, vb, lsum)
    dsb, dgq, dgk = _sb_prep_bwd(dqn, dkn, dvs, proj, gq, gk)
    dproj = jnp.concatenate([dgla, dsb, dglr], axis=1)
    dwin_p = _matmul(dproj, hn0, mode='tn', out_dtype=F32, name="hy_dwin", tm=IN_PAD // 5, tn=D_MODEL, tk=TK_TOKENS)
    dhn0 = _matmul(dproj, win_p, mode='nn', out_dtype=F32, name="hy_dhn", tm=256, tn=D_MODEL, tk=IN_PAD)
    dx, _, dg_mix0 = _rms_bwd(dhn0, x, mixn[0], dh1, name="mix0_dnorm")

    G['hy_w_in_t'] = _ungroup_in_rows(dwin_p)
    G['hy_w_gate2'] = dwg2[:GLA_GATE_RANK]
    G['hy_b_gate'] = dbg
    G['hy_gla_norm'] = dgg
    G['hy_sb_q_norm'] = dgq.reshape(SB_D // SB_HEAD_DIM, SB_HEAD_DIM).sum(axis=0, keepdims=True)
    G['hy_sb_k_norm'] = dgk.reshape(SB_D // SB_HEAD_DIM, SB_HEAD_DIM).sum(axis=0, keepdims=True)
    G['mix_norm'] = jnp.concatenate([dg_mix0, dg_mix1], axis=0)
    G['ffn_norm'] = jnp.concatenate([dg_ffn0, dg_ffn1], axis=0)
    G['ffn_wgu_t'] = [dwgu0, dwgu1]
    G['ffn_w_down'] = [dwd0, dwd1]
    return sq_err, dx, G


MESH_IDS = pl.DeviceIdType.MESH
N_PEER = N_DEV - 1


def _all_gather(blocks):
    n = len(blocks)

    def body(*refs):
        x_refs, out_refs = refs[:n], refs[n:2 * n]
        send_sems, recv_sems, local_sems = refs[2 * n:]
        x, y, c = lax.axis_index("x"), lax.axis_index("y"), lax.axis_index("c")
        me, sibling = (x, y, c), (x, y, 1 - c)
        chips = [(1 - x, y), (x, 1 - y), (1 - x, 1 - y)]

        def slot(a, px, py, pc):
            return out_refs[a].at[4 * px + 2 * py + pc]

        def copy(a, k, blk, to, src=None):
            return pltpu.make_async_remote_copy(
                src_ref=slot(a, *blk) if src is None else src, dst_ref=slot(a, *blk),
                send_sem=send_sems.at[a * N_PEER + k], recv_sem=recv_sems.at[a * N_PEER + k],
                device_id=to, device_id_type=MESH_IDS)

        mine = [pltpu.make_async_copy(x_refs[a], slot(a, *me), local_sems.at[a]) for a in range(n)]
        for cp in mine:
            cp.start()
        first = []
        for a in range(n):
            first.append(copy(a, 0, me, sibling, src=x_refs[a]))
            first += [copy(a, 1 + j, me, (*chip, c), src=x_refs[a]) for j, chip in enumerate(chips)]
        for cp in first:
            cp.start()
        passed = []
        for a in range(n):
            for j, chip in enumerate(chips):
                copy(a, 1 + j, (*chip, c), me).wait_recv()
                fwd = copy(a, 4 + j, (*chip, c), sibling)
                fwd.start()
                passed.append(fwd)
        for a in range(n):
            copy(a, 0, sibling, me).wait_recv()
            for j, chip in enumerate(chips):
                copy(a, 4 + j, (*chip, 1 - c), me).wait_recv()
        for cp in first + passed:
            cp.wait_send()
        for cp in mine:
            cp.wait()

    anyspec = pl.BlockSpec(memory_space=pl.ANY)
    return pl.pallas_call(
        body, name="fsdp_all_gather",
        out_shape=[jax.ShapeDtypeStruct((N_DEV,) + b.shape, b.dtype) for b in blocks],
        in_specs=[anyspec] * n, out_specs=[anyspec] * n,
        scratch_shapes=[pltpu.SemaphoreType.DMA((n * N_PEER,)), pltpu.SemaphoreType.DMA((n * N_PEER,)),
                        pltpu.SemaphoreType.DMA((n,))],
    )(*blocks)


def _scatter_exchange(sends):
    n = len(sends)

    def body(*refs):
        s_refs, r_refs = refs[:n], refs[n:2 * n]
        send_sems, recv_sems, local_sems = refs[2 * n:]
        x, y, c = lax.axis_index("x"), lax.axis_index("y"), lax.axis_index("c")
        me = 4 * x + 2 * y + c
        mine = [pltpu.make_async_copy(s_refs[a].at[me], r_refs[a].at[me], local_sems.at[a]) for a in range(n)]
        for cp in mine:
            cp.start()
        copies = []
        for a in range(n):
            for k in range(1, N_DEV):
                px, py, pc = x ^ ((k >> 2) & 1), y ^ ((k >> 1) & 1), c ^ (k & 1)
                cp = pltpu.make_async_remote_copy(
                    src_ref=s_refs[a].at[4 * px + 2 * py + pc], dst_ref=r_refs[a].at[me],
                    send_sem=send_sems.at[a * N_PEER + k - 1], recv_sem=recv_sems.at[a * N_PEER + k - 1],
                    device_id=(px, py, pc), device_id_type=MESH_IDS)
                cp.start()
                copies.append(cp)
        for cp in copies:
            cp.wait()
        for cp in mine:
            cp.wait()

    anyspec = pl.BlockSpec(memory_space=pl.ANY)
    return pl.pallas_call(
        body, name="fsdp_scatter_exchange",
        out_shape=[jax.ShapeDtypeStruct(s.shape, s.dtype) for s in sends],
        in_specs=[anyspec] * n, out_specs=[anyspec] * n,
        scratch_shapes=[pltpu.SemaphoreType.DMA((n * N_PEER,)), pltpu.SemaphoreType.DMA((n * N_PEER,)),
                        pltpu.SemaphoreType.DMA((n,))],
    )(*sends)


def _sum_contrib(recv, own, *, name, tr):
    _, R, C = recv.shape
    assert R % tr == 0

    def body(r_ref, own_ref, g_ref):
        me = 4 * lax.axis_index("x") + 2 * lax.axis_index("y") + lax.axis_index("c")
        g = jnp.zeros((tr, C), F32)
        for s in range(N_DEV):
            g = g + jnp.where(me == s, own_ref[...], r_ref[s].astype(F32))
        g_ref[...] = g

    row = pl.BlockSpec((tr, C), lambda i: (i, 0))
    return pl.pallas_call(
        body, name=name, grid=(R // tr,),
        in_specs=[pl.BlockSpec((N_DEV, tr, C), lambda i: (0, i, 0)), row], out_specs=row,
        out_shape=jax.ShapeDtypeStruct((R, C), F32),
        compiler_params=_cparams(("parallel",)),
    )(recv, own)


def _adamw(g, w, m, v, *, name, tr):
    R, C = g.shape
    assert R % tr == 0

    def body(g_ref, w_ref, m_ref, v_ref, d_ref, mo_ref, vo_ref):
        gv = g_ref[...]
        mn = ADAM_B1 * m_ref[...] + (1.0 - ADAM_B1) * gv
        vn = ADAM_B2 * v_ref[...] + (1.0 - ADAM_B2) * (gv * gv)
        m_hat = mn / (1.0 - ADAM_B1 ** ADAM_STEP)
        v_hat = vn / (1.0 - ADAM_B2 ** ADAM_STEP)
        d_ref[...] = -ADAM_LR * (m_hat / (jnp.sqrt(v_hat) + ADAM_EPS) + ADAM_WD * w_ref[...])
        mo_ref[...] = mn
        vo_ref[...] = vn

    row = pl.BlockSpec((tr, C), lambda i: (i, 0))
    return pl.pallas_call(
        body, name=name, grid=(R // tr,),
        in_specs=[row] * 4, out_specs=[row] * 3,
        out_shape=[jax.ShapeDtypeStruct((R, C), F32)] * 3,
        compiler_params=_cparams(("parallel",)),
    )(g, w, m, v)


SMALL_SHARDED = ('hy_w_gate2', 'cv_b_pw1', 'cv_w_dw', 'cv_b_dw', 'cv_ln_g', 'cv_ln_b', 'cv_b_pw2')
SMALL_REPLICATED = ('mix_norm', 'ffn_norm', 'hy_b_gate', 'hy_gla_norm', 'hy_sb_q_norm', 'hy_sb_k_norm')
LANES = 128


def _small_rows(n):
    return -(-n // (8 * LANES)) * 8


def _pack_small(parts, lead=()):
    out = []
    for p in parts:
        n = p.shape[-1]
        p = jnp.pad(p, [(0, 0)] * len(lead) + [(0, _small_rows(n) * LANES - n)])
        out.append(p.reshape(*lead, _small_rows(n), LANES))
    return jnp.concatenate(out, axis=len(lead))


def _unpack_small(packed, sizes, lead=()):
    out, r0 = [], 0
    for n in sizes:
        r = _small_rows(n)
        out.append(packed[..., r0:r0 + r, :].reshape(*lead, r * LANES)[..., :n])
        r0 += r
    return out


def _to_blocks(full, axis):
    shp = full.shape
    t = full.reshape(shp[:axis] + (N_DEV, shp[axis] // N_DEV) + shp[axis + 1:])
    return jnp.moveaxis(t, axis, 0)


def _from_blocks(blocks, axis):
    t = jnp.moveaxis(blocks, 0, axis)
    shp = t.shape
    return t.reshape(shp[:axis] + (shp[axis] * shp[axis + 1],) + shp[axis + 2:])


def kernel(x, mix_norm, ffn_norm, hy_w_in, hy_w_gate2, hy_b_gate, hy_gla_norm, hy_sb_q_norm, hy_sb_k_norm, hy_w_out, cv_w_pw1, cv_b_pw1, cv_w_dw, cv_b_dw, cv_ln_g, cv_ln_b, cv_w_pw2, cv_b_pw2, ffn_w_gate, ffn_w_up, ffn_w_down, loss_target, m_mix_norm, m_ffn_norm, m_hy_w_in, m_hy_w_gate2, m_hy_b_gate, m_hy_gla_norm, m_hy_sb_q_norm, m_hy_sb_k_norm, m_hy_w_out, m_cv_w_pw1, m_cv_b_pw1, m_cv_w_dw, m_cv_b_dw, m_cv_ln_g, m_cv_ln_b, m_cv_w_pw2, m_cv_b_pw2, m_ffn_w_gate, m_ffn_w_up, m_ffn_w_down, v_mix_norm, v_ffn_norm, v_hy_w_in, v_hy_w_gate2, v_hy_b_gate, v_hy_gla_norm, v_hy_sb_q_norm, v_hy_sb_k_norm, v_hy_w_out, v_cv_w_pw1, v_cv_b_pw1, v_cv_w_dw, v_cv_b_dw, v_cv_ln_g, v_cv_ln_b, v_cv_w_pw2, v_cv_b_pw2, v_ffn_w_gate, v_ffn_w_up, v_ffn_w_down):
    w_loc = dict(zip(WEIGHT_NAMES, (mix_norm, ffn_norm, hy_w_in, hy_w_gate2, hy_b_gate, hy_gla_norm, hy_sb_q_norm, hy_sb_k_norm, hy_w_out, cv_w_pw1, cv_b_pw1, cv_w_dw, cv_b_dw, cv_ln_g, cv_ln_b, cv_w_pw2, cv_b_pw2, ffn_w_gate, ffn_w_up, ffn_w_down)))
    m_loc = dict(zip(WEIGHT_NAMES, (m_mix_norm, m_ffn_norm, m_hy_w_in, m_hy_w_gate2, m_hy_b_gate, m_hy_gla_norm, m_hy_sb_q_norm, m_hy_sb_k_norm, m_hy_w_out, m_cv_w_pw1, m_cv_b_pw1, m_cv_w_dw, m_cv_b_dw, m_cv_ln_g, m_cv_ln_b, m_cv_w_pw2, m_cv_b_pw2, m_ffn_w_gate, m_ffn_w_up, m_ffn_w_down)))
    v_loc = dict(zip(WEIGHT_NAMES, (v_mix_norm, v_ffn_norm, v_hy_w_in, v_hy_w_gate2, v_hy_b_gate, v_hy_gla_norm, v_hy_sb_q_norm, v_hy_sb_k_norm, v_hy_w_out, v_cv_w_pw1, v_cv_b_pw1, v_cv_w_dw, v_cv_b_dw, v_cv_ln_g, v_cv_ln_b, v_cv_w_pw2, v_cv_b_pw2, v_ffn_w_gate, v_ffn_w_up, v_ffn_w_down)))

    Dm, F8 = D_MODEL, D_FF // N_DEV
    tr_ = lambda a: jnp.swapaxes(a, -1, -2)

    small_local = _pack_small([w_loc[n].reshape(-1) for n in SMALL_SHARDED])
    g_in, g_out, g_pw1, g_pw2, g_gate, g_up, g_down, g_small = _all_gather([
        tr_(hy_w_in[0]).astype(BF16),
        hy_w_out[0].astype(BF16),
        tr_(cv_w_pw1[0]).astype(BF16),
        cv_w_pw2[0].astype(BF16),
        tr_(ffn_w_gate).astype(BF16),
        tr_(ffn_w_up).astype(BF16),
        ffn_w_down.astype(BF16),
        small_local])
    small_sizes = [w_loc[n].size for n in SMALL_SHARDED]
    small_full = dict(zip(SMALL_SHARDED, _unpack_small(g_small, small_sizes, lead=(N_DEV,))))
    W = {n: w_loc[n] for n in SMALL_REPLICATED}
    W['hy_w_in_t'] = g_in.reshape(IN_WIDTH, Dm)
    W['hy_w_out'] = g_out.reshape(Dm, Dm)
    W['cv_w_pw1_t'] = g_pw1.reshape(2 * Dm, Dm)
    W['cv_w_pw2'] = g_pw2.reshape(Dm, Dm)
    W['ffn_wgu_t'] = [jnp.concatenate([g_gate[:, l].reshape(D_FF, Dm), g_up[:, l].reshape(D_FF, Dm)], axis=0)
                      for l in range(2)]
    W['ffn_w_down'] = jnp.stack([g_down[:, l].reshape(D_FF, Dm) for l in range(2)])
    W['hy_w_gate2'] = _from_blocks(small_full['hy_w_gate2'].reshape(N_DEV, GLA_GATE_RANK, GLA_DK // N_DEV), 1).astype(BF16)
    W['cv_w_dw'] = _from_blocks(small_full['cv_w_dw'].reshape(N_DEV, CONV_WIDTH, Dm // N_DEV), 1)
    for n in ('cv_b_pw1', 'cv_b_dw', 'cv_ln_g', 'cv_ln_b', 'cv_b_pw2'):
        W[n] = small_full[n].reshape(-1)

    sq_err, dx, G = _local_step(x[0], loss_target[0], W)

    own_f32 = [
        G['hy_w_in_t'].reshape(N_DEV, IN_WIDTH // N_DEV, Dm),
        G['hy_w_out'].reshape(N_DEV, Dm // N_DEV, Dm),
        G['cv_w_pw1_t'].reshape(N_DEV, 2 * Dm // N_DEV, Dm),
        G['cv_w_pw2'].reshape(N_DEV, Dm // N_DEV, Dm),
        jnp.stack([g.reshape(2, N_DEV, F8, Dm) for g in G['ffn_wgu_t']]).transpose(2, 0, 1, 3, 4).reshape(N_DEV, 4 * F8, Dm),
        jnp.stack([g.reshape(N_DEV, F8, Dm) for g in G['ffn_w_down']], axis=1).reshape(N_DEV, 2 * F8, Dm),
    ]
    small_parts = []
    for n in SMALL_SHARDED:
        axis = SHARD_AXIS[n] - 1
        shard = w_loc[n].shape[1:]
        full = shard[:axis] + (shard[axis] * N_DEV,) + shard[axis + 1:]
        small_parts.append(_to_blocks(G[n].reshape(full), axis).reshape(N_DEV, -1))
    for n in SMALL_REPLICATED:
        small_parts.append(jnp.broadcast_to(G[n].reshape(1, -1), (N_DEV, G[n].size)))
    small_parts.append(jnp.broadcast_to(sq_err.reshape(1, 1), (N_DEV, 1)))
    send_small = _pack_small(small_parts, lead=(N_DEV,))
    recv = _scatter_exchange([a.astype(BF16) for a in own_f32] + [send_small])
    me = 4 * lax.axis_index("x") + 2 * lax.axis_index("y") + lax.axis_index("c")
    own = [lax.dynamic_index_in_dim(a, me, 0, keepdims=False) for a in own_f32 + [send_small]]
    tags = ['hy_w_in', 'hy_w_out', 'cv_w_pw1', 'cv_w_pw2', 'ffn_wgu', 'ffn_w_down', 'small']
    rows_tile = [IN_WIDTH // N_DEV, Dm // N_DEV, 2 * Dm // N_DEV, Dm // N_DEV, F8, F8, recv[-1].shape[1]]
    gsum = [_sum_contrib(r, o, name=f"sum_{t}", tr=tr) for r, o, t, tr in zip(recv, own, tags, rows_tile)]

    grad = {}
    grad['hy_w_in'] = tr_(gsum[0])[None]
    grad['hy_w_out'] = gsum[1][None]
    grad['cv_w_pw1'] = tr_(gsum[2])[None]
    grad['cv_w_pw2'] = gsum[3][None]
    gu = gsum[4].reshape(2, 2, F8, Dm)
    grad['ffn_w_gate'] = tr_(gu[:, 0])
    grad['ffn_w_up'] = tr_(gu[:, 1])
    grad['ffn_w_down'] = gsum[5].reshape(2, F8, Dm)
    small_names = SMALL_SHARDED + SMALL_REPLICATED
    small_all = [w_loc[n].size for n in small_names]
    *small_grads, sq_sum = _unpack_small(gsum[6], small_all + [1])
    for n, a in zip(small_names, small_grads):
        grad[n] = a.reshape(w_loc[n].shape)
    loss = 0.5 / Dm * sq_sum[0]

    delta, new_m, new_v = {}, {}, {}
    view = {'hy_w_in': (Dm, 256), 'hy_w_out': (Dm // N_DEV, Dm // N_DEV), 'cv_w_pw1': (Dm, 256),
            'cv_w_pw2': (Dm // N_DEV, Dm // N_DEV), 'ffn_w_gate': (2 * Dm, 256), 'ffn_w_up': (2 * Dm, 256),
            'ffn_w_down': (2 * F8, F8)}
    for n, (rows, tr) in view.items():
        shp = w_loc[n].shape
        outs = _adamw(grad[n].reshape(rows, -1), w_loc[n].reshape(rows, -1), m_loc[n].reshape(rows, -1),
                      v_loc[n].reshape(rows, -1), name=f"adamw_{n}", tr=tr)
        delta[n], new_m[n], new_v[n] = (o.reshape(shp) for o in outs)
    packed = [_pack_small([d[n].reshape(-1) for n in small_names] + [jnp.zeros((1,), F32)]) for d in (w_loc, m_loc, v_loc)]
    outs = _adamw(gsum[6], *packed, name="adamw_small", tr=gsum[6].shape[0])
    for dst, o in zip((delta, new_m, new_v), outs):
        for n, a in zip(small_names, _unpack_small(o, small_all)):
            dst[n] = a.reshape(w_loc[n].shape)

    return (loss, dx[None], *[grad[n] for n in WEIGHT_NAMES], *[delta[n] for n in WEIGHT_NAMES],
            *[new_m[n] for n in WEIGHT_NAMES], *[new_v[n] for n in WEIGHT_NAMES])
```

```python
import jax
import jax.numpy as jnp
from jax import lax
from jax.experimental import pallas as pl
from jax.experimental.pallas import tpu as pltpu

F32 = jnp.float32
BF16 = jnp.bfloat16
CDT = jnp.bfloat16

D_MODEL = 1024
EPS = 1e-6
CHUNK = 64
GLA_HEADS = 4
GLA_HEAD_K = 64
GLA_HEAD_V = 128
GLA_DK = GLA_HEADS * GLA_HEAD_K
GLA_DV = GLA_HEADS * GLA_HEAD_V
GLA_GATE_RANK = 16
GLA_GATE_NORMALIZER = 16.0
SB_HEAD_DIM = 64
SB_D = 512
SB_TILE = 256
SB_PAIR = 128
SB_SPLIT_LK = 2
SB_SPLIT_G = 1
SB_DEAD = 120.0
IN_WIDTH = 3088
IN_PAD = 3200
CONV_WIDTH = 31
CONV_HALO = 32
D_FF = 2816
N_DEV = 8

ADAM_LR = 0.001
ADAM_B1 = 0.9
ADAM_B2 = 0.999
ADAM_EPS = 1e-08
ADAM_WD = 0.01
ADAM_STEP = 10

VMEM_LIMIT = 56 * 1024 * 1024
TK_TOKENS = 2048

WEIGHT_NAMES = ['mix_norm', 'ffn_norm', 'hy_w_in', 'hy_w_gate2', 'hy_b_gate', 'hy_gla_norm', 'hy_sb_q_norm',
                'hy_sb_k_norm', 'hy_w_out', 'cv_w_pw1', 'cv_b_pw1', 'cv_w_dw', 'cv_b_dw', 'cv_ln_g', 'cv_ln_b',
                'cv_w_pw2', 'cv_b_pw2', 'ffn_w_gate', 'ffn_w_up', 'ffn_w_down']
SHARD_AXIS = {'mix_norm': None, 'ffn_norm': None, 'hy_w_in': 2, 'hy_w_gate2': 2, 'hy_b_gate': None,
              'hy_gla_norm': None, 'hy_sb_q_norm': None, 'hy_sb_k_norm': None, 'hy_w_out': 1, 'cv_w_pw1': 2,
              'cv_b_pw1': 1, 'cv_w_dw': 2, 'cv_b_dw': 1, 'cv_ln_g': 1, 'cv_ln_b': 1, 'cv_w_pw2': 1, 'cv_b_pw2': 1,
              'ffn_w_gate': 2, 'ffn_w_up': 2, 'ffn_w_down': 1}


def _cparams(sem=None, vmem=VMEM_LIMIT):
    return pltpu.CompilerParams(dimension_semantics=sem, vmem_limit_bytes=vmem)


def _log_sigmoid(x):
    return jnp.minimum(x, 0.0) - jnp.log1p(jnp.exp(-jnp.abs(x)))


def _sigmoid(x):
    return 1.0 / (1.0 + jnp.exp(-x))


def _softplus(x):
    return jnp.maximum(x, 0.0) + jnp.log(1.0 + jnp.exp(-jnp.abs(x)))


def _split_bf16(x, n):
    parts = []
    rem = x
    for _ in range(n):
        p = rem.astype(BF16)
        parts.append(p)
        rem = rem - p.astype(F32)
    return parts


def _dot_exact_rhs(x, m, n):
    return sum(jnp.dot(p, m, preferred_element_type=F32) for p in _split_bf16(x, n))


def _dot_exact_lhs(m, x, n):
    return sum(jnp.dot(m, p, preferred_element_type=F32) for p in _split_bf16(x, n))


_NN = (((1,), (0,)), ((), ()))
_NT = (((1,), (1,)), ((), ()))
_TN = (((0,), (0,)), ((), ()))


def _dg(a, b, dn):
    return lax.dot_general(a.astype(CDT), b.astype(CDT), dn, preferred_element_type=F32)


def _matmul(a, b, *, mode, out_dtype, name, tm, tn, tk, bias=None, residual=None):
    if mode == 'nn':
        (M, K), (K2, N) = a.shape, b.shape
    elif mode == 'nt':
        (M, K), (N, K2) = a.shape, b.shape
    else:
        (K, M), (K2, N) = a.shape, b.shape
    assert K == K2 and M % tm == 0 and N % tn == 0 and K % tk == 0, (name, a.shape, b.shape, tm, tn, tk)
    nk = K // tk
    a_spec = pl.BlockSpec((tk, tm), lambda i, j, k: (k, i)) if mode == 'tn' else pl.BlockSpec((tm, tk), lambda i, j, k: (i, k))
    b_spec = pl.BlockSpec((tn, tk), lambda i, j, k: (j, k)) if mode == 'nt' else pl.BlockSpec((tk, tn), lambda i, j, k: (k, j))
    dn = {'nn': _NN, 'nt': _NT, 'tn': _TN}[mode]
    has_bias, has_res = bias is not None, residual is not None

    def body(*refs):
        a_ref, b_ref = refs[0], refs[1]
        pos = 2
        bias_ref = res_ref = None
        if has_bias:
            bias_ref = refs[pos]
            pos += 1
        if has_res:
            res_ref = refs[pos]
            pos += 1
        o_ref = refs[pos]
        acc_ref = refs[pos + 1] if nk > 1 else None
        p = _dg(a_ref[...], b_ref[...], dn)

        def finish(acc):
            if has_bias:
                acc = acc + bias_ref[...]
            if has_res:
                acc = res_ref[...] + acc
            o_ref[...] = acc.astype(o_ref.dtype)

        if nk == 1:
            finish(p)
        else:
            k = pl.program_id(2)

            @pl.when(k == 0)
            def _():
                acc_ref[...] = p

            @pl.when(k > 0)
            def _():
                acc_ref[...] += p

            @pl.when(k == nk - 1)
            def _():
                finish(acc_ref[...])

    in_specs = [a_spec, b_spec]
    args = [a, b]
    if has_bias:
        in_specs.append(pl.BlockSpec((1, tn), lambda i, j, k: (0, j)))
        args.append(bias)
    if has_res:
        in_specs.append(pl.BlockSpec((tm, tn), lambda i, j, k: (i, j)))
        args.append(residual)
    return pl.pallas_call(
        body, name=name, grid=(M // tm, N // tn, nk),
        in_specs=in_specs, out_specs=pl.BlockSpec((tm, tn), lambda i, j, k: (i, j)),
        out_shape=jax.ShapeDtypeStruct((M, N), out_dtype),
        scratch_shapes=[pltpu.VMEM((tm, tn), F32)] if nk > 1 else [],
        compiler_params=_cparams(("parallel", "parallel", "arbitrary")),
    )(*args)


def _rms_fwd(x, g, *, name, tm=512):
    T, Dm = x.shape

    def body(x_ref, g_ref, o_ref):
        xv = x_ref[...]
        r = lax.rsqrt(jnp.mean(xv * xv, axis=-1, keepdims=True) + EPS)
        o_ref[...] = (xv * r * g_ref[...]).astype(o_ref.dtype)

    return pl.pallas_call(
        body, name=name, grid=(T // tm,),
        in_specs=[pl.BlockSpec((tm, Dm), lambda i: (i, 0)), pl.BlockSpec((1, Dm), lambda i: (0, 0))],
        out_specs=pl.BlockSpec((tm, Dm), lambda i: (i, 0)),
        out_shape=jax.ShapeDtypeStruct((T, Dm), CDT),
        compiler_params=_cparams(("parallel",)),
    )(x, g)


def _rms_bwd(dy, x, g, resid, *, name, tm=512):
    T, Dm = x.shape

    def body(dy_ref, x_ref, g_ref, res_ref, dx_ref, dxb_ref, dg_ref):
        i = pl.program_id(0)
        xv, dyv = x_ref[...], dy_ref[...]
        r = lax.rsqrt(jnp.mean(xv * xv, axis=-1, keepdims=True) + EPS)
        u = dyv * g_ref[...]
        dot = jnp.mean(u * xv, axis=-1, keepdims=True)
        dx = res_ref[...] + (r * u - xv * (r * r * r * dot))
        dx_ref[...] = dx
        dxb_ref[...] = dx.astype(dxb_ref.dtype)
        part = jnp.sum(dyv * xv * r, axis=0, keepdims=True)

        @pl.when(i == 0)
        def _():
            dg_ref[...] = part

        @pl.when(i > 0)
        def _():
            dg_ref[...] += part

    row = pl.BlockSpec((tm, Dm), lambda i: (i, 0))
    vec = pl.BlockSpec((1, Dm), lambda i: (0, 0))
    return pl.pallas_call(
        body, name=name, grid=(T // tm,),
        in_specs=[row, row, vec, row], out_specs=[row, row, vec],
        out_shape=[jax.ShapeDtypeStruct((T, Dm), F32), jax.ShapeDtypeStruct((T, Dm), CDT),
                   jax.ShapeDtypeStruct((1, Dm), F32)],
        compiler_params=_cparams(("arbitrary",)),
    )(dy, x, g, resid)


def _loss_head(y, tgt, *, tm=512):
    T, Dm = y.shape

    def body(y_ref, t_ref, s_ref, dy_ref, dyb_ref):
        i = pl.program_id(0)
        e = y_ref[...] - t_ref[...]
        dy = e * (1.0 / Dm)
        dy_ref[...] = dy
        dyb_ref[...] = dy.astype(dyb_ref.dtype)
        part = jnp.sum(jnp.sum(e * e, axis=1, keepdims=True), axis=0, keepdims=True)

        @pl.when(i == 0)
        def _():
            s_ref[...] = part

        @pl.when(i > 0)
        def _():
            s_ref[...] += part

    row = pl.BlockSpec((tm, Dm), lambda i: (i, 0))
    return pl.pallas_call(
        body, name="loss_head", grid=(T // tm,),
        in_specs=[row, row], out_specs=[pl.BlockSpec((1, 1), lambda i: (0, 0)), row, row],
        out_shape=[jax.ShapeDtypeStruct((1, 1), F32), jax.ShapeDtypeStruct((T, Dm), F32),
                   jax.ShapeDtypeStruct((T, Dm), CDT)],
        compiler_params=_cparams(("arbitrary",)),
    )(y, tgt)


def _swiglu_up(hf, wgu_t, *, name, tm=256):
    T, Dm = hf.shape
    F2 = wgu_t.shape[0]
    F = F2 // 2

    def body(h_ref, w_ref, ab_ref, s_ref):
        ab_ref[...] = _dg(h_ref[...], w_ref[...], _NT)
        a = ab_ref[:, :F]
        b = ab_ref[:, F:]
        s_ref[...] = (a * _sigmoid(a) * b).astype(s_ref.dtype)

    return pl.pallas_call(
        body, name=name, grid=(T // tm,),
        in_specs=[pl.BlockSpec((tm, Dm), lambda i: (i, 0)), pl.BlockSpec((F2, Dm), lambda i: (0, 0))],
        out_specs=[pl.BlockSpec((tm, F2), lambda i: (i, 0)), pl.BlockSpec((tm, F), lambda i: (i, 0))],
        out_shape=[jax.ShapeDtypeStruct((T, F2), F32), jax.ShapeDtypeStruct((T, F), CDT)],
        compiler_params=_cparams(("parallel",)),
    )(hf, wgu_t)


def _swiglu_dact(dh_c, wd, ab, *, name, tm=256):
    T, Dm = dh_c.shape
    F2 = ab.shape[1]
    F = F2 // 2

    def body(dh_ref, w_ref, ab_ref, o_ref):
        dsv = _dg(dh_ref[...], w_ref[...], _NT)
        a = ab_ref[:, :F]
        b = ab_ref[:, F:]
        sg = _sigmoid(a)
        o_ref[:, :F] = (dsv * b * (sg * (1.0 + a * (1.0 - sg)))).astype(o_ref.dtype)
        o_ref[:, F:] = (dsv * (a * sg)).astype(o_ref.dtype)

    return pl.pallas_call(
        body, name=name, grid=(T // tm,),
        in_specs=[pl.BlockSpec((tm, Dm), lambda i: (i, 0)), pl.BlockSpec((F, Dm), lambda i: (0, 0)),
                  pl.BlockSpec((tm, F2), lambda i: (i, 0))],
        out_specs=pl.BlockSpec((tm, F2), lambda i: (i, 0)),
        out_shape=jax.ShapeDtypeStruct((T, F2), CDT),
        compiler_params=_cparams(("parallel",)),
    )(dh_c, wd, ab)


SUBLANES = 8


def _shifted_copies(buf, shifted, tm):
    n = tm + CONV_HALO - SUBLANES
    for b in range(1, SUBLANES):
        shifted[b - 1] = buf[pl.ds(b, n), :]


def _rows_from(buf, shifted, offset, tm):
    a, b = divmod(offset, SUBLANES)
    if b == 0:
        return buf[pl.ds(SUBLANES * a, tm), :]
    return shifted[b - 1, pl.ds(SUBLANES * a, tm), :]


def _conv_fwd(a, w_dw, b_dw, ln_g, ln_b, *, tm=256):
    T = a.shape[0]
    Dm = D_MODEL

    def body(a_ref, w_ref, bdw_ref, g_ref, b_ref, s_ref, u_ref, c_ref, ubuf, shifted):
        i = pl.program_id(0)

        @pl.when(i == 0)
        def _():
            ubuf[0:CONV_HALO, :] = jnp.zeros((CONV_HALO, Dm), F32)

        @pl.when(i > 0)
        def _():
            ubuf[0:CONV_HALO, :] = ubuf[tm:tm + CONV_HALO, :]

        u = a_ref[:, :Dm] * _sigmoid(a_ref[:, Dm:])
        ubuf[CONV_HALO:CONV_HALO + tm, :] = u
        u_ref[...] = u
        _shifted_copies(ubuf, shifted, tm)
        acc = jnp.zeros((tm, Dm), F32) + bdw_ref[...]
        for k in range(CONV_WIDTH):
            acc = acc + w_ref[k:k + 1, :] * _rows_from(ubuf, shifted, CONV_HALO - (CONV_WIDTH - 1) + k, tm)
        c_ref[...] = acc
        mu = jnp.mean(acc, axis=-1, keepdims=True)
        cen = acc - mu
        var = jnp.mean(cen * cen, axis=-1, keepdims=True)
        l = cen * lax.rsqrt(var + EPS) * g_ref[...] + b_ref[...]
        s_ref[...] = (l * _sigmoid(l)).astype(s_ref.dtype)

    row = pl.BlockSpec((tm, Dm), lambda i: (i, 0))
    vec = pl.BlockSpec((1, Dm), lambda i: (0, 0))
    return pl.pallas_call(
        body, name="conv_fwd", grid=(T // tm,),
        in_specs=[pl.BlockSpec((tm, 2 * Dm), lambda i: (i, 0)), pl.BlockSpec((CONV_HALO, Dm), lambda i: (0, 0)), vec, vec, vec],
        out_specs=[row, row, row],
        out_shape=[jax.ShapeDtypeStruct((T, Dm), CDT), jax.ShapeDtypeStruct((T, Dm), F32), jax.ShapeDtypeStruct((T, Dm), F32)],
        scratch_shapes=[pltpu.VMEM((tm + CONV_HALO, Dm), F32), pltpu.VMEM((SUBLANES - 1, tm + CONV_HALO - SUBLANES, Dm), F32)],
        compiler_params=_cparams(("arbitrary",)),
    )(a, w_dw, b_dw, ln_g, ln_b)


def _conv_bwd(ds, c, u, a, w_dw, ln_g, ln_b, *, tm=256):
    T = a.shape[0]
    Dm = D_MODEL
    nt = T // tm

    def body(ds_ref, c_ref, u_ref, a_ref, w_ref, g_ref, b_ref,
             da_ref, db1_ref, dw_ref, dbdw_ref, dg_ref, dbln_ref, dcbuf, shifted):
        i = pl.program_id(0)

        @pl.when(i == 0)
        def _():
            dcbuf[tm:tm + CONV_HALO, :] = jnp.zeros((CONV_HALO, Dm), F32)
            db1_ref[...] = jnp.zeros_like(db1_ref)
            dw_ref[...] = jnp.zeros_like(dw_ref)
            dbdw_ref[...] = jnp.zeros_like(dbdw_ref)
            dg_ref[...] = jnp.zeros_like(dg_ref)
            dbln_ref[...] = jnp.zeros_like(dbln_ref)

        @pl.when(i > 0)
        def _():
            dcbuf[tm:tm + CONV_HALO, :] = dcbuf[0:CONV_HALO, :]

        cv = c_ref[...]
        mu = jnp.mean(cv, axis=-1, keepdims=True)
        cen = cv - mu
        var = jnp.mean(cen * cen, axis=-1, keepdims=True)
        rstd = lax.rsqrt(var + EPS)
        n = cen * rstd
        l = n * g_ref[...] + b_ref[...]
        sg = _sigmoid(l)
        dl = ds_ref[...] * (sg * (1.0 + l * (1.0 - sg)))
        dg_ref[...] += jnp.sum(dl * n, axis=0, keepdims=True)
        dbln_ref[...] += jnp.sum(dl, axis=0, keepdims=True)
        dn = dl * g_ref[...]
        dc = rstd * (dn - jnp.mean(dn, axis=-1, keepdims=True) - n * jnp.mean(dn * n, axis=-1, keepdims=True))
        dbdw_ref[...] += jnp.sum(dc, axis=0, keepdims=True)
        dcbuf[0:tm, :] = dc
        _shifted_copies(dcbuf, shifted, tm)
        uv = u_ref[...]
        du = jnp.zeros((tm, Dm), F32)
        for k in range(CONV_WIDTH):
            slab = _rows_from(dcbuf, shifted, CONV_WIDTH - 1 - k, tm)
            du = du + w_ref[k:k + 1, :] * slab
            dw_ref[k:k + 1, :] += jnp.sum(slab * uv, axis=0, keepdims=True)
        a1 = a_ref[:, :Dm]
        s2 = _sigmoid(a_ref[:, Dm:])
        da1 = du * s2
        da2 = du * a1 * (s2 * (1.0 - s2))
        da_ref[:, :Dm] = da1.astype(da_ref.dtype)
        da_ref[:, Dm:] = da2.astype(da_ref.dtype)
        db1_ref[:, :Dm] += jnp.sum(da1, axis=0, keepdims=True)
        db1_ref[:, Dm:] += jnp.sum(da2, axis=0, keepdims=True)

    rev = lambda i: (nt - 1 - i, 0)
    row = pl.BlockSpec((tm, Dm), rev)
    row2 = pl.BlockSpec((tm, 2 * Dm), rev)
    vec = pl.BlockSpec((1, Dm), lambda i: (0, 0))
    vec2 = pl.BlockSpec((1, 2 * Dm), lambda i: (0, 0))
    taps = pl.BlockSpec((CONV_HALO, Dm), lambda i: (0, 0))
    return pl.pallas_call(
        body, name="conv_bwd", grid=(nt,),
        in_specs=[row, row, row, row2, taps, vec, vec],
        out_specs=[row2, vec2, taps, vec, vec, vec],
        out_shape=[jax.ShapeDtypeStruct((T, 2 * Dm), CDT), jax.ShapeDtypeStruct((1, 2 * Dm), F32),
                   jax.ShapeDtypeStruct((CONV_HALO, Dm), F32), jax.ShapeDtypeStruct((1, Dm), F32),
                   jax.ShapeDtypeStruct((1, Dm), F32), jax.ShapeDtypeStruct((1, Dm), F32)],
        scratch_shapes=[pltpu.VMEM((tm + CONV_HALO, Dm), F32), pltpu.VMEM((SUBLANES - 1, tm + CONV_HALO - SUBLANES, Dm), F32)],
        compiler_params=_cparams(("arbitrary",)),
    )(ds, c, u, a, w_dw, ln_g, ln_b)


def _gla_head_masks(width, per_head):
    lane = lax.broadcasted_iota(jnp.int32, (1, width), 1)
    return [((lane >= h * per_head) & (lane < (h + 1) * per_head)).astype(F32) for h in range(GLA_HEADS)]


def _gla_specs(tm, order):
    return [pl.BlockSpec((tm, GLA_DK), lambda i: (order(i), 0)),
            pl.BlockSpec((tm, GLA_DK), lambda i: (order(i), 1)),
            pl.BlockSpec((tm, GLA_DV), lambda i: (order(i), 1)),
            pl.BlockSpec((tm, GLA_DV), lambda i: (order(i), 2)),
            pl.BlockSpec((tm, 128), lambda i: (order(i), 3072 // 128))]


def _gla_chunk_decay(la_c, tri):
    bc = _dot_exact_lhs(tri, la_c, 3)
    b_end = bc[CHUNK - 1:CHUNK, :]
    return b_end, jnp.exp(b_end - bc)


def _gla_fwd(proj, wg2p, b_gate, g_gla, *, tm=256):
    T = proj.shape[0]
    ncs = tm // CHUNK
    scale = GLA_HEAD_K ** -0.5

    def body(q_ref, k_ref, v_ref, r_ref, glr_ref, wg_ref, bg_ref, gg_ref, o_ref, oraw_ref, st_ref, s_scr):
        i = pl.program_id(0)

        @pl.when(i == 0)
        def _():
            s_scr[...] = jnp.zeros_like(s_scr)

        mk = _gla_head_masks(GLA_DK, GLA_HEAD_K)
        rr = lax.broadcasted_iota(jnp.int32, (CHUNK, CHUNK), 0)
        cc = lax.broadcasted_iota(jnp.int32, (CHUNK, CHUNK), 1)
        tri = (cc <= rr).astype(BF16)
        y = _dg(glr_ref[...], wg_ref[...], _NN) + bg_ref[...]
        la = _log_sigmoid(y) / GLA_GATE_NORMALIZER
        qs = q_ref[...] * scale
        for ci in range(ncs):
            rows = slice(ci * CHUNK, (ci + 1) * CHUNK)
            b_end, dec = _gla_chunk_decay(la[rows], tri)
            kend = (k_ref[rows, :] * dec).astype(CDT)
            upd = jnp.zeros((GLA_HEAD_V, GLA_DK), F32)
            for h in range(GLA_HEADS):
                vh = v_ref[rows, h * GLA_HEAD_V:(h + 1) * GLA_HEAD_V]
                upd = upd + mk[h] * _dg(vh, kend, _TN)
            s_new = jnp.exp(b_end) * s_scr[...] + upd
            s_scr[...] = s_new
            st_ref[ci] = s_new
            s_c = s_new.astype(CDT)
            for h in range(GLA_HEADS):
                o_h = _dg(qs[rows] * mk[h], s_c, _NT)
                oraw_ref[rows, h * GLA_HEAD_V:(h + 1) * GLA_HEAD_V] = o_h
        for h in range(GLA_HEADS):
            cols = slice(h * GLA_HEAD_V, (h + 1) * GLA_HEAD_V)
            o_h = oraw_ref[:, cols]
            rs = lax.rsqrt(jnp.mean(o_h * o_h, axis=-1, keepdims=True) + EPS)
            rg = r_ref[:, cols]
            o_ref[:, cols] = (o_h * rs * gg_ref[...] * (rg * _sigmoid(rg))).astype(o_ref.dtype)

    full = lambda shape: pl.BlockSpec(shape, lambda i: tuple(0 for _ in shape))
    return pl.pallas_call(
        body, name="gla_fwd", grid=(T // tm,),
        in_specs=_gla_specs(tm, lambda i: i) + [full((128, GLA_DK)), full((1, GLA_DK)), full((1, GLA_HEAD_V))],
        out_specs=[pl.BlockSpec((tm, GLA_DV), lambda i: (i, 0)), pl.BlockSpec((tm, GLA_DV), lambda i: (i, 0)),
                   pl.BlockSpec((ncs, GLA_HEAD_V, GLA_DK), lambda i: (i, 0, 0))],
        out_shape=[jax.ShapeDtypeStruct((T, GLA_DV), CDT), jax.ShapeDtypeStruct((T, GLA_DV), F32),
                   jax.ShapeDtypeStruct((T // CHUNK, GLA_HEAD_V, GLA_DK), F32)],
        scratch_shapes=[pltpu.VMEM((GLA_HEAD_V, GLA_DK), F32)],
        compiler_params=_cparams(("arbitrary",)),
    )(proj, proj, proj, proj, proj, wg2p, b_gate, g_gla)


def _gla_bwd(d_o, proj, oraw, states, wg2p, b_gate, g_gla, *, tm=256):
    T = proj.shape[0]
    nt = T // tm
    ncs = tm // CHUNK
    scale = GLA_HEAD_K ** -0.5

    def body(do_ref, q_ref, k_ref, v_ref, r_ref, glr_ref, oraw_ref, st_ref, stp_ref, wg_ref, bg_ref, gg_ref,
             dgla_ref, dglr_ref, dwg_ref, dbg_ref, dgg_ref, ds_scr, dy_scr, dor_scr):
        i = pl.program_id(0)
        tile = nt - 1 - i

        @pl.when(i == 0)
        def _():
            ds_scr[...] = jnp.zeros_like(ds_scr)
            dwg_ref[...] = jnp.zeros_like(dwg_ref)
            dbg_ref[...] = jnp.zeros_like(dbg_ref)
            dgg_ref[...] = jnp.zeros_like(dgg_ref)

        mk = _gla_head_masks(GLA_DK, GLA_HEAD_K)
        rr = lax.broadcasted_iota(jnp.int32, (CHUNK, CHUNK), 0)
        cc = lax.broadcasted_iota(jnp.int32, (CHUNK, CHUNK), 1)
        tri = (cc <= rr).astype(BF16)
        tri_t = (cc >= rr).astype(BF16)
        last_row = (lax.broadcasted_iota(jnp.int32, (CHUNK, 1), 0) == CHUNK - 1).astype(F32)

        dgg = jnp.zeros((1, GLA_HEAD_V), F32)
        for h in range(GLA_HEADS):
            cols = slice(h * GLA_HEAD_V, (h + 1) * GLA_HEAD_V)
            o_h = oraw_ref[:, cols]
            rs = lax.rsqrt(jnp.mean(o_h * o_h, axis=-1, keepdims=True) + EPS)
            rg = r_ref[:, cols]
            sg = _sigmoid(rg)
            dov = do_ref[:, cols]
            on = o_h * rs * gg_ref[...]
            d_on = dov * (rg * sg)
            dgla_ref[:, 2 * GLA_DK + GLA_DV + h * GLA_HEAD_V:2 * GLA_DK + GLA_DV + (h + 1) * GLA_HEAD_V] = (
                dov * on * (sg * (1.0 + rg * (1.0 - sg)))).astype(dgla_ref.dtype)
            dgg = dgg + jnp.sum(d_on * o_h * rs, axis=0, keepdims=True)
            uu = d_on * gg_ref[...]
            dor_scr[:, cols] = rs * uu - o_h * (rs * rs * rs * jnp.mean(uu * o_h, axis=-1, keepdims=True))
        dgg_ref[...] += dgg

        y = _dg(glr_ref[...], wg_ref[...], _NN) + bg_ref[...]
        la = _log_sigmoid(y) / GLA_GATE_NORMALIZER
        qs = q_ref[...] * scale
        for ci in reversed(range(ncs)):
            rows = slice(ci * CHUNK, (ci + 1) * CHUNK)
            b_end, dec = _gla_chunk_decay(la[rows], tri)
            decay = jnp.exp(b_end)
            kend = k_ref[rows, :] * dec
            kend_c = kend.astype(CDT)
            s_c = st_ref[ci].astype(CDT)
            if ci > 0:
                s_prev = st_ref[ci - 1]
            else:
                s_prev = jnp.where(tile > 0, stp_ref[0], 0.0)
            dqs = jnp.zeros((CHUNK, GLA_DK), F32)
            dst = ds_scr[...]
            for h in range(GLA_HEADS):
                do_h = dor_scr[rows, h * GLA_HEAD_V:(h + 1) * GLA_HEAD_V].astype(CDT)
                dqs = dqs + mk[h] * _dg(do_h, s_c, _NN)
                dst = dst + mk[h] * _dg(do_h, qs[rows], _TN)
            d_decay = jnp.sum(dst * s_prev, axis=0, keepdims=True)
            ds_scr[...] = decay * dst
            dst_c = dst.astype(CDT)
            dkend = jnp.zeros((CHUNK, GLA_DK), F32)
            for h in range(GLA_HEADS):
                cols = slice(h * GLA_HEAD_V, (h + 1) * GLA_HEAD_V)
                dv_h = _dg(kend * mk[h], dst_c, _NT)
                dgla_ref[rows, 2 * GLA_DK + h * GLA_HEAD_V:2 * GLA_DK + (h + 1) * GLA_HEAD_V] = dv_h.astype(dgla_ref.dtype)
                dkend = dkend + mk[h] * _dg(v_ref[rows, cols], dst_c, _NN)
            dgla_ref[rows, 0:GLA_DK] = (dqs * scale).astype(dgla_ref.dtype)
            dgla_ref[rows, GLA_DK:2 * GLA_DK] = (dkend * dec).astype(dgla_ref.dtype)
            mm = dkend * kend
            db_end = jnp.sum(mm, axis=0, keepdims=True) + d_decay * decay
            dbc = last_row * db_end - mm
            dla = _dot_exact_lhs(tri_t, dbc, 3)
            dy_scr[rows, :] = dla * (1.0 / GLA_GATE_NORMALIZER) * _sigmoid(-y[rows])
        dy = dy_scr[...]
        dbg_ref[...] += jnp.sum(dy, axis=0, keepdims=True)
        dwg_ref[...] += _dg(glr_ref[...], dy, _TN)
        dglr_ref[...] = _dg(dy, wg_ref[...], _NT).astype(dglr_ref.dtype)

    rev = lambda i: nt - 1 - i
    full = lambda shape: pl.BlockSpec(shape, lambda i: tuple(0 for _ in shape))
    st_spec = pl.BlockSpec((ncs, GLA_HEAD_V, GLA_DK), lambda i: (rev(i), 0, 0))
    stp_spec = pl.BlockSpec((1, GLA_HEAD_V, GLA_DK), lambda i: (jnp.maximum(rev(i) * ncs - 1, 0), 0, 0))
    return pl.pallas_call(
        body, name="gla_bwd", grid=(nt,),
        in_specs=[pl.BlockSpec((tm, GLA_DV), lambda i: (rev(i), 0))] + _gla_specs(tm, rev)
        + [pl.BlockSpec((tm, GLA_DV), lambda i: (rev(i), 0)), st_spec, stp_spec,
           full((128, GLA_DK)), full((1, GLA_DK)), full((1, GLA_HEAD_V))],
        out_specs=[pl.BlockSpec((tm, 2 * GLA_DK + 2 * GLA_DV), lambda i: (rev(i), 0)),
                   pl.BlockSpec((tm, 128), lambda i: (rev(i), 0)),
                   full((128, GLA_DK)), full((1, GLA_DK)), full((1, GLA_HEAD_V))],
        out_shape=[jax.ShapeDtypeStruct((T, 2 * GLA_DK + 2 * GLA_DV), CDT), jax.ShapeDtypeStruct((T, 128), CDT),
                   jax.ShapeDtypeStruct((128, GLA_DK), F32), jax.ShapeDtypeStruct((1, GLA_DK), F32),
                   jax.ShapeDtypeStruct((1, GLA_HEAD_V), F32)],
        scratch_shapes=[pltpu.VMEM((GLA_HEAD_V, GLA_DK), F32), pltpu.VMEM((tm, GLA_DK), F32),
                        pltpu.VMEM((tm, GLA_DV), F32)],
        compiler_params=_cparams(("arbitrary",)),
    )(d_o, proj, proj, proj, proj, proj, oraw, states, states, wg2p, b_gate, g_gla)


def _head_mean_matrix():
    r = lax.broadcasted_iota(jnp.int32, (SB_D, SB_D), 0) // SB_HEAD_DIM
    c = lax.broadcasted_iota(jnp.int32, (SB_D, SB_D), 1) // SB_HEAD_DIM
    return jnp.where(r == c, 1.0 / SB_HEAD_DIM, 0.0).astype(BF16)


def _sb_prep(proj, gq, gk, *, tm=256):
    T = proj.shape[0]
    scale = SB_HEAD_DIM ** -0.5

    def body(q_ref, k_ref, v_ref, gq_ref, gk_ref, qn_ref, kn_ref, vb_ref):
        hm = _head_mean_matrix()
        qv, kv = q_ref[...], k_ref[...]
        rq = lax.rsqrt(_dot_exact_rhs(qv * qv, hm, 3) + EPS)
        rk = lax.rsqrt(_dot_exact_rhs(kv * kv, hm, 3) + EPS)
        qn_ref[...] = (qv * rq * gq_ref[...] * scale).astype(qn_ref.dtype)
        kn_ref[...] = (kv * rk * gk_ref[...]).astype(kn_ref.dtype)
        vb_ref[...] = v_ref[...].astype(vb_ref.dtype)

    col = lambda j: pl.BlockSpec((tm, SB_D), lambda i: (i, j))
    vec = pl.BlockSpec((1, SB_D), lambda i: (0, 0))
    out = pl.BlockSpec((tm, SB_D), lambda i: (i, 0))
    return pl.pallas_call(
        body, name="sb_prep", grid=(T // tm,),
        in_specs=[col(3), col(4), col(5), vec, vec], out_specs=[out, out, out],
        out_shape=[jax.ShapeDtypeStruct((T, SB_D), CDT)] * 3,
        compiler_params=_cparams(("parallel",)),
    )(proj, proj, proj, gq, gk)


def _sb_prep_bwd(dqn, dkn, dv, proj, gq, gk, *, tm=256):
    T = proj.shape[0]
    scale = SB_HEAD_DIM ** -0.5

    def body(dqn_ref, dkn_ref, dv_ref, q_ref, k_ref, gq_ref, gk_ref, dsb_ref, dgq_ref, dgk_ref):
        i = pl.program_id(0)

        @pl.when(i == 0)
        def _():
            dgq_ref[...] = jnp.zeros_like(dgq_ref)
            dgk_ref[...] = jnp.zeros_like(dgk_ref)

        hm = _head_mean_matrix()

        def one(dn_ref, x_ref, g_ref, dg_ref, sc, lo):
            xv = x_ref[...]
            dnv = dn_ref[...] * sc
            r = lax.rsqrt(_dot_exact_rhs(xv * xv, hm, 3) + EPS)
            u = dnv * g_ref[...]
            dot = _dot_exact_rhs(u * xv, hm, 3)
            dsb_ref[:, lo:lo + SB_D] = (r * u - xv * (r * r * r * dot)).astype(dsb_ref.dtype)
            dg_ref[...] += jnp.sum(dnv * xv * r, axis=0, keepdims=True)

        one(dqn_ref, q_ref, gq_ref, dgq_ref, scale, 0)
        one(dkn_ref, k_ref, gk_ref, dgk_ref, 1.0, SB_D)
        dsb_ref[:, 2 * SB_D:3 * SB_D] = dv_ref[...].astype(dsb_ref.dtype)

    col = lambda j: pl.BlockSpec((tm, SB_D), lambda i: (i, j))
    vec = pl.BlockSpec((1, SB_D), lambda i: (0, 0))
    row = pl.BlockSpec((tm, SB_D), lambda i: (i, 0))
    return pl.pallas_call(
        body, name="sb_prep_bwd", grid=(T // tm,),
        in_specs=[row, row, row, col(3), col(4), vec, vec],
        out_specs=[pl.BlockSpec((tm, 3 * SB_D), lambda i: (i, 0)), vec, vec],
        out_shape=[jax.ShapeDtypeStruct((T, 3 * SB_D), CDT), jax.ShapeDtypeStruct((1, SB_D), F32),
                   jax.ShapeDtypeStruct((1, SB_D), F32)],
        compiler_params=_cparams(("arbitrary",)),
    )(dqn, dkn, dv, proj, proj, gq, gk)


def _sb_masks():
    lane = lax.broadcasted_iota(jnp.int32, (1, 128), 1)
    m = [lane < SB_HEAD_DIM, lane >= SB_HEAD_DIM]
    return m, [x.astype(F32) for x in m]


def _sb_fwd(qn, kn, vb):
    T = qn.shape[0]
    nq = T // SB_TILE
    B, P = SB_TILE, SB_PAIR
    hs = range(2)

    def body(q_ref, k_ref, v_ref, o_ref, l_ref, done_ref, acc_ref):
        qb = pl.program_id(1)
        m, mf = _sb_masks()
        row = lax.broadcasted_iota(jnp.int32, (B, B), 0)
        col = lax.broadcasted_iota(jnp.int32, (B, B), 1)
        later = (row > col).astype(BF16)
        past = col < row
        q2 = q_ref[...]
        qm = [jnp.where(m[h], q2, jnp.zeros_like(q2)) for h in hs]
        acc_ref[...] = jnp.zeros_like(acc_ref)

        def keys(kb):
            return k_ref[pl.ds(pl.multiple_of(kb * B, B), B), :]

        def values(kb):
            return v_ref[pl.ds(pl.multiple_of(kb * B, B), B), :]

        def scores(kb):
            k2 = keys(kb)
            return [_dg(qm[h], k2, _NT) for h in hs]

        def run(tiles, R, diag):
            zs, cum, rsum = {}, {}, {}
            for t, (z, _) in enumerate(tiles):
                for h in hs:
                    sp = _softplus(z[h])
                    lk = jnp.where(past, sp, 0.0) if diag else sp
                    zs[t, h] = z[h] - sp
                    cum[t, h] = _dot_exact_rhs(lk, later, SB_SPLIT_LK)
                    rsum[t, h] = jnp.sum(lk, axis=1, keepdims=True)
            R = list(R)
            for t, (_, kb) in enumerate(tiles):
                v2 = values(kb)
                for h in hs:
                    w = jnp.exp(zs[t, h] - (cum[t, h] + R[h]))
                    if diag:
                        w = jnp.where(past, w, 0.0)
                    acc_ref[h] += _dg(w, v2, _NN)
                R = [R[h] + rsum[t, h] for h in hs]
            return tuple(R)

        zero = jnp.zeros((B, 1), F32)
        R = run([(scores(qb), qb)], (zero, zero), True)
        R = lax.cond(qb % 2 == 1, lambda r: run([(scores(qb - 1), qb - 1)], r, False), lambda r: r, R)
        npairs = qb // 2

        def live(r):
            return (jnp.minimum(jnp.min(r[0]), jnp.min(r[1])) < SB_DEAD).astype(jnp.int32)

        def pair(carry):
            i, r, za, zb = carry[0], carry[2:4], carry[4:6], carry[6:8]
            ka = 2 * (npairs - 1 - i) + 1
            nxt = (scores(jnp.maximum(ka - 2, 0)), scores(jnp.maximum(ka - 3, 0)))
            r = run([(za, ka), (zb, ka - 1)], r, False)
            return (i + 1, live(r), *r, *nxt[0], *nxt[1])

        first = jnp.maximum(2 * npairs - 1, 0)
        out = lax.while_loop(lambda c: (c[0] < npairs) & (c[1] > 0), pair,
                             (jnp.int32(0), live(R), *R, *scores(first), *scores(jnp.maximum(first - 1, 0))))
        R = out[2:4]
        done_ref[pl.program_id(0), qb] = out[0]
        o_ref[...] = (acc_ref[0] * mf[0] + acc_ref[1] * mf[1]).astype(o_ref.dtype)
        l_ref[0] = R[0] * mf[0] + R[1] * mf[1]

    slab = pl.BlockSpec((T, P), lambda hp, qb: (0, hp))
    blk = pl.BlockSpec((B, P), lambda hp, qb: (qb, hp))
    return pl.pallas_call(
        body, name="sb_fwd", grid=(SB_D // P, nq),
        in_specs=[blk, slab, slab],
        out_specs=[blk, pl.BlockSpec((1, B, P), lambda hp, qb: (hp, qb, 0)), pl.BlockSpec(memory_space=pltpu.SMEM)],
        out_shape=[jax.ShapeDtypeStruct((T, SB_D), CDT), jax.ShapeDtypeStruct((SB_D // P, T, P), F32),
                   jax.ShapeDtypeStruct((SB_D // P, nq), jnp.int32)],
        scratch_shapes=[pltpu.VMEM((2, B, P), F32)],
        compiler_params=_cparams(("arbitrary", "arbitrary")),
    )(qn, kn, vb)


def _sb_bwd(d_o, qn, kn, vb, lsum, done):
    T = qn.shape[0]
    nq = T // SB_TILE
    B, P = SB_TILE, SB_PAIR
    hs = range(2)

    def body(do_ref, q_ref, k_ref, v_ref, l_ref, done_ref, dq_ref, dk_ref, dv_ref, dqacc_ref):
        qb = pl.program_id(1)

        @pl.when(qb == 0)
        def _():
            dk_ref[...] = jnp.zeros_like(dk_ref)
            dv_ref[...] = jnp.zeros_like(dv_ref)

        m, mf = _sb_masks()
        row = lax.broadcasted_iota(jnp.int32, (B, B), 0)
        col = lax.broadcasted_iota(jnp.int32, (B, B), 1)
        upto = (row <= col).astype(BF16)
        before = (row < col).astype(BF16)
        past = col < row
        q2 = q_ref[...]
        qm = [jnp.where(m[h], q2, jnp.zeros_like(q2)) for h in hs]
        do2 = do_ref[...]
        dom = [jnp.where(m[h], do2, 0.0).astype(CDT) for h in hs]
        lb = l_ref[0]
        ltot = [lb[:, 0:1], lb[:, SB_HEAD_DIM:SB_HEAD_DIM + 1]]
        dqacc_ref[...] = jnp.zeros_like(dqacc_ref)

        def rows(kb):
            return pl.ds(pl.multiple_of(kb * B, B), B)

        def scores(kb):
            k2, v2 = k_ref[rows(kb), :], v_ref[rows(kb), :]
            return [_dg(qm[h], k2, _NT) for h in hs] + [_dg(dom[h], v2, _NT) for h in hs]

        def run(tiles, carry, diag):
            Ps, Pg = list(carry[0]), list(carry[1])
            zs, sp_, cum, rest = {}, {}, {}, {}
            for t, tl in enumerate(tiles):
                for h in hs:
                    sp = _softplus(tl[h])
                    lk = jnp.where(past, sp, 0.0) if diag else sp
                    zs[t, h], sp_[t, h] = tl[h] - sp, sp
                    cum[t, h] = _dot_exact_rhs(lk, upto, SB_SPLIT_LK)
                    rest[t, h] = ltot[h] - Ps[h]
                    Ps[h] = Ps[h] + jnp.sum(lk, axis=1, keepdims=True)
            w, g, gx = {}, {}, {}
            for t, tl in enumerate(tiles):
                for h in hs:
                    wt = jnp.exp(zs[t, h] - (rest[t, h] - cum[t, h]))
                    if diag:
                        wt = jnp.where(past, wt, 0.0)
                    w[t, h] = wt
                    g[t, h] = wt * tl[2 + h]
                    gx[t, h] = _dot_exact_rhs(g[t, h], before, SB_SPLIT_G) + Pg[h]
                    Pg[h] = Pg[h] + jnp.sum(g[t, h], axis=1, keepdims=True)
            for t, tl in enumerate(tiles):
                kb = tl[4]
                k2 = k_ref[rows(kb), :]
                for h in hs:
                    sneg = jnp.exp(-sp_[t, h])
                    dz = g[t, h] * sneg - (1.0 - sneg) * gx[t, h]
                    if diag:
                        dz = jnp.where(past, dz, 0.0)
                    dz_c = dz.astype(CDT)
                    dv_ref[rows(kb), :] += _dg(w[t, h], dom[h], _TN)
                    dk_ref[rows(kb), :] += _dg(dz_c, qm[h], _TN)
                    dqacc_ref[h] += _dg(dz_c, k2, _NN)
            return tuple(Ps), tuple(Pg)

        zero = jnp.zeros((B, 1), F32)
        npairs = qb // 2

        def pair(i, carry):
            sums, ta, tb = (carry[0:2], carry[2:4]), carry[4:8], carry[8:12]
            nxt = scores(jnp.minimum(2 * i + 2, qb)) + scores(jnp.minimum(2 * i + 3, qb))
            Ps, Pg = run([(*ta, 2 * i), (*tb, 2 * i + 1)], sums, False)
            return (*Ps, *Pg, *nxt)

        i0 = npairs - jnp.clip(done_ref[pl.program_id(0), qb], 0, npairs)
        out = lax.fori_loop(i0, npairs, pair,
                            (zero, zero, zero, zero, *scores(jnp.minimum(2 * i0, qb)), *scores(jnp.minimum(2 * i0 + 1, qb))))
        sums = (out[0:2], out[2:4])
        sums = lax.cond(qb % 2 == 1, lambda s: run([(*scores(qb - 1), qb - 1)], s, False), lambda s: s, sums)
        run([(*scores(qb), qb)], sums, True)
        dq_ref[...] = dqacc_ref[0] * mf[0] + dqacc_ref[1] * mf[1]

    slab = pl.BlockSpec((T, P), lambda hp, qb: (0, hp))
    blk = pl.BlockSpec((B, P), lambda hp, qb: (qb, hp))
    return pl.pallas_call(
        body, name="sb_bwd", grid=(SB_D // P, nq),
        in_specs=[blk, blk, slab, slab, pl.BlockSpec((1, B, P), lambda hp, qb: (hp, qb, 0)),
                  pl.BlockSpec(memory_space=pltpu.SMEM)],
        out_specs=[blk, slab, slab],
        out_shape=[jax.ShapeDtypeStruct((T, SB_D), F32)] * 3,
        scratch_shapes=[pltpu.VMEM((2, B, P), F32)],
        compiler_params=_cparams(("arbitrary", "arbitrary")),
    )(d_o, qn, kn, vb, lsum, done)


def _regroup_in_rows(wt):
    cut = 2 * GLA_DK + 2 * GLA_DV
    pad = jnp.zeros((IN_PAD - IN_WIDTH, wt.shape[1]), wt.dtype)
    return jnp.concatenate([wt[:cut], wt[cut + GLA_GATE_RANK:], wt[cut:cut + GLA_GATE_RANK], pad], axis=0)


def _ungroup_in_rows(gt):
    cut = 2 * GLA_DK + 2 * GLA_DV
    return jnp.concatenate([gt[:cut], gt[3072:3072 + GLA_GATE_RANK], gt[cut:3072]], axis=0)


def _colsum(v, *, name, tm=512):
    T, C = v.shape

    def body(v_ref, o_ref):
        i = pl.program_id(0)
        part = jnp.sum(v_ref[...], axis=0, keepdims=True)

        @pl.when(i == 0)
        def _():
            o_ref[...] = part

        @pl.when(i > 0)
        def _():
            o_ref[...] += part

    return pl.pallas_call(
        body, name=name, grid=(T // tm,),
        in_specs=[pl.BlockSpec((tm, C), lambda i: (i, 0))], out_specs=pl.BlockSpec((1, C), lambda i: (0, 0)),
        out_shape=jax.ShapeDtypeStruct((1, C), F32),
        compiler_params=_cparams(("arbitrary",)),
    )(v)


def _ffn_fwd(h, g_norm, wgu_t, wd, tag):
    hf = _rms_fwd(h, g_norm, name=f"ffn{tag}_norm")
    ab, s = _swiglu_up(hf, wgu_t, name=f"ffn{tag}_up")
    h_out = _matmul(s, wd, mode='nn', out_dtype=F32, name=f"ffn{tag}_down", tm=512, tn=D_MODEL, tk=D_FF, residual=h)
    return h_out, (hf, ab, s)


def _ffn_bwd(dh, dh_c, h_in, g_norm, wgu_t, wd, saved, tag):
    hf, ab, s = saved
    dwd = _matmul(s, dh_c, mode='tn', out_dtype=F32, name=f"ffn{tag}_dwd", tm=D_FF // 2, tn=D_MODEL, tk=TK_TOKENS)
    dab = _swiglu_dact(dh_c, wd, ab, name=f"ffn{tag}_dact")
    dwgu_t = _matmul(dab, hf, mode='tn', out_dtype=F32, name=f"ffn{tag}_dwgu", tm=D_FF // 2, tn=D_MODEL, tk=TK_TOKENS)
    dhf = _matmul(dab, wgu_t, mode='nn', out_dtype=F32, name=f"ffn{tag}_dhf", tm=256, tn=D_MODEL, tk=2 * D_FF)
    dh_in, dh_in_c, dg = _rms_bwd(dhf, h_in, g_norm, dh, name=f"ffn{tag}_dnorm")
    return dh_in, dh_in_c, dwgu_t, dwd, dg


def _local_step(x, tgt, W):
    row = lambda v: v.reshape(1, -1)
    win_p = _regroup_in_rows(W['hy_w_in_t'])
    wg2p = jnp.pad(W['hy_w_gate2'], ((0, 128 - GLA_GATE_RANK), (0, 0)))
    b_gate = row(W['hy_b_gate'])
    g_gla = row(W['hy_gla_norm'])
    gq = jnp.tile(W['hy_sb_q_norm'].reshape(-1), SB_D // SB_HEAD_DIM).reshape(1, SB_D)
    gk = jnp.tile(W['hy_sb_k_norm'].reshape(-1), SB_D // SB_HEAD_DIM).reshape(1, SB_D)
    w_out = W['hy_w_out']
    wgu = W['ffn_wgu_t']
    wd = [W['ffn_w_down'][l] for l in range(2)]
    w_dw = jnp.pad(W['cv_w_dw'], ((0, CONV_HALO - CONV_WIDTH), (0, 0)))
    mixn = [row(W['mix_norm'][l]) for l in range(2)]
    ffnn = [row(W['ffn_norm'][l]) for l in range(2)]

    hn0 = _rms_fwd(x, mixn[0], name="mix0_norm")
    proj = _matmul(hn0, win_p, mode='nt', out_dtype=F32, name="hy_in", tm=256, tn=IN_PAD, tk=D_MODEL)
    o_gla, o_raw, states = _gla_fwd(proj, wg2p, b_gate, g_gla)
    qn, kn, vb = _sb_prep(proj, gq, gk)
    o_sb, lsum, sb_done = _sb_fwd(qn, kn, vb)
    o_mix = jnp.concatenate([o_gla, o_sb], axis=1)
    h1 = _matmul(o_mix, w_out, mode='nn', out_dtype=F32, name="hy_out", tm=512, tn=D_MODEL, tk=D_MODEL, residual=x)
    h2, ffn0_saved = _ffn_fwd(h1, ffnn[0], wgu[0], wd[0], 0)
    hn1 = _rms_fwd(h2, mixn[1], name="mix1_norm")
    a_cv = _matmul(hn1, W['cv_w_pw1_t'], mode='nt', out_dtype=F32, name="cv_pw1", tm=512, tn=2 * D_MODEL, tk=D_MODEL,
                   bias=row(W['cv_b_pw1']))
    s_cv, u_cv, c_cv = _conv_fwd(a_cv, w_dw, row(W['cv_b_dw']), row(W['cv_ln_g']), row(W['cv_ln_b']))
    h3 = _matmul(s_cv, W['cv_w_pw2'], mode='nn', out_dtype=F32, name="cv_pw2", tm=512, tn=D_MODEL, tk=D_MODEL,
                 bias=row(W['cv_b_pw2']), residual=h2)
    h4, ffn1_saved = _ffn_fwd(h3, ffnn[1], wgu[1], wd[1], 1)
    sq_err, dy, dy_c = _loss_head(h4, tgt)

    G = {}
    dh3, dh3_c, dwgu1, dwd1, dg_ffn1 = _ffn_bwd(dy, dy_c, h3, ffnn[1], wgu[1], wd[1], ffn1_saved, 1)
    G['cv_b_pw2'] = _colsum(dh3, name="cv_db2")
    G['cv_w_pw2'] = _matmul(s_cv, dh3_c, mode='tn', out_dtype=F32, name="cv_dw2", tm=D_MODEL, tn=D_MODEL, tk=TK_TOKENS)
    ds_cv = _matmul(dh3_c, W['cv_w_pw2'], mode='nt', out_dtype=F32, name="cv_ds", tm=512, tn=D_MODEL, tk=D_MODEL)
    da_cv, db1, dwdw, dbdw, dlng, dlnb = _conv_bwd(ds_cv, c_cv, u_cv, a_cv, w_dw, row(W['cv_ln_g']), row(W['cv_ln_b']))
    G['cv_b_pw1'] = db1
    G['cv_w_dw'] = dwdw[:CONV_WIDTH]
    G['cv_b_dw'], G['cv_ln_g'], G['cv_ln_b'] = dbdw, dlng, dlnb
    G['cv_w_pw1_t'] = _matmul(da_cv, hn1, mode='tn', out_dtype=F32, name="cv_dw1", tm=D_MODEL, tn=D_MODEL, tk=TK_TOKENS)
    dhn1 = _matmul(da_cv, W['cv_w_pw1_t'], mode='nn', out_dtype=F32, name="cv_dhn", tm=512, tn=D_MODEL, tk=2 * D_MODEL)
    dh2, dh2_c, dg_mix1 = _rms_bwd(dhn1, h2, mixn[1], dh3, name="mix1_dnorm")
    dh1, dh1_c, dwgu0, dwd0, dg_ffn0 = _ffn_bwd(dh2, dh2_c, h1, ffnn[0], wgu[0], wd[0], ffn0_saved, 0)
    G['hy_w_out'] = _matmul(o_mix, dh1_c, mode='tn', out_dtype=F32, name="hy_dwout", tm=D_MODEL, tn=D_MODEL, tk=TK_TOKENS)
    d_omix = _matmul(dh1_c, w_out, mode='nt', out_dtype=F32, name="hy_domix", tm=512, tn=D_MODEL, tk=D_MODEL)
    dgla, dglr, dwg2, dbg, dgg = _gla_bwd(d_omix[:, :GLA_DV], proj, o_raw, states, wg2p, b_gate, g_gla)
    dqn, dkn, dvs = _sb_bwd(d_omix[:, GLA_DV:], qn, kn, vb, lsum, sb_done)
    dsb, dgq, dgk = _sb_prep_bwd(dqn, dkn, dvs, proj, gq, gk)
    dproj = jnp.concatenate([dgla, dsb, dglr], axis=1)
    dwin_p = _matmul(dproj, hn0, mode='tn', out_dtype=F32, name="hy_dwin", tm=IN_PAD // 5, tn=D_MODEL, tk=TK_TOKENS)
    dhn0 = _matmul(dproj, win_p, mode='nn', out_dtype=F32, name="hy_dhn", tm=256, tn=D_MODEL, tk=IN_PAD)
    dx, _, dg_mix0 = _rms_bwd(dhn0, x, mixn[0], dh1, name="mix0_dnorm")

    G['hy_w_in_t'] = _ungroup_in_rows(dwin_p)
    G['hy_w_gate2'] = dwg2[:GLA_GATE_RANK]
    G['hy_b_gate'] = dbg
    G['hy_gla_norm'] = dgg
    G['hy_sb_q_norm'] = dgq.reshape(SB_D // SB_HEAD_DIM, SB_HEAD_DIM).sum(axis=0, keepdims=True)
    G['hy_sb_k_norm'] = dgk.reshape(SB_D // SB_HEAD_DIM, SB_HEAD_DIM).sum(axis=0, keepdims=True)
    G['mix_norm'] = jnp.concatenate([dg_mix0, dg_mix1], axis=0)
    G['ffn_norm'] = jnp.concatenate([dg_ffn0, dg_ffn1], axis=0)
    G['ffn_wgu_t'] = [dwgu0, dwgu1]
    G['ffn_w_down'] = [dwd0, dwd1]
    return sq_err, dx, G


MESH_IDS = pl.DeviceIdType.MESH
N_PEER = N_DEV - 1


def _all_gather(blocks):
    n = len(blocks)

    def body(*refs):
        x_refs, out_refs = refs[:n], refs[n:2 * n]
        send_sems, recv_sems, local_sems = refs[2 * n:]
        x, y, c = lax.axis_index("x"), lax.axis_index("y"), lax.axis_index("c")
        me, sibling = (x, y, c), (x, y, 1 - c)
        chips = [(1 - x, y), (x, 1 - y), (1 - x, 1 - y)]

        def slot(a, px, py, pc):
            return out_refs[a].at[4 * px + 2 * py + pc]

        def copy(a, k, blk, to, src=None):
            return pltpu.make_async_remote_copy(
                src_ref=slot(a, *blk) if src is None else src, dst_ref=slot(a, *blk),
                send_sem=send_sems.at[a * N_PEER + k], recv_sem=recv_sems.at[a * N_PEER + k],
                device_id=to, device_id_type=MESH_IDS)

        mine = [pltpu.make_async_copy(x_refs[a], slot(a, *me), local_sems.at[a]) for a in range(n)]
        for cp in mine:
            cp.start()
        first = []
        for a in range(n):
            first.append(copy(a, 0, me, sibling, src=x_refs[a]))
            first += [copy(a, 1 + j, me, (*chip, c), src=x_refs[a]) for j, chip in enumerate(chips)]
        for cp in first:
            cp.start()
        passed = []
        for a in range(n):
            for j, chip in enumerate(chips):
                copy(a, 1 + j, (*chip, c), me).wait_recv()
                fwd = copy(a, 4 + j, (*chip, c), sibling)
                fwd.start()
                passed.append(fwd)
        for a in range(n):
            copy(a, 0, sibling, me).wait_recv()
            for j, chip in enumerate(chips):
                copy(a, 4 + j, (*chip, 1 - c), me).wait_recv()
        for cp in first + passed:
            cp.wait_send()
        for cp in mine:
            cp.wait()

    anyspec = pl.BlockSpec(memory_space=pl.ANY)
    return pl.pallas_call(
        body, name="fsdp_all_gather",
        out_shape=[jax.ShapeDtypeStruct((N_DEV,) + b.shape, b.dtype) for b in blocks],
        in_specs=[anyspec] * n, out_specs=[anyspec] * n,
        scratch_shapes=[pltpu.SemaphoreType.DMA((n * N_PEER,)), pltpu.SemaphoreType.DMA((n * N_PEER,)),
                        pltpu.SemaphoreType.DMA((n,))],
    )(*blocks)


def _scatter_exchange(sends):
    n = len(sends)

    def body(*refs):
        s_refs, r_refs = refs[:n], refs[n:2 * n]
        send_sems, recv_sems, local_sems = refs[2 * n:]
        x, y, c = lax.axis_index("x"), lax.axis_index("y"), lax.axis_index("c")
        me = 4 * x + 2 * y + c
        mine = [pltpu.make_async_copy(s_refs[a].at[me], r_refs[a].at[me], local_sems.at[a]) for a in range(n)]
        for cp in mine:
            cp.start()
        copies = []
        for a in range(n):
            for k in range(1, N_DEV):
                px, py, pc = x ^ ((k >> 2) & 1), y ^ ((k >> 1) & 1), c ^ (k & 1)
                cp = pltpu.make_async_remote_copy(
                    src_ref=s_refs[a].at[4 * px + 2 * py + pc], dst_ref=r_refs[a].at[me],
                    send_sem=send_sems.at[a * N_PEER + k - 1], recv_sem=recv_sems.at[a * N_PEER + k - 1],
                    device_id=(px, py, pc), device_id_type=MESH_IDS)
                cp.start()
                copies.append(cp)
        for cp in copies:
            cp.wait()
        for cp in mine:
            cp.wait()

    anyspec = pl.BlockSpec(memory_space=pl.ANY)
    return pl.pallas_call(
        body, name="fsdp_scatter_exchange",
        out_shape=[jax.ShapeDtypeStruct(s.shape, s.dtype) for s in sends],
        in_specs=[anyspec] * n, out_specs=[anyspec] * n,
        scratch_shapes=[pltpu.SemaphoreType.DMA((n * N_PEER,)), pltpu.SemaphoreType.DMA((n * N_PEER,)),
                        pltpu.SemaphoreType.DMA((n,))],
    )(*sends)


def _sum_contrib(recv, own, *, name, tr):
    _, R, C = recv.shape
    assert R % tr == 0

    def body(r_ref, own_ref, g_ref):
        me = 4 * lax.axis_index("x") + 2 * lax.axis_index("y") + lax.axis_index("c")
        g = jnp.zeros((tr, C), F32)
        for s in range(N_DEV):
            g = g + jnp.where(me == s, own_ref[...], r_ref[s].astype(F32))
        g_ref[...] = g

    row = pl.BlockSpec((tr, C), lambda i: (i, 0))
    return pl.pallas_call(
        body, name=name, grid=(R // tr,),
        in_specs=[pl.BlockSpec((N_DEV, tr, C), lambda i: (0, i, 0)), row], out_specs=row,
        out_shape=jax.ShapeDtypeStruct((R, C), F32),
        compiler_params=_cparams(("parallel",)),
    )(recv, own)


def _adamw(g, w, m, v, *, name, tr):
    R, C = g.shape
    assert R % tr == 0

    def body(g_ref, w_ref, m_ref, v_ref, d_ref, mo_ref, vo_ref):
        gv = g_ref[...]
        mn = ADAM_B1 * m_ref[...] + (1.0 - ADAM_B1) * gv
        vn = ADAM_B2 * v_ref[...] + (1.0 - ADAM_B2) * (gv * gv)
        m_hat = mn / (1.0 - ADAM_B1 ** ADAM_STEP)
        v_hat = vn / (1.0 - ADAM_B2 ** ADAM_STEP)
        d_ref[...] = -ADAM_LR * (m_hat / (jnp.sqrt(v_hat) + ADAM_EPS) + ADAM_WD * w_ref[...])
        mo_ref[...] = mn
        vo_ref[...] = vn

    row = pl.BlockSpec((tr, C), lambda i: (i, 0))
    return pl.pallas_call(
        body, name=name, grid=(R // tr,),
        in_specs=[row] * 4, out_specs=[row] * 3,
        out_shape=[jax.ShapeDtypeStruct((R, C), F32)] * 3,
        compiler_params=_cparams(("parallel",)),
    )(g, w, m, v)


SMALL_SHARDED = ('hy_w_gate2', 'cv_b_pw1', 'cv_w_dw', 'cv_b_dw', 'cv_ln_g', 'cv_ln_b', 'cv_b_pw2')
SMALL_REPLICATED = ('mix_norm', 'ffn_norm', 'hy_b_gate', 'hy_gla_norm', 'hy_sb_q_norm', 'hy_sb_k_norm')
LANES = 128


def _small_rows(n):
    return -(-n // (8 * LANES)) * 8


def _pack_small(parts, lead=()):
    out = []
    for p in parts:
        n = p.shape[-1]
        p = jnp.pad(p, [(0, 0)] * len(lead) + [(0, _small_rows(n) * LANES - n)])
        out.append(p.reshape(*lead, _small_rows(n), LANES))
    return jnp.concatenate(out, axis=len(lead))


def _unpack_small(packed, sizes, lead=()):
    out, r0 = [], 0
    for n in sizes:
        r = _small_rows(n)
        out.append(packed[..., r0:r0 + r, :].reshape(*lead, r * LANES)[..., :n])
        r0 += r
    return out


def _to_blocks(full, axis):
    shp = full.shape
    t = full.reshape(shp[:axis] + (N_DEV, shp[axis] // N_DEV) + shp[axis + 1:])
    return jnp.moveaxis(t, axis, 0)


def _from_blocks(blocks, axis):
    t = jnp.moveaxis(blocks, 0, axis)
    shp = t.shape
    return t.reshape(shp[:axis] + (shp[axis] * shp[axis + 1],) + shp[axis + 2:])


def kernel(x, mix_norm, ffn_norm, hy_w_in, hy_w_gate2, hy_b_gate, hy_gla_norm, hy_sb_q_norm, hy_sb_k_norm, hy_w_out, cv_w_pw1, cv_b_pw1, cv_w_dw, cv_b_dw, cv_ln_g, cv_ln_b, cv_w_pw2, cv_b_pw2, ffn_w_gate, ffn_w_up, ffn_w_down, loss_target, m_mix_norm, m_ffn_norm, m_hy_w_in, m_hy_w_gate2, m_hy_b_gate, m_hy_gla_norm, m_hy_sb_q_norm, m_hy_sb_k_norm, m_hy_w_out, m_cv_w_pw1, m_cv_b_pw1, m_cv_w_dw, m_cv_b_dw, m_cv_ln_g, m_cv_ln_b, m_cv_w_pw2, m_cv_b_pw2, m_ffn_w_gate, m_ffn_w_up, m_ffn_w_down, v_mix_norm, v_ffn_norm, v_hy_w_in, v_hy_w_gate2, v_hy_b_gate, v_hy_gla_norm, v_hy_sb_q_norm, v_hy_sb_k_norm, v_hy_w_out, v_cv_w_pw1, v_cv_b_pw1, v_cv_w_dw, v_cv_b_dw, v_cv_ln_g, v_cv_ln_b, v_cv_w_pw2, v_cv_b_pw2, v_ffn_w_gate, v_ffn_w_up, v_ffn_w_down):
    w_loc = dict(zip(WEIGHT_NAMES, (mix_norm, ffn_norm, hy_w_in, hy_w_gate2, hy_b_gate, hy_gla_norm, hy_sb_q_norm, hy_sb_k_norm, hy_w_out, cv_w_pw1, cv_b_pw1, cv_w_dw, cv_b_dw, cv_ln_g, cv_ln_b, cv_w_pw2, cv_b_pw2, ffn_w_gate, ffn_w_up, ffn_w_down)))
    m_loc = dict(zip(WEIGHT_NAMES, (m_mix_norm, m_ffn_norm, m_hy_w_in, m_hy_w_gate2, m_hy_b_gate, m_hy_gla_norm, m_hy_sb_q_norm, m_hy_sb_k_norm, m_hy_w_out, m_cv_w_pw1, m_cv_b_pw1, m_cv_w_dw, m_cv_b_dw, m_cv_ln_g, m_cv_ln_b, m_cv_w_pw2, m_cv_b_pw2, m_ffn_w_gate, m_ffn_w_up, m_ffn_w_down)))
    v_loc = dict(zip(WEIGHT_NAMES, (v_mix_norm, v_ffn_norm, v_hy_w_in, v_hy_w_gate2, v_hy_b_gate, v_hy_gla_norm, v_hy_sb_q_norm, v_hy_sb_k_norm, v_hy_w_out, v_cv_w_pw1, v_cv_b_pw1, v_cv_w_dw, v_cv_b_dw, v_cv_ln_g, v_cv_ln_b, v_cv_w_pw2, v_cv_b_pw2, v_ffn_w_gate, v_ffn_w_up, v_ffn_w_down)))

    Dm, F8 = D_MODEL, D_FF // N_DEV
    tr_ = lambda a: jnp.swapaxes(a, -1, -2)

    small_local = _pack_small([w_loc[n].reshape(-1) for n in SMALL_SHARDED])
    g_in, g_out, g_pw1, g_pw2, g_gate, g_up, g_down, g_small = _all_gather([
        tr_(hy_w_in[0]).astype(BF16),
        hy_w_out[0].astype(BF16),
        tr_(cv_w_pw1[0]).astype(BF16),
        cv_w_pw2[0].astype(BF16),
        tr_(ffn_w_gate).astype(BF16),
        tr_(ffn_w_up).astype(BF16),
        ffn_w_down.astype(BF16),
        small_local])
    small_sizes = [w_loc[n].size for n in SMALL_SHARDED]
    small_full = dict(zip(SMALL_SHARDED, _unpack_small(g_small, small_sizes, lead=(N_DEV,))))
    W = {n: w_loc[n] for n in SMALL_REPLICATED}
    W['hy_w_in_t'] = g_in.reshape(IN_WIDTH, Dm)
    W['hy_w_out'] = g_out.reshape(Dm, Dm)
    W['cv_w_pw1_t'] = g_pw1.reshape(2 * Dm, Dm)
    W['cv_w_pw2'] = g_pw2.reshape(Dm, Dm)
    W['ffn_wgu_t'] = [jnp.concatenate([g_gate[:, l].reshape(D_FF, Dm), g_up[:, l].reshape(D_FF, Dm)], axis=0)
                      for l in range(2)]
    W['ffn_w_down'] = jnp.stack([g_down[:, l].reshape(D_FF, Dm) for l in range(2)])
    W['hy_w_gate2'] = _from_blocks(small_full['hy_w_gate2'].reshape(N_DEV, GLA_GATE_RANK, GLA_DK // N_DEV), 1).astype(BF16)
    W['cv_w_dw'] = _from_blocks(small_full['cv_w_dw'].reshape(N_DEV, CONV_WIDTH, Dm // N_DEV), 1)
    for n in ('cv_b_pw1', 'cv_b_dw', 'cv_ln_g', 'cv_ln_b', 'cv_b_pw2'):
        W[n] = small_full[n].reshape(-1)

    sq_err, dx, G = _local_step(x[0], loss_target[0], W)

    own_f32 = [
        G['hy_w_in_t'].reshape(N_DEV, IN_WIDTH // N_DEV, Dm),
        G['hy_w_out'].reshape(N_DEV, Dm // N_DEV, Dm),
        G['cv_w_pw1_t'].reshape(N_DEV, 2 * Dm // N_DEV, Dm),
        G['cv_w_pw2'].reshape(N_DEV, Dm // N_DEV, Dm),
        jnp.stack([g.reshape(2, N_DEV, F8, Dm) for g in G['ffn_wgu_t']]).transpose(2, 0, 1, 3, 4).reshape(N_DEV, 4 * F8, Dm),
        jnp.stack([g.reshape(N_DEV, F8, Dm) for g in G['ffn_w_down']], axis=1).reshape(N_DEV, 2 * F8, Dm),
    ]
    small_parts = []
    for n in SMALL_SHARDED:
        axis = SHARD_AXIS[n] - 1
        shard = w_loc[n].shape[1:]
        full = shard[:axis] + (shard[axis] * N_DEV,) + shard[axis + 1:]
        small_parts.append(_to_blocks(G[n].reshape(full), axis).reshape(N_DEV, -1))
    for n in SMALL_REPLICATED:
        small_parts.append(jnp.broadcast_to(G[n].reshape(1, -1), (N_DEV, G[n].size)))
    small_parts.append(jnp.broadcast_to(sq_err.reshape(1, 1), (N_DEV, 1)))
    send_small = _pack_small(small_parts, lead=(N_DEV,))
    recv = _scatter_exchange([a.astype(BF16) for a in own_f32] + [send_small])
    me = 4 * lax.axis_index("x") + 2 * lax.axis_index("y") + lax.axis_index("c")
    own = [lax.dynamic_index_in_dim(a, me, 0, keepdims=False) for a in own_f32 + [send_small]]
    tags = ['hy_w_in', 'hy_w_out', 'cv_w_pw1', 'cv_w_pw2', 'ffn_wgu', 'ffn_w_down', 'small']
    rows_tile = [IN_WIDTH // N_DEV, Dm // N_DEV, 2 * Dm // N_DEV, Dm // N_DEV, F8, F8, recv[-1].shape[1]]
    gsum = [_sum_contrib(r, o, name=f"sum_{t}", tr=tr) for r, o, t, tr in zip(recv, own, tags, rows_tile)]

    grad = {}
    grad['hy_w_in'] = tr_(gsum[0])[None]
    grad['hy_w_out'] = gsum[1][None]
    grad['cv_w_pw1'] = tr_(gsum[2])[None]
    grad['cv_w_pw2'] = gsum[3][None]
    gu = gsum[4].reshape(2, 2, F8, Dm)
    grad['ffn_w_gate'] = tr_(gu[:, 0])
    grad['ffn_w_up'] = tr_(gu[:, 1])
    grad['ffn_w_down'] = gsum[5].reshape(2, F8, Dm)
    small_names = SMALL_SHARDED + SMALL_REPLICATED
    small_all = [w_loc[n].size for n in small_names]
    *small_grads, sq_sum = _unpack_small(gsum[6], small_all + [1])
    for n, a in zip(small_names, small_grads):
        grad[n] = a.reshape(w_loc[n].shape)
    loss = 0.5 / Dm * sq_sum[0]

    delta, new_m, new_v = {}, {}, {}
    view = {'hy_w_in': (Dm, 256), 'hy_w_out': (Dm // N_DEV, Dm // N_DEV), 'cv_w_pw1': (Dm, 256),
            'cv_w_pw2': (Dm // N_DEV, Dm // N_DEV), 'ffn_w_gate': (2 * Dm, 256), 'ffn_w_up': (2 * Dm, 256),
            'ffn_w_down': (2 * F8, F8)}
    for n, (rows, tr) in view.items():
        shp = w_loc[n].shape
        outs = _adamw(grad[n].reshape(rows, -1), w_loc[n].reshape(rows, -1), m_loc[n].reshape(rows, -1),
                      v_loc[n].reshape(rows, -1), name=f"adamw_{n}", tr=tr)
        delta[n], new_m[n], new_v[n] = (o.reshape(shp) for o in outs)
    packed = [_pack_small([d[n].reshape(-1) for n in small_names] + [jnp.zeros((1,), F32)]) for d in (w_loc, m_loc, v_loc)]
    outs = _adamw(gsum[6], *packed, name="adamw_small", tr=gsum[6].shape[0])
    for dst, o in zip((delta, new_m, new_v), outs):
        for n, a in zip(small_names, _unpack_small(o, small_all)):
            dst[n] = a.reshape(w_loc[n].shape)

    return (loss, dx[None], *[grad[n] for n in WEIGHT_NAMES], *[delta[n] for n in WEIGHT_NAMES],
            *[new_m[n] for n in WEIGHT_NAMES], *[new_v[n] for n in WEIGHT_NAMES])
```

```python
import jax
import jax.numpy as jnp
from jax import lax
from jax.experimental import pallas as pl
from jax.experimental.pallas import tpu as pltpu

F32 = jnp.float32
BF16 = jnp.bfloat16
CDT = jnp.bfloat16

D_MODEL = 1024
EPS = 1e-6
CHUNK = 64
GLA_HEADS = 4
GLA_HEAD_K = 64
GLA_HEAD_V = 128
GLA_DK = GLA_HEADS * GLA_HEAD_K
GLA_DV = GLA_HEADS * GLA_HEAD_V
GLA_GATE_RANK = 16
GLA_GATE_NORMALIZER = 16.0
SB_HEAD_DIM = 64
SB_D = 512
SB_TILE = 256
SB_PAIR = 128
SB_SPLIT_LK = 2
SB_SPLIT_G = 1
SB_DEAD = 120.0
IN_WIDTH = 3088
IN_PAD = 3200
CONV_WIDTH = 31
CONV_HALO = 32
D_FF = 2816
N_DEV = 8

ADAM_LR = 0.001
ADAM_B1 = 0.9
ADAM_B2 = 0.999
ADAM_EPS = 1e-08
ADAM_WD = 0.01
ADAM_STEP = 10

VMEM_LIMIT = 56 * 1024 * 1024
TK_TOKENS = 2048

WEIGHT_NAMES = ['mix_norm', 'ffn_norm', 'hy_w_in', 'hy_w_gate2', 'hy_b_gate', 'hy_gla_norm', 'hy_sb_q_norm',
                'hy_sb_k_norm', 'hy_w_out', 'cv_w_pw1', 'cv_b_pw1', 'cv_w_dw', 'cv_b_dw', 'cv_ln_g', 'cv_ln_b',
                'cv_w_pw2', 'cv_b_pw2', 'ffn_w_gate', 'ffn_w_up', 'ffn_w_down']
SHARD_AXIS = {'mix_norm': None, 'ffn_norm': None, 'hy_w_in': 2, 'hy_w_gate2': 2, 'hy_b_gate': None,
              'hy_gla_norm': None, 'hy_sb_q_norm': None, 'hy_sb_k_norm': None, 'hy_w_out': 1, 'cv_w_pw1': 2,
              'cv_b_pw1': 1, 'cv_w_dw': 2, 'cv_b_dw': 1, 'cv_ln_g': 1, 'cv_ln_b': 1, 'cv_w_pw2': 1, 'cv_b_pw2': 1,
              'ffn_w_gate': 2, 'ffn_w_up': 2, 'ffn_w_down': 1}


def _cparams(sem=None, vmem=VMEM_LIMIT):
    return pltpu.CompilerParams(dimension_semantics=sem, vmem_limit_bytes=vmem)


def _log_sigmoid(x):
    return jnp.minimum(x, 0.0) - jnp.log1p(jnp.exp(-jnp.abs(x)))


def _sigmoid(x):
    return 1.0 / (1.0 + jnp.exp(-x))


def _softplus(x):
    return jnp.maximum(x, 0.0) + jnp.log(1.0 + jnp.exp(-jnp.abs(x)))


def _split_bf16(x, n):
    parts = []
    rem = x
    for _ in range(n):
        p = rem.astype(BF16)
        parts.append(p)
        rem = rem - p.astype(F32)
    return parts


def _dot_exact_rhs(x, m, n):
    return sum(jnp.dot(p, m, preferred_element_type=F32) for p in _split_bf16(x, n))


def _dot_exact_lhs(m, x, n):
    return sum(jnp.dot(m, p, preferred_element_type=F32) for p in _split_bf16(x, n))


_NN = (((1,), (0,)), ((), ()))
_NT = (((1,), (1,)), ((), ()))
_TN = (((0,), (0,)), ((), ()))


def _dg(a, b, dn):
    return lax.dot_general(a.astype(CDT), b.astype(CDT), dn, preferred_element_type=F32)


def _matmul(a, b, *, mode, out_dtype, name, tm, tn, tk, bias=None, residual=None):
    if mode == 'nn':
        (M, K), (K2, N) = a.shape, b.shape
    elif mode == 'nt':
        (M, K), (N, K2) = a.shape, b.shape
    else:
        (K, M), (K2, N) = a.shape, b.shape
    assert K == K2 and M % tm == 0 and N % tn == 0 and K % tk == 0, (name, a.shape, b.shape, tm, tn, tk)
    nk = K // tk
    a_spec = pl.BlockSpec((tk, tm), lambda i, j, k: (k, i)) if mode == 'tn' else pl.BlockSpec((tm, tk), lambda i, j, k: (i, k))
    b_spec = pl.BlockSpec((tn, tk), lambda i, j, k: (j, k)) if mode == 'nt' else pl.BlockSpec((tk, tn), lambda i, j, k: (k, j))
    dn = {'nn': _NN, 'nt': _NT, 'tn': _TN}[mode]
    has_bias, has_res = bias is not None, residual is not None

    def body(*refs):
        a_ref, b_ref = refs[0], refs[1]
        pos = 2
        bias_ref = res_ref = None
        if has_bias:
            bias_ref = refs[pos]
            pos += 1
        if has_res:
            res_ref = refs[pos]
            pos += 1
        o_ref = refs[pos]
        acc_ref = refs[pos + 1] if nk > 1 else None
        p = _dg(a_ref[...], b_ref[...], dn)

        def finish(acc):
            if has_bias:
                acc = acc + bias_ref[...]
            if has_res:
                acc = res_ref[...] + acc
            o_ref[...] = acc.astype(o_ref.dtype)

        if nk == 1:
            finish(p)
        else:
            k = pl.program_id(2)

            @pl.when(k == 0)
            def _():
                acc_ref[...] = p

            @pl.when(k > 0)
            def _():
                acc_ref[...] += p

            @pl.when(k == nk - 1)
            def _():
                finish(acc_ref[...])

    in_specs = [a_spec, b_spec]
    args = [a, b]
    if has_bias:
        in_specs.append(pl.BlockSpec((1, tn), lambda i, j, k: (0, j)))
        args.append(bias)
    if has_res:
        in_specs.append(pl.BlockSpec((tm, tn), lambda i, j, k: (i, j)))
        args.append(residual)
    return pl.pallas_call(
        body, name=name, grid=(M // tm, N // tn, nk),
        in_specs=in_specs, out_specs=pl.BlockSpec((tm, tn), lambda i, j, k: (i, j)),
        out_shape=jax.ShapeDtypeStruct((M, N), out_dtype),
        scratch_shapes=[pltpu.VMEM((tm, tn), F32)] if nk > 1 else [],
        compiler_params=_cparams(("parallel", "parallel", "arbitrary")),
    )(*args)


def _rms_fwd(x, g, *, name, tm=512):
    T, Dm = x.shape

    def body(x_ref, g_ref, o_ref):
        xv = x_ref[...]
        r = lax.rsqrt(jnp.mean(xv * xv, axis=-1, keepdims=True) + EPS)
        o_ref[...] = (xv * r * g_ref[...]).astype(o_ref.dtype)

    return pl.pallas_call(
        body, name=name, grid=(T // tm,),
        in_specs=[pl.BlockSpec((tm, Dm), lambda i: (i, 0)), pl.BlockSpec((1, Dm), lambda i: (0, 0))],
        out_specs=pl.BlockSpec((tm, Dm), lambda i: (i, 0)),
        out_shape=jax.ShapeDtypeStruct((T, Dm), CDT),
        compiler_params=_cparams(("parallel",)),
    )(x, g)


def _rms_bwd(dy, x, g, resid, *, name, tm=512):
    T, Dm = x.shape

    def body(dy_ref, x_ref, g_ref, res_ref, dx_ref, dxb_ref, dg_ref):
        i = pl.program_id(0)
        xv, dyv = x_ref[...], dy_ref[...]
        r = lax.rsqrt(jnp.mean(xv * xv, axis=-1, keepdims=True) + EPS)
        u = dyv * g_ref[...]
        dot = jnp.mean(u * xv, axis=-1, keepdims=True)
        dx = res_ref[...] + (r * u - xv * (r * r * r * dot))
        dx_ref[...] = dx
        dxb_ref[...] = dx.astype(dxb_ref.dtype)
        part = jnp.sum(dyv * xv * r, axis=0, keepdims=True)

        @pl.when(i == 0)
        def _():
            dg_ref[...] = part

        @pl.when(i > 0)
        def _():
            dg_ref[...] += part

    row = pl.BlockSpec((tm, Dm), lambda i: (i, 0))
    vec = pl.BlockSpec((1, Dm), lambda i: (0, 0))
    return pl.pallas_call(
        body, name=name, grid=(T // tm,),
        in_specs=[row, row, vec, row], out_specs=[row, row, vec],
        out_shape=[jax.ShapeDtypeStruct((T, Dm), F32), jax.ShapeDtypeStruct((T, Dm), CDT),
                   jax.ShapeDtypeStruct((1, Dm), F32)],
        compiler_params=_cparams(("arbitrary",)),
    )(dy, x, g, resid)


def _loss_head(y, tgt, *, tm=512):
    T, Dm = y.shape

    def body(y_ref, t_ref, s_ref, dy_ref, dyb_ref):
        i = pl.program_id(0)
        e = y_ref[...] - t_ref[...]
        dy = e * (1.0 / Dm)
        dy_ref[...] = dy
        dyb_ref[...] = dy.astype(dyb_ref.dtype)
        part = jnp.sum(jnp.sum(e * e, axis=1, keepdims=True), axis=0, keepdims=True)

        @pl.when(i == 0)
        def _():
            s_ref[...] = part

        @pl.when(i > 0)
        def _():
            s_ref[...] += part

    row = pl.BlockSpec((tm, Dm), lambda i: (i, 0))
    return pl.pallas_call(
        body, name="loss_head", grid=(T // tm,),
        in_specs=[row, row], out_specs=[pl.BlockSpec((1, 1), lambda i: (0, 0)), row, row],
        out_shape=[jax.ShapeDtypeStruct((1, 1), F32), jax.ShapeDtypeStruct((T, Dm), F32),
                   jax.ShapeDtypeStruct((T, Dm), CDT)],
        compiler_params=_cparams(("arbitrary",)),
    )(y, tgt)


def _swiglu_up(hf, wgu_t, *, name, tm=256):
    T, Dm = hf.shape
    F2 = wgu_t.shape[0]
    F = F2 // 2

    def body(h_ref, w_ref, ab_ref, s_ref):
        ab_ref[...] = _dg(h_ref[...], w_ref[...], _NT)
        a = ab_ref[:, :F]
        b = ab_ref[:, F:]
        s_ref[...] = (a * _sigmoid(a) * b).astype(s_ref.dtype)

    return pl.pallas_call(
        body, name=name, grid=(T // tm,),
        in_specs=[pl.BlockSpec((tm, Dm), lambda i: (i, 0)), pl.BlockSpec((F2, Dm), lambda i: (0, 0))],
        out_specs=[pl.BlockSpec((tm, F2), lambda i: (i, 0)), pl.BlockSpec((tm, F), lambda i: (i, 0))],
        out_shape=[jax.ShapeDtypeStruct((T, F2), F32), jax.ShapeDtypeStruct((T, F), CDT)],
        compiler_params=_cparams(("parallel",)),
    )(hf, wgu_t)


def _swiglu_dact(dh_c, wd, ab, *, name, tm=256):
    T, Dm = dh_c.shape
    F2 = ab.shape[1]
    F = F2 // 2

    def body(dh_ref, w_ref, ab_ref, o_ref):
        dsv = _dg(dh_ref[...], w_ref[...], _NT)
        a = ab_ref[:, :F]
        b = ab_ref[:, F:]
        sg = _sigmoid(a)
        o_ref[:, :F] = (dsv * b * (sg * (1.0 + a * (1.0 - sg)))).astype(o_ref.dtype)
        o_ref[:, F:] = (dsv * (a * sg)).astype(o_ref.dtype)

    return pl.pallas_call(
        body, name=name, grid=(T // tm,),
        in_specs=[pl.BlockSpec((tm, Dm), lambda i: (i, 0)), pl.BlockSpec((F, Dm), lambda i: (0, 0)),
                  pl.BlockSpec((tm, F2), lambda i: (i, 0))],
        out_specs=pl.BlockSpec((tm, F2), lambda i: (i, 0)),
        out_shape=jax.ShapeDtypeStruct((T, F2), CDT),
        compiler_params=_cparams(("parallel",)),
    )(dh_c, wd, ab)


SUBLANES = 8


def _shifted_copies(buf, shifted, tm):
    n = tm + CONV_HALO - SUBLANES
    for b in range(1, SUBLANES):
        shifted[b - 1] = buf[pl.ds(b, n), :]


def _rows_from(buf, shifted, offset, tm):
    a, b = divmod(offset, SUBLANES)
    if b == 0:
        return buf[pl.ds(SUBLANES * a, tm), :]
    return shifted[b - 1, pl.ds(SUBLANES * a, tm), :]


def _conv_fwd(a, w_dw, b_dw, ln_g, ln_b, *, tm=256):
    T = a.shape[0]
    Dm = D_MODEL

    def body(a_ref, w_ref, bdw_ref, g_ref, b_ref, s_ref, u_ref, c_ref, ubuf, shifted):
        i = pl.program_id(0)

        @pl.when(i == 0)
        def _():
            ubuf[0:CONV_HALO, :] = jnp.zeros((CONV_HALO, Dm), F32)

        @pl.when(i > 0)
        def _():
            ubuf[0:CONV_HALO, :] = ubuf[tm:tm + CONV_HALO, :]

        u = a_ref[:, :Dm] * _sigmoid(a_ref[:, Dm:])
        ubuf[CONV_HALO:CONV_HALO + tm, :] = u
        u_ref[...] = u
        _shifted_copies(ubuf, shifted, tm)
        acc = jnp.zeros((tm, Dm), F32) + bdw_ref[...]
        for k in range(CONV_WIDTH):
            acc = acc + w_ref[k:k + 1, :] * _rows_from(ubuf, shifted, CONV_HALO - (CONV_WIDTH - 1) + k, tm)
        c_ref[...] = acc
        mu = jnp.mean(acc, axis=-1, keepdims=True)
        cen = acc - mu
        var = jnp.mean(cen * cen, axis=-1, keepdims=True)
        l = cen * lax.rsqrt(var + EPS) * g_ref[...] + b_ref[...]
        s_ref[...] = (l * _sigmoid(l)).astype(s_ref.dtype)

    row = pl.BlockSpec((tm, Dm), lambda i: (i, 0))
    vec = pl.BlockSpec((1, Dm), lambda i: (0, 0))
    return pl.pallas_call(
        body, name="conv_fwd", grid=(T // tm,),
        in_specs=[pl.BlockSpec((tm, 2 * Dm), lambda i: (i, 0)), pl.BlockSpec((CONV_HALO, Dm), lambda i: (0, 0)), vec, vec, vec],
        out_specs=[row, row, row],
        out_shape=[jax.ShapeDtypeStruct((T, Dm), CDT), jax.ShapeDtypeStruct((T, Dm), F32), jax.ShapeDtypeStruct((T, Dm), F32)],
        scratch_shapes=[pltpu.VMEM((tm + CONV_HALO, Dm), F32), pltpu.VMEM((SUBLANES - 1, tm + CONV_HALO - SUBLANES, Dm), F32)],
        compiler_params=_cparams(("arbitrary",)),
    )(a, w_dw, b_dw, ln_g, ln_b)


def _conv_bwd(ds, c, u, a, w_dw, ln_g, ln_b, sends, *, tm=256):
    T = a.shape[0]
    Dm = D_MODEL
    nt = T // tm
    ns = len(sends)

    def body(*refs):
        ds_ref, c_ref, u_ref, a_ref, w_ref, g_ref, b_ref = refs[:7]
        da_ref, db1_ref, dw_ref, dbdw_ref, dg_ref, dbln_ref = refs[7 + ns:13 + ns]
        dcbuf, shifted = refs[13 + 2 * ns:15 + 2 * ns]
        x_start, x_finish = _scatter_phases(refs[7:7 + ns], refs[13 + ns:13 + 2 * ns], *refs[15 + 2 * ns:])
        i = pl.program_id(0)
        pl.when(i == 0)(x_start)

        @pl.when(i == 0)
        def _():
            dcbuf[tm:tm + CONV_HALO, :] = jnp.zeros((CONV_HALO, Dm), F32)
            db1_ref[...] = jnp.zeros_like(db1_ref)
            dw_ref[...] = jnp.zeros_like(dw_ref)
            dbdw_ref[...] = jnp.zeros_like(dbdw_ref)
            dg_ref[...] = jnp.zeros_like(dg_ref)
            dbln_ref[...] = jnp.zeros_like(dbln_ref)

        @pl.when(i > 0)
        def _():
            dcbuf[tm:tm + CONV_HALO, :] = dcbuf[0:CONV_HALO, :]

        cv = c_ref[...]
        mu = jnp.mean(cv, axis=-1, keepdims=True)
        cen = cv - mu
        var = jnp.mean(cen * cen, axis=-1, keepdims=True)
        rstd = lax.rsqrt(var + EPS)
        n = cen * rstd
        l = n * g_ref[...] + b_ref[...]
        sg = _sigmoid(l)
        dl = ds_ref[...] * (sg * (1.0 + l * (1.0 - sg)))
        dg_ref[...] += jnp.sum(dl * n, axis=0, keepdims=True)
        dbln_ref[...] += jnp.sum(dl, axis=0, keepdims=True)
        dn = dl * g_ref[...]
        dc = rstd * (dn - jnp.mean(dn, axis=-1, keepdims=True) - n * jnp.mean(dn * n, axis=-1, keepdims=True))
        dbdw_ref[...] += jnp.sum(dc, axis=0, keepdims=True)
        dcbuf[0:tm, :] = dc
        _shifted_copies(dcbuf, shifted, tm)
        uv = u_ref[...]
        du = jnp.zeros((tm, Dm), F32)
        for k in range(CONV_WIDTH):
            slab = _rows_from(dcbuf, shifted, CONV_WIDTH - 1 - k, tm)
            du = du + w_ref[k:k + 1, :] * slab
            dw_ref[k:k + 1, :] += jnp.sum(slab * uv, axis=0, keepdims=True)
        a1 = a_ref[:, :Dm]
        s2 = _sigmoid(a_ref[:, Dm:])
        da1 = du * s2
        da2 = du * a1 * (s2 * (1.0 - s2))
        da_ref[:, :Dm] = da1.astype(da_ref.dtype)
        da_ref[:, Dm:] = da2.astype(da_ref.dtype)
        db1_ref[:, :Dm] += jnp.sum(da1, axis=0, keepdims=True)
        db1_ref[:, Dm:] += jnp.sum(da2, axis=0, keepdims=True)
        pl.when(i == nt - 1)(x_finish)

    rev = lambda i: (nt - 1 - i, 0)
    row = pl.BlockSpec((tm, Dm), rev)
    row2 = pl.BlockSpec((tm, 2 * Dm), rev)
    vec = pl.BlockSpec((1, Dm), lambda i: (0, 0))
    vec2 = pl.BlockSpec((1, 2 * Dm), lambda i: (0, 0))
    taps = pl.BlockSpec((CONV_HALO, Dm), lambda i: (0, 0))
    anyspec = pl.BlockSpec(memory_space=pl.ANY)
    outs = pl.pallas_call(
        body, name="conv_bwd", grid=(nt,),
        in_specs=[row, row, row, row2, taps, vec, vec] + [anyspec] * ns,
        out_specs=[row2, vec2, taps, vec, vec, vec] + [anyspec] * ns,
        out_shape=[jax.ShapeDtypeStruct((T, 2 * Dm), CDT), jax.ShapeDtypeStruct((1, 2 * Dm), F32),
                   jax.ShapeDtypeStruct((CONV_HALO, Dm), F32), jax.ShapeDtypeStruct((1, Dm), F32),
                   jax.ShapeDtypeStruct((1, Dm), F32), jax.ShapeDtypeStruct((1, Dm), F32)]
        + [jax.ShapeDtypeStruct(s.shape, s.dtype) for s in sends],
        scratch_shapes=[pltpu.VMEM((tm + CONV_HALO, Dm), F32), pltpu.VMEM((SUBLANES - 1, tm + CONV_HALO - SUBLANES, Dm), F32)]
        + _exchange_sems(ns),
        compiler_params=_cparams(("arbitrary",)),
    )(ds, c, u, a, w_dw, ln_g, ln_b, *sends)
    return outs[:6], outs[6:]


def _gla_head_masks(width, per_head):
    lane = lax.broadcasted_iota(jnp.int32, (1, width), 1)
    return [((lane >= h * per_head) & (lane < (h + 1) * per_head)).astype(F32) for h in range(GLA_HEADS)]


def _gla_specs(tm, order):
    return [pl.BlockSpec((tm, GLA_DK), lambda i: (order(i), 0)),
            pl.BlockSpec((tm, GLA_DK), lambda i: (order(i), 1)),
            pl.BlockSpec((tm, GLA_DV), lambda i: (order(i), 1)),
            pl.BlockSpec((tm, GLA_DV), lambda i: (order(i), 2)),
            pl.BlockSpec((tm, 128), lambda i: (order(i), 3072 // 128))]


def _gla_chunk_decay(la_c, tri):
    bc = _dot_exact_lhs(tri, la_c, 3)
    b_end = bc[CHUNK - 1:CHUNK, :]
    return b_end, jnp.exp(b_end - bc)


def _gla_fwd(proj, wg2p, b_gate, g_gla, *, tm=256):
    T = proj.shape[0]
    ncs = tm // CHUNK
    scale = GLA_HEAD_K ** -0.5

    def body(q_ref, k_ref, v_ref, r_ref, glr_ref, wg_ref, bg_ref, gg_ref, o_ref, oraw_ref, st_ref, s_scr):
        i = pl.program_id(0)

        @pl.when(i == 0)
        def _():
            s_scr[...] = jnp.zeros_like(s_scr)

        mk = _gla_head_masks(GLA_DK, GLA_HEAD_K)
        rr = lax.broadcasted_iota(jnp.int32, (CHUNK, CHUNK), 0)
        cc = lax.broadcasted_iota(jnp.int32, (CHUNK, CHUNK), 1)
        tri = (cc <= rr).astype(BF16)
        y = _dg(glr_ref[...], wg_ref[...], _NN) + bg_ref[...]
        la = _log_sigmoid(y) / GLA_GATE_NORMALIZER
        qs = q_ref[...] * scale
        for ci in range(ncs):
            rows = slice(ci * CHUNK, (ci + 1) * CHUNK)
            b_end, dec = _gla_chunk_decay(la[rows], tri)
            kend = (k_ref[rows, :] * dec).astype(CDT)
            upd = jnp.zeros((GLA_HEAD_V, GLA_DK), F32)
            for h in range(GLA_HEADS):
                vh = v_ref[rows, h * GLA_HEAD_V:(h + 1) * GLA_HEAD_V]
                upd = upd + mk[h] * _dg(vh, kend, _TN)
            s_new = jnp.exp(b_end) * s_scr[...] + upd
            s_scr[...] = s_new
            st_ref[ci] = s_new
            s_c = s_new.astype(CDT)
            for h in range(GLA_HEADS):
                o_h = _dg(qs[rows] * mk[h], s_c, _NT)
                oraw_ref[rows, h * GLA_HEAD_V:(h + 1) * GLA_HEAD_V] = o_h
        for h in range(GLA_HEADS):
            cols = slice(h * GLA_HEAD_V, (h + 1) * GLA_HEAD_V)
            o_h = oraw_ref[:, cols]
            rs = lax.rsqrt(jnp.mean(o_h * o_h, axis=-1, keepdims=True) + EPS)
            rg = r_ref[:, cols]
            o_ref[:, cols] = (o_h * rs * gg_ref[...] * (rg * _sigmoid(rg))).astype(o_ref.dtype)

    full = lambda shape: pl.BlockSpec(shape, lambda i: tuple(0 for _ in shape))
    return pl.pallas_call(
        body, name="gla_fwd", grid=(T // tm,),
        in_specs=_gla_specs(tm, lambda i: i) + [full((128, GLA_DK)), full((1, GLA_DK)), full((1, GLA_HEAD_V))],
        out_specs=[pl.BlockSpec((tm, GLA_DV), lambda i: (i, 0)), pl.BlockSpec((tm, GLA_DV), lambda i: (i, 0)),
                   pl.BlockSpec((ncs, GLA_HEAD_V, GLA_DK), lambda i: (i, 0, 0))],
        out_shape=[jax.ShapeDtypeStruct((T, GLA_DV), CDT), jax.ShapeDtypeStruct((T, GLA_DV), F32),
                   jax.ShapeDtypeStruct((T // CHUNK, GLA_HEAD_V, GLA_DK), F32)],
        scratch_shapes=[pltpu.VMEM((GLA_HEAD_V, GLA_DK), F32)],
        compiler_params=_cparams(("arbitrary",)),
    )(proj, proj, proj, proj, proj, wg2p, b_gate, g_gla)


def _gla_bwd(d_o, proj, oraw, states, wg2p, b_gate, g_gla, *, tm=256):
    T = proj.shape[0]
    nt = T // tm
    ncs = tm // CHUNK
    scale = GLA_HEAD_K ** -0.5

    def body(do_ref, q_ref, k_ref, v_ref, r_ref, glr_ref, oraw_ref, st_ref, stp_ref, wg_ref, bg_ref, gg_ref,
             dgla_ref, dglr_ref, dwg_ref, dbg_ref, dgg_ref, ds_scr, dy_scr, dor_scr):
        i = pl.program_id(0)
        tile = nt - 1 - i

        @pl.when(i == 0)
        def _():
            ds_scr[...] = jnp.zeros_like(ds_scr)
            dwg_ref[...] = jnp.zeros_like(dwg_ref)
            dbg_ref[...] = jnp.zeros_like(dbg_ref)
            dgg_ref[...] = jnp.zeros_like(dgg_ref)

        mk = _gla_head_masks(GLA_DK, GLA_HEAD_K)
        rr = lax.broadcasted_iota(jnp.int32, (CHUNK, CHUNK), 0)
        cc = lax.broadcasted_iota(jnp.int32, (CHUNK, CHUNK), 1)
        tri = (cc <= rr).astype(BF16)
        tri_t = (cc >= rr).astype(BF16)
        last_row = (lax.broadcasted_iota(jnp.int32, (CHUNK, 1), 0) == CHUNK - 1).astype(F32)

        dgg = jnp.zeros((1, GLA_HEAD_V), F32)
        for h in range(GLA_HEADS):
            cols = slice(h * GLA_HEAD_V, (h + 1) * GLA_HEAD_V)
            o_h = oraw_ref[:, cols]
            rs = lax.rsqrt(jnp.mean(o_h * o_h, axis=-1, keepdims=True) + EPS)
            rg = r_ref[:, cols]
            sg = _sigmoid(rg)
            dov = do_ref[:, cols]
            on = o_h * rs * gg_ref[...]
            d_on = dov * (rg * sg)
            dgla_ref[:, 2 * GLA_DK + GLA_DV + h * GLA_HEAD_V:2 * GLA_DK + GLA_DV + (h + 1) * GLA_HEAD_V] = (
                dov * on * (sg * (1.0 + rg * (1.0 - sg)))).astype(dgla_ref.dtype)
            dgg = dgg + jnp.sum(d_on * o_h * rs, axis=0, keepdims=True)
            uu = d_on * gg_ref[...]
            dor_scr[:, cols] = rs * uu - o_h * (rs * rs * rs * jnp.mean(uu * o_h, axis=-1, keepdims=True))
        dgg_ref[...] += dgg

        y = _dg(glr_ref[...], wg_ref[...], _NN) + bg_ref[...]
        la = _log_sigmoid(y) / GLA_GATE_NORMALIZER
        qs = q_ref[...] * scale
        for ci in reversed(range(ncs)):
            rows = slice(ci * CHUNK, (ci + 1) * CHUNK)
            b_end, dec = _gla_chunk_decay(la[rows], tri)
            decay = jnp.exp(b_end)
            kend = k_ref[rows, :] * dec
            kend_c = kend.astype(CDT)
            s_c = st_ref[ci].astype(CDT)
            if ci > 0:
                s_prev = st_ref[ci - 1]
            else:
                s_prev = jnp.where(tile > 0, stp_ref[0], 0.0)
            dqs = jnp.zeros((CHUNK, GLA_DK), F32)
            dst = ds_scr[...]
            for h in range(GLA_HEADS):
                do_h = dor_scr[rows, h * GLA_HEAD_V:(h + 1) * GLA_HEAD_V].astype(CDT)
                dqs = dqs + mk[h] * _dg(do_h, s_c, _NN)
                dst = dst + mk[h] * _dg(do_h, qs[rows], _TN)
            d_decay = jnp.sum(dst * s_prev, axis=0, keepdims=True)
            ds_scr[...] = decay * dst
            dst_c = dst.astype(CDT)
            dkend = jnp.zeros((CHUNK, GLA_DK), F32)
            for h in range(GLA_HEADS):
                cols = slice(h * GLA_HEAD_V, (h + 1) * GLA_HEAD_V)
                dv_h = _dg(kend * mk[h], dst_c, _NT)
                dgla_ref[rows, 2 * GLA_DK + h * GLA_HEAD_V:2 * GLA_DK + (h + 1) * GLA_HEAD_V] = dv_h.astype(dgla_ref.dtype)
                dkend = dkend + mk[h] * _dg(v_ref[rows, cols], dst_c, _NN)
            dgla_ref[rows, 0:GLA_DK] = (dqs * scale).astype(dgla_ref.dtype)
            dgla_ref[rows, GLA_DK:2 * GLA_DK] = (dkend * dec).astype(dgla_ref.dtype)
            mm = dkend * kend
            db_end = jnp.sum(mm, axis=0, keepdims=True) + d_decay * decay
            dbc = last_row * db_end - mm
            dla = _dot_exact_lhs(tri_t, dbc, 3)
            dy_scr[rows, :] = dla * (1.0 / GLA_GATE_NORMALIZER) * _sigmoid(-y[rows])
        dy = dy_scr[...]
        dbg_ref[...] += jnp.sum(dy, axis=0, keepdims=True)
        dwg_ref[...] += _dg(glr_ref[...], dy, _TN)
        dglr_ref[...] = _dg(dy, wg_ref[...], _NT).astype(dglr_ref.dtype)

    rev = lambda i: nt - 1 - i
    full = lambda shape: pl.BlockSpec(shape, lambda i: tuple(0 for _ in shape))
    st_spec = pl.BlockSpec((ncs, GLA_HEAD_V, GLA_DK), lambda i: (rev(i), 0, 0))
    stp_spec = pl.BlockSpec((1, GLA_HEAD_V, GLA_DK), lambda i: (jnp.maximum(rev(i) * ncs - 1, 0), 0, 0))
    return pl.pallas_call(
        body, name="gla_bwd", grid=(nt,),
        in_specs=[pl.BlockSpec((tm, GLA_DV), lambda i: (rev(i), 0))] + _gla_specs(tm, rev)
        + [pl.BlockSpec((tm, GLA_DV), lambda i: (rev(i), 0)), st_spec, stp_spec,
           full((128, GLA_DK)), full((1, GLA_DK)), full((1, GLA_HEAD_V))],
        out_specs=[pl.BlockSpec((tm, 2 * GLA_DK + 2 * GLA_DV), lambda i: (rev(i), 0)),
                   pl.BlockSpec((tm, 128), lambda i: (rev(i), 0)),
                   full((128, GLA_DK)), full((1, GLA_DK)), full((1, GLA_HEAD_V))],
        out_shape=[jax.ShapeDtypeStruct((T, 2 * GLA_DK + 2 * GLA_DV), CDT), jax.ShapeDtypeStruct((T, 128), CDT),
                   jax.ShapeDtypeStruct((128, GLA_DK), F32), jax.ShapeDtypeStruct((1, GLA_DK), F32),
                   jax.ShapeDtypeStruct((1, GLA_HEAD_V), F32)],
        scratch_shapes=[pltpu.VMEM((GLA_HEAD_V, GLA_DK), F32), pltpu.VMEM((tm, GLA_DK), F32),
                        pltpu.VMEM((tm, GLA_DV), F32)],
        compiler_params=_cparams(("arbitrary",)),
    )(d_o, proj, proj, proj, proj, proj, oraw, states, states, wg2p, b_gate, g_gla)


def _head_mean_matrix():
    r = lax.broadcasted_iota(jnp.int32, (SB_D, SB_D), 0) // SB_HEAD_DIM
    c = lax.broadcasted_iota(jnp.int32, (SB_D, SB_D), 1) // SB_HEAD_DIM
    return jnp.where(r == c, 1.0 / SB_HEAD_DIM, 0.0).astype(BF16)


def _sb_prep(proj, gq, gk, *, tm=256):
    T = proj.shape[0]
    scale = SB_HEAD_DIM ** -0.5

    def body(q_ref, k_ref, v_ref, gq_ref, gk_ref, qn_ref, kn_ref, vb_ref):
        hm = _head_mean_matrix()
        qv, kv = q_ref[...], k_ref[...]
        rq = lax.rsqrt(_dot_exact_rhs(qv * qv, hm, 3) + EPS)
        rk = lax.rsqrt(_dot_exact_rhs(kv * kv, hm, 3) + EPS)
        qn_ref[...] = (qv * rq * gq_ref[...] * scale).astype(qn_ref.dtype)
        kn_ref[...] = (kv * rk * gk_ref[...]).astype(kn_ref.dtype)
        vb_ref[...] = v_ref[...].astype(vb_ref.dtype)

    col = lambda j: pl.BlockSpec((tm, SB_D), lambda i: (i, j))
    vec = pl.BlockSpec((1, SB_D), lambda i: (0, 0))
    out = pl.BlockSpec((tm, SB_D), lambda i: (i, 0))
    return pl.pallas_call(
        body, name="sb_prep", grid=(T // tm,),
        in_specs=[col(3), col(4), col(5), vec, vec], out_specs=[out, out, out],
        out_shape=[jax.ShapeDtypeStruct((T, SB_D), CDT)] * 3,
        compiler_params=_cparams(("parallel",)),
    )(proj, proj, proj, gq, gk)


def _sb_prep_bwd(dqn, dkn, dv, proj, gq, gk, *, tm=256):
    T = proj.shape[0]
    scale = SB_HEAD_DIM ** -0.5

    def body(dqn_ref, dkn_ref, dv_ref, q_ref, k_ref, gq_ref, gk_ref, dsb_ref, dgq_ref, dgk_ref):
        i = pl.program_id(0)

        @pl.when(i == 0)
        def _():
            dgq_ref[...] = jnp.zeros_like(dgq_ref)
            dgk_ref[...] = jnp.zeros_like(dgk_ref)

        hm = _head_mean_matrix()

        def one(dn_ref, x_ref, g_ref, dg_ref, sc, lo):
            xv = x_ref[...]
            dnv = dn_ref[...] * sc
            r = lax.rsqrt(_dot_exact_rhs(xv * xv, hm, 3) + EPS)
            u = dnv * g_ref[...]
            dot = _dot_exact_rhs(u * xv, hm, 3)
            dsb_ref[:, lo:lo + SB_D] = (r * u - xv * (r * r * r * dot)).astype(dsb_ref.dtype)
            dg_ref[...] += jnp.sum(dnv * xv * r, axis=0, keepdims=True)

        one(dqn_ref, q_ref, gq_ref, dgq_ref, scale, 0)
        one(dkn_ref, k_ref, gk_ref, dgk_ref, 1.0, SB_D)
        dsb_ref[:, 2 * SB_D:3 * SB_D] = dv_ref[...].astype(dsb_ref.dtype)

    col = lambda j: pl.BlockSpec((tm, SB_D), lambda i: (i, j))
    vec = pl.BlockSpec((1, SB_D), lambda i: (0, 0))
    row = pl.BlockSpec((tm, SB_D), lambda i: (i, 0))
    return pl.pallas_call(
        body, name="sb_prep_bwd", grid=(T // tm,),
        in_specs=[row, row, row, col(3), col(4), vec, vec],
        out_specs=[pl.BlockSpec((tm, 3 * SB_D), lambda i: (i, 0)), vec, vec],
        out_shape=[jax.ShapeDtypeStruct((T, 3 * SB_D), CDT), jax.ShapeDtypeStruct((1, SB_D), F32),
                   jax.ShapeDtypeStruct((1, SB_D), F32)],
        compiler_params=_cparams(("arbitrary",)),
    )(dqn, dkn, dv, proj, proj, gq, gk)


def _sb_masks():
    lane = lax.broadcasted_iota(jnp.int32, (1, 128), 1)
    m = [lane < SB_HEAD_DIM, lane >= SB_HEAD_DIM]
    return m, [x.astype(F32) for x in m]


def _sb_fwd(qn, kn, vb, blocks):
    T = qn.shape[0]
    nq = T // SB_TILE
    B, P = SB_TILE, SB_PAIR
    hs = range(2)
    n = len(blocks)
    nhp = SB_D // P

    def body(*refs):
        q_ref, k_ref, v_ref = refs[:3]
        o_ref, l_ref, done_ref = refs[3 + n:6 + n]
        acc_ref = refs[6 + 2 * n]
        hp, qb = pl.program_id(0), pl.program_id(1)
        g_start, g_forward, g_finish = _gather_phases(refs[3:3 + n], refs[6 + n:6 + 2 * n], *refs[7 + 2 * n:])
        pl.when((hp == 0) & (qb == 0))(g_start)
        pl.when((hp == nhp - 1) & (qb == 0))(g_forward)
        m, mf = _sb_masks()
        row = lax.broadcasted_iota(jnp.int32, (B, B), 0)
        col = lax.broadcasted_iota(jnp.int32, (B, B), 1)
        later = (row > col).astype(BF16)
        past = col < row
        q2 = q_ref[...]
        qm = [jnp.where(m[h], q2, jnp.zeros_like(q2)) for h in hs]
        acc_ref[...] = jnp.zeros_like(acc_ref)

        def keys(kb):
            return k_ref[pl.ds(pl.multiple_of(kb * B, B), B), :]

        def values(kb):
            return v_ref[pl.ds(pl.multiple_of(kb * B, B), B), :]

        def scores(kb):
            k2 = keys(kb)
            return [_dg(qm[h], k2, _NT) for h in hs]

        def run(tiles, R, diag):
            zs, cum, rsum = {}, {}, {}
            for t, (z, _) in enumerate(tiles):
                for h in hs:
                    sp = _softplus(z[h])
                    lk = jnp.where(past, sp, 0.0) if diag else sp
                    zs[t, h] = z[h] - sp
                    cum[t, h] = _dot_exact_rhs(lk, later, SB_SPLIT_LK)
                    rsum[t, h] = jnp.sum(lk, axis=1, keepdims=True)
            R = list(R)
            for t, (_, kb) in enumerate(tiles):
                v2 = values(kb)
                for h in hs:
                    w = jnp.exp(zs[t, h] - (cum[t, h] + R[h]))
                    if diag:
                        w = jnp.where(past, w, 0.0)
                    acc_ref[h] += _dg(w, v2, _NN)
                R = [R[h] + rsum[t, h] for h in hs]
            return tuple(R)

        zero = jnp.zeros((B, 1), F32)
        R = run([(scores(qb), qb)], (zero, zero), True)
        R = lax.cond(qb % 2 == 1, lambda r: run([(scores(qb - 1), qb - 1)], r, False), lambda r: r, R)
        npairs = qb // 2

        def live(r):
            return (jnp.minimum(jnp.min(r[0]), jnp.min(r[1])) < SB_DEAD).astype(jnp.int32)

        def pair(carry):
            i, r, za, zb = carry[0], carry[2:4], carry[4:6], carry[6:8]
            ka = 2 * (npairs - 1 - i) + 1
            nxt = (scores(jnp.maximum(ka - 2, 0)), scores(jnp.maximum(ka - 3, 0)))
            r = run([(za, ka), (zb, ka - 1)], r, False)
            return (i + 1, live(r), *r, *nxt[0], *nxt[1])

        first = jnp.maximum(2 * npairs - 1, 0)
        out = lax.while_loop(lambda c: (c[0] < npairs) & (c[1] > 0), pair,
                             (jnp.int32(0), live(R), *R, *scores(first), *scores(jnp.maximum(first - 1, 0))))
        R = out[2:4]
        done_ref[hp, qb] = out[0]
        o_ref[...] = (acc_ref[0] * mf[0] + acc_ref[1] * mf[1]).astype(o_ref.dtype)
        l_ref[0] = R[0] * mf[0] + R[1] * mf[1]
        pl.when((hp == nhp - 1) & (qb == nq - 1))(g_finish)

    slab = pl.BlockSpec((T, P), lambda hp, qb: (0, hp))
    blk = pl.BlockSpec((B, P), lambda hp, qb: (qb, hp))
    anyspec = pl.BlockSpec(memory_space=pl.ANY)
    outs = pl.pallas_call(
        body, name="sb_fwd", grid=(nhp, nq),
        in_specs=[blk, slab, slab] + [anyspec] * n,
        out_specs=[blk, pl.BlockSpec((1, B, P), lambda hp, qb: (hp, qb, 0)), pl.BlockSpec(memory_space=pltpu.SMEM)]
        + [anyspec] * n,
        out_shape=[jax.ShapeDtypeStruct((T, SB_D), CDT), jax.ShapeDtypeStruct((nhp, T, P), F32),
                   jax.ShapeDtypeStruct((nhp, nq), jnp.int32)]
        + [jax.ShapeDtypeStruct((N_DEV,) + b.shape, b.dtype) for b in blocks],
        scratch_shapes=[pltpu.VMEM((2, B, P), F32)] + _exchange_sems(n),
        compiler_params=_cparams(("arbitrary", "arbitrary")),
    )(qn, kn, vb, *blocks)
    return outs[0], outs[1], outs[2], outs[3:]


def _sb_bwd(d_o, qn, kn, vb, lsum, done):
    T = qn.shape[0]
    nq = T // SB_TILE
    B, P = SB_TILE, SB_PAIR
    hs = range(2)

    def body(do_ref, q_ref, k_ref, v_ref, l_ref, done_ref, dq_ref, dk_ref, dv_ref, dqacc_ref):
        qb = pl.program_id(1)

        @pl.when(qb == 0)
        def _():
            dk_ref[...] = jnp.zeros_like(dk_ref)
            dv_ref[...] = jnp.zeros_like(dv_ref)

        m, mf = _sb_masks()
        row = lax.broadcasted_iota(jnp.int32, (B, B), 0)
        col = lax.broadcasted_iota(jnp.int32, (B, B), 1)
        upto = (row <= col).astype(BF16)
        before = (row < col).astype(BF16)
        past = col < row
        q2 = q_ref[...]
        qm = [jnp.where(m[h], q2, jnp.zeros_like(q2)) for h in hs]
        do2 = do_ref[...]
        dom = [jnp.where(m[h], do2, 0.0).astype(CDT) for h in hs]
        lb = l_ref[0]
        ltot = [lb[:, 0:1], lb[:, SB_HEAD_DIM:SB_HEAD_DIM + 1]]
        dqacc_ref[...] = jnp.zeros_like(dqacc_ref)

        def rows(kb):
            return pl.ds(pl.multiple_of(kb * B, B), B)

        def scores(kb):
            k2, v2 = k_ref[rows(kb), :], v_ref[rows(kb), :]
            return [_dg(qm[h], k2, _NT) for h in hs] + [_dg(dom[h], v2, _NT) for h in hs]

        def run(tiles, carry, diag):
            Ps, Pg = list(carry[0]), list(carry[1])
            zs, sp_, cum, rest = {}, {}, {}, {}
            for t, tl in enumerate(tiles):
                for h in hs:
                    sp = _softplus(tl[h])
                    lk = jnp.where(past, sp, 0.0) if diag else sp
                    zs[t, h], sp_[t, h] = tl[h] - sp, sp
                    cum[t, h] = _dot_exact_rhs(lk, upto, SB_SPLIT_LK)
                    rest[t, h] = ltot[h] - Ps[h]
                    Ps[h] = Ps[h] + jnp.sum(lk, axis=1, keepdims=True)
            w, g, gx = {}, {}, {}
            for t, tl in enumerate(tiles):
                for h in hs:
                    wt = jnp.exp(zs[t, h] - (rest[t, h] - cum[t, h]))
                    if diag:
                        wt = jnp.where(past, wt, 0.0)
                    w[t, h] = wt
                    g[t, h] = wt * tl[2 + h]
                    gx[t, h] = _dot_exact_rhs(g[t, h], before, SB_SPLIT_G) + Pg[h]
                    Pg[h] = Pg[h] + jnp.sum(g[t, h], axis=1, keepdims=True)
            for t, tl in enumerate(tiles):
                kb = tl[4]
                k2 = k_ref[rows(kb), :]
                for h in hs:
                    sneg = jnp.exp(-sp_[t, h])
                    dz = g[t, h] * sneg - (1.0 - sneg) * gx[t, h]
                    if diag:
                        dz = jnp.where(past, dz, 0.0)
                    dz_c = dz.astype(CDT)
                    dv_ref[rows(kb), :] += _dg(w[t, h], dom[h], _TN)
                    dk_ref[rows(kb), :] += _dg(dz_c, qm[h], _TN)
                    dqacc_ref[h] += _dg(dz_c, k2, _NN)
            return tuple(Ps), tuple(Pg)

        zero = jnp.zeros((B, 1), F32)
        npairs = qb // 2

        def pair(i, carry):
            sums, ta, tb = (carry[0:2], carry[2:4]), carry[4:8], carry[8:12]
            nxt = scores(jnp.minimum(2 * i + 2, qb)) + scores(jnp.minimum(2 * i + 3, qb))
            Ps, Pg = run([(*ta, 2 * i), (*tb, 2 * i + 1)], sums, False)
            return (*Ps, *Pg, *nxt)

        i0 = npairs - jnp.clip(done_ref[pl.program_id(0), qb], 0, npairs)
        out = lax.fori_loop(i0, npairs, pair,
                            (zero, zero, zero, zero, *scores(jnp.minimum(2 * i0, qb)), *scores(jnp.minimum(2 * i0 + 1, qb))))
        sums = (out[0:2], out[2:4])
        sums = lax.cond(qb % 2 == 1, lambda s: run([(*scores(qb - 1), qb - 1)], s, False), lambda s: s, sums)
        run([(*scores(qb), qb)], sums, True)
        dq_ref[...] = dqacc_ref[0] * mf[0] + dqacc_ref[1] * mf[1]

    slab = pl.BlockSpec((T, P), lambda hp, qb: (0, hp))
    blk = pl.BlockSpec((B, P), lambda hp, qb: (qb, hp))
    return pl.pallas_call(
        body, name="sb_bwd", grid=(SB_D // P, nq),
        in_specs=[blk, blk, slab, slab, pl.BlockSpec((1, B, P), lambda hp, qb: (hp, qb, 0)),
                  pl.BlockSpec(memory_space=pltpu.SMEM)],
        out_specs=[blk, slab, slab],
        out_shape=[jax.ShapeDtypeStruct((T, SB_D), F32)] * 3,
        scratch_shapes=[pltpu.VMEM((2, B, P), F32)],
        compiler_params=_cparams(("arbitrary", "arbitrary")),
    )(d_o, qn, kn, vb, lsum, done)


def _regroup_in_rows(wt):
    cut = 2 * GLA_DK + 2 * GLA_DV
    pad = jnp.zeros((IN_PAD - IN_WIDTH, wt.shape[1]), wt.dtype)
    return jnp.concatenate([wt[:cut], wt[cut + GLA_GATE_RANK:], wt[cut:cut + GLA_GATE_RANK], pad], axis=0)


def _ungroup_in_rows(gt):
    cut = 2 * GLA_DK + 2 * GLA_DV
    return jnp.concatenate([gt[:cut], gt[3072:3072 + GLA_GATE_RANK], gt[cut:3072]], axis=0)


def _colsum(v, *, name, tm=512):
    T, C = v.shape

    def body(v_ref, o_ref):
        i = pl.program_id(0)
        part = jnp.sum(v_ref[...], axis=0, keepdims=True)

        @pl.when(i == 0)
        def _():
            o_ref[...] = part

        @pl.when(i > 0)
        def _():
            o_ref[...] += part

    return pl.pallas_call(
        body, name=name, grid=(T // tm,),
        in_specs=[pl.BlockSpec((tm, C), lambda i: (i, 0))], out_specs=pl.BlockSpec((1, C), lambda i: (0, 0)),
        out_shape=jax.ShapeDtypeStruct((1, C), F32),
        compiler_params=_cparams(("arbitrary",)),
    )(v)


def _ffn_fwd(h, g_norm, wgu_t, wd, tag):
    hf = _rms_fwd(h, g_norm, name=f"ffn{tag}_norm")
    ab, s = _swiglu_up(hf, wgu_t, name=f"ffn{tag}_up")
    h_out = _matmul(s, wd, mode='nn', out_dtype=F32, name=f"ffn{tag}_down", tm=512, tn=D_MODEL, tk=D_FF, residual=h)
    return h_out, (hf, ab, s)


def _ffn_bwd(dh, dh_c, h_in, g_norm, wgu_t, wd, saved, tag):
    hf, ab, s = saved
    dwd = _matmul(s, dh_c, mode='tn', out_dtype=F32, name=f"ffn{tag}_dwd", tm=D_FF // 2, tn=D_MODEL, tk=TK_TOKENS)
    dab = _swiglu_dact(dh_c, wd, ab, name=f"ffn{tag}_dact")
    dwgu_t = _matmul(dab, hf, mode='tn', out_dtype=F32, name=f"ffn{tag}_dwgu", tm=D_FF // 2, tn=D_MODEL, tk=TK_TOKENS)
    dhf = _matmul(dab, wgu_t, mode='nn', out_dtype=F32, name=f"ffn{tag}_dhf", tm=256, tn=D_MODEL, tk=2 * D_FF)
    dh_in, dh_in_c, dg = _rms_bwd(dhf, h_in, g_norm, dh, name=f"ffn{tag}_dnorm")
    return dh_in, dh_in_c, dwgu_t, dwd, dg


def _late_weights(gathered):
    g_out, g_pw1, g_pw2, g_gate, g_up, g_down = gathered
    Dm = D_MODEL
    return {
        'hy_w_out': g_out.reshape(Dm, Dm),
        'cv_w_pw1_t': g_pw1.reshape(2 * Dm, Dm),
        'cv_w_pw2': g_pw2.reshape(Dm, Dm),
        'ffn_wgu_t': [jnp.concatenate([g_gate[:, l].reshape(D_FF, Dm), g_up[:, l].reshape(D_FF, Dm)], axis=0)
                      for l in range(2)],
        'ffn_w_down': [g_down[:, l].reshape(D_FF, Dm) for l in range(2)],
    }


def _local_step(x, tgt, W, late_blocks):
    row = lambda v: v.reshape(1, -1)
    win_p = _regroup_in_rows(W['hy_w_in_t'])
    wg2p = jnp.pad(W['hy_w_gate2'], ((0, 128 - GLA_GATE_RANK), (0, 0)))
    b_gate = row(W['hy_b_gate'])
    g_gla = row(W['hy_gla_norm'])
    gq = jnp.tile(W['hy_sb_q_norm'].reshape(-1), SB_D // SB_HEAD_DIM).reshape(1, SB_D)
    gk = jnp.tile(W['hy_sb_k_norm'].reshape(-1), SB_D // SB_HEAD_DIM).reshape(1, SB_D)
    w_dw = jnp.pad(W['cv_w_dw'], ((0, CONV_HALO - CONV_WIDTH), (0, 0)))
    mixn = [row(W['mix_norm'][l]) for l in range(2)]
    ffnn = [row(W['ffn_norm'][l]) for l in range(2)]

    hn0 = _rms_fwd(x, mixn[0], name="mix0_norm")
    proj = _matmul(hn0, win_p, mode='nt', out_dtype=F32, name="hy_in", tm=256, tn=IN_PAD, tk=D_MODEL)
    o_gla, o_raw, states = _gla_fwd(proj, wg2p, b_gate, g_gla)
    qn, kn, vb = _sb_prep(proj, gq, gk)
    o_sb, lsum, sb_done, gathered = _sb_fwd(qn, kn, vb, late_blocks)
    W = {**W, **_late_weights(gathered)}
    w_out, wgu, wd = W['hy_w_out'], W['ffn_wgu_t'], W['ffn_w_down']
    o_mix = jnp.concatenate([o_gla, o_sb], axis=1)
    h1 = _matmul(o_mix, w_out, mode='nn', out_dtype=F32, name="hy_out", tm=512, tn=D_MODEL, tk=D_MODEL, residual=x)
    h2, ffn0_saved = _ffn_fwd(h1, ffnn[0], wgu[0], wd[0], 0)
    hn1 = _rms_fwd(h2, mixn[1], name="mix1_norm")
    a_cv = _matmul(hn1, W['cv_w_pw1_t'], mode='nt', out_dtype=F32, name="cv_pw1", tm=512, tn=2 * D_MODEL, tk=D_MODEL,
                   bias=row(W['cv_b_pw1']))
    s_cv, u_cv, c_cv = _conv_fwd(a_cv, w_dw, row(W['cv_b_dw']), row(W['cv_ln_g']), row(W['cv_ln_b']))
    h3 = _matmul(s_cv, W['cv_w_pw2'], mode='nn', out_dtype=F32, name="cv_pw2", tm=512, tn=D_MODEL, tk=D_MODEL,
                 bias=row(W['cv_b_pw2']), residual=h2)
    h4, ffn1_saved = _ffn_fwd(h3, ffnn[1], wgu[1], wd[1], 1)
    sq_err, dy, dy_c = _loss_head(h4, tgt)

    G = {}
    dh3, dh3_c, dwgu1, dwd1, dg_ffn1 = _ffn_bwd(dy, dy_c, h3, ffnn[1], wgu[1], wd[1], ffn1_saved, 1)
    G['cv_b_pw2'] = _colsum(dh3, name="cv_db2")
    G['cv_w_pw2'] = _matmul(s_cv, dh3_c, mode='tn', out_dtype=F32, name="cv_dw2", tm=D_MODEL, tn=D_MODEL, tk=TK_TOKENS)
    ds_cv = _matmul(dh3_c, W['cv_w_pw2'], mode='nt', out_dtype=F32, name="cv_ds", tm=512, tn=D_MODEL, tk=D_MODEL)
    F8 = D_FF // N_DEV
    early_own = [G['cv_w_pw2'].reshape(N_DEV, D_MODEL // N_DEV, D_MODEL),
                 dwgu1.reshape(2, N_DEV, F8, D_MODEL).transpose(1, 0, 2, 3).reshape(N_DEV, 2 * F8, D_MODEL),
                 dwd1.reshape(N_DEV, F8, D_MODEL)]
    (da_cv, db1, dwdw, dbdw, dlng, dlnb), early_recv = _conv_bwd(
        ds_cv, c_cv, u_cv, a_cv, w_dw, row(W['cv_ln_g']), row(W['cv_ln_b']), [a.astype(BF16) for a in early_own])
    G['cv_b_pw1'] = db1
    G['cv_w_dw'] = dwdw[:CONV_WIDTH]
    G['cv_b_dw'], G['cv_ln_g'], G['cv_ln_b'] = dbdw, dlng, dlnb
    G['cv_w_pw1_t'] = _matmul(da_cv, hn1, mode='tn', out_dtype=F32, name="cv_dw1", tm=D_MODEL, tn=D_MODEL, tk=TK_TOKENS)
    dhn1 = _matmul(da_cv, W['cv_w_pw1_t'], mode='nn', out_dtype=F32, name="cv_dhn", tm=512, tn=D_MODEL, tk=2 * D_MODEL)
    dh2, dh2_c, dg_mix1 = _rms_bwd(dhn1, h2, mixn[1], dh3, name="mix1_dnorm")
    dh1, dh1_c, dwgu0, dwd0, dg_ffn0 = _ffn_bwd(dh2, dh2_c, h1, ffnn[0], wgu[0], wd[0], ffn0_saved, 0)
    G['hy_w_out'] = _matmul(o_mix, dh1_c, mode='tn', out_dtype=F32, name="hy_dwout", tm=D_MODEL, tn=D_MODEL, tk=TK_TOKENS)
    d_omix = _matmul(dh1_c, w_out, mode='nt', out_dtype=F32, name="hy_domix", tm=512, tn=D_MODEL, tk=D_MODEL)
    dgla, dglr, dwg2, dbg, dgg = _gla_bwd(d_omix[:, :GLA_DV], proj, o_raw, states, wg2p, b_gate, g_gla)
    dqn, dkn, dvs = _sb_bwd(d_omix[:, GLA_DV:], qn, kn, vb, lsum, sb_done)
    dsb, dgq, dgk = _sb_prep_bwd(dqn, dkn, dvs, proj, gq, gk)
    dproj = jnp.concatenate([dgla, dsb, dglr], axis=1)
    dwin_p = _matmul(dproj, hn0, mode='tn', out_dtype=F32, name="hy_dwin", tm=IN_PAD // 5, tn=D_MODEL, tk=TK_TOKENS)
    dhn0 = _matmul(dproj, win_p, mode='nn', out_dtype=F32, name="hy_dhn", tm=256, tn=D_MODEL, tk=IN_PAD)
    dx, _, dg_mix0 = _rms_bwd(dhn0, x, mixn[0], dh1, name="mix0_dnorm")

    G['hy_w_in_t'] = _ungroup_in_rows(dwin_p)
    G['hy_w_gate2'] = dwg2[:GLA_GATE_RANK]
    G['hy_b_gate'] = dbg
    G['hy_gla_norm'] = dgg
    G['hy_sb_q_norm'] = dgq.reshape(SB_D // SB_HEAD_DIM, SB_HEAD_DIM).sum(axis=0, keepdims=True)
    G['hy_sb_k_norm'] = dgk.reshape(SB_D // SB_HEAD_DIM, SB_HEAD_DIM).sum(axis=0, keepdims=True)
    G['mix_norm'] = jnp.concatenate([dg_mix0, dg_mix1], axis=0)
    G['ffn_norm'] = jnp.concatenate([dg_ffn0, dg_ffn1], axis=0)
    G['ffn_wgu_t0'] = dwgu0
    G['ffn_w_down0'] = dwd0
    return sq_err, dx, G, (early_own, early_recv)


MESH_IDS = pl.DeviceIdType.MESH
N_PEER = N_DEV - 1


def _exchange_sems(n):
    return [pltpu.SemaphoreType.DMA((n * N_PEER,)), pltpu.SemaphoreType.DMA((n * N_PEER,)),
            pltpu.SemaphoreType.DMA((n,))]


def _gather_phases(x_refs, out_refs, send_sems, recv_sems, local_sems):
    n = len(x_refs)
    x, y, c = lax.axis_index("x"), lax.axis_index("y"), lax.axis_index("c")
    me, sibling = (x, y, c), (x, y, 1 - c)
    chips = [(1 - x, y), (x, 1 - y), (1 - x, 1 - y)]

    def slot(a, px, py, pc):
        return out_refs[a].at[4 * px + 2 * py + pc]

    def copy(a, k, blk, to, src=None):
        return pltpu.make_async_remote_copy(
            src_ref=slot(a, *blk) if src is None else src, dst_ref=slot(a, *blk),
            send_sem=send_sems.at[a * N_PEER + k], recv_sem=recv_sems.at[a * N_PEER + k],
            device_id=to, device_id_type=MESH_IDS)

    def local(a):
        return pltpu.make_async_copy(x_refs[a], slot(a, *me), local_sems.at[a])

    def first(a):
        return [copy(a, 0, me, sibling, src=x_refs[a])] + [copy(a, 1 + j, me, (*chip, c), src=x_refs[a])
                                                           for j, chip in enumerate(chips)]

    def passed(a):
        return [copy(a, 4 + j, (*chip, c), sibling) for j, chip in enumerate(chips)]

    def start():
        for a in range(n):
            local(a).start()
        for a in range(n):
            for cp in first(a):
                cp.start()

    def forward():
        for a in range(n):
            for j, chip in enumerate(chips):
                copy(a, 1 + j, (*chip, c), me).wait_recv()
                copy(a, 4 + j, (*chip, c), sibling).start()

    def finish():
        for a in range(n):
            copy(a, 0, sibling, me).wait_recv()
            for j, chip in enumerate(chips):
                copy(a, 4 + j, (*chip, 1 - c), me).wait_recv()
        for a in range(n):
            for cp in first(a) + passed(a):
                cp.wait_send()
            local(a).wait()

    return start, forward, finish


def _all_gather(blocks):
    n = len(blocks)

    def body(*refs):
        start, forward, finish = _gather_phases(refs[:n], refs[n:2 * n], *refs[2 * n:])
        start()
        forward()
        finish()

    anyspec = pl.BlockSpec(memory_space=pl.ANY)
    return pl.pallas_call(
        body, name="fsdp_all_gather",
        out_shape=[jax.ShapeDtypeStruct((N_DEV,) + b.shape, b.dtype) for b in blocks],
        in_specs=[anyspec] * n, out_specs=[anyspec] * n,
        scratch_shapes=_exchange_sems(n),
    )(*blocks)


def _scatter_phases(s_refs, r_refs, send_sems, recv_sems, local_sems):
    n = len(s_refs)
    x, y, c = lax.axis_index("x"), lax.axis_index("y"), lax.axis_index("c")
    me = 4 * x + 2 * y + c

    def local(a):
        return pltpu.make_async_copy(s_refs[a].at[me], r_refs[a].at[me], local_sems.at[a])

    def copy(a, k):
        px, py, pc = x ^ ((k >> 2) & 1), y ^ ((k >> 1) & 1), c ^ (k & 1)
        return pltpu.make_async_remote_copy(
            src_ref=s_refs[a].at[4 * px + 2 * py + pc], dst_ref=r_refs[a].at[me],
            send_sem=send_sems.at[a * N_PEER + k - 1], recv_sem=recv_sems.at[a * N_PEER + k - 1],
            device_id=(px, py, pc), device_id_type=MESH_IDS)

    def start():
        for a in range(n):
            local(a).start()
        for a in range(n):
            for k in range(1, N_DEV):
                copy(a, k).start()

    def finish():
        for a in range(n):
            for k in range(1, N_DEV):
                copy(a, k).wait()
            local(a).wait()

    return start, finish


def _scatter_exchange(sends):
    n = len(sends)

    def body(*refs):
        start, finish = _scatter_phases(refs[:n], refs[n:2 * n], *refs[2 * n:])
        start()
        finish()

    anyspec = pl.BlockSpec(memory_space=pl.ANY)
    return pl.pallas_call(
        body, name="fsdp_scatter_exchange",
        out_shape=[jax.ShapeDtypeStruct(s.shape, s.dtype) for s in sends],
        in_specs=[anyspec] * n, out_specs=[anyspec] * n,
        scratch_shapes=_exchange_sems(n),
    )(*sends)


def _sum_contrib(recv, own, *, name, tr):
    _, R, C = recv.shape
    assert R % tr == 0

    def body(r_ref, own_ref, g_ref):
        me = 4 * lax.axis_index("x") + 2 * lax.axis_index("y") + lax.axis_index("c")
        g = jnp.zeros((tr, C), F32)
        for s in range(N_DEV):
            g = g + jnp.where(me == s, own_ref[...], r_ref[s].astype(F32))
        g_ref[...] = g

    row = pl.BlockSpec((tr, C), lambda i: (i, 0))
    return pl.pallas_call(
        body, name=name, grid=(R // tr,),
        in_specs=[pl.BlockSpec((N_DEV, tr, C), lambda i: (0, i, 0)), row], out_specs=row,
        out_shape=jax.ShapeDtypeStruct((R, C), F32),
        compiler_params=_cparams(("parallel",)),
    )(recv, own)


def _adamw(g, w, m, v, *, name, tr):
    R, C = g.shape
    assert R % tr == 0

    def body(g_ref, w_ref, m_ref, v_ref, d_ref, mo_ref, vo_ref):
        gv = g_ref[...]
        mn = ADAM_B1 * m_ref[...] + (1.0 - ADAM_B1) * gv
        vn = ADAM_B2 * v_ref[...] + (1.0 - ADAM_B2) * (gv * gv)
        m_hat = mn / (1.0 - ADAM_B1 ** ADAM_STEP)
        v_hat = vn / (1.0 - ADAM_B2 ** ADAM_STEP)
        d_ref[...] = -ADAM_LR * (m_hat / (jnp.sqrt(v_hat) + ADAM_EPS) + ADAM_WD * w_ref[...])
        mo_ref[...] = mn
        vo_ref[...] = vn

    row = pl.BlockSpec((tr, C), lambda i: (i, 0))
    return pl.pallas_call(
        body, name=name, grid=(R // tr,),
        in_specs=[row] * 4, out_specs=[row] * 3,
        out_shape=[jax.ShapeDtypeStruct((R, C), F32)] * 3,
        compiler_params=_cparams(("parallel",)),
    )(g, w, m, v)


SMALL_SHARDED = ('hy_w_gate2', 'cv_b_pw1', 'cv_w_dw', 'cv_b_dw', 'cv_ln_g', 'cv_ln_b', 'cv_b_pw2')
SMALL_REPLICATED = ('mix_norm', 'ffn_norm', 'hy_b_gate', 'hy_gla_norm', 'hy_sb_q_norm', 'hy_sb_k_norm')
LANES = 128


def _small_rows(n):
    return -(-n // (8 * LANES)) * 8


def _pack_small(parts, lead=()):
    out = []
    for p in parts:
        n = p.shape[-1]
        p = jnp.pad(p, [(0, 0)] * len(lead) + [(0, _small_rows(n) * LANES - n)])
        out.append(p.reshape(*lead, _small_rows(n), LANES))
    return jnp.concatenate(out, axis=len(lead))


def _unpack_small(packed, sizes, lead=()):
    out, r0 = [], 0
    for n in sizes:
        r = _small_rows(n)
        out.append(packed[..., r0:r0 + r, :].reshape(*lead, r * LANES)[..., :n])
        r0 += r
    return out


def _to_blocks(full, axis):
    shp = full.shape
    t = full.reshape(shp[:axis] + (N_DEV, shp[axis] // N_DEV) + shp[axis + 1:])
    return jnp.moveaxis(t, axis, 0)


def _from_blocks(blocks, axis):
    t = jnp.moveaxis(blocks, 0, axis)
    shp = t.shape
    return t.reshape(shp[:axis] + (shp[axis] * shp[axis + 1],) + shp[axis + 2:])


def kernel(x, mix_norm, ffn_norm, hy_w_in, hy_w_gate2, hy_b_gate, hy_gla_norm, hy_sb_q_norm, hy_sb_k_norm, hy_w_out, cv_w_pw1, cv_b_pw1, cv_w_dw, cv_b_dw, cv_ln_g, cv_ln_b, cv_w_pw2, cv_b_pw2, ffn_w_gate, ffn_w_up, ffn_w_down, loss_target, m_mix_norm, m_ffn_norm, m_hy_w_in, m_hy_w_gate2, m_hy_b_gate, m_hy_gla_norm, m_hy_sb_q_norm, m_hy_sb_k_norm, m_hy_w_out, m_cv_w_pw1, m_cv_b_pw1, m_cv_w_dw, m_cv_b_dw, m_cv_ln_g, m_cv_ln_b, m_cv_w_pw2, m_cv_b_pw2, m_ffn_w_gate, m_ffn_w_up, m_ffn_w_down, v_mix_norm, v_ffn_norm, v_hy_w_in, v_hy_w_gate2, v_hy_b_gate, v_hy_gla_norm, v_hy_sb_q_norm, v_hy_sb_k_norm, v_hy_w_out, v_cv_w_pw1, v_cv_b_pw1, v_cv_w_dw, v_cv_b_dw, v_cv_ln_g, v_cv_ln_b, v_cv_w_pw2, v_cv_b_pw2, v_ffn_w_gate, v_ffn_w_up, v_ffn_w_down):
    w_loc = dict(zip(WEIGHT_NAMES, (mix_norm, ffn_norm, hy_w_in, hy_w_gate2, hy_b_gate, hy_gla_norm, hy_sb_q_norm, hy_sb_k_norm, hy_w_out, cv_w_pw1, cv_b_pw1, cv_w_dw, cv_b_dw, cv_ln_g, cv_ln_b, cv_w_pw2, cv_b_pw2, ffn_w_gate, ffn_w_up, ffn_w_down)))
    m_loc = dict(zip(WEIGHT_NAMES, (m_mix_norm, m_ffn_norm, m_hy_w_in, m_hy_w_gate2, m_hy_b_gate, m_hy_gla_norm, m_hy_sb_q_norm, m_hy_sb_k_norm, m_hy_w_out, m_cv_w_pw1, m_cv_b_pw1, m_cv_w_dw, m_cv_b_dw, m_cv_ln_g, m_cv_ln_b, m_cv_w_pw2, m_cv_b_pw2, m_ffn_w_gate, m_ffn_w_up, m_ffn_w_down)))
    v_loc = dict(zip(WEIGHT_NAMES, (v_mix_norm, v_ffn_norm, v_hy_w_in, v_hy_w_gate2, v_hy_b_gate, v_hy_gla_norm, v_hy_sb_q_norm, v_hy_sb_k_norm, v_hy_w_out, v_cv_w_pw1, v_cv_b_pw1, v_cv_w_dw, v_cv_b_dw, v_cv_ln_g, v_cv_ln_b, v_cv_w_pw2, v_cv_b_pw2, v_ffn_w_gate, v_ffn_w_up, v_ffn_w_down)))

    Dm, F8 = D_MODEL, D_FF // N_DEV
    tr_ = lambda a: jnp.swapaxes(a, -1, -2)

    small_local = _pack_small([w_loc[n].reshape(-1) for n in SMALL_SHARDED])
    g_in, g_small = _all_gather([tr_(hy_w_in[0]).astype(BF16), small_local])
    late_blocks = [hy_w_out[0].astype(BF16),
                   tr_(cv_w_pw1[0]).astype(BF16),
                   cv_w_pw2[0].astype(BF16),
                   tr_(ffn_w_gate).astype(BF16),
                   tr_(ffn_w_up).astype(BF16),
                   ffn_w_down.astype(BF16)]
    small_sizes = [w_loc[n].size for n in SMALL_SHARDED]
    small_full = dict(zip(SMALL_SHARDED, _unpack_small(g_small, small_sizes, lead=(N_DEV,))))
    W = {n: w_loc[n] for n in SMALL_REPLICATED}
    W['hy_w_in_t'] = g_in.reshape(IN_WIDTH, Dm)
    W['hy_w_gate2'] = _from_blocks(small_full['hy_w_gate2'].reshape(N_DEV, GLA_GATE_RANK, GLA_DK // N_DEV), 1).astype(BF16)
    W['cv_w_dw'] = _from_blocks(small_full['cv_w_dw'].reshape(N_DEV, CONV_WIDTH, Dm // N_DEV), 1)
    for n in ('cv_b_pw1', 'cv_b_dw', 'cv_ln_g', 'cv_ln_b', 'cv_b_pw2'):
        W[n] = small_full[n].reshape(-1)

    sq_err, dx, G, (early_own, early_recv) = _local_step(x[0], loss_target[0], W, late_blocks)

    own_f32 = [
        G['hy_w_in_t'].reshape(N_DEV, IN_WIDTH // N_DEV, Dm),
        G['hy_w_out'].reshape(N_DEV, Dm // N_DEV, Dm),
        G['cv_w_pw1_t'].reshape(N_DEV, 2 * Dm // N_DEV, Dm),
        G['ffn_wgu_t0'].reshape(2, N_DEV, F8, Dm).transpose(1, 0, 2, 3).reshape(N_DEV, 2 * F8, Dm),
        G['ffn_w_down0'].reshape(N_DEV, F8, Dm),
    ]
    small_parts = []
    for n in SMALL_SHARDED:
        axis = SHARD_AXIS[n] - 1
        shard = w_loc[n].shape[1:]
        full = shard[:axis] + (shard[axis] * N_DEV,) + shard[axis + 1:]
        small_parts.append(_to_blocks(G[n].reshape(full), axis).reshape(N_DEV, -1))
    for n in SMALL_REPLICATED:
        small_parts.append(jnp.broadcast_to(G[n].reshape(1, -1), (N_DEV, G[n].size)))
    small_parts.append(jnp.broadcast_to(sq_err.reshape(1, 1), (N_DEV, 1)))
    send_small = _pack_small(small_parts, lead=(N_DEV,))
    recv = _scatter_exchange([a.astype(BF16) for a in own_f32] + [send_small])
    me = 4 * lax.axis_index("x") + 2 * lax.axis_index("y") + lax.axis_index("c")
    tags = ['hy_w_in', 'hy_w_out', 'cv_w_pw1', 'ffn_wgu0', 'ffn_w_down0', 'small', 'cv_w_pw2', 'ffn_wgu1', 'ffn_w_down1']
    own_all = own_f32 + [send_small] + list(early_own)
    recv_all = list(recv) + list(early_recv)
    gsum = dict((t, _sum_contrib(r, lax.dynamic_index_in_dim(o, me, 0, keepdims=False), name=f"sum_{t}", tr=r.shape[1]))
                for t, r, o in zip(tags, recv_all, own_all))

    grad = {}
    grad['hy_w_in'] = tr_(gsum['hy_w_in'])[None]
    grad['hy_w_out'] = gsum['hy_w_out'][None]
    grad['cv_w_pw1'] = tr_(gsum['cv_w_pw1'])[None]
    grad['cv_w_pw2'] = gsum['cv_w_pw2'][None]
    gu = jnp.stack([gsum['ffn_wgu0'], gsum['ffn_wgu1']]).reshape(2, 2, F8, Dm)
    grad['ffn_w_gate'] = tr_(gu[:, 0])
    grad['ffn_w_up'] = tr_(gu[:, 1])
    grad['ffn_w_down'] = jnp.stack([gsum['ffn_w_down0'], gsum['ffn_w_down1']])
    small_names = SMALL_SHARDED + SMALL_REPLICATED
    small_all = [w_loc[n].size for n in small_names]
    *small_grads, sq_sum = _unpack_small(gsum['small'], small_all + [1])
    for n, a in zip(small_names, small_grads):
        grad[n] = a.reshape(w_loc[n].shape)
    loss = 0.5 / Dm * sq_sum[0]

    delta, new_m, new_v = {}, {}, {}
    view = {'hy_w_in': (Dm, 256), 'hy_w_out': (Dm // N_DEV, Dm // N_DEV), 'cv_w_pw1': (Dm, 256),
            'cv_w_pw2': (Dm // N_DEV, Dm // N_DEV), 'ffn_w_gate': (2 * Dm, 256), 'ffn_w_up': (2 * Dm, 256),
            'ffn_w_down': (2 * F8, F8)}
    for n, (rows, tr) in view.items():
        shp = w_loc[n].shape
        outs = _adamw(grad[n].reshape(rows, -1), w_loc[n].reshape(rows, -1), m_loc[n].reshape(rows, -1),
                      v_loc[n].reshape(rows, -1), name=f"adamw_{n}", tr=tr)
        delta[n], new_m[n], new_v[n] = (o.reshape(shp) for o in outs)
    packed = [_pack_small([d[n].reshape(-1) for n in small_names] + [jnp.zeros((1,), F32)]) for d in (w_loc, m_loc, v_loc)]
    outs = _adamw(gsum['small'], *packed, name="adamw_small", tr=gsum['small'].shape[0])
    for dst, o in zip((delta, new_m, new_v), outs):
        for n, a in zip(small_names, _unpack_small(o, small_all)):
            dst[n] = a.reshape(w_loc[n].shape)

    return (loss, dx[None], *[grad[n] for n in WEIGHT_NAMES], *[delta[n] for n in WEIGHT_NAMES],
            *[new_m[n] for n in WEIGHT_NAMES], *[new_v[n] for n in WEIGHT_NAMES])
```

```python
import jax
import jax.numpy as jnp
from jax import lax
from jax.experimental import pallas as pl
from jax.experimental.pallas import tpu as pltpu

F32 = jnp.float32
BF16 = jnp.bfloat16
CDT = jnp.bfloat16

D_MODEL = 1024
EPS = 1e-6
CHUNK = 64
GLA_HEADS = 4
GLA_HEAD_K = 64
GLA_HEAD_V = 128
GLA_DK = GLA_HEADS * GLA_HEAD_K
GLA_DV = GLA_HEADS * GLA_HEAD_V
GLA_GATE_RANK = 16
GLA_GATE_NORMALIZER = 16.0
SB_HEAD_DIM = 64
SB_D = 512
SB_TILE = 256
SB_PAIR = 128
SB_SPLIT_LK = 2
SB_SPLIT_G = 1
SB_DEAD = 120.0
IN_WIDTH = 3088
IN_PAD = 3200
CONV_WIDTH = 31
CONV_HALO = 32
D_FF = 2816
N_DEV = 8

ADAM_LR = 0.001
ADAM_B1 = 0.9
ADAM_B2 = 0.999
ADAM_EPS = 1e-08
ADAM_WD = 0.01
ADAM_STEP = 10

VMEM_LIMIT = 56 * 1024 * 1024
TK_TOKENS = 2048

WEIGHT_NAMES = ['mix_norm', 'ffn_norm', 'hy_w_in', 'hy_w_gate2', 'hy_b_gate', 'hy_gla_norm', 'hy_sb_q_norm',
                'hy_sb_k_norm', 'hy_w_out', 'cv_w_pw1', 'cv_b_pw1', 'cv_w_dw', 'cv_b_dw', 'cv_ln_g', 'cv_ln_b',
                'cv_w_pw2', 'cv_b_pw2', 'ffn_w_gate', 'ffn_w_up', 'ffn_w_down']
SHARD_AXIS = {'mix_norm': None, 'ffn_norm': None, 'hy_w_in': 2, 'hy_w_gate2': 2, 'hy_b_gate': None,
              'hy_gla_norm': None, 'hy_sb_q_norm': None, 'hy_sb_k_norm': None, 'hy_w_out': 1, 'cv_w_pw1': 2,
              'cv_b_pw1': 1, 'cv_w_dw': 2, 'cv_b_dw': 1, 'cv_ln_g': 1, 'cv_ln_b': 1, 'cv_w_pw2': 1, 'cv_b_pw2': 1,
              'ffn_w_gate': 2, 'ffn_w_up': 2, 'ffn_w_down': 1}


def _cparams(sem=None, vmem=VMEM_LIMIT):
    return pltpu.CompilerParams(dimension_semantics=sem, vmem_limit_bytes=vmem)


def _log_sigmoid(x):
    return jnp.minimum(x, 0.0) - jnp.log1p(jnp.exp(-jnp.abs(x)))


def _sigmoid(x):
    return 1.0 / (1.0 + jnp.exp(-x))


def _softplus(x):
    return jnp.maximum(x, 0.0) + jnp.log(1.0 + jnp.exp(-jnp.abs(x)))


def _split_bf16(x, n):
    parts = []
    rem = x
    for _ in range(n):
        p = rem.astype(BF16)
        parts.append(p)
        rem = rem - p.astype(F32)
    return parts


def _dot_exact_rhs(x, m, n):
    return sum(jnp.dot(p, m, preferred_element_type=F32) for p in _split_bf16(x, n))


def _dot_exact_lhs(m, x, n):
    return sum(jnp.dot(m, p, preferred_element_type=F32) for p in _split_bf16(x, n))


_NN = (((1,), (0,)), ((), ()))
_NT = (((1,), (1,)), ((), ()))
_TN = (((0,), (0,)), ((), ()))


def _dg(a, b, dn):
    return lax.dot_general(a.astype(CDT), b.astype(CDT), dn, preferred_element_type=F32)


def _matmul(a, b, *, mode, out_dtype, name, tm, tn, tk, bias=None, residual=None):
    if mode == 'nn':
        (M, K), (K2, N) = a.shape, b.shape
    elif mode == 'nt':
        (M, K), (N, K2) = a.shape, b.shape
    else:
        (K, M), (K2, N) = a.shape, b.shape
    assert K == K2 and M % tm == 0 and N % tn == 0 and K % tk == 0, (name, a.shape, b.shape, tm, tn, tk)
    nk = K // tk
    a_spec = pl.BlockSpec((tk, tm), lambda i, j, k: (k, i)) if mode == 'tn' else pl.BlockSpec((tm, tk), lambda i, j, k: (i, k))
    b_spec = pl.BlockSpec((tn, tk), lambda i, j, k: (j, k)) if mode == 'nt' else pl.BlockSpec((tk, tn), lambda i, j, k: (k, j))
    dn = {'nn': _NN, 'nt': _NT, 'tn': _TN}[mode]
    has_bias, has_res = bias is not None, residual is not None

    def body(*refs):
        a_ref, b_ref = refs[0], refs[1]
        pos = 2
        bias_ref = res_ref = None
        if has_bias:
            bias_ref = refs[pos]
            pos += 1
        if has_res:
            res_ref = refs[pos]
            pos += 1
        o_ref = refs[pos]
        acc_ref = refs[pos + 1] if nk > 1 else None
        p = _dg(a_ref[...], b_ref[...], dn)

        def finish(acc):
            if has_bias:
                acc = acc + bias_ref[...]
            if has_res:
                acc = res_ref[...] + acc
            o_ref[...] = acc.astype(o_ref.dtype)

        if nk == 1:
            finish(p)
        else:
            k = pl.program_id(2)

            @pl.when(k == 0)
            def _():
                acc_ref[...] = p

            @pl.when(k > 0)
            def _():
                acc_ref[...] += p

            @pl.when(k == nk - 1)
            def _():
                finish(acc_ref[...])

    in_specs = [a_spec, b_spec]
    args = [a, b]
    if has_bias:
        in_specs.append(pl.BlockSpec((1, tn), lambda i, j, k: (0, j)))
        args.append(bias)
    if has_res:
        in_specs.append(pl.BlockSpec((tm, tn), lambda i, j, k: (i, j)))
        args.append(residual)
    return pl.pallas_call(
        body, name=name, grid=(M // tm, N // tn, nk),
        in_specs=in_specs, out_specs=pl.BlockSpec((tm, tn), lambda i, j, k: (i, j)),
        out_shape=jax.ShapeDtypeStruct((M, N), out_dtype),
        scratch_shapes=[pltpu.VMEM((tm, tn), F32)] if nk > 1 else [],
        compiler_params=_cparams(("parallel", "parallel", "arbitrary")),
    )(*args)


def _rms_fwd(x, g, *, name, tm=512):
    T, Dm = x.shape

    def body(x_ref, g_ref, o_ref):
        xv = x_ref[...]
        r = lax.rsqrt(jnp.mean(xv * xv, axis=-1, keepdims=True) + EPS)
        o_ref[...] = (xv * r * g_ref[...]).astype(o_ref.dtype)

    return pl.pallas_call(
        body, name=name, grid=(T // tm,),
        in_specs=[pl.BlockSpec((tm, Dm), lambda i: (i, 0)), pl.BlockSpec((1, Dm), lambda i: (0, 0))],
        out_specs=pl.BlockSpec((tm, Dm), lambda i: (i, 0)),
        out_shape=jax.ShapeDtypeStruct((T, Dm), CDT),
        compiler_params=_cparams(("parallel",)),
    )(x, g)


def _rms_bwd(dy, x, g, resid, *, name, tm=512):
    T, Dm = x.shape

    def body(dy_ref, x_ref, g_ref, res_ref, dx_ref, dxb_ref, dg_ref):
        i = pl.program_id(0)
        xv, dyv = x_ref[...], dy_ref[...]
        r = lax.rsqrt(jnp.mean(xv * xv, axis=-1, keepdims=True) + EPS)
        u = dyv * g_ref[...]
        dot = jnp.mean(u * xv, axis=-1, keepdims=True)
        dx = res_ref[...] + (r * u - xv * (r * r * r * dot))
        dx_ref[...] = dx
        dxb_ref[...] = dx.astype(dxb_ref.dtype)
        part = jnp.sum(dyv * xv * r, axis=0, keepdims=True)

        @pl.when(i == 0)
        def _():
            dg_ref[...] = part

        @pl.when(i > 0)
        def _():
            dg_ref[...] += part

    row = pl.BlockSpec((tm, Dm), lambda i: (i, 0))
    vec = pl.BlockSpec((1, Dm), lambda i: (0, 0))
    return pl.pallas_call(
        body, name=name, grid=(T // tm,),
        in_specs=[row, row, vec, row], out_specs=[row, row, vec],
        out_shape=[jax.ShapeDtypeStruct((T, Dm), F32), jax.ShapeDtypeStruct((T, Dm), CDT),
                   jax.ShapeDtypeStruct((1, Dm), F32)],
        compiler_params=_cparams(("arbitrary",)),
    )(dy, x, g, resid)


def _loss_head(y, tgt, *, tm=512):
    T, Dm = y.shape

    def body(y_ref, t_ref, s_ref, dy_ref, dyb_ref):
        i = pl.program_id(0)
        e = y_ref[...] - t_ref[...]
        dy = e * (1.0 / Dm)
        dy_ref[...] = dy
        dyb_ref[...] = dy.astype(dyb_ref.dtype)
        part = jnp.sum(jnp.sum(e * e, axis=1, keepdims=True), axis=0, keepdims=True)

        @pl.when(i == 0)
        def _():
            s_ref[...] = part

        @pl.when(i > 0)
        def _():
            s_ref[...] += part

    row = pl.BlockSpec((tm, Dm), lambda i: (i, 0))
    return pl.pallas_call(
        body, name="loss_head", grid=(T // tm,),
        in_specs=[row, row], out_specs=[pl.BlockSpec((1, 1), lambda i: (0, 0)), row, row],
        out_shape=[jax.ShapeDtypeStruct((1, 1), F32), jax.ShapeDtypeStruct((T, Dm), F32),
                   jax.ShapeDtypeStruct((T, Dm), CDT)],
        compiler_params=_cparams(("arbitrary",)),
    )(y, tgt)


def _swiglu_up(hf, wgu_t, *, name, tm=256):
    T, Dm = hf.shape
    F2 = wgu_t.shape[0]
    F = F2 // 2

    def body(h_ref, w_ref, ab_ref, s_ref):
        p = _dg(h_ref[...], w_ref[...], _NT)
        ab_ref[...] = p.astype(ab_ref.dtype)
        a = p[:, :F]
        b = p[:, F:]
        s_ref[...] = (a * _sigmoid(a) * b).astype(s_ref.dtype)

    return pl.pallas_call(
        body, name=name, grid=(T // tm,),
        in_specs=[pl.BlockSpec((tm, Dm), lambda i: (i, 0)), pl.BlockSpec((F2, Dm), lambda i: (0, 0))],
        out_specs=[pl.BlockSpec((tm, F2), lambda i: (i, 0)), pl.BlockSpec((tm, F), lambda i: (i, 0))],
        out_shape=[jax.ShapeDtypeStruct((T, F2), CDT), jax.ShapeDtypeStruct((T, F), CDT)],
        compiler_params=_cparams(("parallel",)),
    )(hf, wgu_t)


def _swiglu_dact(dh_c, wd, ab, *, name, tm=256):
    T, Dm = dh_c.shape
    F2 = ab.shape[1]
    F = F2 // 2

    def body(dh_ref, w_ref, ab_ref, o_ref):
        dsv = _dg(dh_ref[...], w_ref[...], _NT)
        a = ab_ref[:, :F].astype(F32)
        b = ab_ref[:, F:].astype(F32)
        sg = _sigmoid(a)
        o_ref[:, :F] = (dsv * b * (sg * (1.0 + a * (1.0 - sg)))).astype(o_ref.dtype)
        o_ref[:, F:] = (dsv * (a * sg)).astype(o_ref.dtype)

    return pl.pallas_call(
        body, name=name, grid=(T // tm,),
        in_specs=[pl.BlockSpec((tm, Dm), lambda i: (i, 0)), pl.BlockSpec((F, Dm), lambda i: (0, 0)),
                  pl.BlockSpec((tm, F2), lambda i: (i, 0))],
        out_specs=pl.BlockSpec((tm, F2), lambda i: (i, 0)),
        out_shape=jax.ShapeDtypeStruct((T, F2), CDT),
        compiler_params=_cparams(("parallel",)),
    )(dh_c, wd, ab)


SUBLANES = 8


def _shifted_copies(buf, shifted, tm):
    n = tm + CONV_HALO - SUBLANES
    for b in range(1, SUBLANES):
        shifted[b - 1] = buf[pl.ds(b, n), :]


CONV_ROWS = 64
CONV_LANES = 128


def _rows_from(buf, shifted, offset, rows, r0, cols):
    a, b = divmod(offset, SUBLANES)
    if b == 0:
        return buf[pl.ds(r0 + SUBLANES * a, rows), cols]
    return shifted[b - 1, pl.ds(r0 + SUBLANES * a, rows), cols]


def _conv_fwd(a, w_dw, b_dw, ln_g, ln_b, *, tm=256):
    T = a.shape[0]
    Dm = D_MODEL

    def body(a_ref, w_ref, bdw_ref, g_ref, b_ref, s_ref, u_ref, c_ref, ubuf, shifted):
        i = pl.program_id(0)

        @pl.when(i == 0)
        def _():
            ubuf[0:CONV_HALO, :] = jnp.zeros((CONV_HALO, Dm), F32)

        @pl.when(i > 0)
        def _():
            ubuf[0:CONV_HALO, :] = ubuf[tm:tm + CONV_HALO, :]

        u = a_ref[:, :Dm] * _sigmoid(a_ref[:, Dm:])
        ubuf[CONV_HALO:CONV_HALO + tm, :] = u
        u_ref[...] = u
        _shifted_copies(ubuf, shifted, tm)
        acc = jnp.zeros((tm, Dm), F32) + bdw_ref[...]
        for k in range(CONV_WIDTH):
            acc = acc + w_ref[k:k + 1, :] * _rows_from(ubuf, shifted, CONV_HALO - (CONV_WIDTH - 1) + k, tm, 0, slice(None))
        c_ref[...] = acc
        mu = jnp.mean(acc, axis=-1, keepdims=True)
        cen = acc - mu
        var = jnp.mean(cen * cen, axis=-1, keepdims=True)
        l = cen * lax.rsqrt(var + EPS) * g_ref[...] + b_ref[...]
        s_ref[...] = (l * _sigmoid(l)).astype(s_ref.dtype)

    row = pl.BlockSpec((tm, Dm), lambda i: (i, 0))
    vec = pl.BlockSpec((1, Dm), lambda i: (0, 0))
    return pl.pallas_call(
        body, name="conv_fwd", grid=(T // tm,),
        in_specs=[pl.BlockSpec((tm, 2 * Dm), lambda i: (i, 0)), pl.BlockSpec((CONV_HALO, Dm), lambda i: (0, 0)), vec, vec, vec],
        out_specs=[row, row, row],
        out_shape=[jax.ShapeDtypeStruct((T, Dm), CDT), jax.ShapeDtypeStruct((T, Dm), F32), jax.ShapeDtypeStruct((T, Dm), F32)],
        scratch_shapes=[pltpu.VMEM((tm + CONV_HALO, Dm), F32), pltpu.VMEM((SUBLANES - 1, tm + CONV_HALO - SUBLANES, Dm), F32)],
        compiler_params=_cparams(("arbitrary",)),
    )(a, w_dw, b_dw, ln_g, ln_b)


def _conv_bwd(ds, c, u, a, w_dw, ln_g, ln_b, sends, *, tm=256):
    T = a.shape[0]
    Dm = D_MODEL
    nt = T // tm
    ns = len(sends)

    def body(*refs):
        ds_ref, c_ref, u_ref, a_ref, w_ref, g_ref, b_ref = refs[:7]
        da_ref, db1_ref, dw_ref, dbdw_ref, dg_ref, dbln_ref = refs[7 + ns:13 + ns]
        dcbuf, shifted, du_scr = refs[13 + 2 * ns:16 + 2 * ns]
        x_start, x_finish = _scatter_phases(refs[7:7 + ns], refs[13 + ns:13 + 2 * ns], *refs[16 + 2 * ns:])
        i = pl.program_id(0)
        pl.when(i == 0)(x_start)

        @pl.when(i == 0)
        def _():
            dcbuf[tm:tm + CONV_HALO, :] = jnp.zeros((CONV_HALO, Dm), F32)
            db1_ref[...] = jnp.zeros_like(db1_ref)
            dw_ref[...] = jnp.zeros_like(dw_ref)
            dbdw_ref[...] = jnp.zeros_like(dbdw_ref)
            dg_ref[...] = jnp.zeros_like(dg_ref)
            dbln_ref[...] = jnp.zeros_like(dbln_ref)

        @pl.when(i > 0)
        def _():
            dcbuf[tm:tm + CONV_HALO, :] = dcbuf[0:CONV_HALO, :]

        cv = c_ref[...]
        mu = jnp.mean(cv, axis=-1, keepdims=True)
        cen = cv - mu
        var = jnp.mean(cen * cen, axis=-1, keepdims=True)
        rstd = lax.rsqrt(var + EPS)
        n = cen * rstd
        l = n * g_ref[...] + b_ref[...]
        sg = _sigmoid(l)
        dl = ds_ref[...] * (sg * (1.0 + l * (1.0 - sg)))
        dg_ref[...] += jnp.sum(dl * n, axis=0, keepdims=True)
        dbln_ref[...] += jnp.sum(dl, axis=0, keepdims=True)
        dn = dl * g_ref[...]
        dc = rstd * (dn - jnp.mean(dn, axis=-1, keepdims=True) - n * jnp.mean(dn * n, axis=-1, keepdims=True))
        dbdw_ref[...] += jnp.sum(dc, axis=0, keepdims=True)
        dcbuf[0:tm, :] = dc
        _shifted_copies(dcbuf, shifted, tm)
        groups = CONV_ROWS // SUBLANES
        for cb in range(Dm // CONV_LANES):
            cols = slice(cb * CONV_LANES, (cb + 1) * CONV_LANES)

            def block(r, dw_part, cols=cols):
                r0 = pl.multiple_of(r * CONV_ROWS, CONV_ROWS)
                uv = u_ref[pl.ds(r0, CONV_ROWS), cols]
                du_b = jnp.zeros((CONV_ROWS, CONV_LANES), F32)
                out = []
                for k in range(CONV_WIDTH):
                    slab = _rows_from(dcbuf, shifted, CONV_WIDTH - 1 - k, CONV_ROWS, r0, cols)
                    du_b = du_b + w_ref[k:k + 1, cols] * slab
                    out.append(dw_part[k] + jnp.sum((slab * uv).reshape(groups, SUBLANES, CONV_LANES), axis=0))
                du_scr[pl.ds(r0, CONV_ROWS), cols] = du_b
                return tuple(out)

            zero = jnp.zeros((SUBLANES, CONV_LANES), F32)
            dw_part = lax.fori_loop(0, tm // CONV_ROWS, block, (zero,) * CONV_WIDTH)
            for k in range(CONV_WIDTH):
                dw_ref[k:k + 1, cols] += jnp.sum(dw_part[k], axis=0, keepdims=True)
        du = du_scr[...]
        a1 = a_ref[:, :Dm]
        s2 = _sigmoid(a_ref[:, Dm:])
        da1 = du * s2
        da2 = du * a1 * (s2 * (1.0 - s2))
        da_ref[:, :Dm] = da1.astype(da_ref.dtype)
        da_ref[:, Dm:] = da2.astype(da_ref.dtype)
        db1_ref[:, :Dm] += jnp.sum(da1, axis=0, keepdims=True)
        db1_ref[:, Dm:] += jnp.sum(da2, axis=0, keepdims=True)
        pl.when(i == nt - 1)(x_finish)

    rev = lambda i: (nt - 1 - i, 0)
    row = pl.BlockSpec((tm, Dm), rev)
    row2 = pl.BlockSpec((tm, 2 * Dm), rev)
    vec = pl.BlockSpec((1, Dm), lambda i: (0, 0))
    vec2 = pl.BlockSpec((1, 2 * Dm), lambda i: (0, 0))
    taps = pl.BlockSpec((CONV_HALO, Dm), lambda i: (0, 0))
    anyspec = pl.BlockSpec(memory_space=pl.ANY)
    outs = pl.pallas_call(
        body, name="conv_bwd", grid=(nt,),
        in_specs=[row, row, row, row2, taps, vec, vec] + [anyspec] * ns,
        out_specs=[row2, vec2, taps, vec, vec, vec] + [anyspec] * ns,
        out_shape=[jax.ShapeDtypeStruct((T, 2 * Dm), CDT), jax.ShapeDtypeStruct((1, 2 * Dm), F32),
                   jax.ShapeDtypeStruct((CONV_HALO, Dm), F32), jax.ShapeDtypeStruct((1, Dm), F32),
                   jax.ShapeDtypeStruct((1, Dm), F32), jax.ShapeDtypeStruct((1, Dm), F32)]
        + [jax.ShapeDtypeStruct(s.shape, s.dtype) for s in sends],
        scratch_shapes=[pltpu.VMEM((tm + CONV_HALO, Dm), F32), pltpu.VMEM((SUBLANES - 1, tm + CONV_HALO - SUBLANES, Dm), F32),
                        pltpu.VMEM((tm, Dm), F32)]
        + _exchange_sems(ns),
        compiler_params=_cparams(("arbitrary",)),
    )(ds, c, u, a, w_dw, ln_g, ln_b, *sends)
    return outs[:6], outs[6:]


def _gla_head_masks(width, per_head):
    lane = lax.broadcasted_iota(jnp.int32, (1, width), 1)
    return [((lane >= h * per_head) & (lane < (h + 1) * per_head)).astype(F32) for h in range(GLA_HEADS)]


def _gla_specs(tm, order):
    return [pl.BlockSpec((tm, GLA_DK), lambda i: (order(i), 0)),
            pl.BlockSpec((tm, GLA_DK), lambda i: (order(i), 1)),
            pl.BlockSpec((tm, GLA_DV), lambda i: (order(i), 1)),
            pl.BlockSpec((tm, GLA_DV), lambda i: (order(i), 2)),
            pl.BlockSpec((tm, 128), lambda i: (order(i), 3072 // 128))]


def _gla_chunk_decay(la_c, tri):
    bc = _dot_exact_lhs(tri, la_c, 3)
    b_end = bc[CHUNK - 1:CHUNK, :]
    return b_end, jnp.exp(b_end - bc)


def _gla_fwd(proj, wg2p, b_gate, g_gla, *, tm=256):
    T = proj.shape[0]
    ncs = tm // CHUNK
    scale = GLA_HEAD_K ** -0.5

    def body(q_ref, k_ref, v_ref, r_ref, glr_ref, wg_ref, bg_ref, gg_ref, o_ref, oraw_ref, st_ref, s_scr):
        i = pl.program_id(0)

        @pl.when(i == 0)
        def _():
            s_scr[...] = jnp.zeros_like(s_scr)

        mk = _gla_head_masks(GLA_DK, GLA_HEAD_K)
        rr = lax.broadcasted_iota(jnp.int32, (CHUNK, CHUNK), 0)
        cc = lax.broadcasted_iota(jnp.int32, (CHUNK, CHUNK), 1)
        tri = (cc <= rr).astype(BF16)
        y = _dg(glr_ref[...], wg_ref[...], _NN) + bg_ref[...]
        la = _log_sigmoid(y) / GLA_GATE_NORMALIZER
        qs = q_ref[...] * scale
        for ci in range(ncs):
            rows = slice(ci * CHUNK, (ci + 1) * CHUNK)
            b_end, dec = _gla_chunk_decay(la[rows], tri)
            kend = (k_ref[rows, :] * dec).astype(CDT)
            upd = jnp.zeros((GLA_HEAD_V, GLA_DK), F32)
            for h in range(GLA_HEADS):
                vh = v_ref[rows, h * GLA_HEAD_V:(h + 1) * GLA_HEAD_V]
                upd = upd + mk[h] * _dg(vh, kend, _TN)
            s_new = jnp.exp(b_end) * s_scr[...] + upd
            s_scr[...] = s_new
            st_ref[ci] = s_new
            s_c = s_new.astype(CDT)
            for h in range(GLA_HEADS):
                o_h = _dg(qs[rows] * mk[h], s_c, _NT)
                oraw_ref[rows, h * GLA_HEAD_V:(h + 1) * GLA_HEAD_V] = o_h
        for h in range(GLA_HEADS):
            cols = slice(h * GLA_HEAD_V, (h + 1) * GLA_HEAD_V)
            o_h = oraw_ref[:, cols]
            rs = lax.rsqrt(jnp.mean(o_h * o_h, axis=-1, keepdims=True) + EPS)
            rg = r_ref[:, cols]
            o_ref[:, cols] = (o_h * rs * gg_ref[...] * (rg * _sigmoid(rg))).astype(o_ref.dtype)

    full = lambda shape: pl.BlockSpec(shape, lambda i: tuple(0 for _ in shape))
    return pl.pallas_call(
        body, name="gla_fwd", grid=(T // tm,),
        in_specs=_gla_specs(tm, lambda i: i) + [full((128, GLA_DK)), full((1, GLA_DK)), full((1, GLA_HEAD_V))],
        out_specs=[pl.BlockSpec((tm, GLA_DV), lambda i: (i, 0)), pl.BlockSpec((tm, GLA_DV), lambda i: (i, 0)),
                   pl.BlockSpec((ncs, GLA_HEAD_V, GLA_DK), lambda i: (i, 0, 0))],
        out_shape=[jax.ShapeDtypeStruct((T, GLA_DV), CDT), jax.ShapeDtypeStruct((T, GLA_DV), F32),
                   jax.ShapeDtypeStruct((T // CHUNK, GLA_HEAD_V, GLA_DK), F32)],
        scratch_shapes=[pltpu.VMEM((GLA_HEAD_V, GLA_DK), F32)],
        compiler_params=_cparams(("arbitrary",)),
    )(proj, proj, proj, proj, proj, wg2p, b_gate, g_gla)


def _gla_bwd(d_o, proj, oraw, states, wg2p, b_gate, g_gla, *, tm=256):
    T = proj.shape[0]
    nt = T // tm
    ncs = tm // CHUNK
    scale = GLA_HEAD_K ** -0.5

    def body(do_ref, q_ref, k_ref, v_ref, r_ref, glr_ref, oraw_ref, st_ref, stp_ref, wg_ref, bg_ref, gg_ref,
             dgla_ref, dglr_ref, dwg_ref, dbg_ref, dgg_ref, ds_scr, dy_scr, dor_scr):
        i = pl.program_id(0)
        tile = nt - 1 - i

        @pl.when(i == 0)
        def _():
            ds_scr[...] = jnp.zeros_like(ds_scr)
            dwg_ref[...] = jnp.zeros_like(dwg_ref)
            dbg_ref[...] = jnp.zeros_like(dbg_ref)
            dgg_ref[...] = jnp.zeros_like(dgg_ref)

        mk = _gla_head_masks(GLA_DK, GLA_HEAD_K)
        rr = lax.broadcasted_iota(jnp.int32, (CHUNK, CHUNK), 0)
        cc = lax.broadcasted_iota(jnp.int32, (CHUNK, CHUNK), 1)
        tri = (cc <= rr).astype(BF16)
        tri_t = (cc >= rr).astype(BF16)
        last_row = (lax.broadcasted_iota(jnp.int32, (CHUNK, 1), 0) == CHUNK - 1).astype(F32)

        dgg = jnp.zeros((1, GLA_HEAD_V), F32)
        for h in range(GLA_HEADS):
            cols = slice(h * GLA_HEAD_V, (h + 1) * GLA_HEAD_V)
            o_h = oraw_ref[:, cols]
            rs = lax.rsqrt(jnp.mean(o_h * o_h, axis=-1, keepdims=True) + EPS)
            rg = r_ref[:, cols]
            sg = _sigmoid(rg)
            dov = do_ref[:, cols]
            on = o_h * rs * gg_ref[...]
            d_on = dov * (rg * sg)
            dgla_ref[:, 2 * GLA_DK + GLA_DV + h * GLA_HEAD_V:2 * GLA_DK + GLA_DV + (h + 1) * GLA_HEAD_V] = (
                dov * on * (sg * (1.0 + rg * (1.0 - sg)))).astype(dgla_ref.dtype)
            dgg = dgg + jnp.sum(d_on * o_h * rs, axis=0, keepdims=True)
            uu = d_on * gg_ref[...]
            dor_scr[:, cols] = rs * uu - o_h * (rs * rs * rs * jnp.mean(uu * o_h, axis=-1, keepdims=True))
        dgg_ref[...] += dgg

        y = _dg(glr_ref[...], wg_ref[...], _NN) + bg_ref[...]
        la = _log_sigmoid(y) / GLA_GATE_NORMALIZER
        qs = q_ref[...] * scale
        for ci in reversed(range(ncs)):
            rows = slice(ci * CHUNK, (ci + 1) * CHUNK)
            b_end, dec = _gla_chunk_decay(la[rows], tri)
            decay = jnp.exp(b_end)
            kend = k_ref[rows, :] * dec
            kend_c = kend.astype(CDT)
            s_c = st_ref[ci].astype(CDT)
            if ci > 0:
                s_prev = st_ref[ci - 1]
            else:
                s_prev = jnp.where(tile > 0, stp_ref[0], 0.0)
            dqs = jnp.zeros((CHUNK, GLA_DK), F32)
            dst = ds_scr[...]
            for h in range(GLA_HEADS):
                do_h = dor_scr[rows, h * GLA_HEAD_V:(h + 1) * GLA_HEAD_V].astype(CDT)
                dqs = dqs + mk[h] * _dg(do_h, s_c, _NN)
                dst = dst + mk[h] * _dg(do_h, qs[rows], _TN)
            d_decay = jnp.sum(dst * s_prev, axis=0, keepdims=True)
            ds_scr[...] = decay * dst
            dst_c = dst.astype(CDT)
            dkend = jnp.zeros((CHUNK, GLA_DK), F32)
            for h in range(GLA_HEADS):
                cols = slice(h * GLA_HEAD_V, (h + 1) * GLA_HEAD_V)
                dv_h = _dg(kend * mk[h], dst_c, _NT)
                dgla_ref[rows, 2 * GLA_DK + h * GLA_HEAD_V:2 * GLA_DK + (h + 1) * GLA_HEAD_V] = dv_h.astype(dgla_ref.dtype)
                dkend = dkend + mk[h] * _dg(v_ref[rows, cols], dst_c, _NN)
            dgla_ref[rows, 0:GLA_DK] = (dqs * scale).astype(dgla_ref.dtype)
            dgla_ref[rows, GLA_DK:2 * GLA_DK] = (dkend * dec).astype(dgla_ref.dtype)
            mm = dkend * kend
            db_end = jnp.sum(mm, axis=0, keepdims=True) + d_decay * decay
            dbc = last_row * db_end - mm
            dla = _dot_exact_lhs(tri_t, dbc, 3)
            dy_scr[rows, :] = dla * (1.0 / GLA_GATE_NORMALIZER) * _sigmoid(-y[rows])
        dy = dy_scr[...]
        dbg_ref[...] += jnp.sum(dy, axis=0, keepdims=True)
        dwg_ref[...] += _dg(glr_ref[...], dy, _TN)
        dglr_ref[...] = _dg(dy, wg_ref[...], _NT).astype(dglr_ref.dtype)

    rev = lambda i: nt - 1 - i
    full = lambda shape: pl.BlockSpec(shape, lambda i: tuple(0 for _ in shape))
    st_spec = pl.BlockSpec((ncs, GLA_HEAD_V, GLA_DK), lambda i: (rev(i), 0, 0))
    stp_spec = pl.BlockSpec((1, GLA_HEAD_V, GLA_DK), lambda i: (jnp.maximum(rev(i) * ncs - 1, 0), 0, 0))
    return pl.pallas_call(
        body, name="gla_bwd", grid=(nt,),
        in_specs=[pl.BlockSpec((tm, GLA_DV), lambda i: (rev(i), 0))] + _gla_specs(tm, rev)
        + [pl.BlockSpec((tm, GLA_DV), lambda i: (rev(i), 0)), st_spec, stp_spec,
           full((128, GLA_DK)), full((1, GLA_DK)), full((1, GLA_HEAD_V))],
        out_specs=[pl.BlockSpec((tm, 2 * GLA_DK + 2 * GLA_DV), lambda i: (rev(i), 0)),
                   pl.BlockSpec((tm, 128), lambda i: (rev(i), 0)),
                   full((128, GLA_DK)), full((1, GLA_DK)), full((1, GLA_HEAD_V))],
        out_shape=[jax.ShapeDtypeStruct((T, 2 * GLA_DK + 2 * GLA_DV), CDT), jax.ShapeDtypeStruct((T, 128), CDT),
                   jax.ShapeDtypeStruct((128, GLA_DK), F32), jax.ShapeDtypeStruct((1, GLA_DK), F32),
                   jax.ShapeDtypeStruct((1, GLA_HEAD_V), F32)],
        scratch_shapes=[pltpu.VMEM((GLA_HEAD_V, GLA_DK), F32), pltpu.VMEM((tm, GLA_DK), F32),
                        pltpu.VMEM((tm, GLA_DV), F32)],
        compiler_params=_cparams(("arbitrary",)),
    )(d_o, proj, proj, proj, proj, proj, oraw, states, states, wg2p, b_gate, g_gla)


def _head_mean_matrix():
    r = lax.broadcasted_iota(jnp.int32, (SB_D, SB_D), 0) // SB_HEAD_DIM
    c = lax.broadcasted_iota(jnp.int32, (SB_D, SB_D), 1) // SB_HEAD_DIM
    return jnp.where(r == c, 1.0 / SB_HEAD_DIM, 0.0).astype(BF16)


def _sb_prep(proj, gq, gk, *, tm=256):
    T = proj.shape[0]
    scale = SB_HEAD_DIM ** -0.5

    def body(q_ref, k_ref, v_ref, gq_ref, gk_ref, qn_ref, kn_ref, vb_ref):
        hm = _head_mean_matrix()
        qv, kv = q_ref[...], k_ref[...]
        rq = lax.rsqrt(_dot_exact_rhs(qv * qv, hm, 3) + EPS)
        rk = lax.rsqrt(_dot_exact_rhs(kv * kv, hm, 3) + EPS)
        qn_ref[...] = (qv * rq * gq_ref[...] * scale).astype(qn_ref.dtype)
        kn_ref[...] = (kv * rk * gk_ref[...]).astype(kn_ref.dtype)
        vb_ref[...] = v_ref[...].astype(vb_ref.dtype)

    col = lambda j: pl.BlockSpec((tm, SB_D), lambda i: (i, j))
    vec = pl.BlockSpec((1, SB_D), lambda i: (0, 0))
    out = pl.BlockSpec((tm, SB_D), lambda i: (i, 0))
    return pl.pallas_call(
        body, name="sb_prep", grid=(T // tm,),
        in_specs=[col(3), col(4), col(5), vec, vec], out_specs=[out, out, out],
        out_shape=[jax.ShapeDtypeStruct((T, SB_D), CDT)] * 3,
        compiler_params=_cparams(("parallel",)),
    )(proj, proj, proj, gq, gk)


def _sb_prep_bwd(dqn, dkn, dv, proj, gq, gk, *, tm=256):
    T = proj.shape[0]
    scale = SB_HEAD_DIM ** -0.5

    def body(dqn_ref, dkn_ref, dv_ref, q_ref, k_ref, gq_ref, gk_ref, dsb_ref, dgq_ref, dgk_ref):
        i = pl.program_id(0)

        @pl.when(i == 0)
        def _():
            dgq_ref[...] = jnp.zeros_like(dgq_ref)
            dgk_ref[...] = jnp.zeros_like(dgk_ref)

        hm = _head_mean_matrix()

        def one(dn_ref, x_ref, g_ref, dg_ref, sc, lo):
            xv = x_ref[...]
            dnv = dn_ref[...] * sc
            r = lax.rsqrt(_dot_exact_rhs(xv * xv, hm, 3) + EPS)
            u = dnv * g_ref[...]
            dot = _dot_exact_rhs(u * xv, hm, 3)
            dsb_ref[:, lo:lo + SB_D] = (r * u - xv * (r * r * r * dot)).astype(dsb_ref.dtype)
            dg_ref[...] += jnp.sum(dnv * xv * r, axis=0, keepdims=True)

        one(dqn_ref, q_ref, gq_ref, dgq_ref, scale, 0)
        one(dkn_ref, k_ref, gk_ref, dgk_ref, 1.0, SB_D)
        dsb_ref[:, 2 * SB_D:3 * SB_D] = dv_ref[...].astype(dsb_ref.dtype)

    col = lambda j: pl.BlockSpec((tm, SB_D), lambda i: (i, j))
    vec = pl.BlockSpec((1, SB_D), lambda i: (0, 0))
    row = pl.BlockSpec((tm, SB_D), lambda i: (i, 0))
    return pl.pallas_call(
        body, name="sb_prep_bwd", grid=(T // tm,),
        in_specs=[row, row, row, col(3), col(4), vec, vec],
        out_specs=[pl.BlockSpec((tm, 3 * SB_D), lambda i: (i, 0)), vec, vec],
        out_shape=[jax.ShapeDtypeStruct((T, 3 * SB_D), CDT), jax.ShapeDtypeStruct((1, SB_D), F32),
                   jax.ShapeDtypeStruct((1, SB_D), F32)],
        compiler_params=_cparams(("arbitrary",)),
    )(dqn, dkn, dv, proj, proj, gq, gk)


def _sb_masks():
    lane = lax.broadcasted_iota(jnp.int32, (1, 128), 1)
    m = [lane < SB_HEAD_DIM, lane >= SB_HEAD_DIM]
    return m, [x.astype(F32) for x in m]


def _sb_fwd(qn, kn, vb, blocks):
    T = qn.shape[0]
    nq = T // SB_TILE
    B, P = SB_TILE, SB_PAIR
    hs = range(2)
    n = len(blocks)
    nhp = SB_D // P

    def body(*refs):
        q_ref, k_ref, v_ref = refs[:3]
        o_ref, l_ref, done_ref = refs[3 + n:6 + n]
        acc_ref = refs[6 + 2 * n]
        hp, qb = pl.program_id(0), pl.program_id(1)
        g_start, g_forward, g_finish = _gather_phases(refs[3:3 + n], refs[6 + n:6 + 2 * n], *refs[7 + 2 * n:])
        pl.when((hp == 0) & (qb == 0))(g_start)
        pl.when((hp == nhp - 1) & (qb == 0))(g_forward)
        m, mf = _sb_masks()
        row = lax.broadcasted_iota(jnp.int32, (B, B), 0)
        col = lax.broadcasted_iota(jnp.int32, (B, B), 1)
        later = (row > col).astype(BF16)
        past = col < row
        q2 = q_ref[...]
        qm = [jnp.where(m[h], q2, jnp.zeros_like(q2)) for h in hs]
        acc_ref[...] = jnp.zeros_like(acc_ref)

        def keys(kb):
            return k_ref[pl.ds(pl.multiple_of(kb * B, B), B), :]

        def values(kb):
            return v_ref[pl.ds(pl.multiple_of(kb * B, B), B), :]

        def scores(kb):
            k2 = keys(kb)
            return [_dg(qm[h], k2, _NT) for h in hs]

        def run(tiles, R, diag):
            zs, cum, rsum = {}, {}, {}
            for t, (z, _) in enumerate(tiles):
                for h in hs:
                    sp = _softplus(z[h])
                    lk = jnp.where(past, sp, 0.0) if diag else sp
                    zs[t, h] = z[h] - sp
                    cum[t, h] = _dot_exact_rhs(lk, later, SB_SPLIT_LK)
                    rsum[t, h] = jnp.sum(lk, axis=1, keepdims=True)
            R = list(R)
            for t, (_, kb) in enumerate(tiles):
                v2 = values(kb)
                for h in hs:
                    w = jnp.exp(zs[t, h] - (cum[t, h] + R[h]))
                    if diag:
                        w = jnp.where(past, w, 0.0)
                    acc_ref[h] += _dg(w, v2, _NN)
                R = [R[h] + rsum[t, h] for h in hs]
            return tuple(R)

        zero = jnp.zeros((B, 1), F32)
        R = run([(scores(qb), qb)], (zero, zero), True)
        R = lax.cond(qb % 2 == 1, lambda r: run([(scores(qb - 1), qb - 1)], r, False), lambda r: r, R)
        npairs = qb // 2

        def live(r):
            return (jnp.minimum(jnp.min(r[0]), jnp.min(r[1])) < SB_DEAD).astype(jnp.int32)

        def pair(carry):
            i, r, za, zb = carry[0], carry[2:4], carry[4:6], carry[6:8]
            ka = 2 * (npairs - 1 - i) + 1
            nxt = (scores(jnp.maximum(ka - 2, 0)), scores(jnp.maximum(ka - 3, 0)))
            r = run([(za, ka), (zb, ka - 1)], r, False)
            return (i + 1, live(r), *r, *nxt[0], *nxt[1])

        first = jnp.maximum(2 * npairs - 1, 0)
        out = lax.while_loop(lambda c: (c[0] < npairs) & (c[1] > 0), pair,
                             (jnp.int32(0), live(R), *R, *scores(first), *scores(jnp.maximum(first - 1, 0))))
        R = out[2:4]
        done_ref[hp, qb] = out[0]
        o_ref[...] = (acc_ref[0] * mf[0] + acc_ref[1] * mf[1]).astype(o_ref.dtype)
        l_ref[0] = R[0] * mf[0] + R[1] * mf[1]
        pl.when((hp == nhp - 1) & (qb == nq - 1))(g_finish)

    slab = pl.BlockSpec((T, P), lambda hp, qb: (0, hp))
    blk = pl.BlockSpec((B, P), lambda hp, qb: (qb, hp))
    anyspec = pl.BlockSpec(memory_space=pl.ANY)
    outs = pl.pallas_call(
        body, name="sb_fwd", grid=(nhp, nq),
        in_specs=[blk, slab, slab] + [anyspec] * n,
        out_specs=[blk, pl.BlockSpec((1, B, P), lambda hp, qb: (hp, qb, 0)), pl.BlockSpec(memory_space=pltpu.SMEM)]
        + [anyspec] * n,
        out_shape=[jax.ShapeDtypeStruct((T, SB_D), CDT), jax.ShapeDtypeStruct((nhp, T, P), F32),
                   jax.ShapeDtypeStruct((nhp, nq), jnp.int32)]
        + [jax.ShapeDtypeStruct((N_DEV,) + b.shape, b.dtype) for b in blocks],
        scratch_shapes=[pltpu.VMEM((2, B, P), F32)] + _exchange_sems(n),
        compiler_params=_cparams(("arbitrary", "arbitrary")),
    )(qn, kn, vb, *blocks)
    return outs[0], outs[1], outs[2], outs[3:]


def _sb_bwd(d_o, qn, kn, vb, lsum, done):
    T = qn.shape[0]
    nq = T // SB_TILE
    B, P = SB_TILE, SB_PAIR
    hs = range(2)

    def body(do_ref, q_ref, k_ref, v_ref, l_ref, done_ref, dq_ref, dk_ref, dv_ref, dqacc_ref):
        qb = pl.program_id(1)

        @pl.when(qb == 0)
        def _():
            dk_ref[...] = jnp.zeros_like(dk_ref)
            dv_ref[...] = jnp.zeros_like(dv_ref)

        m, mf = _sb_masks()
        row = lax.broadcasted_iota(jnp.int32, (B, B), 0)
        col = lax.broadcasted_iota(jnp.int32, (B, B), 1)
        upto = (row <= col).astype(BF16)
        before = (row < col).astype(BF16)
        past = col < row
        q2 = q_ref[...]
        qm = [jnp.where(m[h], q2, jnp.zeros_like(q2)) for h in hs]
        do2 = do_ref[...]
        dom = [jnp.where(m[h], do2, 0.0).astype(CDT) for h in hs]
        lb = l_ref[0]
        ltot = [lb[:, 0:1], lb[:, SB_HEAD_DIM:SB_HEAD_DIM + 1]]
        dqacc_ref[...] = jnp.zeros_like(dqacc_ref)

        def rows(kb):
            return pl.ds(pl.multiple_of(kb * B, B), B)

        def scores(kb):
            k2, v2 = k_ref[rows(kb), :], v_ref[rows(kb), :]
            return [_dg(qm[h], k2, _NT) for h in hs] + [_dg(dom[h], v2, _NT) for h in hs]

        def run(tiles, carry, diag):
            Ps, Pg = list(carry[0]), list(carry[1])
            zs, sp_, cum, rest = {}, {}, {}, {}
            for t, tl in enumerate(tiles):
                for h in hs:
                    sp = _softplus(tl[h])
                    lk = jnp.where(past, sp, 0.0) if diag else sp
                    zs[t, h], sp_[t, h] = tl[h] - sp, sp
                    cum[t, h] = _dot_exact_rhs(lk, upto, SB_SPLIT_LK)
                    rest[t, h] = ltot[h] - Ps[h]
                    Ps[h] = Ps[h] + jnp.sum(lk, axis=1, keepdims=True)
            w, g, gx = {}, {}, {}
            for t, tl in enumerate(tiles):
                for h in hs:
                    wt = jnp.exp(zs[t, h] - (rest[t, h] - cum[t, h]))
                    if diag:
                        wt = jnp.where(past, wt, 0.0)
                    w[t, h] = wt
                    g[t, h] = wt * tl[2 + h]
                    gx[t, h] = _dot_exact_rhs(g[t, h], before, SB_SPLIT_G) + Pg[h]
                    Pg[h] = Pg[h] + jnp.sum(g[t, h], axis=1, keepdims=True)
            for t, tl in enumerate(tiles):
                kb = tl[4]
                k2 = k_ref[rows(kb), :]
                for h in hs:
                    sneg = jnp.exp(-sp_[t, h])
                    dz = g[t, h] * sneg - (1.0 - sneg) * gx[t, h]
                    if diag:
                        dz = jnp.where(past, dz, 0.0)
                    dz_c = dz.astype(CDT)
                    dv_ref[rows(kb), :] += _dg(w[t, h], dom[h], _TN)
                    dk_ref[rows(kb), :] += _dg(dz_c, qm[h], _TN)
                    dqacc_ref[h] += _dg(dz_c, k2, _NN)
            return tuple(Ps), tuple(Pg)

        zero = jnp.zeros((B, 1), F32)
        npairs = qb // 2

        def pair(i, carry):
            sums, ta, tb = (carry[0:2], carry[2:4]), carry[4:8], carry[8:12]
            nxt = scores(jnp.minimum(2 * i + 2, qb)) + scores(jnp.minimum(2 * i + 3, qb))
            Ps, Pg = run([(*ta, 2 * i), (*tb, 2 * i + 1)], sums, False)
            return (*Ps, *Pg, *nxt)

        i0 = npairs - jnp.clip(done_ref[pl.program_id(0), qb], 0, npairs)
        out = lax.fori_loop(i0, npairs, pair,
                            (zero, zero, zero, zero, *scores(jnp.minimum(2 * i0, qb)), *scores(jnp.minimum(2 * i0 + 1, qb))))
        sums = (out[0:2], out[2:4])
        sums = lax.cond(qb % 2 == 1, lambda s: run([(*scores(qb - 1), qb - 1)], s, False), lambda s: s, sums)
        run([(*scores(qb), qb)], sums, True)
        dq_ref[...] = dqacc_ref[0] * mf[0] + dqacc_ref[1] * mf[1]

    slab = pl.BlockSpec((T, P), lambda hp, qb: (0, hp))
    blk = pl.BlockSpec((B, P), lambda hp, qb: (qb, hp))
    return pl.pallas_call(
        body, name="sb_bwd", grid=(SB_D // P, nq),
        in_specs=[blk, blk, slab, slab, pl.BlockSpec((1, B, P), lambda hp, qb: (hp, qb, 0)),
                  pl.BlockSpec(memory_space=pltpu.SMEM)],
        out_specs=[blk, slab, slab],
        out_shape=[jax.ShapeDtypeStruct((T, SB_D), F32)] * 3,
        scratch_shapes=[pltpu.VMEM((2, B, P), F32)],
        compiler_params=_cparams(("arbitrary", "arbitrary")),
    )(d_o, qn, kn, vb, lsum, done)


def _regroup_in_rows(wt):
    cut = 2 * GLA_DK + 2 * GLA_DV
    pad = jnp.zeros((IN_PAD - IN_WIDTH, wt.shape[1]), wt.dtype)
    return jnp.concatenate([wt[:cut], wt[cut + GLA_GATE_RANK:], wt[cut:cut + GLA_GATE_RANK], pad], axis=0)


def _ungroup_in_rows(gt):
    cut = 2 * GLA_DK + 2 * GLA_DV
    return jnp.concatenate([gt[:cut], gt[3072:3072 + GLA_GATE_RANK], gt[cut:3072]], axis=0)


def _colsum(v, *, name, tm=512):
    T, C = v.shape

    def body(v_ref, o_ref):
        i = pl.program_id(0)
        part = jnp.sum(v_ref[...], axis=0, keepdims=True)

        @pl.when(i == 0)
        def _():
            o_ref[...] = part

        @pl.when(i > 0)
        def _():
            o_ref[...] += part

    return pl.pallas_call(
        body, name=name, grid=(T // tm,),
        in_specs=[pl.BlockSpec((tm, C), lambda i: (i, 0))], out_specs=pl.BlockSpec((1, C), lambda i: (0, 0)),
        out_shape=jax.ShapeDtypeStruct((1, C), F32),
        compiler_params=_cparams(("arbitrary",)),
    )(v)


def _ffn_fwd(h, g_norm, wgu_t, wd, tag):
    hf = _rms_fwd(h, g_norm, name=f"ffn{tag}_norm")
    ab, s = _swiglu_up(hf, wgu_t, name=f"ffn{tag}_up")
    h_out = _matmul(s, wd, mode='nn', out_dtype=F32, name=f"ffn{tag}_down", tm=512, tn=D_MODEL, tk=D_FF, residual=h)
    return h_out, (hf, ab, s)


def _ffn_bwd(dh, dh_c, h_in, g_norm, wgu_t, wd, saved, tag):
    hf, ab, s = saved
    dwd = _matmul(s, dh_c, mode='tn', out_dtype=F32, name=f"ffn{tag}_dwd", tm=D_FF // 2, tn=D_MODEL, tk=TK_TOKENS)
    dab = _swiglu_dact(dh_c, wd, ab, name=f"ffn{tag}_dact")
    dwgu_t = _matmul(dab, hf, mode='tn', out_dtype=F32, name=f"ffn{tag}_dwgu", tm=D_FF // 2, tn=D_MODEL, tk=TK_TOKENS)
    dhf = _matmul(dab, wgu_t, mode='nn', out_dtype=F32, name=f"ffn{tag}_dhf", tm=256, tn=D_MODEL, tk=2 * D_FF)
    dh_in, dh_in_c, dg = _rms_bwd(dhf, h_in, g_norm, dh, name=f"ffn{tag}_dnorm")
    return dh_in, dh_in_c, dwgu_t, dwd, dg


def _late_weights(gathered):
    g_out, g_pw1, g_pw2, g_gate, g_up, g_down = gathered
    Dm = D_MODEL
    return {
        'hy_w_out': g_out.reshape(Dm, Dm),
        'cv_w_pw1_t': g_pw1.reshape(2 * Dm, Dm),
        'cv_w_pw2': g_pw2.reshape(Dm, Dm),
        'ffn_wgu_t': [jnp.concatenate([g_gate[:, l].reshape(D_FF, Dm), g_up[:, l].reshape(D_FF, Dm)], axis=0)
                      for l in range(2)],
        'ffn_w_down': [g_down[:, l].reshape(D_FF, Dm) for l in range(2)],
    }


def _local_step(x, tgt, W, late_blocks):
    row = lambda v: v.reshape(1, -1)
    win_p = _regroup_in_rows(W['hy_w_in_t'])
    wg2p = jnp.pad(W['hy_w_gate2'], ((0, 128 - GLA_GATE_RANK), (0, 0)))
    b_gate = row(W['hy_b_gate'])
    g_gla = row(W['hy_gla_norm'])
    gq = jnp.tile(W['hy_sb_q_norm'].reshape(-1), SB_D // SB_HEAD_DIM).reshape(1, SB_D)
    gk = jnp.tile(W['hy_sb_k_norm'].reshape(-1), SB_D // SB_HEAD_DIM).reshape(1, SB_D)
    w_dw = jnp.pad(W['cv_w_dw'], ((0, CONV_HALO - CONV_WIDTH), (0, 0)))
    mixn = [row(W['mix_norm'][l]) for l in range(2)]
    ffnn = [row(W['ffn_norm'][l]) for l in range(2)]

    hn0 = _rms_fwd(x, mixn[0], name="mix0_norm")
    proj = _matmul(hn0, win_p, mode='nt', out_dtype=F32, name="hy_in", tm=256, tn=IN_PAD, tk=D_MODEL)
    o_gla, o_raw, states = _gla_fwd(proj, wg2p, b_gate, g_gla)
    qn, kn, vb = _sb_prep(proj, gq, gk)
    o_sb, lsum, sb_done, gathered = _sb_fwd(qn, kn, vb, late_blocks)
    W = {**W, **_late_weights(gathered)}
    w_out, wgu, wd = W['hy_w_out'], W['ffn_wgu_t'], W['ffn_w_down']
    o_mix = jnp.concatenate([o_gla, o_sb], axis=1)
    h1 = _matmul(o_mix, w_out, mode='nn', out_dtype=F32, name="hy_out", tm=512, tn=D_MODEL, tk=D_MODEL, residual=x)
    h2, ffn0_saved = _ffn_fwd(h1, ffnn[0], wgu[0], wd[0], 0)
    hn1 = _rms_fwd(h2, mixn[1], name="mix1_norm")
    a_cv = _matmul(hn1, W['cv_w_pw1_t'], mode='nt', out_dtype=F32, name="cv_pw1", tm=512, tn=2 * D_MODEL, tk=D_MODEL,
                   bias=row(W['cv_b_pw1']))
    s_cv, u_cv, c_cv = _conv_fwd(a_cv, w_dw, row(W['cv_b_dw']), row(W['cv_ln_g']), row(W['cv_ln_b']))
    h3 = _matmul(s_cv, W['cv_w_pw2'], mode='nn', out_dtype=F32, name="cv_pw2", tm=512, tn=D_MODEL, tk=D_MODEL,
                 bias=row(W['cv_b_pw2']), residual=h2)
    h4, ffn1_saved = _ffn_fwd(h3, ffnn[1], wgu[1], wd[1], 1)
    sq_err, dy, dy_c = _loss_head(h4, tgt)

    G = {}
    dh3, dh3_c, dwgu1, dwd1, dg_ffn1 = _ffn_bwd(dy, dy_c, h3, ffnn[1], wgu[1], wd[1], ffn1_saved, 1)
    G['cv_b_pw2'] = _colsum(dh3, name="cv_db2")
    G['cv_w_pw2'] = _matmul(s_cv, dh3_c, mode='tn', out_dtype=F32, name="cv_dw2", tm=D_MODEL, tn=D_MODEL, tk=TK_TOKENS)
    ds_cv = _matmul(dh3_c, W['cv_w_pw2'], mode='nt', out_dtype=F32, name="cv_ds", tm=512, tn=D_MODEL, tk=D_MODEL)
    F8 = D_FF // N_DEV
    early_own = [G['cv_w_pw2'].reshape(N_DEV, D_MODEL // N_DEV, D_MODEL),
                 dwgu1.reshape(2, N_DEV, F8, D_MODEL).transpose(1, 0, 2, 3).reshape(N_DEV, 2 * F8, D_MODEL),
                 dwd1.reshape(N_DEV, F8, D_MODEL)]
    (da_cv, db1, dwdw, dbdw, dlng, dlnb), early_recv = _conv_bwd(
        ds_cv, c_cv, u_cv, a_cv, w_dw, row(W['cv_ln_g']), row(W['cv_ln_b']), [a.astype(BF16) for a in early_own])
    G['cv_b_pw1'] = db1
    G['cv_w_dw'] = dwdw[:CONV_WIDTH]
    G['cv_b_dw'], G['cv_ln_g'], G['cv_ln_b'] = dbdw, dlng, dlnb
    G['cv_w_pw1_t'] = _matmul(da_cv, hn1, mode='tn', out_dtype=F32, name="cv_dw1", tm=D_MODEL, tn=D_MODEL, tk=TK_TOKENS)
    dhn1 = _matmul(da_cv, W['cv_w_pw1_t'], mode='nn', out_dtype=F32, name="cv_dhn", tm=512, tn=D_MODEL, tk=2 * D_MODEL)
    dh2, dh2_c, dg_mix1 = _rms_bwd(dhn1, h2, mixn[1], dh3, name="mix1_dnorm")
    dh1, dh1_c, dwgu0, dwd0, dg_ffn0 = _ffn_bwd(dh2, dh2_c, h1, ffnn[0], wgu[0], wd[0], ffn0_saved, 0)
    G['hy_w_out'] = _matmul(o_mix, dh1_c, mode='tn', out_dtype=F32, name="hy_dwout", tm=D_MODEL, tn=D_MODEL, tk=TK_TOKENS)
    d_omix = _matmul(dh1_c, w_out, mode='nt', out_dtype=F32, name="hy_domix", tm=512, tn=D_MODEL, tk=D_MODEL)
    dgla, dglr, dwg2, dbg, dgg = _gla_bwd(d_omix[:, :GLA_DV], proj, o_raw, states, wg2p, b_gate, g_gla)
    dqn, dkn, dvs = _sb_bwd(d_omix[:, GLA_DV:], qn, kn, vb, lsum, sb_done)
    dsb, dgq, dgk = _sb_prep_bwd(dqn, dkn, dvs, proj, gq, gk)
    dproj = jnp.concatenate([dgla, dsb, dglr], axis=1)
    dwin_p = _matmul(dproj, hn0, mode='tn', out_dtype=F32, name="hy_dwin", tm=IN_PAD // 5, tn=D_MODEL, tk=TK_TOKENS)
    dhn0 = _matmul(dproj, win_p, mode='nn', out_dtype=F32, name="hy_dhn", tm=256, tn=D_MODEL, tk=IN_PAD)
    dx, _, dg_mix0 = _rms_bwd(dhn0, x, mixn[0], dh1, name="mix0_dnorm")

    G['hy_w_in_t'] = _ungroup_in_rows(dwin_p)
    G['hy_w_gate2'] = dwg2[:GLA_GATE_RANK]
    G['hy_b_gate'] = dbg
    G['hy_gla_norm'] = dgg
    G['hy_sb_q_norm'] = dgq.reshape(SB_D // SB_HEAD_DIM, SB_HEAD_DIM).sum(axis=0, keepdims=True)
    G['hy_sb_k_norm'] = dgk.reshape(SB_D // SB_HEAD_DIM, SB_HEAD_DIM).sum(axis=0, keepdims=True)
    G['mix_norm'] = jnp.concatenate([dg_mix0, dg_mix1], axis=0)
    G['ffn_norm'] = jnp.concatenate([dg_ffn0, dg_ffn1], axis=0)
    G['ffn_wgu_t0'] = dwgu0
    G['ffn_w_down0'] = dwd0
    return sq_err, dx, G, (early_own, early_recv)


MESH_IDS = pl.DeviceIdType.MESH
N_PEER = N_DEV - 1


def _exchange_sems(n):
    return [pltpu.SemaphoreType.DMA((n * N_PEER,)), pltpu.SemaphoreType.DMA((n * N_PEER,)),
            pltpu.SemaphoreType.DMA((n,))]


def _gather_phases(x_refs, out_refs, send_sems, recv_sems, local_sems):
    n = len(x_refs)
    x, y, c = lax.axis_index("x"), lax.axis_index("y"), lax.axis_index("c")
    me, sibling = (x, y, c), (x, y, 1 - c)
    chips = [(1 - x, y), (x, 1 - y), (1 - x, 1 - y)]

    def slot(a, px, py, pc):
        return out_refs[a].at[4 * px + 2 * py + pc]

    def copy(a, k, blk, to, src=None):
        return pltpu.make_async_remote_copy(
            src_ref=slot(a, *blk) if src is None else src, dst_ref=slot(a, *blk),
            send_sem=send_sems.at[a * N_PEER + k], recv_sem=recv_sems.at[a * N_PEER + k],
            device_id=to, device_id_type=MESH_IDS)

    def local(a):
        return pltpu.make_async_copy(x_refs[a], slot(a, *me), local_sems.at[a])

    def first(a):
        return [copy(a, 0, me, sibling, src=x_refs[a])] + [copy(a, 1 + j, me, (*chip, c), src=x_refs[a])
                                                           for j, chip in enumerate(chips)]

    def passed(a):
        return [copy(a, 4 + j, (*chip, c), sibling) for j, chip in enumerate(chips)]

    def start():
        for a in range(n):
            local(a).start()
        for a in range(n):
            for cp in first(a):
                cp.start()

    def forward():
        for a in range(n):
            for j, chip in enumerate(chips):
                copy(a, 1 + j, (*chip, c), me).wait_recv()
                copy(a, 4 + j, (*chip, c), sibling).start()

    def finish():
        for a in range(n):
            copy(a, 0, sibling, me).wait_recv()
            for j, chip in enumerate(chips):
                copy(a, 4 + j, (*chip, 1 - c), me).wait_recv()
        for a in range(n):
            for cp in first(a) + passed(a):
                cp.wait_send()
            local(a).wait()

    return start, forward, finish


def _all_gather(blocks):
    n = len(blocks)

    def body(*refs):
        start, forward, finish = _gather_phases(refs[:n], refs[n:2 * n], *refs[2 * n:])
        start()
        forward()
        finish()

    anyspec = pl.BlockSpec(memory_space=pl.ANY)
    return pl.pallas_call(
        body, name="fsdp_all_gather",
        out_shape=[jax.ShapeDtypeStruct((N_DEV,) + b.shape, b.dtype) for b in blocks],
        in_specs=[anyspec] * n, out_specs=[anyspec] * n,
        scratch_shapes=_exchange_sems(n),
    )(*blocks)


def _scatter_phases(s_refs, r_refs, send_sems, recv_sems, local_sems):
    n = len(s_refs)
    x, y, c = lax.axis_index("x"), lax.axis_index("y"), lax.axis_index("c")
    me = 4 * x + 2 * y + c

    def local(a):
        return pltpu.make_async_copy(s_refs[a].at[me], r_refs[a].at[me], local_sems.at[a])

    def copy(a, k):
        px, py, pc = x ^ ((k >> 2) & 1), y ^ ((k >> 1) & 1), c ^ (k & 1)
        return pltpu.make_async_remote_copy(
            src_ref=s_refs[a].at[4 * px + 2 * py + pc], dst_ref=r_refs[a].at[me],
            send_sem=send_sems.at[a * N_PEER + k - 1], recv_sem=recv_sems.at[a * N_PEER + k - 1],
            device_id=(px, py, pc), device_id_type=MESH_IDS)

    def start():
        for a in range(n):
            local(a).start()
        for a in range(n):
            for k in range(1, N_DEV):
                copy(a, k).start()

    def finish():
        for a in range(n):
            for k in range(1, N_DEV):
                copy(a, k).wait()
            local(a).wait()

    return start, finish


def _scatter_exchange(sends):
    n = len(sends)

    def body(*refs):
        start, finish = _scatter_phases(refs[:n], refs[n:2 * n], *refs[2 * n:])
        start()
        finish()

    anyspec = pl.BlockSpec(memory_space=pl.ANY)
    return pl.pallas_call(
        body, name="fsdp_scatter_exchange",
        out_shape=[jax.ShapeDtypeStruct(s.shape, s.dtype) for s in sends],
        in_specs=[anyspec] * n, out_specs=[anyspec] * n,
        scratch_shapes=_exchange_sems(n),
    )(*sends)


def _sum_contrib(recv, own, *, name, tr):
    _, R, C = recv.shape
    assert R % tr == 0

    def body(r_ref, own_ref, g_ref):
        me = 4 * lax.axis_index("x") + 2 * lax.axis_index("y") + lax.axis_index("c")
        g = jnp.zeros((tr, C), F32)
        for s in range(N_DEV):
            g = g + jnp.where(me == s, own_ref[...], r_ref[s].astype(F32))
        g_ref[...] = g

    row = pl.BlockSpec((tr, C), lambda i: (i, 0))
    return pl.pallas_call(
        body, name=name, grid=(R // tr,),
        in_specs=[pl.BlockSpec((N_DEV, tr, C), lambda i: (0, i, 0)), row], out_specs=row,
        out_shape=jax.ShapeDtypeStruct((R, C), F32),
        compiler_params=_cparams(("parallel",)),
    )(recv, own)


def _adamw(g, w, m, v, *, name, tr):
    R, C = g.shape
    assert R % tr == 0

    def body(g_ref, w_ref, m_ref, v_ref, d_ref, mo_ref, vo_ref):
        gv = g_ref[...]
        mn = ADAM_B1 * m_ref[...] + (1.0 - ADAM_B1) * gv
        vn = ADAM_B2 * v_ref[...] + (1.0 - ADAM_B2) * (gv * gv)
        m_hat = mn / (1.0 - ADAM_B1 ** ADAM_STEP)
        v_hat = vn / (1.0 - ADAM_B2 ** ADAM_STEP)
        d_ref[...] = -ADAM_LR * (m_hat / (jnp.sqrt(v_hat) + ADAM_EPS) + ADAM_WD * w_ref[...])
        mo_ref[...] = mn
        vo_ref[...] = vn

    row = pl.BlockSpec((tr, C), lambda i: (i, 0))
    return pl.pallas_call(
        body, name=name, grid=(R // tr,),
        in_specs=[row] * 4, out_specs=[row] * 3,
        out_shape=[jax.ShapeDtypeStruct((R, C), F32)] * 3,
        compiler_params=_cparams(("parallel",)),
    )(g, w, m, v)


SMALL_SHARDED = ('hy_w_gate2', 'cv_b_pw1', 'cv_w_dw', 'cv_b_dw', 'cv_ln_g', 'cv_ln_b', 'cv_b_pw2')
SMALL_REPLICATED = ('mix_norm', 'ffn_norm', 'hy_b_gate', 'hy_gla_norm', 'hy_sb_q_norm', 'hy_sb_k_norm')
LANES = 128


def _small_rows(n):
    return -(-n // (8 * LANES)) * 8


def _pack_small(parts, lead=()):
    out = []
    for p in parts:
        n = p.shape[-1]
        p = jnp.pad(p, [(0, 0)] * len(lead) + [(0, _small_rows(n) * LANES - n)])
        out.append(p.reshape(*lead, _small_rows(n), LANES))
    return jnp.concatenate(out, axis=len(lead))


def _unpack_small(packed, sizes, lead=()):
    out, r0 = [], 0
    for n in sizes:
        r = _small_rows(n)
        out.append(packed[..., r0:r0 + r, :].reshape(*lead, r * LANES)[..., :n])
        r0 += r
    return out


def _to_blocks(full, axis):
    shp = full.shape
    t = full.reshape(shp[:axis] + (N_DEV, shp[axis] // N_DEV) + shp[axis + 1:])
    return jnp.moveaxis(t, axis, 0)


def _from_blocks(blocks, axis):
    t = jnp.moveaxis(blocks, 0, axis)
    shp = t.shape
    return t.reshape(shp[:axis] + (shp[axis] * shp[axis + 1],) + shp[axis + 2:])


def kernel(x, mix_norm, ffn_norm, hy_w_in, hy_w_gate2, hy_b_gate, hy_gla_norm, hy_sb_q_norm, hy_sb_k_norm, hy_w_out, cv_w_pw1, cv_b_pw1, cv_w_dw, cv_b_dw, cv_ln_g, cv_ln_b, cv_w_pw2, cv_b_pw2, ffn_w_gate, ffn_w_up, ffn_w_down, loss_target, m_mix_norm, m_ffn_norm, m_hy_w_in, m_hy_w_gate2, m_hy_b_gate, m_hy_gla_norm, m_hy_sb_q_norm, m_hy_sb_k_norm, m_hy_w_out, m_cv_w_pw1, m_cv_b_pw1, m_cv_w_dw, m_cv_b_dw, m_cv_ln_g, m_cv_ln_b, m_cv_w_pw2, m_cv_b_pw2, m_ffn_w_gate, m_ffn_w_up, m_ffn_w_down, v_mix_norm, v_ffn_norm, v_hy_w_in, v_hy_w_gate2, v_hy_b_gate, v_hy_gla_norm, v_hy_sb_q_norm, v_hy_sb_k_norm, v_hy_w_out, v_cv_w_pw1, v_cv_b_pw1, v_cv_w_dw, v_cv_b_dw, v_cv_ln_g, v_cv_ln_b, v_cv_w_pw2, v_cv_b_pw2, v_ffn_w_gate, v_ffn_w_up, v_ffn_w_down):
    w_loc = dict(zip(WEIGHT_NAMES, (mix_norm, ffn_norm, hy_w_in, hy_w_gate2, hy_b_gate, hy_gla_norm, hy_sb_q_norm, hy_sb_k_norm, hy_w_out, cv_w_pw1, cv_b_pw1, cv_w_dw, cv_b_dw, cv_ln_g, cv_ln_b, cv_w_pw2, cv_b_pw2, ffn_w_gate, ffn_w_up, ffn_w_down)))
    m_loc = dict(zip(WEIGHT_NAMES, (m_mix_norm, m_ffn_norm, m_hy_w_in, m_hy_w_gate2, m_hy_b_gate, m_hy_gla_norm, m_hy_sb_q_norm, m_hy_sb_k_norm, m_hy_w_out, m_cv_w_pw1, m_cv_b_pw1, m_cv_w_dw, m_cv_b_dw, m_cv_ln_g, m_cv_ln_b, m_cv_w_pw2, m_cv_b_pw2, m_ffn_w_gate, m_ffn_w_up, m_ffn_w_down)))
    v_loc = dict(zip(WEIGHT_NAMES, (v_mix_norm, v_ffn_norm, v_hy_w_in, v_hy_w_gate2, v_hy_b_gate, v_hy_gla_norm, v_hy_sb_q_norm, v_hy_sb_k_norm, v_hy_w_out, v_cv_w_pw1, v_cv_b_pw1, v_cv_w_dw, v_cv_b_dw, v_cv_ln_g, v_cv_ln_b, v_cv_w_pw2, v_cv_b_pw2, v_ffn_w_gate, v_ffn_w_up, v_ffn_w_down)))

    Dm, F8 = D_MODEL, D_FF // N_DEV
    tr_ = lambda a: jnp.swapaxes(a, -1, -2)

    small_local = _pack_small([w_loc[n].reshape(-1) for n in SMALL_SHARDED])
    g_in, g_small = _all_gather([tr_(hy_w_in[0]).astype(BF16), small_local])
    late_blocks = [hy_w_out[0].astype(BF16),
                   tr_(cv_w_pw1[0]).astype(BF16),
                   cv_w_pw2[0].astype(BF16),
                   tr_(ffn_w_gate).astype(BF16),
                   tr_(ffn_w_up).astype(BF16),
                   ffn_w_down.astype(BF16)]
    small_sizes = [w_loc[n].size for n in SMALL_SHARDED]
    small_full = dict(zip(SMALL_SHARDED, _unpack_small(g_small, small_sizes, lead=(N_DEV,))))
    W = {n: w_loc[n] for n in SMALL_REPLICATED}
    W['hy_w_in_t'] = g_in.reshape(IN_WIDTH, Dm)
    W['hy_w_gate2'] = _from_blocks(small_full['hy_w_gate2'].reshape(N_DEV, GLA_GATE_RANK, GLA_DK // N_DEV), 1).astype(BF16)
    W['cv_w_dw'] = _from_blocks(small_full['cv_w_dw'].reshape(N_DEV, CONV_WIDTH, Dm // N_DEV), 1)
    for n in ('cv_b_pw1', 'cv_b_dw', 'cv_ln_g', 'cv_ln_b', 'cv_b_pw2'):
        W[n] = small_full[n].reshape(-1)

    sq_err, dx, G, (early_own, early_recv) = _local_step(x[0], loss_target[0], W, late_blocks)

    own_f32 = [
        G['hy_w_in_t'].reshape(N_DEV, IN_WIDTH // N_DEV, Dm),
        G['hy_w_out'].reshape(N_DEV, Dm // N_DEV, Dm),
        G['cv_w_pw1_t'].reshape(N_DEV, 2 * Dm // N_DEV, Dm),
        G['ffn_wgu_t0'].reshape(2, N_DEV, F8, Dm).transpose(1, 0, 2, 3).reshape(N_DEV, 2 * F8, Dm),
        G['ffn_w_down0'].reshape(N_DEV, F8, Dm),
    ]
    small_parts = []
    for n in SMALL_SHARDED:
        axis = SHARD_AXIS[n] - 1
        shard = w_loc[n].shape[1:]
        full = shard[:axis] + (shard[axis] * N_DEV,) + shard[axis + 1:]
        small_parts.append(_to_blocks(G[n].reshape(full), axis).reshape(N_DEV, -1))
    for n in SMALL_REPLICATED:
        small_parts.append(jnp.broadcast_to(G[n].reshape(1, -1), (N_DEV, G[n].size)))
    small_parts.append(jnp.broadcast_to(sq_err.reshape(1, 1), (N_DEV, 1)))
    send_small = _pack_small(small_parts, lead=(N_DEV,))
    recv = _scatter_exchange([a.astype(BF16) for a in own_f32] + [send_small])
    me = 4 * lax.axis_index("x") + 2 * lax.axis_index("y") + lax.axis_index("c")
    tags = ['hy_w_in', 'hy_w_out', 'cv_w_pw1', 'ffn_wgu0', 'ffn_w_down0', 'small', 'cv_w_pw2', 'ffn_wgu1', 'ffn_w_down1']
    own_all = own_f32 + [send_small] + list(early_own)
    recv_all = list(recv) + list(early_recv)
    gsum = dict((t, _sum_contrib(r, lax.dynamic_index_in_dim(o, me, 0, keepdims=False), name=f"sum_{t}", tr=r.shape[1]))
                for t, r, o in zip(tags, recv_all, own_all))

    grad = {}
    grad['hy_w_in'] = tr_(gsum['hy_w_in'])[None]
    grad['hy_w_out'] = gsum['hy_w_out'][None]
    grad['cv_w_pw1'] = tr_(gsum['cv_w_pw1'])[None]
    grad['cv_w_pw2'] = gsum['cv_w_pw2'][None]
    gu = jnp.stack([gsum['ffn_wgu0'], gsum['ffn_wgu1']]).reshape(2, 2, F8, Dm)
    grad['ffn_w_gate'] = tr_(gu[:, 0])
    grad['ffn_w_up'] = tr_(gu[:, 1])
    grad['ffn_w_down'] = jnp.stack([gsum['ffn_w_down0'], gsum['ffn_w_down1']])
    small_names = SMALL_SHARDED + SMALL_REPLICATED
    small_all = [w_loc[n].size for n in small_names]
    *small_grads, sq_sum = _unpack_small(gsum['small'], small_all + [1])
    for n, a in zip(small_names, small_grads):
        grad[n] = a.reshape(w_loc[n].shape)
    loss = 0.5 / Dm * sq_sum[0]

    delta, new_m, new_v = {}, {}, {}
    view = {'hy_w_in': (Dm, 256), 'hy_w_out': (Dm // N_DEV, Dm // N_DEV), 'cv_w_pw1': (Dm, 256),
            'cv_w_pw2': (Dm // N_DEV, Dm // N_DEV), 'ffn_w_gate': (2 * Dm, 256), 'ffn_w_up': (2 * Dm, 256),
            'ffn_w_down': (2 * F8, F8)}
    for n, (rows, tr) in view.items():
        shp = w_loc[n].shape
        outs = _adamw(grad[n].reshape(rows, -1), w_loc[n].reshape(rows, -1), m_loc[n].reshape(rows, -1),
                      v_loc[n].reshape(rows, -1), name=f"adamw_{n}", tr=tr)
        delta[n], new_m[n], new_v[n] = (o.reshape(shp) for o in outs)
    packed = [_pack_small([d[n].reshape(-1) for n in small_names] + [jnp.zeros((1,), F32)]) for d in (w_loc, m_loc, v_loc)]
    outs = _adamw(gsum['small'], *packed, name="adamw_small", tr=gsum['small'].shape[0])
    for dst, o in zip((delta, new_m, new_v), outs):
        for n, a in zip(small_names, _unpack_small(o, small_all)):
            dst[n] = a.reshape(w_loc[n].shape)

    return (loss, dx[None], *[grad[n] for n in WEIGHT_NAMES], *[delta[n] for n in WEIGHT_NAMES],
            *[new_m[n] for n in WEIGHT_NAMES], *[new_v[n] for n in WEIGHT_NAMES])
```

```python
import jax
import jax.numpy as jnp
from jax import lax
from jax.experimental import pallas as pl
from jax.experimental.pallas import tpu as pltpu

F32 = jnp.float32
BF16 = jnp.bfloat16
CDT = jnp.bfloat16

D_MODEL = 1024
EPS = 1e-6
CHUNK = 64
GLA_HEADS = 4
GLA_HEAD_K = 64
GLA_HEAD_V = 128
GLA_DK = GLA_HEADS * GLA_HEAD_K
GLA_DV = GLA_HEADS * GLA_HEAD_V
GLA_GATE_RANK = 16
GLA_GATE_NORMALIZER = 16.0
SB_HEAD_DIM = 64
SB_D = 512
SB_TILE = 256
SB_PAIR = 128
SB_SPLIT_LK = 2
SB_SPLIT_G = 1
SB_DEAD = 120.0
IN_WIDTH = 3088
IN_PAD = 3200
CONV_WIDTH = 31
CONV_HALO = 32
D_FF = 2816
N_DEV = 8

ADAM_LR = 0.001
ADAM_B1 = 0.9
ADAM_B2 = 0.999
ADAM_EPS = 1e-08
ADAM_WD = 0.01
ADAM_STEP = 10

VMEM_LIMIT = 56 * 1024 * 1024
TK_TOKENS = 2048

WEIGHT_NAMES = ['mix_norm', 'ffn_norm', 'hy_w_in', 'hy_w_gate2', 'hy_b_gate', 'hy_gla_norm', 'hy_sb_q_norm',
                'hy_sb_k_norm', 'hy_w_out', 'cv_w_pw1', 'cv_b_pw1', 'cv_w_dw', 'cv_b_dw', 'cv_ln_g', 'cv_ln_b',
                'cv_w_pw2', 'cv_b_pw2', 'ffn_w_gate', 'ffn_w_up', 'ffn_w_down']
SHARD_AXIS = {'mix_norm': None, 'ffn_norm': None, 'hy_w_in': 2, 'hy_w_gate2': 2, 'hy_b_gate': None,
              'hy_gla_norm': None, 'hy_sb_q_norm': None, 'hy_sb_k_norm': None, 'hy_w_out': 1, 'cv_w_pw1': 2,
              'cv_b_pw1': 1, 'cv_w_dw': 2, 'cv_b_dw': 1, 'cv_ln_g': 1, 'cv_ln_b': 1, 'cv_w_pw2': 1, 'cv_b_pw2': 1,
              'ffn_w_gate': 2, 'ffn_w_up': 2, 'ffn_w_down': 1}


def _cparams(sem=None, vmem=VMEM_LIMIT):
    return pltpu.CompilerParams(dimension_semantics=sem, vmem_limit_bytes=vmem)


def _log_sigmoid(x):
    return jnp.minimum(x, 0.0) - jnp.log1p(jnp.exp(-jnp.abs(x)))


def _sigmoid(x):
    return 1.0 / (1.0 + jnp.exp(-x))


def _softplus(x):
    return jnp.maximum(x, 0.0) + jnp.log(1.0 + jnp.exp(-jnp.abs(x)))


def _split_bf16(x, n):
    parts = []
    rem = x
    for _ in range(n):
        p = rem.astype(BF16)
        parts.append(p)
        rem = rem - p.astype(F32)
    return parts


def _dot_exact_rhs(x, m, n):
    return sum(jnp.dot(p, m, preferred_element_type=F32) for p in _split_bf16(x, n))


def _dot_exact_lhs(m, x, n):
    return sum(jnp.dot(m, p, preferred_element_type=F32) for p in _split_bf16(x, n))


_NN = (((1,), (0,)), ((), ()))
_NT = (((1,), (1,)), ((), ()))
_TN = (((0,), (0,)), ((), ()))


def _dg(a, b, dn):
    return lax.dot_general(a.astype(CDT), b.astype(CDT), dn, preferred_element_type=F32)


def _matmul(a, b, *, mode, out_dtype, name, tm, tn, tk, bias=None, residual=None):
    if mode == 'nn':
        (M, K), (K2, N) = a.shape, b.shape
    elif mode == 'nt':
        (M, K), (N, K2) = a.shape, b.shape
    else:
        (K, M), (K2, N) = a.shape, b.shape
    assert K == K2 and M % tm == 0 and N % tn == 0 and K % tk == 0, (name, a.shape, b.shape, tm, tn, tk)
    nk = K // tk
    a_spec = pl.BlockSpec((tk, tm), lambda i, j, k: (k, i)) if mode == 'tn' else pl.BlockSpec((tm, tk), lambda i, j, k: (i, k))
    b_spec = pl.BlockSpec((tn, tk), lambda i, j, k: (j, k)) if mode == 'nt' else pl.BlockSpec((tk, tn), lambda i, j, k: (k, j))
    dn = {'nn': _NN, 'nt': _NT, 'tn': _TN}[mode]
    has_bias, has_res = bias is not None, residual is not None

    def body(*refs):
        a_ref, b_ref = refs[0], refs[1]
        pos = 2
        bias_ref = res_ref = None
        if has_bias:
            bias_ref = refs[pos]
            pos += 1
        if has_res:
            res_ref = refs[pos]
            pos += 1
        o_ref = refs[pos]
        acc_ref = refs[pos + 1] if nk > 1 else None
        p = _dg(a_ref[...], b_ref[...], dn)

        def finish(acc):
            if has_bias:
                acc = acc + bias_ref[...]
            if has_res:
                acc = res_ref[...] + acc
            o_ref[...] = acc.astype(o_ref.dtype)

        if nk == 1:
            finish(p)
        else:
            k = pl.program_id(2)

            @pl.when(k == 0)
            def _():
                acc_ref[...] = p

            @pl.when(k > 0)
            def _():
                acc_ref[...] += p

            @pl.when(k == nk - 1)
            def _():
                finish(acc_ref[...])

    in_specs = [a_spec, b_spec]
    args = [a, b]
    if has_bias:
        in_specs.append(pl.BlockSpec((1, tn), lambda i, j, k: (0, j)))
        args.append(bias)
    if has_res:
        in_specs.append(pl.BlockSpec((tm, tn), lambda i, j, k: (i, j)))
        args.append(residual)
    return pl.pallas_call(
        body, name=name, grid=(M // tm, N // tn, nk),
        in_specs=in_specs, out_specs=pl.BlockSpec((tm, tn), lambda i, j, k: (i, j)),
        out_shape=jax.ShapeDtypeStruct((M, N), out_dtype),
        scratch_shapes=[pltpu.VMEM((tm, tn), F32)] if nk > 1 else [],
        compiler_params=_cparams(("parallel", "parallel", "arbitrary")),
    )(*args)


def _rms_fwd(x, g, *, name, tm=512):
    T, Dm = x.shape

    def body(x_ref, g_ref, o_ref):
        xv = x_ref[...]
        r = lax.rsqrt(jnp.mean(xv * xv, axis=-1, keepdims=True) + EPS)
        o_ref[...] = (xv * r * g_ref[...]).astype(o_ref.dtype)

    return pl.pallas_call(
        body, name=name, grid=(T // tm,),
        in_specs=[pl.BlockSpec((tm, Dm), lambda i: (i, 0)), pl.BlockSpec((1, Dm), lambda i: (0, 0))],
        out_specs=pl.BlockSpec((tm, Dm), lambda i: (i, 0)),
        out_shape=jax.ShapeDtypeStruct((T, Dm), CDT),
        compiler_params=_cparams(("parallel",)),
    )(x, g)


def _rms_bwd(dy, x, g, resid, *, name, tm=512):
    T, Dm = x.shape

    def body(dy_ref, x_ref, g_ref, res_ref, dx_ref, dxb_ref, dg_ref):
        i = pl.program_id(0)
        xv, dyv = x_ref[...], dy_ref[...]
        r = lax.rsqrt(jnp.mean(xv * xv, axis=-1, keepdims=True) + EPS)
        u = dyv * g_ref[...]
        dot = jnp.mean(u * xv, axis=-1, keepdims=True)
        dx = res_ref[...] + (r * u - xv * (r * r * r * dot))
        dx_ref[...] = dx
        dxb_ref[...] = dx.astype(dxb_ref.dtype)
        part = jnp.sum(dyv * xv * r, axis=0, keepdims=True)

        @pl.when(i == 0)
        def _():
            dg_ref[...] = part

        @pl.when(i > 0)
        def _():
            dg_ref[...] += part

    row = pl.BlockSpec((tm, Dm), lambda i: (i, 0))
    vec = pl.BlockSpec((1, Dm), lambda i: (0, 0))
    return pl.pallas_call(
        body, name=name, grid=(T // tm,),
        in_specs=[row, row, vec, row], out_specs=[row, row, vec],
        out_shape=[jax.ShapeDtypeStruct((T, Dm), F32), jax.ShapeDtypeStruct((T, Dm), CDT),
                   jax.ShapeDtypeStruct((1, Dm), F32)],
        compiler_params=_cparams(("arbitrary",)),
    )(dy, x, g, resid)


def _loss_head(y, tgt, *, tm=512):
    T, Dm = y.shape

    def body(y_ref, t_ref, s_ref, dy_ref, dyb_ref):
        i = pl.program_id(0)
        e = y_ref[...] - t_ref[...]
        dy = e * (1.0 / Dm)
        dy_ref[...] = dy
        dyb_ref[...] = dy.astype(dyb_ref.dtype)
        part = jnp.sum(jnp.sum(e * e, axis=1, keepdims=True), axis=0, keepdims=True)

        @pl.when(i == 0)
        def _():
            s_ref[...] = part

        @pl.when(i > 0)
        def _():
            s_ref[...] += part

    row = pl.BlockSpec((tm, Dm), lambda i: (i, 0))
    return pl.pallas_call(
        body, name="loss_head", grid=(T // tm,),
        in_specs=[row, row], out_specs=[pl.BlockSpec((1, 1), lambda i: (0, 0)), row, row],
        out_shape=[jax.ShapeDtypeStruct((1, 1), F32), jax.ShapeDtypeStruct((T, Dm), F32),
                   jax.ShapeDtypeStruct((T, Dm), CDT)],
        compiler_params=_cparams(("arbitrary",)),
    )(y, tgt)


def _swiglu_up(hf, wgu_t, *, name, tm=256):
    T, Dm = hf.shape
    F2 = wgu_t.shape[0]
    F = F2 // 2

    def body(h_ref, w_ref, ab_ref, s_ref):
        p = _dg(h_ref[...], w_ref[...], _NT)
        ab_ref[...] = p.astype(ab_ref.dtype)
        a = p[:, :F]
        b = p[:, F:]
        s_ref[...] = (a * _sigmoid(a) * b).astype(s_ref.dtype)

    return pl.pallas_call(
        body, name=name, grid=(T // tm,),
        in_specs=[pl.BlockSpec((tm, Dm), lambda i: (i, 0)), pl.BlockSpec((F2, Dm), lambda i: (0, 0))],
        out_specs=[pl.BlockSpec((tm, F2), lambda i: (i, 0)), pl.BlockSpec((tm, F), lambda i: (i, 0))],
        out_shape=[jax.ShapeDtypeStruct((T, F2), CDT), jax.ShapeDtypeStruct((T, F), CDT)],
        compiler_params=_cparams(("parallel",)),
    )(hf, wgu_t)


def _swiglu_dact(dh_c, wd, ab, *, name, tm=256):
    T, Dm = dh_c.shape
    F2 = ab.shape[1]
    F = F2 // 2

    def body(dh_ref, w_ref, ab_ref, o_ref):
        dsv = _dg(dh_ref[...], w_ref[...], _NT)
        a = ab_ref[:, :F].astype(F32)
        b = ab_ref[:, F:].astype(F32)
        sg = _sigmoid(a)
        o_ref[:, :F] = (dsv * b * (sg * (1.0 + a * (1.0 - sg)))).astype(o_ref.dtype)
        o_ref[:, F:] = (dsv * (a * sg)).astype(o_ref.dtype)

    return pl.pallas_call(
        body, name=name, grid=(T // tm,),
        in_specs=[pl.BlockSpec((tm, Dm), lambda i: (i, 0)), pl.BlockSpec((F, Dm), lambda i: (0, 0)),
                  pl.BlockSpec((tm, F2), lambda i: (i, 0))],
        out_specs=pl.BlockSpec((tm, F2), lambda i: (i, 0)),
        out_shape=jax.ShapeDtypeStruct((T, F2), CDT),
        compiler_params=_cparams(("parallel",)),
    )(dh_c, wd, ab)


SUBLANES = 8


def _shifted_copies(buf, shifted, tm):
    n = tm + CONV_HALO - SUBLANES
    for b in range(1, SUBLANES):
        shifted[b - 1] = buf[pl.ds(b, n), :]


CONV_ROWS = 64
CONV_LANES = 128


def _rows_from(buf, shifted, offset, rows, r0, cols):
    a, b = divmod(offset, SUBLANES)
    if b == 0:
        return buf[pl.ds(r0 + SUBLANES * a, rows), cols]
    return shifted[b - 1, pl.ds(r0 + SUBLANES * a, rows), cols]


def _conv_fwd(a, w_dw, b_dw, ln_g, ln_b, *, tm=256):
    T = a.shape[0]
    Dm = D_MODEL

    def body(a_ref, w_ref, bdw_ref, g_ref, b_ref, s_ref, u_ref, c_ref, ubuf, shifted):
        i = pl.program_id(0)

        @pl.when(i == 0)
        def _():
            ubuf[0:CONV_HALO, :] = jnp.zeros((CONV_HALO, Dm), F32)

        @pl.when(i > 0)
        def _():
            ubuf[0:CONV_HALO, :] = ubuf[tm:tm + CONV_HALO, :]

        u = a_ref[:, :Dm] * _sigmoid(a_ref[:, Dm:])
        ubuf[CONV_HALO:CONV_HALO + tm, :] = u
        u_ref[...] = u
        _shifted_copies(ubuf, shifted, tm)
        acc = jnp.zeros((tm, Dm), F32) + bdw_ref[...]
        for k in range(CONV_WIDTH):
            acc = acc + w_ref[k:k + 1, :] * _rows_from(ubuf, shifted, CONV_HALO - (CONV_WIDTH - 1) + k, tm, 0, slice(None))
        c_ref[...] = acc
        mu = jnp.mean(acc, axis=-1, keepdims=True)
        cen = acc - mu
        var = jnp.mean(cen * cen, axis=-1, keepdims=True)
        l = cen * lax.rsqrt(var + EPS) * g_ref[...] + b_ref[...]
        s_ref[...] = (l * _sigmoid(l)).astype(s_ref.dtype)

    row = pl.BlockSpec((tm, Dm), lambda i: (i, 0))
    vec = pl.BlockSpec((1, Dm), lambda i: (0, 0))
    return pl.pallas_call(
        body, name="conv_fwd", grid=(T // tm,),
        in_specs=[pl.BlockSpec((tm, 2 * Dm), lambda i: (i, 0)), pl.BlockSpec((CONV_HALO, Dm), lambda i: (0, 0)), vec, vec, vec],
        out_specs=[row, row, row],
        out_shape=[jax.ShapeDtypeStruct((T, Dm), CDT), jax.ShapeDtypeStruct((T, Dm), F32), jax.ShapeDtypeStruct((T, Dm), F32)],
        scratch_shapes=[pltpu.VMEM((tm + CONV_HALO, Dm), F32), pltpu.VMEM((SUBLANES - 1, tm + CONV_HALO - SUBLANES, Dm), F32)],
        compiler_params=_cparams(("arbitrary",)),
    )(a, w_dw, b_dw, ln_g, ln_b)


def _conv_bwd(ds, c, u, a, w_dw, ln_g, ln_b, sends, *, tm=256):
    T = a.shape[0]
    Dm = D_MODEL
    nt = T // tm
    ns = len(sends)

    def body(*refs):
        ds_ref, c_ref, u_ref, a_ref, w_ref, g_ref, b_ref = refs[:7]
        da_ref, db1_ref, dw_ref, dbdw_ref, dg_ref, dbln_ref = refs[7 + ns:13 + ns]
        dcbuf, shifted, du_scr = refs[13 + 2 * ns:16 + 2 * ns]
        x_start, x_finish = _scatter_phases(refs[7:7 + ns], refs[13 + ns:13 + 2 * ns], *refs[16 + 2 * ns:])
        i = pl.program_id(0)
        pl.when(i == 0)(x_start)

        @pl.when(i == 0)
        def _():
            dcbuf[tm:tm + CONV_HALO, :] = jnp.zeros((CONV_HALO, Dm), F32)
            db1_ref[...] = jnp.zeros_like(db1_ref)
            dw_ref[...] = jnp.zeros_like(dw_ref)
            dbdw_ref[...] = jnp.zeros_like(dbdw_ref)
            dg_ref[...] = jnp.zeros_like(dg_ref)
            dbln_ref[...] = jnp.zeros_like(dbln_ref)

        @pl.when(i > 0)
        def _():
            dcbuf[tm:tm + CONV_HALO, :] = dcbuf[0:CONV_HALO, :]

        cv = c_ref[...]
        mu = jnp.mean(cv, axis=-1, keepdims=True)
        cen = cv - mu
        var = jnp.mean(cen * cen, axis=-1, keepdims=True)
        rstd = lax.rsqrt(var + EPS)
        n = cen * rstd
        l = n * g_ref[...] + b_ref[...]
        sg = _sigmoid(l)
        dl = ds_ref[...] * (sg * (1.0 + l * (1.0 - sg)))
        dg_ref[...] += jnp.sum(dl * n, axis=0, keepdims=True)
        dbln_ref[...] += jnp.sum(dl, axis=0, keepdims=True)
        dn = dl * g_ref[...]
        dc = rstd * (dn - jnp.mean(dn, axis=-1, keepdims=True) - n * jnp.mean(dn * n, axis=-1, keepdims=True))
        dbdw_ref[...] += jnp.sum(dc, axis=0, keepdims=True)
        dcbuf[0:tm, :] = dc
        _shifted_copies(dcbuf, shifted, tm)
        groups = CONV_ROWS // SUBLANES
        for cb in range(Dm // CONV_LANES):
            cols = slice(cb * CONV_LANES, (cb + 1) * CONV_LANES)

            def block(r, dw_part, cols=cols):
                r0 = pl.multiple_of(r * CONV_ROWS, CONV_ROWS)
                uv = u_ref[pl.ds(r0, CONV_ROWS), cols]
                du_b = jnp.zeros((CONV_ROWS, CONV_LANES), F32)
                out = []
                for k in range(CONV_WIDTH):
                    slab = _rows_from(dcbuf, shifted, CONV_WIDTH - 1 - k, CONV_ROWS, r0, cols)
                    du_b = du_b + w_ref[k:k + 1, cols] * slab
                    out.append(dw_part[k] + jnp.sum((slab * uv).reshape(groups, SUBLANES, CONV_LANES), axis=0))
                du_scr[pl.ds(r0, CONV_ROWS), cols] = du_b
                return tuple(out)

            zero = jnp.zeros((SUBLANES, CONV_LANES), F32)
            dw_part = lax.fori_loop(0, tm // CONV_ROWS, block, (zero,) * CONV_WIDTH)
            for k in range(CONV_WIDTH):
                dw_ref[k:k + 1, cols] += jnp.sum(dw_part[k], axis=0, keepdims=True)
        du = du_scr[...]
        a1 = a_ref[:, :Dm]
        s2 = _sigmoid(a_ref[:, Dm:])
        da1 = du * s2
        da2 = du * a1 * (s2 * (1.0 - s2))
        da_ref[:, :Dm] = da1.astype(da_ref.dtype)
        da_ref[:, Dm:] = da2.astype(da_ref.dtype)
        db1_ref[:, :Dm] += jnp.sum(da1, axis=0, keepdims=True)
        db1_ref[:, Dm:] += jnp.sum(da2, axis=0, keepdims=True)
        pl.when(i == nt - 1)(x_finish)

    rev = lambda i: (nt - 1 - i, 0)
    row = pl.BlockSpec((tm, Dm), rev)
    row2 = pl.BlockSpec((tm, 2 * Dm), rev)
    vec = pl.BlockSpec((1, Dm), lambda i: (0, 0))
    vec2 = pl.BlockSpec((1, 2 * Dm), lambda i: (0, 0))
    taps = pl.BlockSpec((CONV_HALO, Dm), lambda i: (0, 0))
    anyspec = pl.BlockSpec(memory_space=pl.ANY)
    outs = pl.pallas_call(
        body, name="conv_bwd", grid=(nt,),
        in_specs=[row, row, row, row2, taps, vec, vec] + [anyspec] * ns,
        out_specs=[row2, vec2, taps, vec, vec, vec] + [anyspec] * ns,
        out_shape=[jax.ShapeDtypeStruct((T, 2 * Dm), CDT), jax.ShapeDtypeStruct((1, 2 * Dm), F32),
                   jax.ShapeDtypeStruct((CONV_HALO, Dm), F32), jax.ShapeDtypeStruct((1, Dm), F32),
                   jax.ShapeDtypeStruct((1, Dm), F32), jax.ShapeDtypeStruct((1, Dm), F32)]
        + [jax.ShapeDtypeStruct(s.shape, s.dtype) for s in sends],
        scratch_shapes=[pltpu.VMEM((tm + CONV_HALO, Dm), F32), pltpu.VMEM((SUBLANES - 1, tm + CONV_HALO - SUBLANES, Dm), F32),
                        pltpu.VMEM((tm, Dm), F32)]
        + _exchange_sems(ns),
        compiler_params=_cparams(("arbitrary",)),
    )(ds, c, u, a, w_dw, ln_g, ln_b, *sends)
    return outs[:6], outs[6:]


def _gla_head_masks(width, per_head):
    lane = lax.broadcasted_iota(jnp.int32, (1, width), 1)
    return [((lane >= h * per_head) & (lane < (h + 1) * per_head)).astype(F32) for h in range(GLA_HEADS)]


def _gla_specs(tm, order):
    return [pl.BlockSpec((tm, GLA_DK), lambda i: (order(i), 0)),
            pl.BlockSpec((tm, GLA_DK), lambda i: (order(i), 1)),
            pl.BlockSpec((tm, GLA_DV), lambda i: (order(i), 1)),
            pl.BlockSpec((tm, GLA_DV), lambda i: (order(i), 2)),
            pl.BlockSpec((tm, 128), lambda i: (order(i), 3072 // 128))]


def _gla_chunk_decay(la_c, tri):
    bc = _dot_exact_lhs(tri, la_c, 3)
    b_end = bc[CHUNK - 1:CHUNK, :]
    return b_end, jnp.exp(b_end - bc)


def _gla_fwd(proj, wg2p, b_gate, g_gla, *, tm=256):
    T = proj.shape[0]
    ncs = tm // CHUNK
    scale = GLA_HEAD_K ** -0.5

    def body(q_ref, k_ref, v_ref, r_ref, glr_ref, wg_ref, bg_ref, gg_ref, o_ref, oraw_ref, st_ref, s_scr):
        i = pl.program_id(0)

        @pl.when(i == 0)
        def _():
            s_scr[...] = jnp.zeros_like(s_scr)

        mk = _gla_head_masks(GLA_DK, GLA_HEAD_K)
        rr = lax.broadcasted_iota(jnp.int32, (CHUNK, CHUNK), 0)
        cc = lax.broadcasted_iota(jnp.int32, (CHUNK, CHUNK), 1)
        tri = (cc <= rr).astype(BF16)
        y = _dg(glr_ref[...], wg_ref[...], _NN) + bg_ref[...]
        la = _log_sigmoid(y) / GLA_GATE_NORMALIZER
        qs = q_ref[...] * scale
        for ci in range(ncs):
            rows = slice(ci * CHUNK, (ci + 1) * CHUNK)
            b_end, dec = _gla_chunk_decay(la[rows], tri)
            kend = (k_ref[rows, :] * dec).astype(CDT)
            upd = jnp.zeros((GLA_HEAD_V, GLA_DK), F32)
            for h in range(GLA_HEADS):
                vh = v_ref[rows, h * GLA_HEAD_V:(h + 1) * GLA_HEAD_V]
                upd = upd + mk[h] * _dg(vh, kend, _TN)
            s_new = jnp.exp(b_end) * s_scr[...] + upd
            s_scr[...] = s_new
            st_ref[ci] = s_new
            s_c = s_new.astype(CDT)
            for h in range(GLA_HEADS):
                o_h = _dg(qs[rows] * mk[h], s_c, _NT)
                oraw_ref[rows, h * GLA_HEAD_V:(h + 1) * GLA_HEAD_V] = o_h
        for h in range(GLA_HEADS):
            cols = slice(h * GLA_HEAD_V, (h + 1) * GLA_HEAD_V)
            o_h = oraw_ref[:, cols]
            rs = lax.rsqrt(jnp.mean(o_h * o_h, axis=-1, keepdims=True) + EPS)
            rg = r_ref[:, cols]
            o_ref[:, cols] = (o_h * rs * gg_ref[...] * (rg * _sigmoid(rg))).astype(o_ref.dtype)

    full = lambda shape: pl.BlockSpec(shape, lambda i: tuple(0 for _ in shape))
    return pl.pallas_call(
        body, name="gla_fwd", grid=(T // tm,),
        in_specs=_gla_specs(tm, lambda i: i) + [full((128, GLA_DK)), full((1, GLA_DK)), full((1, GLA_HEAD_V))],
        out_specs=[pl.BlockSpec((tm, GLA_DV), lambda i: (i, 0)), pl.BlockSpec((tm, GLA_DV), lambda i: (i, 0)),
                   pl.BlockSpec((ncs, GLA_HEAD_V, GLA_DK), lambda i: (i, 0, 0))],
        out_shape=[jax.ShapeDtypeStruct((T, GLA_DV), CDT), jax.ShapeDtypeStruct((T, GLA_DV), F32),
                   jax.ShapeDtypeStruct((T // CHUNK, GLA_HEAD_V, GLA_DK), F32)],
        scratch_shapes=[pltpu.VMEM((GLA_HEAD_V, GLA_DK), F32)],
        compiler_params=_cparams(("arbitrary",)),
    )(proj, proj, proj, proj, proj, wg2p, b_gate, g_gla)


def _gla_bwd(d_o, proj, oraw, states, wg2p, b_gate, g_gla, *, tm=256):
    T = proj.shape[0]
    nt = T // tm
    ncs = tm // CHUNK
    scale = GLA_HEAD_K ** -0.5

    def body(do_ref, q_ref, k_ref, v_ref, r_ref, glr_ref, oraw_ref, st_ref, stp_ref, wg_ref, bg_ref, gg_ref,
             dgla_ref, dglr_ref, dwg_ref, dbg_ref, dgg_ref, ds_scr, dy_scr, dor_scr):
        i = pl.program_id(0)
        tile = nt - 1 - i

        @pl.when(i == 0)
        def _():
            ds_scr[...] = jnp.zeros_like(ds_scr)
            dwg_ref[...] = jnp.zeros_like(dwg_ref)
            dbg_ref[...] = jnp.zeros_like(dbg_ref)
            dgg_ref[...] = jnp.zeros_like(dgg_ref)

        mk = _gla_head_masks(GLA_DK, GLA_HEAD_K)
        rr = lax.broadcasted_iota(jnp.int32, (CHUNK, CHUNK), 0)
        cc = lax.broadcasted_iota(jnp.int32, (CHUNK, CHUNK), 1)
        tri = (cc <= rr).astype(BF16)
        tri_t = (cc >= rr).astype(BF16)
        last_row = (lax.broadcasted_iota(jnp.int32, (CHUNK, 1), 0) == CHUNK - 1).astype(F32)

        dgg = jnp.zeros((1, GLA_HEAD_V), F32)
        for h in range(GLA_HEADS):
            cols = slice(h * GLA_HEAD_V, (h + 1) * GLA_HEAD_V)
            o_h = oraw_ref[:, cols]
            rs = lax.rsqrt(jnp.mean(o_h * o_h, axis=-1, keepdims=True) + EPS)
            rg = r_ref[:, cols]
            sg = _sigmoid(rg)
            dov = do_ref[:, cols]
            on = o_h * rs * gg_ref[...]
            d_on = dov * (rg * sg)
            dgla_ref[:, 2 * GLA_DK + GLA_DV + h * GLA_HEAD_V:2 * GLA_DK + GLA_DV + (h + 1) * GLA_HEAD_V] = (
                dov * on * (sg * (1.0 + rg * (1.0 - sg)))).astype(dgla_ref.dtype)
            dgg = dgg + jnp.sum(d_on * o_h * rs, axis=0, keepdims=True)
            uu = d_on * gg_ref[...]
            dor_scr[:, cols] = rs * uu - o_h * (rs * rs * rs * jnp.mean(uu * o_h, axis=-1, keepdims=True))
        dgg_ref[...] += dgg

        y = _dg(glr_ref[...], wg_ref[...], _NN) + bg_ref[...]
        la = _log_sigmoid(y) / GLA_GATE_NORMALIZER
        qs = q_ref[...] * scale
        for ci in reversed(range(ncs)):
            rows = slice(ci * CHUNK, (ci + 1) * CHUNK)
            b_end, dec = _gla_chunk_decay(la[rows], tri)
            decay = jnp.exp(b_end)
            kend = k_ref[rows, :] * dec
            kend_c = kend.astype(CDT)
            s_c = st_ref[ci].astype(CDT)
            if ci > 0:
                s_prev = st_ref[ci - 1]
            else:
                s_prev = jnp.where(tile > 0, stp_ref[0], 0.0)
            dqs = jnp.zeros((CHUNK, GLA_DK), F32)
            dst = ds_scr[...]
            for h in range(GLA_HEADS):
                do_h = dor_scr[rows, h * GLA_HEAD_V:(h + 1) * GLA_HEAD_V].astype(CDT)
                dqs = dqs + mk[h] * _dg(do_h, s_c, _NN)
                dst = dst + mk[h] * _dg(do_h, qs[rows], _TN)
            d_decay = jnp.sum(dst * s_prev, axis=0, keepdims=True)
            ds_scr[...] = decay * dst
            dst_c = dst.astype(CDT)
            dkend = jnp.zeros((CHUNK, GLA_DK), F32)
            for h in range(GLA_HEADS):
                cols = slice(h * GLA_HEAD_V, (h + 1) * GLA_HEAD_V)
                dv_h = _dg(kend * mk[h], dst_c, _NT)
                dgla_ref[rows, 2 * GLA_DK + h * GLA_HEAD_V:2 * GLA_DK + (h + 1) * GLA_HEAD_V] = dv_h.astype(dgla_ref.dtype)
                dkend = dkend + mk[h] * _dg(v_ref[rows, cols], dst_c, _NN)
            dgla_ref[rows, 0:GLA_DK] = (dqs * scale).astype(dgla_ref.dtype)
            dgla_ref[rows, GLA_DK:2 * GLA_DK] = (dkend * dec).astype(dgla_ref.dtype)
            mm = dkend * kend
            db_end = jnp.sum(mm, axis=0, keepdims=True) + d_decay * decay
            dbc = last_row * db_end - mm
            dla = _dot_exact_lhs(tri_t, dbc, 3)
            dy_scr[rows, :] = dla * (1.0 / GLA_GATE_NORMALIZER) * _sigmoid(-y[rows])
        dy = dy_scr[...]
        dbg_ref[...] += jnp.sum(dy, axis=0, keepdims=True)
        dwg_ref[...] += _dg(glr_ref[...], dy, _TN)
        dglr_ref[...] = _dg(dy, wg_ref[...], _NT).astype(dglr_ref.dtype)

    rev = lambda i: nt - 1 - i
    full = lambda shape: pl.BlockSpec(shape, lambda i: tuple(0 for _ in shape))
    st_spec = pl.BlockSpec((ncs, GLA_HEAD_V, GLA_DK), lambda i: (rev(i), 0, 0))
    stp_spec = pl.BlockSpec((1, GLA_HEAD_V, GLA_DK), lambda i: (jnp.maximum(rev(i) * ncs - 1, 0), 0, 0))
    return pl.pallas_call(
        body, name="gla_bwd", grid=(nt,),
        in_specs=[pl.BlockSpec((tm, GLA_DV), lambda i: (rev(i), 0))] + _gla_specs(tm, rev)
        + [pl.BlockSpec((tm, GLA_DV), lambda i: (rev(i), 0)), st_spec, stp_spec,
           full((128, GLA_DK)), full((1, GLA_DK)), full((1, GLA_HEAD_V))],
        out_specs=[pl.BlockSpec((tm, 2 * GLA_DK + 2 * GLA_DV), lambda i: (rev(i), 0)),
                   pl.BlockSpec((tm, 128), lambda i: (rev(i), 0)),
                   full((128, GLA_DK)), full((1, GLA_DK)), full((1, GLA_HEAD_V))],
        out_shape=[jax.ShapeDtypeStruct((T, 2 * GLA_DK + 2 * GLA_DV), CDT), jax.ShapeDtypeStruct((T, 128), CDT),
                   jax.ShapeDtypeStruct((128, GLA_DK), F32), jax.ShapeDtypeStruct((1, GLA_DK), F32),
                   jax.ShapeDtypeStruct((1, GLA_HEAD_V), F32)],
        scratch_shapes=[pltpu.VMEM((GLA_HEAD_V, GLA_DK), F32), pltpu.VMEM((tm, GLA_DK), F32),
                        pltpu.VMEM((tm, GLA_DV), F32)],
        compiler_params=_cparams(("arbitrary",)),
    )(d_o, proj, proj, proj, proj, proj, oraw, states, states, wg2p, b_gate, g_gla)


def _head_mean_matrix():
    r = lax.broadcasted_iota(jnp.int32, (SB_D, SB_D), 0) // SB_HEAD_DIM
    c = lax.broadcasted_iota(jnp.int32, (SB_D, SB_D), 1) // SB_HEAD_DIM
    return jnp.where(r == c, 1.0 / SB_HEAD_DIM, 0.0).astype(BF16)


def _sb_prep(proj, gq, gk, *, tm=256):
    T = proj.shape[0]
    scale = SB_HEAD_DIM ** -0.5

    def body(q_ref, k_ref, v_ref, gq_ref, gk_ref, qn_ref, kn_ref, vb_ref):
        hm = _head_mean_matrix()
        qv, kv = q_ref[...], k_ref[...]
        rq = lax.rsqrt(_dot_exact_rhs(qv * qv, hm, 3) + EPS)
        rk = lax.rsqrt(_dot_exact_rhs(kv * kv, hm, 3) + EPS)
        qn_ref[...] = (qv * rq * gq_ref[...] * scale).astype(qn_ref.dtype)
        kn_ref[...] = (kv * rk * gk_ref[...]).astype(kn_ref.dtype)
        vb_ref[...] = v_ref[...].astype(vb_ref.dtype)

    col = lambda j: pl.BlockSpec((tm, SB_D), lambda i: (i, j))
    vec = pl.BlockSpec((1, SB_D), lambda i: (0, 0))
    out = pl.BlockSpec((tm, SB_D), lambda i: (i, 0))
    return pl.pallas_call(
        body, name="sb_prep", grid=(T // tm,),
        in_specs=[col(3), col(4), col(5), vec, vec], out_specs=[out, out, out],
        out_shape=[jax.ShapeDtypeStruct((T, SB_D), CDT)] * 3,
        compiler_params=_cparams(("parallel",)),
    )(proj, proj, proj, gq, gk)


def _sb_prep_bwd(dqn, dkn, dv, proj, gq, gk, *, tm=256):
    T = proj.shape[0]
    scale = SB_HEAD_DIM ** -0.5

    def body(dqn_ref, dkn_ref, dv_ref, q_ref, k_ref, gq_ref, gk_ref, dsb_ref, dgq_ref, dgk_ref):
        i = pl.program_id(0)

        @pl.when(i == 0)
        def _():
            dgq_ref[...] = jnp.zeros_like(dgq_ref)
            dgk_ref[...] = jnp.zeros_like(dgk_ref)

        hm = _head_mean_matrix()

        def one(dn_ref, x_ref, g_ref, dg_ref, sc, lo):
            xv = x_ref[...]
            dnv = dn_ref[...] * sc
            r = lax.rsqrt(_dot_exact_rhs(xv * xv, hm, 3) + EPS)
            u = dnv * g_ref[...]
            dot = _dot_exact_rhs(u * xv, hm, 3)
            dsb_ref[:, lo:lo + SB_D] = (r * u - xv * (r * r * r * dot)).astype(dsb_ref.dtype)
            dg_ref[...] += jnp.sum(dnv * xv * r, axis=0, keepdims=True)

        one(dqn_ref, q_ref, gq_ref, dgq_ref, scale, 0)
        one(dkn_ref, k_ref, gk_ref, dgk_ref, 1.0, SB_D)
        dsb_ref[:, 2 * SB_D:3 * SB_D] = dv_ref[...].astype(dsb_ref.dtype)

    col = lambda j: pl.BlockSpec((tm, SB_D), lambda i: (i, j))
    vec = pl.BlockSpec((1, SB_D), lambda i: (0, 0))
    row = pl.BlockSpec((tm, SB_D), lambda i: (i, 0))
    return pl.pallas_call(
        body, name="sb_prep_bwd", grid=(T // tm,),
        in_specs=[row, row, row, col(3), col(4), vec, vec],
        out_specs=[pl.BlockSpec((tm, 3 * SB_D), lambda i: (i, 0)), vec, vec],
        out_shape=[jax.ShapeDtypeStruct((T, 3 * SB_D), CDT), jax.ShapeDtypeStruct((1, SB_D), F32),
                   jax.ShapeDtypeStruct((1, SB_D), F32)],
        compiler_params=_cparams(("arbitrary",)),
    )(dqn, dkn, dv, proj, proj, gq, gk)


def _sb_masks():
    lane = lax.broadcasted_iota(jnp.int32, (1, 128), 1)
    m = [lane < SB_HEAD_DIM, lane >= SB_HEAD_DIM]
    return m, [x.astype(F32) for x in m]


def _sb_fwd(qn, kn, vb, blocks):
    T = qn.shape[0]
    nq = T // SB_TILE
    B, P = SB_TILE, SB_PAIR
    hs = range(2)
    n = len(blocks)
    nhp = SB_D // P

    def body(*refs):
        q_ref, k_ref, v_ref = refs[:3]
        o_ref, l_ref, done_ref = refs[3 + n:6 + n]
        acc_ref = refs[6 + 2 * n]
        hp, qb = pl.program_id(0), pl.program_id(1)
        g_start, g_forward, g_finish = _gather_phases(refs[3:3 + n], refs[6 + n:6 + 2 * n], *refs[7 + 2 * n:])
        pl.when((hp == 0) & (qb == 0))(g_start)
        pl.when((hp == nhp - 1) & (qb == 0))(g_forward)
        m, mf = _sb_masks()
        row = lax.broadcasted_iota(jnp.int32, (B, B), 0)
        col = lax.broadcasted_iota(jnp.int32, (B, B), 1)
        later = (row > col).astype(BF16)
        past = col < row
        q2 = q_ref[...]
        qm = [jnp.where(m[h], q2, jnp.zeros_like(q2)) for h in hs]
        acc_ref[...] = jnp.zeros_like(acc_ref)

        def keys(kb):
            return k_ref[pl.ds(pl.multiple_of(kb * B, B), B), :]

        def values(kb):
            return v_ref[pl.ds(pl.multiple_of(kb * B, B), B), :]

        def scores(kb):
            k2 = keys(kb)
            return [_dg(qm[h], k2, _NT) for h in hs]

        def run(tiles, R):
            zs, cum, rsum = {}, {}, {}
            for t, (z, _, diag) in enumerate(tiles):
                for h in hs:
                    sp = _softplus(z[h])
                    lk = jnp.where(past, sp, 0.0) if diag else sp
                    zs[t, h] = z[h] - sp
                    cum[t, h] = _dot_exact_rhs(lk, later, SB_SPLIT_LK)
                    rsum[t, h] = jnp.sum(lk, axis=1, keepdims=True)
            R = list(R)
            for t, (_, kb, diag) in enumerate(tiles):
                v2 = values(kb)
                for h in hs:
                    w = jnp.exp(zs[t, h] - (cum[t, h] + R[h]))
                    if diag:
                        w = jnp.where(past, w, 0.0)
                    acc_ref[h] += _dg(w, v2, _NN)
                R = [R[h] + rsum[t, h] for h in hs]
            return tuple(R)

        def live(r):
            return (jnp.minimum(jnp.min(r[0]), jnp.min(r[1])) < SB_DEAD).astype(jnp.int32)

        zero = jnp.zeros((B, 1), F32)
        R = lax.cond(qb > 0,
                     lambda r: run([(scores(qb), qb, True), (scores(jnp.maximum(qb - 1, 0)), jnp.maximum(qb - 1, 0), False)], r),
                     lambda r: run([(scores(qb), qb, True)], r), (zero, zero))
        rest = jnp.maximum(qb - 1, 0)
        npairs = rest // 2

        def pair(carry):
            i, r, za, zb = carry[0], carry[2:4], carry[4:6], carry[6:8]
            ka = qb - 2 - 2 * i
            nxt = (scores(jnp.maximum(ka - 2, 0)), scores(jnp.maximum(ka - 3, 0)))
            r = run([(za, ka, False), (zb, ka - 1, False)], r)
            return (i + 1, live(r), *r, *nxt[0], *nxt[1])

        out = lax.while_loop(lambda c: (c[0] < npairs) & (c[1] > 0), pair,
                             (jnp.int32(0), live(R), *R, *scores(jnp.maximum(qb - 2, 0)), *scores(jnp.maximum(qb - 3, 0))))
        last = (rest % 2 == 1) & (out[0] == npairs) & (out[1] > 0)
        tile0 = jnp.int32(0)
        R = lax.cond(last, lambda r: run([(scores(tile0), tile0, False)], r), lambda r: r, out[2:4])
        done_ref[hp, qb] = 2 * out[0] + last.astype(jnp.int32)
        o_ref[...] = (acc_ref[0] * mf[0] + acc_ref[1] * mf[1]).astype(o_ref.dtype)
        l_ref[0] = R[0] * mf[0] + R[1] * mf[1]
        pl.when((hp == nhp - 1) & (qb == nq - 1))(g_finish)

    slab = pl.BlockSpec((T, P), lambda hp, qb: (0, hp))
    blk = pl.BlockSpec((B, P), lambda hp, qb: (qb, hp))
    anyspec = pl.BlockSpec(memory_space=pl.ANY)
    outs = pl.pallas_call(
        body, name="sb_fwd", grid=(nhp, nq),
        in_specs=[blk, slab, slab] + [anyspec] * n,
        out_specs=[blk, pl.BlockSpec((1, B, P), lambda hp, qb: (hp, qb, 0)), pl.BlockSpec(memory_space=pltpu.SMEM)]
        + [anyspec] * n,
        out_shape=[jax.ShapeDtypeStruct((T, SB_D), CDT), jax.ShapeDtypeStruct((nhp, T, P), F32),
                   jax.ShapeDtypeStruct((nhp, nq), jnp.int32)]
        + [jax.ShapeDtypeStruct((N_DEV,) + b.shape, b.dtype) for b in blocks],
        scratch_shapes=[pltpu.VMEM((2, B, P), F32)] + _exchange_sems(n),
        compiler_params=_cparams(("arbitrary", "arbitrary")),
    )(qn, kn, vb, *blocks)
    return outs[0], outs[1], outs[2], outs[3:]


def _sb_bwd(d_o, qn, kn, vb, lsum, done):
    T = qn.shape[0]
    nq = T // SB_TILE
    B, P = SB_TILE, SB_PAIR
    hs = range(2)

    def body(do_ref, q_ref, k_ref, v_ref, l_ref, done_ref, dq_ref, dk_ref, dv_ref, dqacc_ref):
        qb = pl.program_id(1)

        @pl.when(qb == 0)
        def _():
            dk_ref[...] = jnp.zeros_like(dk_ref)
            dv_ref[...] = jnp.zeros_like(dv_ref)

        m, mf = _sb_masks()
        row = lax.broadcasted_iota(jnp.int32, (B, B), 0)
        col = lax.broadcasted_iota(jnp.int32, (B, B), 1)
        upto = (row <= col).astype(BF16)
        before = (row < col).astype(BF16)
        past = col < row
        q2 = q_ref[...]
        qm = [jnp.where(m[h], q2, jnp.zeros_like(q2)) for h in hs]
        do2 = do_ref[...]
        dom = [jnp.where(m[h], do2, 0.0).astype(CDT) for h in hs]
        lb = l_ref[0]
        ltot = [lb[:, 0:1], lb[:, SB_HEAD_DIM:SB_HEAD_DIM + 1]]
        dqacc_ref[...] = jnp.zeros_like(dqacc_ref)

        def rows(kb):
            return pl.ds(pl.multiple_of(kb * B, B), B)

        def scores(kb):
            k2, v2 = k_ref[rows(kb), :], v_ref[rows(kb), :]
            return [_dg(qm[h], k2, _NT) for h in hs] + [_dg(dom[h], v2, _NT) for h in hs]

        def run(tiles, carry):
            Ps, Pg = list(carry[0]), list(carry[1])
            zs, sp_, cum, rest = {}, {}, {}, {}
            for t, tl in enumerate(tiles):
                diag = tl[5]
                for h in hs:
                    sp = _softplus(tl[h])
                    lk = jnp.where(past, sp, 0.0) if diag else sp
                    zs[t, h], sp_[t, h] = tl[h] - sp, sp
                    cum[t, h] = _dot_exact_rhs(lk, upto, SB_SPLIT_LK)
                    rest[t, h] = ltot[h] - Ps[h]
                    Ps[h] = Ps[h] + jnp.sum(lk, axis=1, keepdims=True)
            w, g, gx = {}, {}, {}
            for t, tl in enumerate(tiles):
                diag = tl[5]
                for h in hs:
                    wt = jnp.exp(zs[t, h] - (rest[t, h] - cum[t, h]))
                    if diag:
                        wt = jnp.where(past, wt, 0.0)
                    w[t, h] = wt
                    g[t, h] = wt * tl[2 + h]
                    gx[t, h] = _dot_exact_rhs(g[t, h], before, SB_SPLIT_G) + Pg[h]
                    Pg[h] = Pg[h] + jnp.sum(g[t, h], axis=1, keepdims=True)
            for t, tl in enumerate(tiles):
                kb, diag = tl[4], tl[5]
                k2 = k_ref[rows(kb), :]
                for h in hs:
                    sneg = jnp.exp(-sp_[t, h])
                    dz = g[t, h] * sneg - (1.0 - sneg) * gx[t, h]
                    if diag:
                        dz = jnp.where(past, dz, 0.0)
                    dz_c = dz.astype(CDT)
                    dv_ref[rows(kb), :] += _dg(w[t, h], dom[h], _TN)
                    dk_ref[rows(kb), :] += _dg(dz_c, qm[h], _TN)
                    dqacc_ref[h] += _dg(dz_c, k2, _NN)
            return tuple(Ps), tuple(Pg)

        zero = jnp.zeros((B, 1), F32)
        rest_tiles = jnp.maximum(qb - 1, 0)
        done = jnp.clip(done_ref[pl.program_id(0), qb], 0, rest_tiles)
        npairs = done // 2
        tile0 = jnp.int32(0)
        sums = lax.cond(done % 2 == 1, lambda s: run([(*scores(tile0), tile0, False)], s), lambda s: s,
                        ((zero, zero), (zero, zero)))
        left0 = qb - 1 - 2 * npairs

        def pair(i, carry):
            sums, ta, tb = (carry[0:2], carry[2:4]), carry[4:8], carry[8:12]
            ka = left0 + 2 * i
            nxt = scores(jnp.minimum(ka + 2, qb)) + scores(jnp.minimum(ka + 3, qb))
            Ps, Pg = run([(*ta, ka, False), (*tb, ka + 1, False)], sums)
            return (*Ps, *Pg, *nxt)

        first = jnp.clip(left0, 0, qb)
        out = lax.fori_loop(0, npairs, pair,
                            (*sums[0], *sums[1], *scores(first), *scores(jnp.minimum(first + 1, qb))))
        sums = (out[0:2], out[2:4])
        near = jnp.maximum(qb - 1, 0)
        lax.cond(qb > 0,
                 lambda s: run([(*scores(near), near, False), (*scores(qb), qb, True)], s),
                 lambda s: run([(*scores(qb), qb, True)], s), sums)
        dq_ref[...] = dqacc_ref[0] * mf[0] + dqacc_ref[1] * mf[1]

    slab = pl.BlockSpec((T, P), lambda hp, qb: (0, hp))
    blk = pl.BlockSpec((B, P), lambda hp, qb: (qb, hp))
    return pl.pallas_call(
        body, name="sb_bwd", grid=(SB_D // P, nq),
        in_specs=[blk, blk, slab, slab, pl.BlockSpec((1, B, P), lambda hp, qb: (hp, qb, 0)),
                  pl.BlockSpec(memory_space=pltpu.SMEM)],
        out_specs=[blk, slab, slab],
        out_shape=[jax.ShapeDtypeStruct((T, SB_D), F32)] * 3,
        scratch_shapes=[pltpu.VMEM((2, B, P), F32)],
        compiler_params=_cparams(("arbitrary", "arbitrary")),
    )(d_o, qn, kn, vb, lsum, done)


def _regroup_in_rows(wt):
    cut = 2 * GLA_DK + 2 * GLA_DV
    pad = jnp.zeros((IN_PAD - IN_WIDTH, wt.shape[1]), wt.dtype)
    return jnp.concatenate([wt[:cut], wt[cut + GLA_GATE_RANK:], wt[cut:cut + GLA_GATE_RANK], pad], axis=0)


def _ungroup_in_rows(gt):
    cut = 2 * GLA_DK + 2 * GLA_DV
    return jnp.concatenate([gt[:cut], gt[3072:3072 + GLA_GATE_RANK], gt[cut:3072]], axis=0)


def _colsum(v, *, name, tm=512):
    T, C = v.shape

    def body(v_ref, o_ref):
        i = pl.program_id(0)
        part = jnp.sum(v_ref[...], axis=0, keepdims=True)

        @pl.when(i == 0)
        def _():
            o_ref[...] = part

        @pl.when(i > 0)
        def _():
            o_ref[...] += part

    return pl.pallas_call(
        body, name=name, grid=(T // tm,),
        in_specs=[pl.BlockSpec((tm, C), lambda i: (i, 0))], out_specs=pl.BlockSpec((1, C), lambda i: (0, 0)),
        out_shape=jax.ShapeDtypeStruct((1, C), F32),
        compiler_params=_cparams(("arbitrary",)),
    )(v)


def _ffn_fwd(h, g_norm, wgu_t, wd, tag):
    hf = _rms_fwd(h, g_norm, name=f"ffn{tag}_norm")
    ab, s = _swiglu_up(hf, wgu_t, name=f"ffn{tag}_up")
    h_out = _matmul(s, wd, mode='nn', out_dtype=F32, name=f"ffn{tag}_down", tm=512, tn=D_MODEL, tk=D_FF, residual=h)
    return h_out, (hf, ab, s)


def _ffn_bwd(dh, dh_c, h_in, g_norm, wgu_t, wd, saved, tag):
    hf, ab, s = saved
    dwd = _matmul(s, dh_c, mode='tn', out_dtype=F32, name=f"ffn{tag}_dwd", tm=D_FF // 2, tn=D_MODEL, tk=TK_TOKENS)
    dab = _swiglu_dact(dh_c, wd, ab, name=f"ffn{tag}_dact")
    dwgu_t = _matmul(dab, hf, mode='tn', out_dtype=F32, name=f"ffn{tag}_dwgu", tm=D_FF // 2, tn=D_MODEL, tk=TK_TOKENS)
    dhf = _matmul(dab, wgu_t, mode='nn', out_dtype=F32, name=f"ffn{tag}_dhf", tm=256, tn=D_MODEL, tk=2 * D_FF)
    dh_in, dh_in_c, dg = _rms_bwd(dhf, h_in, g_norm, dh, name=f"ffn{tag}_dnorm")
    return dh_in, dh_in_c, dwgu_t, dwd, dg


def _late_weights(gathered):
    g_out, g_pw1, g_pw2, g_gate, g_up, g_down = gathered
    Dm = D_MODEL
    return {
        'hy_w_out': g_out.reshape(Dm, Dm),
        'cv_w_pw1_t': g_pw1.reshape(2 * Dm, Dm),
        'cv_w_pw2': g_pw2.reshape(Dm, Dm),
        'ffn_wgu_t': [jnp.concatenate([g_gate[:, l].reshape(D_FF, Dm), g_up[:, l].reshape(D_FF, Dm)], axis=0)
                      for l in range(2)],
        'ffn_w_down': [g_down[:, l].reshape(D_FF, Dm) for l in range(2)],
    }


def _local_step(x, tgt, W, late_blocks):
    row = lambda v: v.reshape(1, -1)
    win_p = _regroup_in_rows(W['hy_w_in_t'])
    wg2p = jnp.pad(W['hy_w_gate2'], ((0, 128 - GLA_GATE_RANK), (0, 0)))
    b_gate = row(W['hy_b_gate'])
    g_gla = row(W['hy_gla_norm'])
    gq = jnp.tile(W['hy_sb_q_norm'].reshape(-1), SB_D // SB_HEAD_DIM).reshape(1, SB_D)
    gk = jnp.tile(W['hy_sb_k_norm'].reshape(-1), SB_D // SB_HEAD_DIM).reshape(1, SB_D)
    w_dw = jnp.pad(W['cv_w_dw'], ((0, CONV_HALO - CONV_WIDTH), (0, 0)))
    mixn = [row(W['mix_norm'][l]) for l in range(2)]
    ffnn = [row(W['ffn_norm'][l]) for l in range(2)]

    hn0 = _rms_fwd(x, mixn[0], name="mix0_norm")
    proj = _matmul(hn0, win_p, mode='nt', out_dtype=F32, name="hy_in", tm=256, tn=IN_PAD, tk=D_MODEL)
    o_gla, o_raw, states = _gla_fwd(proj, wg2p, b_gate, g_gla)
    qn, kn, vb = _sb_prep(proj, gq, gk)
    o_sb, lsum, sb_done, gathered = _sb_fwd(qn, kn, vb, late_blocks)
    W = {**W, **_late_weights(gathered)}
    w_out, wgu, wd = W['hy_w_out'], W['ffn_wgu_t'], W['ffn_w_down']
    o_mix = jnp.concatenate([o_gla, o_sb], axis=1)
    h1 = _matmul(o_mix, w_out, mode='nn', out_dtype=F32, name="hy_out", tm=512, tn=D_MODEL, tk=D_MODEL, residual=x)
    h2, ffn0_saved = _ffn_fwd(h1, ffnn[0], wgu[0], wd[0], 0)
    hn1 = _rms_fwd(h2, mixn[1], name="mix1_norm")
    a_cv = _matmul(hn1, W['cv_w_pw1_t'], mode='nt', out_dtype=F32, name="cv_pw1", tm=512, tn=2 * D_MODEL, tk=D_MODEL,
                   bias=row(W['cv_b_pw1']))
    s_cv, u_cv, c_cv = _conv_fwd(a_cv, w_dw, row(W['cv_b_dw']), row(W['cv_ln_g']), row(W['cv_ln_b']))
    h3 = _matmul(s_cv, W['cv_w_pw2'], mode='nn', out_dtype=F32, name="cv_pw2", tm=512, tn=D_MODEL, tk=D_MODEL,
                 bias=row(W['cv_b_pw2']), residual=h2)
    h4, ffn1_saved = _ffn_fwd(h3, ffnn[1], wgu[1], wd[1], 1)
    sq_err, dy, dy_c = _loss_head(h4, tgt)

    G = {}
    dh3, dh3_c, dwgu1, dwd1, dg_ffn1 = _ffn_bwd(dy, dy_c, h3, ffnn[1], wgu[1], wd[1], ffn1_saved, 1)
    G['cv_b_pw2'] = _colsum(dh3, name="cv_db2")
    G['cv_w_pw2'] = _matmul(s_cv, dh3_c, mode='tn', out_dtype=F32, name="cv_dw2", tm=D_MODEL, tn=D_MODEL, tk=TK_TOKENS)
    ds_cv = _matmul(dh3_c, W['cv_w_pw2'], mode='nt', out_dtype=F32, name="cv_ds", tm=512, tn=D_MODEL, tk=D_MODEL)
    F8 = D_FF // N_DEV
    early_own = [G['cv_w_pw2'].reshape(N_DEV, D_MODEL // N_DEV, D_MODEL),
                 dwgu1.reshape(2, N_DEV, F8, D_MODEL).transpose(1, 0, 2, 3).reshape(N_DEV, 2 * F8, D_MODEL),
                 dwd1.reshape(N_DEV, F8, D_MODEL)]
    (da_cv, db1, dwdw, dbdw, dlng, dlnb), early_recv = _conv_bwd(
        ds_cv, c_cv, u_cv, a_cv, w_dw, row(W['cv_ln_g']), row(W['cv_ln_b']), [a.astype(BF16) for a in early_own])
    G['cv_b_pw1'] = db1
    G['cv_w_dw'] = dwdw[:CONV_WIDTH]
    G['cv_b_dw'], G['cv_ln_g'], G['cv_ln_b'] = dbdw, dlng, dlnb
    G['cv_w_pw1_t'] = _matmul(da_cv, hn1, mode='tn', out_dtype=F32, name="cv_dw1", tm=D_MODEL, tn=D_MODEL, tk=TK_TOKENS)
    dhn1 = _matmul(da_cv, W['cv_w_pw1_t'], mode='nn', out_dtype=F32, name="cv_dhn", tm=512, tn=D_MODEL, tk=2 * D_MODEL)
    dh2, dh2_c, dg_mix1 = _rms_bwd(dhn1, h2, mixn[1], dh3, name="mix1_dnorm")
    dh1, dh1_c, dwgu0, dwd0, dg_ffn0 = _ffn_bwd(dh2, dh2_c, h1, ffnn[0], wgu[0], wd[0], ffn0_saved, 0)
    G['hy_w_out'] = _matmul(o_mix, dh1_c, mode='tn', out_dtype=F32, name="hy_dwout", tm=D_MODEL, tn=D_MODEL, tk=TK_TOKENS)
    d_omix = _matmul(dh1_c, w_out, mode='nt', out_dtype=F32, name="hy_domix", tm=512, tn=D_MODEL, tk=D_MODEL)
    dgla, dglr, dwg2, dbg, dgg = _gla_bwd(d_omix[:, :GLA_DV], proj, o_raw, states, wg2p, b_gate, g_gla)
    dqn, dkn, dvs = _sb_bwd(d_omix[:, GLA_DV:], qn, kn, vb, lsum, sb_done)
    dsb, dgq, dgk = _sb_prep_bwd(dqn, dkn, dvs, proj, gq, gk)
    dproj = jnp.concatenate([dgla, dsb, dglr], axis=1)
    dwin_p = _matmul(dproj, hn0, mode='tn', out_dtype=F32, name="hy_dwin", tm=IN_PAD // 5, tn=D_MODEL, tk=TK_TOKENS)
    dhn0 = _matmul(dproj, win_p, mode='nn', out_dtype=F32, name="hy_dhn", tm=256, tn=D_MODEL, tk=IN_PAD)
    dx, _, dg_mix0 = _rms_bwd(dhn0, x, mixn[0], dh1, name="mix0_dnorm")

    G['hy_w_in_t'] = _ungroup_in_rows(dwin_p)
    G['hy_w_gate2'] = dwg2[:GLA_GATE_RANK]
    G['hy_b_gate'] = dbg
    G['hy_gla_norm'] = dgg
    G['hy_sb_q_norm'] = dgq.reshape(SB_D // SB_HEAD_DIM, SB_HEAD_DIM).sum(axis=0, keepdims=True)
    G['hy_sb_k_norm'] = dgk.reshape(SB_D // SB_HEAD_DIM, SB_HEAD_DIM).sum(axis=0, keepdims=True)
    G['mix_norm'] = jnp.concatenate([dg_mix0, dg_mix1], axis=0)
    G['ffn_norm'] = jnp.concatenate([dg_ffn0, dg_ffn1], axis=0)
    G['ffn_wgu_t0'] = dwgu0
    G['ffn_w_down0'] = dwd0
    return sq_err, dx, G, (early_own, early_recv)


MESH_IDS = pl.DeviceIdType.MESH
N_PEER = N_DEV - 1


def _exchange_sems(n):
    return [pltpu.SemaphoreType.DMA((n * N_PEER,)), pltpu.SemaphoreType.DMA((n * N_PEER,)),
            pltpu.SemaphoreType.DMA((n,))]


def _gather_phases(x_refs, out_refs, send_sems, recv_sems, local_sems):
    n = len(x_refs)
    x, y, c = lax.axis_index("x"), lax.axis_index("y"), lax.axis_index("c")
    me, sibling = (x, y, c), (x, y, 1 - c)
    chips = [(1 - x, y), (x, 1 - y), (1 - x, 1 - y)]

    def slot(a, px, py, pc):
        return out_refs[a].at[4 * px + 2 * py + pc]

    def copy(a, k, blk, to, src=None):
        return pltpu.make_async_remote_copy(
            src_ref=slot(a, *blk) if src is None else src, dst_ref=slot(a, *blk),
            send_sem=send_sems.at[a * N_PEER + k], recv_sem=recv_sems.at[a * N_PEER + k],
            device_id=to, device_id_type=MESH_IDS)

    def local(a):
        return pltpu.make_async_copy(x_refs[a], slot(a, *me), local_sems.at[a])

    def first(a):
        return [copy(a, 0, me, sibling, src=x_refs[a])] + [copy(a, 1 + j, me, (*chip, c), src=x_refs[a])
                                                           for j, chip in enumerate(chips)]

    def passed(a):
        return [copy(a, 4 + j, (*chip, c), sibling) for j, chip in enumerate(chips)]

    def start():
        for a in range(n):
            local(a).start()
        for a in range(n):
            for cp in first(a):
                cp.start()

    def forward():
        for a in range(n):
            for j, chip in enumerate(chips):
                copy(a, 1 + j, (*chip, c), me).wait_recv()
                copy(a, 4 + j, (*chip, c), sibling).start()

    def finish():
        for a in range(n):
            copy(a, 0, sibling, me).wait_recv()
            for j, chip in enumerate(chips):
                copy(a, 4 + j, (*chip, 1 - c), me).wait_recv()
        for a in range(n):
            for cp in first(a) + passed(a):
                cp.wait_send()
            local(a).wait()

    return start, forward, finish


def _all_gather(blocks):
    n = len(blocks)

    def body(*refs):
        start, forward, finish = _gather_phases(refs[:n], refs[n:2 * n], *refs[2 * n:])
        start()
        forward()
        finish()

    anyspec = pl.BlockSpec(memory_space=pl.ANY)
    return pl.pallas_call(
        body, name="fsdp_all_gather",
        out_shape=[jax.ShapeDtypeStruct((N_DEV,) + b.shape, b.dtype) for b in blocks],
        in_specs=[anyspec] * n, out_specs=[anyspec] * n,
        scratch_shapes=_exchange_sems(n),
    )(*blocks)


def _scatter_phases(s_refs, r_refs, send_sems, recv_sems, local_sems):
    n = len(s_refs)
    x, y, c = lax.axis_index("x"), lax.axis_index("y"), lax.axis_index("c")
    me = 4 * x + 2 * y + c

    def local(a):
        return pltpu.make_async_copy(s_refs[a].at[me], r_refs[a].at[me], local_sems.at[a])

    def copy(a, k):
        px, py, pc = x ^ ((k >> 2) & 1), y ^ ((k >> 1) & 1), c ^ (k & 1)
        return pltpu.make_async_remote_copy(
            src_ref=s_refs[a].at[4 * px + 2 * py + pc], dst_ref=r_refs[a].at[me],
            send_sem=send_sems.at[a * N_PEER + k - 1], recv_sem=recv_sems.at[a * N_PEER + k - 1],
            device_id=(px, py, pc), device_id_type=MESH_IDS)

    def start():
        for a in range(n):
            local(a).start()
        for a in range(n):
            for k in range(1, N_DEV):
                copy(a, k).start()

    def finish():
        for a in range(n):
            for k in range(1, N_DEV):
                copy(a, k).wait()
            local(a).wait()

    return start, finish


def _scatter_exchange(sends):
    n = len(sends)

    def body(*refs):
        start, finish = _scatter_phases(refs[:n], refs[n:2 * n], *refs[2 * n:])
        start()
        finish()

    anyspec = pl.BlockSpec(memory_space=pl.ANY)
    return pl.pallas_call(
        body, name="fsdp_scatter_exchange",
        out_shape=[jax.ShapeDtypeStruct(s.shape, s.dtype) for s in sends],
        in_specs=[anyspec] * n, out_specs=[anyspec] * n,
        scratch_shapes=_exchange_sems(n),
    )(*sends)


def _sum_contrib(recv, own, *, name, tr):
    _, R, C = recv.shape
    assert R % tr == 0

    def body(r_ref, own_ref, g_ref):
        me = 4 * lax.axis_index("x") + 2 * lax.axis_index("y") + lax.axis_index("c")
        g = jnp.zeros((tr, C), F32)
        for s in range(N_DEV):
            g = g + jnp.where(me == s, own_ref[...], r_ref[s].astype(F32))
        g_ref[...] = g

    row = pl.BlockSpec((tr, C), lambda i: (i, 0))
    return pl.pallas_call(
        body, name=name, grid=(R // tr,),
        in_specs=[pl.BlockSpec((N_DEV, tr, C), lambda i: (0, i, 0)), row], out_specs=row,
        out_shape=jax.ShapeDtypeStruct((R, C), F32),
        compiler_params=_cparams(("parallel",)),
    )(recv, own)


def _adamw(g, w, m, v, *, name, tr):
    R, C = g.shape
    assert R % tr == 0

    def body(g_ref, w_ref, m_ref, v_ref, d_ref, mo_ref, vo_ref):
        gv = g_ref[...]
        mn = ADAM_B1 * m_ref[...] + (1.0 - ADAM_B1) * gv
        vn = ADAM_B2 * v_ref[...] + (1.0 - ADAM_B2) * (gv * gv)
        m_hat = mn / (1.0 - ADAM_B1 ** ADAM_STEP)
        v_hat = vn / (1.0 - ADAM_B2 ** ADAM_STEP)
        d_ref[...] = -ADAM_LR * (m_hat / (jnp.sqrt(v_hat) + ADAM_EPS) + ADAM_WD * w_ref[...])
        mo_ref[...] = mn
        vo_ref[...] = vn

    row = pl.BlockSpec((tr, C), lambda i: (i, 0))
    return pl.pallas_call(
        body, name=name, grid=(R // tr,),
        in_specs=[row] * 4, out_specs=[row] * 3,
        out_shape=[jax.ShapeDtypeStruct((R, C), F32)] * 3,
        compiler_params=_cparams(("parallel",)),
    )(g, w, m, v)


SMALL_SHARDED = ('hy_w_gate2', 'cv_b_pw1', 'cv_w_dw', 'cv_b_dw', 'cv_ln_g', 'cv_ln_b', 'cv_b_pw2')
SMALL_REPLICATED = ('mix_norm', 'ffn_norm', 'hy_b_gate', 'hy_gla_norm', 'hy_sb_q_norm', 'hy_sb_k_norm')
LANES = 128


def _small_rows(n):
    return -(-n // (8 * LANES)) * 8


def _pack_small(parts, lead=()):
    out = []
    for p in parts:
        n = p.shape[-1]
        p = jnp.pad(p, [(0, 0)] * len(lead) + [(0, _small_rows(n) * LANES - n)])
        out.append(p.reshape(*lead, _small_rows(n), LANES))
    return jnp.concatenate(out, axis=len(lead))


def _unpack_small(packed, sizes, lead=()):
    out, r0 = [], 0
    for n in sizes:
        r = _small_rows(n)
        out.append(packed[..., r0:r0 + r, :].reshape(*lead, r * LANES)[..., :n])
        r0 += r
    return out


def _to_blocks(full, axis):
    shp = full.shape
    t = full.reshape(shp[:axis] + (N_DEV, shp[axis] // N_DEV) + shp[axis + 1:])
    return jnp.moveaxis(t, axis, 0)


def _from_blocks(blocks, axis):
    t = jnp.moveaxis(blocks, 0, axis)
    shp = t.shape
    return t.reshape(shp[:axis] + (shp[axis] * shp[axis + 1],) + shp[axis + 2:])


def kernel(x, mix_norm, ffn_norm, hy_w_in, hy_w_gate2, hy_b_gate, hy_gla_norm, hy_sb_q_norm, hy_sb_k_norm, hy_w_out, cv_w_pw1, cv_b_pw1, cv_w_dw, cv_b_dw, cv_ln_g, cv_ln_b, cv_w_pw2, cv_b_pw2, ffn_w_gate, ffn_w_up, ffn_w_down, loss_target, m_mix_norm, m_ffn_norm, m_hy_w_in, m_hy_w_gate2, m_hy_b_gate, m_hy_gla_norm, m_hy_sb_q_norm, m_hy_sb_k_norm, m_hy_w_out, m_cv_w_pw1, m_cv_b_pw1, m_cv_w_dw, m_cv_b_dw, m_cv_ln_g, m_cv_ln_b, m_cv_w_pw2, m_cv_b_pw2, m_ffn_w_gate, m_ffn_w_up, m_ffn_w_down, v_mix_norm, v_ffn_norm, v_hy_w_in, v_hy_w_gate2, v_hy_b_gate, v_hy_gla_norm, v_hy_sb_q_norm, v_hy_sb_k_norm, v_hy_w_out, v_cv_w_pw1, v_cv_b_pw1, v_cv_w_dw, v_cv_b_dw, v_cv_ln_g, v_cv_ln_b, v_cv_w_pw2, v_cv_b_pw2, v_ffn_w_gate, v_ffn_w_up, v_ffn_w_down):
    w_loc = dict(zip(WEIGHT_NAMES, (mix_norm, ffn_norm, hy_w_in, hy_w_gate2, hy_b_gate, hy_gla_norm, hy_sb_q_norm, hy_sb_k_norm, hy_w_out, cv_w_pw1, cv_b_pw1, cv_w_dw, cv_b_dw, cv_ln_g, cv_ln_b, cv_w_pw2, cv_b_pw2, ffn_w_gate, ffn_w_up, ffn_w_down)))
    m_loc = dict(zip(WEIGHT_NAMES, (m_mix_norm, m_ffn_norm, m_hy_w_in, m_hy_w_gate2, m_hy_b_gate, m_hy_gla_norm, m_hy_sb_q_norm, m_hy_sb_k_norm, m_hy_w_out, m_cv_w_pw1, m_cv_b_pw1, m_cv_w_dw, m_cv_b_dw, m_cv_ln_g, m_cv_ln_b, m_cv_w_pw2, m_cv_b_pw2, m_ffn_w_gate, m_ffn_w_up, m_ffn_w_down)))
    v_loc = dict(zip(WEIGHT_NAMES, (v_mix_norm, v_ffn_norm, v_hy_w_in, v_hy_w_gate2, v_hy_b_gate, v_hy_gla_norm, v_hy_sb_q_norm, v_hy_sb_k_norm, v_hy_w_out, v_cv_w_pw1, v_cv_b_pw1, v_cv_w_dw, v_cv_b_dw, v_cv_ln_g, v_cv_ln_b, v_cv_w_pw2, v_cv_b_pw2, v_ffn_w_gate, v_ffn_w_up, v_ffn_w_down)))

    Dm, F8 = D_MODEL, D_FF // N_DEV
    tr_ = lambda a: jnp.swapaxes(a, -1, -2)

    small_local = _pack_small([w_loc[n].reshape(-1) for n in SMALL_SHARDED])
    g_in, g_small = _all_gather([tr_(hy_w_in[0]).astype(BF16), small_local])
    late_blocks = [hy_w_out[0].astype(BF16),
                   tr_(cv_w_pw1[0]).astype(BF16),
                   cv_w_pw2[0].astype(BF16),
                   tr_(ffn_w_gate).astype(BF16),
                   tr_(ffn_w_up).astype(BF16),
                   ffn_w_down.astype(BF16)]
    small_sizes = [w_loc[n].size for n in SMALL_SHARDED]
    small_full = dict(zip(SMALL_SHARDED, _unpack_small(g_small, small_sizes, lead=(N_DEV,))))
    W = {n: w_loc[n] for n in SMALL_REPLICATED}
    W['hy_w_in_t'] = g_in.reshape(IN_WIDTH, Dm)
    W['hy_w_gate2'] = _from_blocks(small_full['hy_w_gate2'].reshape(N_DEV, GLA_GATE_RANK, GLA_DK // N_DEV), 1).astype(BF16)
    W['cv_w_dw'] = _from_blocks(small_full['cv_w_dw'].reshape(N_DEV, CONV_WIDTH, Dm // N_DEV), 1)
    for n in ('cv_b_pw1', 'cv_b_dw', 'cv_ln_g', 'cv_ln_b', 'cv_b_pw2'):
        W[n] = small_full[n].reshape(-1)

    sq_err, dx, G, (early_own, early_recv) = _local_step(x[0], loss_target[0], W, late_blocks)

    own_f32 = [
        G['hy_w_in_t'].reshape(N_DEV, IN_WIDTH // N_DEV, Dm),
        G['hy_w_out'].reshape(N_DEV, Dm // N_DEV, Dm),
        G['cv_w_pw1_t'].reshape(N_DEV, 2 * Dm // N_DEV, Dm),
        G['ffn_wgu_t0'].reshape(2, N_DEV, F8, Dm).transpose(1, 0, 2, 3).reshape(N_DEV, 2 * F8, Dm),
        G['ffn_w_down0'].reshape(N_DEV, F8, Dm),
    ]
    small_parts = []
    for n in SMALL_SHARDED:
        axis = SHARD_AXIS[n] - 1
        shard = w_loc[n].shape[1:]
        full = shard[:axis] + (shard[axis] * N_DEV,) + shard[axis + 1:]
        small_parts.append(_to_blocks(G[n].reshape(full), axis).reshape(N_DEV, -1))
    for n in SMALL_REPLICATED:
        small_parts.append(jnp.broadcast_to(G[n].reshape(1, -1), (N_DEV, G[n].size)))
    small_parts.append(jnp.broadcast_to(sq_err.reshape(1, 1), (N_DEV, 1)))
    send_small = _pack_small(small_parts, lead=(N_DEV,))
    recv = _scatter_exchange([a.astype(BF16) for a in own_f32] + [send_small])
    me = 4 * lax.axis_index("x") + 2 * lax.axis_index("y") + lax.axis_index("c")
    tags = ['hy_w_in', 'hy_w_out', 'cv_w_pw1', 'ffn_wgu0', 'ffn_w_down0', 'small', 'cv_w_pw2', 'ffn_wgu1', 'ffn_w_down1']
    own_all = own_f32 + [send_small] + list(early_own)
    recv_all = list(recv) + list(early_recv)
    gsum = dict((t, _sum_contrib(r, lax.dynamic_index_in_dim(o, me, 0, keepdims=False), name=f"sum_{t}", tr=r.shape[1]))
                for t, r, o in zip(tags, recv_all, own_all))

    grad = {}
    grad['hy_w_in'] = tr_(gsum['hy_w_in'])[None]
    grad['hy_w_out'] = gsum['hy_w_out'][None]
    grad['cv_w_pw1'] = tr_(gsum['cv_w_pw1'])[None]
    grad['cv_w_pw2'] = gsum['cv_w_pw2'][None]
    gu = jnp.stack([gsum['ffn_wgu0'], gsum['ffn_wgu1']]).reshape(2, 2, F8, Dm)
    grad['ffn_w_gate'] = tr_(gu[:, 0])
    grad['ffn_w_up'] = tr_(gu[:, 1])
    grad['ffn_w_down'] = jnp.stack([gsum['ffn_w_down0'], gsum['ffn_w_down1']])
    small_names = SMALL_SHARDED + SMALL_REPLICATED
    small_all = [w_loc[n].size for n in small_names]
    *small_grads, sq_sum = _unpack_small(gsum['small'], small_all + [1])
    for n, a in zip(small_names, small_grads):
        grad[n] = a.reshape(w_loc[n].shape)
    loss = 0.5 / Dm * sq_sum[0]

    delta, new_m, new_v = {}, {}, {}
    view = {'hy_w_in': (Dm, 256), 'hy_w_out': (Dm // N_DEV, Dm // N_DEV), 'cv_w_pw1': (Dm, 256),
            'cv_w_pw2': (Dm // N_DEV, Dm // N_DEV), 'ffn_w_gate': (2 * Dm, 256), 'ffn_w_up': (2 * Dm, 256),
            'ffn_w_down': (2 * F8, F8)}
    for n, (rows, tr) in view.items():
        shp = w_loc[n].shape
        outs = _adamw(grad[n].reshape(rows, -1), w_loc[n].reshape(rows, -1), m_loc[n].reshape(rows, -1),
                      v_loc[n].reshape(rows, -1), name=f"adamw_{n}", tr=tr)
        delta[n], new_m[n], new_v[n] = (o.reshape(shp) for o in outs)
    packed = [_pack_small([d[n].reshape(-1) for n in small_names] + [jnp.zeros((1,), F32)]) for d in (w_loc, m_loc, v_loc)]
    outs = _adamw(gsum['small'], *packed, name="adamw_small", tr=gsum['small'].shape[0])
    for dst, o in zip((delta, new_m, new_v), outs):
        for n, a in zip(small_names, _unpack_small(o, small_all)):
            dst[n] = a.reshape(w_loc[n].shape)

    return (loss, dx[None], *[grad[n] for n in WEIGHT_NAMES], *[delta[n] for n in WEIGHT_NAMES],
            *[new_m[n] for n in WEIGHT_NAMES], *[new_v[n] for n in WEIGHT_NAMES])
```

```python
import jax
import jax.numpy as jnp
from jax import lax
from jax.experimental import pallas as pl
from jax.experimental.pallas import tpu as pltpu

F32 = jnp.float32
BF16 = jnp.bfloat16
CDT = jnp.bfloat16

D_MODEL = 1024
EPS = 1e-6
CHUNK = 64
GLA_HEADS = 4
GLA_HEAD_K = 64
GLA_HEAD_V = 128
GLA_DK = GLA_HEADS * GLA_HEAD_K
GLA_DV = GLA_HEADS * GLA_HEAD_V
GLA_GATE_RANK = 16
GLA_GATE_NORMALIZER = 16.0
SB_HEAD_DIM = 64
SB_D = 512
SB_TILE = 256
SB_PAIR = 128
SB_SPLIT_LK = 2
SB_SPLIT_G = 1
SB_DEAD = 120.0
IN_WIDTH = 3088
IN_PAD = 3200
CONV_WIDTH = 31
CONV_HALO = 32
D_FF = 2816
N_DEV = 8

ADAM_LR = 0.001
ADAM_B1 = 0.9
ADAM_B2 = 0.999
ADAM_EPS = 1e-08
ADAM_WD = 0.01
ADAM_STEP = 10

VMEM_LIMIT = 56 * 1024 * 1024
TK_TOKENS = 2048

WEIGHT_NAMES = ['mix_norm', 'ffn_norm', 'hy_w_in', 'hy_w_gate2', 'hy_b_gate', 'hy_gla_norm', 'hy_sb_q_norm',
                'hy_sb_k_norm', 'hy_w_out', 'cv_w_pw1', 'cv_b_pw1', 'cv_w_dw', 'cv_b_dw', 'cv_ln_g', 'cv_ln_b',
                'cv_w_pw2', 'cv_b_pw2', 'ffn_w_gate', 'ffn_w_up', 'ffn_w_down']
SHARD_AXIS = {'mix_norm': None, 'ffn_norm': None, 'hy_w_in': 2, 'hy_w_gate2': 2, 'hy_b_gate': None,
              'hy_gla_norm': None, 'hy_sb_q_norm': None, 'hy_sb_k_norm': None, 'hy_w_out': 1, 'cv_w_pw1': 2,
              'cv_b_pw1': 1, 'cv_w_dw': 2, 'cv_b_dw': 1, 'cv_ln_g': 1, 'cv_ln_b': 1, 'cv_w_pw2': 1, 'cv_b_pw2': 1,
              'ffn_w_gate': 2, 'ffn_w_up': 2, 'ffn_w_down': 1}


def _cparams(sem=None, vmem=VMEM_LIMIT):
    return pltpu.CompilerParams(dimension_semantics=sem, vmem_limit_bytes=vmem)


def _log_sigmoid(x):
    return jnp.minimum(x, 0.0) - jnp.log1p(jnp.exp(-jnp.abs(x)))


def _sigmoid(x):
    return 1.0 / (1.0 + jnp.exp(-x))


def _softplus(x):
    return jnp.maximum(x, 0.0) + jnp.log(1.0 + jnp.exp(-jnp.abs(x)))


def _split_bf16(x, n):
    parts = []
    rem = x
    for _ in range(n):
        p = rem.astype(BF16)
        parts.append(p)
        rem = rem - p.astype(F32)
    return parts


def _dot_exact_rhs(x, m, n):
    return sum(jnp.dot(p, m, preferred_element_type=F32) for p in _split_bf16(x, n))


def _dot_exact_lhs(m, x, n):
    return sum(jnp.dot(m, p, preferred_element_type=F32) for p in _split_bf16(x, n))


_NN = (((1,), (0,)), ((), ()))
_NT = (((1,), (1,)), ((), ()))
_TN = (((0,), (0,)), ((), ()))


def _dg(a, b, dn):
    return lax.dot_general(a.astype(CDT), b.astype(CDT), dn, preferred_element_type=F32)


def _matmul(a, b, *, mode, out_dtype, name, tm, tn, tk, bias=None, residual=None):
    if mode == 'nn':
        (M, K), (K2, N) = a.shape, b.shape
    elif mode == 'nt':
        (M, K), (N, K2) = a.shape, b.shape
    else:
        (K, M), (K2, N) = a.shape, b.shape
    assert K == K2 and M % tm == 0 and N % tn == 0 and K % tk == 0, (name, a.shape, b.shape, tm, tn, tk)
    nk = K // tk
    a_spec = pl.BlockSpec((tk, tm), lambda i, j, k: (k, i)) if mode == 'tn' else pl.BlockSpec((tm, tk), lambda i, j, k: (i, k))
    b_spec = pl.BlockSpec((tn, tk), lambda i, j, k: (j, k)) if mode == 'nt' else pl.BlockSpec((tk, tn), lambda i, j, k: (k, j))
    dn = {'nn': _NN, 'nt': _NT, 'tn': _TN}[mode]
    has_bias, has_res = bias is not None, residual is not None

    def body(*refs):
        a_ref, b_ref = refs[0], refs[1]
        pos = 2
        bias_ref = res_ref = None
        if has_bias:
            bias_ref = refs[pos]
            pos += 1
        if has_res:
            res_ref = refs[pos]
            pos += 1
        o_ref = refs[pos]
        acc_ref = refs[pos + 1] if nk > 1 else None
        p = _dg(a_ref[...], b_ref[...], dn)

        def finish(acc):
            if has_bias:
                acc = acc + bias_ref[...]
            if has_res:
                acc = res_ref[...] + acc
            o_ref[...] = acc.astype(o_ref.dtype)

        if nk == 1:
            finish(p)
        else:
            k = pl.program_id(2)

            @pl.when(k == 0)
            def _():
                acc_ref[...] = p

            @pl.when(k > 0)
            def _():
                acc_ref[...] += p

            @pl.when(k == nk - 1)
            def _():
                finish(acc_ref[...])

    in_specs = [a_spec, b_spec]
    args = [a, b]
    if has_bias:
        in_specs.append(pl.BlockSpec((1, tn), lambda i, j, k: (0, j)))
        args.append(bias)
    if has_res:
        in_specs.append(pl.BlockSpec((tm, tn), lambda i, j, k: (i, j)))
        args.append(residual)
    return pl.pallas_call(
        body, name=name, grid=(M // tm, N // tn, nk),
        in_specs=in_specs, out_specs=pl.BlockSpec((tm, tn), lambda i, j, k: (i, j)),
        out_shape=jax.ShapeDtypeStruct((M, N), out_dtype),
        scratch_shapes=[pltpu.VMEM((tm, tn), F32)] if nk > 1 else [],
        compiler_params=_cparams(("parallel", "parallel", "arbitrary")),
    )(*args)


def _rms_fwd(x, g, *, name, tm=512):
    T, Dm = x.shape

    def body(x_ref, g_ref, o_ref):
        xv = x_ref[...]
        r = lax.rsqrt(jnp.mean(xv * xv, axis=-1, keepdims=True) + EPS)
        o_ref[...] = (xv * r * g_ref[...]).astype(o_ref.dtype)

    return pl.pallas_call(
        body, name=name, grid=(T // tm,),
        in_specs=[pl.BlockSpec((tm, Dm), lambda i: (i, 0)), pl.BlockSpec((1, Dm), lambda i: (0, 0))],
        out_specs=pl.BlockSpec((tm, Dm), lambda i: (i, 0)),
        out_shape=jax.ShapeDtypeStruct((T, Dm), CDT),
        compiler_params=_cparams(("parallel",)),
    )(x, g)


def _rms_bwd(dy, x, g, resid, *, name, tm=512):
    T, Dm = x.shape

    def body(dy_ref, x_ref, g_ref, res_ref, dx_ref, dxb_ref, dg_ref):
        i = pl.program_id(0)
        xv, dyv = x_ref[...], dy_ref[...]
        r = lax.rsqrt(jnp.mean(xv * xv, axis=-1, keepdims=True) + EPS)
        u = dyv * g_ref[...]
        dot = jnp.mean(u * xv, axis=-1, keepdims=True)
        dx = res_ref[...] + (r * u - xv * (r * r * r * dot))
        dx_ref[...] = dx
        dxb_ref[...] = dx.astype(dxb_ref.dtype)
        part = jnp.sum(dyv * xv * r, axis=0, keepdims=True)

        @pl.when(i == 0)
        def _():
            dg_ref[...] = part

        @pl.when(i > 0)
        def _():
            dg_ref[...] += part

    row = pl.BlockSpec((tm, Dm), lambda i: (i, 0))
    vec = pl.BlockSpec((1, Dm), lambda i: (0, 0))
    return pl.pallas_call(
        body, name=name, grid=(T // tm,),
        in_specs=[row, row, vec, row], out_specs=[row, row, vec],
        out_shape=[jax.ShapeDtypeStruct((T, Dm), F32), jax.ShapeDtypeStruct((T, Dm), CDT),
                   jax.ShapeDtypeStruct((1, Dm), F32)],
        compiler_params=_cparams(("arbitrary",)),
    )(dy, x, g, resid)


def _loss_head(y, tgt, *, tm=512):
    T, Dm = y.shape

    def body(y_ref, t_ref, s_ref, dy_ref, dyb_ref):
        i = pl.program_id(0)
        e = y_ref[...] - t_ref[...]
        dy = e * (1.0 / Dm)
        dy_ref[...] = dy
        dyb_ref[...] = dy.astype(dyb_ref.dtype)
        part = jnp.sum(jnp.sum(e * e, axis=1, keepdims=True), axis=0, keepdims=True)

        @pl.when(i == 0)
        def _():
            s_ref[...] = part

        @pl.when(i > 0)
        def _():
            s_ref[...] += part

    row = pl.BlockSpec((tm, Dm), lambda i: (i, 0))
    return pl.pallas_call(
        body, name="loss_head", grid=(T // tm,),
        in_specs=[row, row], out_specs=[pl.BlockSpec((1, 1), lambda i: (0, 0)), row, row],
        out_shape=[jax.ShapeDtypeStruct((1, 1), F32), jax.ShapeDtypeStruct((T, Dm), F32),
                   jax.ShapeDtypeStruct((T, Dm), CDT)],
        compiler_params=_cparams(("arbitrary",)),
    )(y, tgt)


def _swiglu_up(hf, wg_t, wu_t, *, name, tm=256):
    T, Dm = hf.shape
    F = wg_t.shape[0]

    def body(h_ref, wg_ref, wu_ref, ab_ref, s_ref):
        h = h_ref[...]
        a = _dg(h, wg_ref[...], _NT)
        b = _dg(h, wu_ref[...], _NT)
        ab_ref[:, :F] = a.astype(ab_ref.dtype)
        ab_ref[:, F:] = b.astype(ab_ref.dtype)
        s_ref[...] = (a * _sigmoid(a) * b).astype(s_ref.dtype)

    wspec = pl.BlockSpec((F, Dm), lambda i: (0, 0))
    return pl.pallas_call(
        body, name=name, grid=(T // tm,),
        in_specs=[pl.BlockSpec((tm, Dm), lambda i: (i, 0)), wspec, wspec],
        out_specs=[pl.BlockSpec((tm, 2 * F), lambda i: (i, 0)), pl.BlockSpec((tm, F), lambda i: (i, 0))],
        out_shape=[jax.ShapeDtypeStruct((T, 2 * F), CDT), jax.ShapeDtypeStruct((T, F), CDT)],
        compiler_params=_cparams(("parallel",)),
    )(hf, wg_t, wu_t)


def _swiglu_dhf(dab, wg_t, wu_t, *, name, tm=256):
    T = dab.shape[0]
    F, Dm = wg_t.shape

    def body(d_ref, wg_ref, wu_ref, o_ref):
        o_ref[...] = _dg(d_ref[:, :F], wg_ref[...], _NN) + _dg(d_ref[:, F:], wu_ref[...], _NN)

    wspec = pl.BlockSpec((F, Dm), lambda i: (0, 0))
    return pl.pallas_call(
        body, name=name, grid=(T // tm,),
        in_specs=[pl.BlockSpec((tm, 2 * F), lambda i: (i, 0)), wspec, wspec],
        out_specs=pl.BlockSpec((tm, Dm), lambda i: (i, 0)),
        out_shape=jax.ShapeDtypeStruct((T, Dm), F32),
        compiler_params=_cparams(("parallel",)),
    )(dab, wg_t, wu_t)


def _swiglu_dact(dh_c, wd, ab, *, name, tm=256):
    T, Dm = dh_c.shape
    F2 = ab.shape[1]
    F = F2 // 2

    def body(dh_ref, w_ref, ab_ref, o_ref):
        dsv = _dg(dh_ref[...], w_ref[...], _NT)
        a = ab_ref[:, :F].astype(F32)
        b = ab_ref[:, F:].astype(F32)
        sg = _sigmoid(a)
        o_ref[:, :F] = (dsv * b * (sg * (1.0 + a * (1.0 - sg)))).astype(o_ref.dtype)
        o_ref[:, F:] = (dsv * (a * sg)).astype(o_ref.dtype)

    return pl.pallas_call(
        body, name=name, grid=(T // tm,),
        in_specs=[pl.BlockSpec((tm, Dm), lambda i: (i, 0)), pl.BlockSpec((F, Dm), lambda i: (0, 0)),
                  pl.BlockSpec((tm, F2), lambda i: (i, 0))],
        out_specs=pl.BlockSpec((tm, F2), lambda i: (i, 0)),
        out_shape=jax.ShapeDtypeStruct((T, F2), CDT),
        compiler_params=_cparams(("parallel",)),
    )(dh_c, wd, ab)


SUBLANES = 8


def _shifted_copies(buf, shifted, tm):
    n = tm + CONV_HALO - SUBLANES
    for b in range(1, SUBLANES):
        shifted[b - 1] = buf[pl.ds(b, n), :]


CONV_ROWS = 64
CONV_LANES = 128


def _rows_from(buf, shifted, offset, rows, r0, cols):
    a, b = divmod(offset, SUBLANES)
    if b == 0:
        return buf[pl.ds(r0 + SUBLANES * a, rows), cols]
    return shifted[b - 1, pl.ds(r0 + SUBLANES * a, rows), cols]


def _conv_fwd(a, w_dw, b_dw, ln_g, ln_b, *, tm=256):
    T = a.shape[0]
    Dm = D_MODEL

    def body(a_ref, w_ref, bdw_ref, g_ref, b_ref, s_ref, u_ref, c_ref, ubuf, shifted):
        i = pl.program_id(0)

        @pl.when(i == 0)
        def _():
            ubuf[0:CONV_HALO, :] = jnp.zeros((CONV_HALO, Dm), F32)

        @pl.when(i > 0)
        def _():
            ubuf[0:CONV_HALO, :] = ubuf[tm:tm + CONV_HALO, :]

        u = a_ref[:, :Dm] * _sigmoid(a_ref[:, Dm:])
        ubuf[CONV_HALO:CONV_HALO + tm, :] = u
        u_ref[...] = u
        _shifted_copies(ubuf, shifted, tm)
        acc = jnp.zeros((tm, Dm), F32) + bdw_ref[...]
        for k in range(CONV_WIDTH):
            acc = acc + w_ref[k:k + 1, :] * _rows_from(ubuf, shifted, CONV_HALO - (CONV_WIDTH - 1) + k, tm, 0, slice(None))
        c_ref[...] = acc
        mu = jnp.mean(acc, axis=-1, keepdims=True)
        cen = acc - mu
        var = jnp.mean(cen * cen, axis=-1, keepdims=True)
        l = cen * lax.rsqrt(var + EPS) * g_ref[...] + b_ref[...]
        s_ref[...] = (l * _sigmoid(l)).astype(s_ref.dtype)

    row = pl.BlockSpec((tm, Dm), lambda i: (i, 0))
    vec = pl.BlockSpec((1, Dm), lambda i: (0, 0))
    return pl.pallas_call(
        body, name="conv_fwd", grid=(T // tm,),
        in_specs=[pl.BlockSpec((tm, 2 * Dm), lambda i: (i, 0)), pl.BlockSpec((CONV_HALO, Dm), lambda i: (0, 0)), vec, vec, vec],
        out_specs=[row, row, row],
        out_shape=[jax.ShapeDtypeStruct((T, Dm), CDT), jax.ShapeDtypeStruct((T, Dm), F32), jax.ShapeDtypeStruct((T, Dm), F32)],
        scratch_shapes=[pltpu.VMEM((tm + CONV_HALO, Dm), F32), pltpu.VMEM((SUBLANES - 1, tm + CONV_HALO - SUBLANES, Dm), F32)],
        compiler_params=_cparams(("arbitrary",)),
    )(a, w_dw, b_dw, ln_g, ln_b)


def _conv_bwd(ds, c, u, a, w_dw, ln_g, ln_b, sends, *, tm=256):
    T = a.shape[0]
    Dm = D_MODEL
    nt = T // tm
    ns = len(sends)

    def body(*refs):
        ds_ref, c_ref, u_ref, a_ref, w_ref, g_ref, b_ref = refs[:7]
        da_ref, db1_ref, dw_ref, dbdw_ref, dg_ref, dbln_ref = refs[7 + ns:13 + ns]
        dcbuf, shifted, du_scr = refs[13 + 2 * ns:16 + 2 * ns]
        x_start, x_finish = _scatter_phases(refs[7:7 + ns], refs[13 + ns:13 + 2 * ns], *refs[16 + 2 * ns:])
        i = pl.program_id(0)
        pl.when(i == 0)(x_start)

        @pl.when(i == 0)
        def _():
            dcbuf[tm:tm + CONV_HALO, :] = jnp.zeros((CONV_HALO, Dm), F32)
            db1_ref[...] = jnp.zeros_like(db1_ref)
            dw_ref[...] = jnp.zeros_like(dw_ref)
            dbdw_ref[...] = jnp.zeros_like(dbdw_ref)
            dg_ref[...] = jnp.zeros_like(dg_ref)
            dbln_ref[...] = jnp.zeros_like(dbln_ref)

        @pl.when(i > 0)
        def _():
            dcbuf[tm:tm + CONV_HALO, :] = dcbuf[0:CONV_HALO, :]

        cv = c_ref[...]
        mu = jnp.mean(cv, axis=-1, keepdims=True)
        cen = cv - mu
        var = jnp.mean(cen * cen, axis=-1, keepdims=True)
        rstd = lax.rsqrt(var + EPS)
        n = cen * rstd
        l = n * g_ref[...] + b_ref[...]
        sg = _sigmoid(l)
        dl = ds_ref[...] * (sg * (1.0 + l * (1.0 - sg)))
        dg_ref[...] += jnp.sum(dl * n, axis=0, keepdims=True)
        dbln_ref[...] += jnp.sum(dl, axis=0, keepdims=True)
        dn = dl * g_ref[...]
        dc = rstd * (dn - jnp.mean(dn, axis=-1, keepdims=True) - n * jnp.mean(dn * n, axis=-1, keepdims=True))
        dbdw_ref[...] += jnp.sum(dc, axis=0, keepdims=True)
        dcbuf[0:tm, :] = dc
        _shifted_copies(dcbuf, shifted, tm)
        groups = CONV_ROWS // SUBLANES
        for cb in range(Dm // CONV_LANES):
            cols = slice(cb * CONV_LANES, (cb + 1) * CONV_LANES)

            def block(r, dw_part, cols=cols):
                r0 = pl.multiple_of(r * CONV_ROWS, CONV_ROWS)
                uv = u_ref[pl.ds(r0, CONV_ROWS), cols]
                du_b = jnp.zeros((CONV_ROWS, CONV_LANES), F32)
                out = []
                for k in range(CONV_WIDTH):
                    slab = _rows_from(dcbuf, shifted, CONV_WIDTH - 1 - k, CONV_ROWS, r0, cols)
                    du_b = du_b + w_ref[k:k + 1, cols] * slab
                    out.append(dw_part[k] + jnp.sum((slab * uv).reshape(groups, SUBLANES, CONV_LANES), axis=0))
                du_scr[pl.ds(r0, CONV_ROWS), cols] = du_b
                return tuple(out)

            zero = jnp.zeros((SUBLANES, CONV_LANES), F32)
            dw_part = lax.fori_loop(0, tm // CONV_ROWS, block, (zero,) * CONV_WIDTH)
            for k in range(CONV_WIDTH):
                dw_ref[k:k + 1, cols] += jnp.sum(dw_part[k], axis=0, keepdims=True)
        du = du_scr[...]
        a1 = a_ref[:, :Dm]
        s2 = _sigmoid(a_ref[:, Dm:])
        da1 = du * s2
        da2 = du * a1 * (s2 * (1.0 - s2))
        da_ref[:, :Dm] = da1.astype(da_ref.dtype)
        da_ref[:, Dm:] = da2.astype(da_ref.dtype)
        db1_ref[:, :Dm] += jnp.sum(da1, axis=0, keepdims=True)
        db1_ref[:, Dm:] += jnp.sum(da2, axis=0, keepdims=True)
        pl.when(i == nt - 1)(x_finish)

    rev = lambda i: (nt - 1 - i, 0)
    row = pl.BlockSpec((tm, Dm), rev)
    row2 = pl.BlockSpec((tm, 2 * Dm), rev)
    vec = pl.BlockSpec((1, Dm), lambda i: (0, 0))
    vec2 = pl.BlockSpec((1, 2 * Dm), lambda i: (0, 0))
    taps = pl.BlockSpec((CONV_HALO, Dm), lambda i: (0, 0))
    anyspec = pl.BlockSpec(memory_space=pl.ANY)
    outs = pl.pallas_call(
        body, name="conv_bwd", grid=(nt,),
        in_specs=[row, row, row, row2, taps, vec, vec] + [anyspec] * ns,
        out_specs=[row2, vec2, taps, vec, vec, vec] + [anyspec] * ns,
        out_shape=[jax.ShapeDtypeStruct((T, 2 * Dm), CDT), jax.ShapeDtypeStruct((1, 2 * Dm), F32),
                   jax.ShapeDtypeStruct((CONV_HALO, Dm), F32), jax.ShapeDtypeStruct((1, Dm), F32),
                   jax.ShapeDtypeStruct((1, Dm), F32), jax.ShapeDtypeStruct((1, Dm), F32)]
        + [jax.ShapeDtypeStruct(s.shape, s.dtype) for s in sends],
        scratch_shapes=[pltpu.VMEM((tm + CONV_HALO, Dm), F32), pltpu.VMEM((SUBLANES - 1, tm + CONV_HALO - SUBLANES, Dm), F32),
                        pltpu.VMEM((tm, Dm), F32)]
        + _exchange_sems(ns),
        compiler_params=_cparams(("arbitrary",)),
    )(ds, c, u, a, w_dw, ln_g, ln_b, *sends)
    return outs[:6], outs[6:]


def _gla_head_masks(width, per_head):
    lane = lax.broadcasted_iota(jnp.int32, (1, width), 1)
    return [((lane >= h * per_head) & (lane < (h + 1) * per_head)).astype(F32) for h in range(GLA_HEADS)]


def _gla_specs(tm, order):
    return [pl.BlockSpec((tm, GLA_DK), lambda i: (order(i), 0)),
            pl.BlockSpec((tm, GLA_DK), lambda i: (order(i), 1)),
            pl.BlockSpec((tm, GLA_DV), lambda i: (order(i), 1)),
            pl.BlockSpec((tm, GLA_DV), lambda i: (order(i), 2)),
            pl.BlockSpec((tm, 128), lambda i: (order(i), 3072 // 128))]


def _gla_chunk_decay(la_c, tri):
    bc = _dot_exact_lhs(tri, la_c, 3)
    b_end = bc[CHUNK - 1:CHUNK, :]
    return b_end, jnp.exp(b_end - bc)


def _gla_fwd(proj, wg2p, b_gate, g_gla, *, tm=256):
    T = proj.shape[0]
    ncs = tm // CHUNK
    scale = GLA_HEAD_K ** -0.5

    def body(q_ref, k_ref, v_ref, r_ref, glr_ref, wg_ref, bg_ref, gg_ref, o_ref, oraw_ref, st_ref, s_scr):
        i = pl.program_id(0)

        @pl.when(i == 0)
        def _():
            s_scr[...] = jnp.zeros_like(s_scr)

        mk = _gla_head_masks(GLA_DK, GLA_HEAD_K)
        rr = lax.broadcasted_iota(jnp.int32, (CHUNK, CHUNK), 0)
        cc = lax.broadcasted_iota(jnp.int32, (CHUNK, CHUNK), 1)
        tri = (cc <= rr).astype(BF16)
        y = _dg(glr_ref[...], wg_ref[...], _NN) + bg_ref[...]
        la = _log_sigmoid(y) / GLA_GATE_NORMALIZER
        qs = q_ref[...] * scale
        for ci in range(ncs):
            rows = slice(ci * CHUNK, (ci + 1) * CHUNK)
            b_end, dec = _gla_chunk_decay(la[rows], tri)
            kend = (k_ref[rows, :] * dec).astype(CDT)
            upd = jnp.zeros((GLA_HEAD_V, GLA_DK), F32)
            for h in range(GLA_HEADS):
                vh = v_ref[rows, h * GLA_HEAD_V:(h + 1) * GLA_HEAD_V]
                upd = upd + mk[h] * _dg(vh, kend, _TN)
            s_new = jnp.exp(b_end) * s_scr[...] + upd
            s_scr[...] = s_new
            st_ref[ci] = s_new
            s_c = s_new.astype(CDT)
            for h in range(GLA_HEADS):
                o_h = _dg(qs[rows] * mk[h], s_c, _NT)
                oraw_ref[rows, h * GLA_HEAD_V:(h + 1) * GLA_HEAD_V] = o_h
        for h in range(GLA_HEADS):
            cols = slice(h * GLA_HEAD_V, (h + 1) * GLA_HEAD_V)
            o_h = oraw_ref[:, cols]
            rs = lax.rsqrt(jnp.mean(o_h * o_h, axis=-1, keepdims=True) + EPS)
            rg = r_ref[:, cols]
            o_ref[:, cols] = (o_h * rs * gg_ref[...] * (rg * _sigmoid(rg))).astype(o_ref.dtype)

    full = lambda shape: pl.BlockSpec(shape, lambda i: tuple(0 for _ in shape))
    return pl.pallas_call(
        body, name="gla_fwd", grid=(T // tm,),
        in_specs=_gla_specs(tm, lambda i: i) + [full((128, GLA_DK)), full((1, GLA_DK)), full((1, GLA_HEAD_V))],
        out_specs=[pl.BlockSpec((tm, GLA_DV), lambda i: (i, 0)), pl.BlockSpec((tm, GLA_DV), lambda i: (i, 0)),
                   pl.BlockSpec((ncs, GLA_HEAD_V, GLA_DK), lambda i: (i, 0, 0))],
        out_shape=[jax.ShapeDtypeStruct((T, GLA_DV), CDT), jax.ShapeDtypeStruct((T, GLA_DV), F32),
                   jax.ShapeDtypeStruct((T // CHUNK, GLA_HEAD_V, GLA_DK), F32)],
        scratch_shapes=[pltpu.VMEM((GLA_HEAD_V, GLA_DK), F32)],
        compiler_params=_cparams(("arbitrary",)),
    )(proj, proj, proj, proj, proj, wg2p, b_gate, g_gla)


def _gla_bwd(d_o, proj, oraw, states, wg2p, b_gate, g_gla, *, tm=256):
    T = proj.shape[0]
    nt = T // tm
    ncs = tm // CHUNK
    scale = GLA_HEAD_K ** -0.5

    def body(do_ref, q_ref, k_ref, v_ref, r_ref, glr_ref, oraw_ref, st_ref, stp_ref, wg_ref, bg_ref, gg_ref,
             dgla_ref, dglr_ref, dwg_ref, dbg_ref, dgg_ref, ds_scr, dy_scr, dor_scr):
        i = pl.program_id(0)
        tile = nt - 1 - i

        @pl.when(i == 0)
        def _():
            ds_scr[...] = jnp.zeros_like(ds_scr)
            dwg_ref[...] = jnp.zeros_like(dwg_ref)
            dbg_ref[...] = jnp.zeros_like(dbg_ref)
            dgg_ref[...] = jnp.zeros_like(dgg_ref)

        mk = _gla_head_masks(GLA_DK, GLA_HEAD_K)
        rr = lax.broadcasted_iota(jnp.int32, (CHUNK, CHUNK), 0)
        cc = lax.broadcasted_iota(jnp.int32, (CHUNK, CHUNK), 1)
        tri = (cc <= rr).astype(BF16)
        tri_t = (cc >= rr).astype(BF16)
        last_row = (lax.broadcasted_iota(jnp.int32, (CHUNK, 1), 0) == CHUNK - 1).astype(F32)

        dgg = jnp.zeros((1, GLA_HEAD_V), F32)
        for h in range(GLA_HEADS):
            cols = slice(h * GLA_HEAD_V, (h + 1) * GLA_HEAD_V)
            o_h = oraw_ref[:, cols]
            rs = lax.rsqrt(jnp.mean(o_h * o_h, axis=-1, keepdims=True) + EPS)
            rg = r_ref[:, cols]
            sg = _sigmoid(rg)
            dov = do_ref[:, cols]
            on = o_h * rs * gg_ref[...]
            d_on = dov * (rg * sg)
            dgla_ref[:, 2 * GLA_DK + GLA_DV + h * GLA_HEAD_V:2 * GLA_DK + GLA_DV + (h + 1) * GLA_HEAD_V] = (
                dov * on * (sg * (1.0 + rg * (1.0 - sg)))).astype(dgla_ref.dtype)
            dgg = dgg + jnp.sum(d_on * o_h * rs, axis=0, keepdims=True)
            uu = d_on * gg_ref[...]
            dor_scr[:, cols] = rs * uu - o_h * (rs * rs * rs * jnp.mean(uu * o_h, axis=-1, keepdims=True))
        dgg_ref[...] += dgg

        y = _dg(glr_ref[...], wg_ref[...], _NN) + bg_ref[...]
        la = _log_sigmoid(y) / GLA_GATE_NORMALIZER
        qs = q_ref[...] * scale
        for ci in reversed(range(ncs)):
            rows = slice(ci * CHUNK, (ci + 1) * CHUNK)
            b_end, dec = _gla_chunk_decay(la[rows], tri)
            decay = jnp.exp(b_end)
            kend = k_ref[rows, :] * dec
            kend_c = kend.astype(CDT)
            s_c = st_ref[ci].astype(CDT)
            if ci > 0:
                s_prev = st_ref[ci - 1]
            else:
                s_prev = jnp.where(tile > 0, stp_ref[0], 0.0)
            dqs = jnp.zeros((CHUNK, GLA_DK), F32)
            dst = ds_scr[...]
            for h in range(GLA_HEADS):
                do_h = dor_scr[rows, h * GLA_HEAD_V:(h + 1) * GLA_HEAD_V].astype(CDT)
                dqs = dqs + mk[h] * _dg(do_h, s_c, _NN)
                dst = dst + mk[h] * _dg(do_h, qs[rows], _TN)
            d_decay = jnp.sum(dst * s_prev, axis=0, keepdims=True)
            ds_scr[...] = decay * dst
            dst_c = dst.astype(CDT)
            dkend = jnp.zeros((CHUNK, GLA_DK), F32)
            for h in range(GLA_HEADS):
                cols = slice(h * GLA_HEAD_V, (h + 1) * GLA_HEAD_V)
                dv_h = _dg(kend * mk[h], dst_c, _NT)
                dgla_ref[rows, 2 * GLA_DK + h * GLA_HEAD_V:2 * GLA_DK + (h + 1) * GLA_HEAD_V] = dv_h.astype(dgla_ref.dtype)
                dkend = dkend + mk[h] * _dg(v_ref[rows, cols], dst_c, _NN)
            dgla_ref[rows, 0:GLA_DK] = (dqs * scale).astype(dgla_ref.dtype)
            dgla_ref[rows, GLA_DK:2 * GLA_DK] = (dkend * dec).astype(dgla_ref.dtype)
            mm = dkend * kend
            db_end = jnp.sum(mm, axis=0, keepdims=True) + d_decay * decay
            dbc = last_row * db_end - mm
            dla = _dot_exact_lhs(tri_t, dbc, 3)
            dy_scr[rows, :] = dla * (1.0 / GLA_GATE_NORMALIZER) * _sigmoid(-y[rows])
        dy = dy_scr[...]
        dbg_ref[...] += jnp.sum(dy, axis=0, keepdims=True)
        dwg_ref[...] += _dg(glr_ref[...], dy, _TN)
        dglr_ref[...] = _dg(dy, wg_ref[...], _NT).astype(dglr_ref.dtype)

    rev = lambda i: nt - 1 - i
    full = lambda shape: pl.BlockSpec(shape, lambda i: tuple(0 for _ in shape))
    st_spec = pl.BlockSpec((ncs, GLA_HEAD_V, GLA_DK), lambda i: (rev(i), 0, 0))
    stp_spec = pl.BlockSpec((1, GLA_HEAD_V, GLA_DK), lambda i: (jnp.maximum(rev(i) * ncs - 1, 0), 0, 0))
    return pl.pallas_call(
        body, name="gla_bwd", grid=(nt,),
        in_specs=[pl.BlockSpec((tm, GLA_DV), lambda i: (rev(i), 0))] + _gla_specs(tm, rev)
        + [pl.BlockSpec((tm, GLA_DV), lambda i: (rev(i), 0)), st_spec, stp_spec,
           full((128, GLA_DK)), full((1, GLA_DK)), full((1, GLA_HEAD_V))],
        out_specs=[pl.BlockSpec((tm, 2 * GLA_DK + 2 * GLA_DV), lambda i: (rev(i), 0)),
                   pl.BlockSpec((tm, 128), lambda i: (rev(i), 0)),
                   full((128, GLA_DK)), full((1, GLA_DK)), full((1, GLA_HEAD_V))],
        out_shape=[jax.ShapeDtypeStruct((T, 2 * GLA_DK + 2 * GLA_DV), CDT), jax.ShapeDtypeStruct((T, 128), CDT),
                   jax.ShapeDtypeStruct((128, GLA_DK), F32), jax.ShapeDtypeStruct((1, GLA_DK), F32),
                   jax.ShapeDtypeStruct((1, GLA_HEAD_V), F32)],
        scratch_shapes=[pltpu.VMEM((GLA_HEAD_V, GLA_DK), F32), pltpu.VMEM((tm, GLA_DK), F32),
                        pltpu.VMEM((tm, GLA_DV), F32)],
        compiler_params=_cparams(("arbitrary",)),
    )(d_o, proj, proj, proj, proj, proj, oraw, states, states, wg2p, b_gate, g_gla)


def _head_mean_matrix():
    r = lax.broadcasted_iota(jnp.int32, (SB_D, SB_D), 0) // SB_HEAD_DIM
    c = lax.broadcasted_iota(jnp.int32, (SB_D, SB_D), 1) // SB_HEAD_DIM
    return jnp.where(r == c, 1.0 / SB_HEAD_DIM, 0.0).astype(BF16)


def _sb_prep(proj, gq, gk, *, tm=256):
    T = proj.shape[0]
    scale = SB_HEAD_DIM ** -0.5

    def body(q_ref, k_ref, v_ref, gq_ref, gk_ref, qn_ref, kn_ref, vb_ref):
        hm = _head_mean_matrix()
        qv, kv = q_ref[...], k_ref[...]
        rq = lax.rsqrt(_dot_exact_rhs(qv * qv, hm, 3) + EPS)
        rk = lax.rsqrt(_dot_exact_rhs(kv * kv, hm, 3) + EPS)
        qn_ref[...] = (qv * rq * gq_ref[...] * scale).astype(qn_ref.dtype)
        kn_ref[...] = (kv * rk * gk_ref[...]).astype(kn_ref.dtype)
        vb_ref[...] = v_ref[...].astype(vb_ref.dtype)

    col = lambda j: pl.BlockSpec((tm, SB_D), lambda i: (i, j))
    vec = pl.BlockSpec((1, SB_D), lambda i: (0, 0))
    out = pl.BlockSpec((tm, SB_D), lambda i: (i, 0))
    return pl.pallas_call(
        body, name="sb_prep", grid=(T // tm,),
        in_specs=[col(3), col(4), col(5), vec, vec], out_specs=[out, out, out],
        out_shape=[jax.ShapeDtypeStruct((T, SB_D), CDT)] * 3,
        compiler_params=_cparams(("parallel",)),
    )(proj, proj, proj, gq, gk)


def _sb_prep_bwd(dqn, dkn, dv, proj, gq, gk, *, tm=256):
    T = proj.shape[0]
    scale = SB_HEAD_DIM ** -0.5

    def body(dqn_ref, dkn_ref, dv_ref, q_ref, k_ref, gq_ref, gk_ref, dsb_ref, dgq_ref, dgk_ref):
        i = pl.program_id(0)

        @pl.when(i == 0)
        def _():
            dgq_ref[...] = jnp.zeros_like(dgq_ref)
            dgk_ref[...] = jnp.zeros_like(dgk_ref)

        hm = _head_mean_matrix()

        def one(dn_ref, x_ref, g_ref, dg_ref, sc, lo):
            xv = x_ref[...]
            dnv = dn_ref[...] * sc
            r = lax.rsqrt(_dot_exact_rhs(xv * xv, hm, 3) + EPS)
            u = dnv * g_ref[...]
            dot = _dot_exact_rhs(u * xv, hm, 3)
            dsb_ref[:, lo:lo + SB_D] = (r * u - xv * (r * r * r * dot)).astype(dsb_ref.dtype)
            dg_ref[...] += jnp.sum(dnv * xv * r, axis=0, keepdims=True)

        one(dqn_ref, q_ref, gq_ref, dgq_ref, scale, 0)
        one(dkn_ref, k_ref, gk_ref, dgk_ref, 1.0, SB_D)
        dsb_ref[:, 2 * SB_D:3 * SB_D] = dv_ref[...].astype(dsb_ref.dtype)

    col = lambda j: pl.BlockSpec((tm, SB_D), lambda i: (i, j))
    vec = pl.BlockSpec((1, SB_D), lambda i: (0, 0))
    row = pl.BlockSpec((tm, SB_D), lambda i: (i, 0))
    return pl.pallas_call(
        body, name="sb_prep_bwd", grid=(T // tm,),
        in_specs=[row, row, row, col(3), col(4), vec, vec],
        out_specs=[pl.BlockSpec((tm, 3 * SB_D), lambda i: (i, 0)), vec, vec],
        out_shape=[jax.ShapeDtypeStruct((T, 3 * SB_D), CDT), jax.ShapeDtypeStruct((1, SB_D), F32),
                   jax.ShapeDtypeStruct((1, SB_D), F32)],
        compiler_params=_cparams(("arbitrary",)),
    )(dqn, dkn, dv, proj, proj, gq, gk)


def _sb_masks():
    lane = lax.broadcasted_iota(jnp.int32, (1, 128), 1)
    m = [lane < SB_HEAD_DIM, lane >= SB_HEAD_DIM]
    return m, [x.astype(F32) for x in m]


def _sb_fwd(qn, kn, vb, blocks):
    T = qn.shape[0]
    nq = T // SB_TILE
    B, P = SB_TILE, SB_PAIR
    hs = range(2)
    n = len(blocks)
    nhp = SB_D // P

    def body(*refs):
        q_ref, k_ref, v_ref = refs[:3]
        o_ref, l_ref, done_ref = refs[3 + n:6 + n]
        acc_ref = refs[6 + 2 * n]
        hp, qb = pl.program_id(0), pl.program_id(1)
        g_start, g_forward, g_finish = _gather_phases(refs[3:3 + n], refs[6 + n:6 + 2 * n], *refs[7 + 2 * n:])
        pl.when((hp == 0) & (qb == 0))(g_start)
        pl.when((hp == nhp - 1) & (qb == 0))(g_forward)
        m, mf = _sb_masks()
        row = lax.broadcasted_iota(jnp.int32, (B, B), 0)
        col = lax.broadcasted_iota(jnp.int32, (B, B), 1)
        later = (row > col).astype(BF16)
        past = col < row
        q2 = q_ref[...]
        qm = [jnp.where(m[h], q2, jnp.zeros_like(q2)) for h in hs]
        acc_ref[...] = jnp.zeros_like(acc_ref)

        def keys(kb):
            return k_ref[pl.ds(pl.multiple_of(kb * B, B), B), :]

        def values(kb):
            return v_ref[pl.ds(pl.multiple_of(kb * B, B), B), :]

        def scores(kb):
            k2 = keys(kb)
            return [_dg(qm[h], k2, _NT) for h in hs]

        def run(tiles, R):
            zs, cum, rsum = {}, {}, {}
            for t, (z, _, diag) in enumerate(tiles):
                for h in hs:
                    sp = _softplus(z[h])
                    lk = jnp.where(past, sp, 0.0) if diag else sp
                    zs[t, h] = z[h] - sp
                    cum[t, h] = _dot_exact_rhs(lk, later, SB_SPLIT_LK)
                    rsum[t, h] = jnp.sum(lk, axis=1, keepdims=True)
            R = list(R)
            for t, (_, kb, diag) in enumerate(tiles):
                v2 = values(kb)
                for h in hs:
                    w = jnp.exp(zs[t, h] - (cum[t, h] + R[h]))
                    if diag:
                        w = jnp.where(past, w, 0.0)
                    acc_ref[h] += _dg(w, v2, _NN)
                R = [R[h] + rsum[t, h] for h in hs]
            return tuple(R)

        def live(r):
            return (jnp.minimum(jnp.min(r[0]), jnp.min(r[1])) < SB_DEAD).astype(jnp.int32)

        zero = jnp.zeros((B, 1), F32)
        R = lax.cond(qb > 0,
                     lambda r: run([(scores(qb), qb, True), (scores(jnp.maximum(qb - 1, 0)), jnp.maximum(qb - 1, 0), False)], r),
                     lambda r: run([(scores(qb), qb, True)], r), (zero, zero))
        rest = jnp.maximum(qb - 1, 0)
        npairs = rest // 2

        def pair(carry):
            i, r, za, zb = carry[0], carry[2:4], carry[4:6], carry[6:8]
            ka = qb - 2 - 2 * i
            nxt = (scores(jnp.maximum(ka - 2, 0)), scores(jnp.maximum(ka - 3, 0)))
            r = run([(za, ka, False), (zb, ka - 1, False)], r)
            return (i + 1, live(r), *r, *nxt[0], *nxt[1])

        out = lax.while_loop(lambda c: (c[0] < npairs) & (c[1] > 0), pair,
                             (jnp.int32(0), live(R), *R, *scores(jnp.maximum(qb - 2, 0)), *scores(jnp.maximum(qb - 3, 0))))
        last = (rest % 2 == 1) & (out[0] == npairs) & (out[1] > 0)
        tile0 = jnp.int32(0)
        R = lax.cond(last, lambda r: run([(scores(tile0), tile0, False)], r), lambda r: r, out[2:4])
        done_ref[hp, qb] = 2 * out[0] + last.astype(jnp.int32)
        o_ref[...] = (acc_ref[0] * mf[0] + acc_ref[1] * mf[1]).astype(o_ref.dtype)
        l_ref[0] = R[0] * mf[0] + R[1] * mf[1]
        pl.when((hp == nhp - 1) & (qb == nq - 1))(g_finish)

    slab = pl.BlockSpec((T, P), lambda hp, qb: (0, hp))
    blk = pl.BlockSpec((B, P), lambda hp, qb: (qb, hp))
    anyspec = pl.BlockSpec(memory_space=pl.ANY)
    outs = pl.pallas_call(
        body, name="sb_fwd", grid=(nhp, nq),
        in_specs=[blk, slab, slab] + [anyspec] * n,
        out_specs=[blk, pl.BlockSpec((1, B, P), lambda hp, qb: (hp, qb, 0)), pl.BlockSpec(memory_space=pltpu.SMEM)]
        + [anyspec] * n,
        out_shape=[jax.ShapeDtypeStruct((T, SB_D), CDT), jax.ShapeDtypeStruct((nhp, T, P), F32),
                   jax.ShapeDtypeStruct((nhp, nq), jnp.int32)]
        + [jax.ShapeDtypeStruct((N_DEV,) + b.shape, b.dtype) for b in blocks],
        scratch_shapes=[pltpu.VMEM((2, B, P), F32)] + _exchange_sems(n),
        compiler_params=_cparams(("arbitrary", "arbitrary")),
    )(qn, kn, vb, *blocks)
    return outs[0], outs[1], outs[2], outs[3:]


def _sb_bwd(d_o, qn, kn, vb, lsum, done):
    T = qn.shape[0]
    nq = T // SB_TILE
    B, P = SB_TILE, SB_PAIR
    hs = range(2)
    do_first = (d_o.shape[1] - SB_D) // P

    def body(do_ref, q_ref, k_ref, v_ref, l_ref, done_ref, dq_ref, dk_ref, dv_ref, dqacc_ref):
        qb = pl.program_id(1)

        @pl.when(qb == 0)
        def _():
            dk_ref[...] = jnp.zeros_like(dk_ref)
            dv_ref[...] = jnp.zeros_like(dv_ref)

        m, mf = _sb_masks()
        row = lax.broadcasted_iota(jnp.int32, (B, B), 0)
        col = lax.broadcasted_iota(jnp.int32, (B, B), 1)
        upto = (row <= col).astype(BF16)
        before = (row < col).astype(BF16)
        past = col < row
        q2 = q_ref[...]
        qm = [jnp.where(m[h], q2, jnp.zeros_like(q2)) for h in hs]
        do2 = do_ref[...]
        dom = [jnp.where(m[h], do2, 0.0).astype(CDT) for h in hs]
        lb = l_ref[0]
        ltot = [lb[:, 0:1], lb[:, SB_HEAD_DIM:SB_HEAD_DIM + 1]]
        dqacc_ref[...] = jnp.zeros_like(dqacc_ref)

        def rows(kb):
            return pl.ds(pl.multiple_of(kb * B, B), B)

        def scores(kb):
            k2, v2 = k_ref[rows(kb), :], v_ref[rows(kb), :]
            return [_dg(qm[h], k2, _NT) for h in hs] + [_dg(dom[h], v2, _NT) for h in hs]

        def run(tiles, carry):
            Ps, Pg = list(carry[0]), list(carry[1])
            zs, sp_, cum, rest = {}, {}, {}, {}
            for t, tl in enumerate(tiles):
                diag = tl[5]
                for h in hs:
                    sp = _softplus(tl[h])
                    lk = jnp.where(past, sp, 0.0) if diag else sp
                    zs[t, h], sp_[t, h] = tl[h] - sp, sp
                    cum[t, h] = _dot_exact_rhs(lk, upto, SB_SPLIT_LK)
                    rest[t, h] = ltot[h] - Ps[h]
                    Ps[h] = Ps[h] + jnp.sum(lk, axis=1, keepdims=True)
            w, g, gx = {}, {}, {}
            for t, tl in enumerate(tiles):
                diag = tl[5]
                for h in hs:
                    wt = jnp.exp(zs[t, h] - (rest[t, h] - cum[t, h]))
                    if diag:
                        wt = jnp.where(past, wt, 0.0)
                    w[t, h] = wt
                    g[t, h] = wt * tl[2 + h]
                    gx[t, h] = _dot_exact_rhs(g[t, h], before, SB_SPLIT_G) + Pg[h]
                    Pg[h] = Pg[h] + jnp.sum(g[t, h], axis=1, keepdims=True)
            for t, tl in enumerate(tiles):
                kb, diag = tl[4], tl[5]
                k2 = k_ref[rows(kb), :]
                for h in hs:
                    sneg = jnp.exp(-sp_[t, h])
                    dz = g[t, h] * sneg - (1.0 - sneg) * gx[t, h]
                    if diag:
                        dz = jnp.where(past, dz, 0.0)
                    dz_c = dz.astype(CDT)
                    dv_ref[rows(kb), :] += _dg(w[t, h], dom[h], _TN)
                    dk_ref[rows(kb), :] += _dg(dz_c, qm[h], _TN)
                    dqacc_ref[h] += _dg(dz_c, k2, _NN)
            return tuple(Ps), tuple(Pg)

        zero = jnp.zeros((B, 1), F32)
        rest_tiles = jnp.maximum(qb - 1, 0)
        done = jnp.clip(done_ref[pl.program_id(0), qb], 0, rest_tiles)
        npairs = done // 2
        tile0 = jnp.int32(0)
        sums = lax.cond(done % 2 == 1, lambda s: run([(*scores(tile0), tile0, False)], s), lambda s: s,
                        ((zero, zero), (zero, zero)))
        left0 = qb - 1 - 2 * npairs

        def pair(i, carry):
            sums, ta, tb = (carry[0:2], carry[2:4]), carry[4:8], carry[8:12]
            ka = left0 + 2 * i
            nxt = scores(jnp.minimum(ka + 2, qb)) + scores(jnp.minimum(ka + 3, qb))
            Ps, Pg = run([(*ta, ka, False), (*tb, ka + 1, False)], sums)
            return (*Ps, *Pg, *nxt)

        first = jnp.clip(left0, 0, qb)
        out = lax.fori_loop(0, npairs, pair,
                            (*sums[0], *sums[1], *scores(first), *scores(jnp.minimum(first + 1, qb))))
        sums = (out[0:2], out[2:4])
        near = jnp.maximum(qb - 1, 0)
        lax.cond(qb > 0,
                 lambda s: run([(*scores(near), near, False), (*scores(qb), qb, True)], s),
                 lambda s: run([(*scores(qb), qb, True)], s), sums)
        dq_ref[...] = dqacc_ref[0] * mf[0] + dqacc_ref[1] * mf[1]

    slab = pl.BlockSpec((T, P), lambda hp, qb: (0, hp))
    blk = pl.BlockSpec((B, P), lambda hp, qb: (qb, hp))
    return pl.pallas_call(
        body, name="sb_bwd", grid=(SB_D // P, nq),
        in_specs=[pl.BlockSpec((B, P), lambda hp, qb: (qb, hp + do_first)), blk, slab, slab,
                  pl.BlockSpec((1, B, P), lambda hp, qb: (hp, qb, 0)),
                  pl.BlockSpec(memory_space=pltpu.SMEM)],
        out_specs=[blk, slab, slab],
        out_shape=[jax.ShapeDtypeStruct((T, SB_D), F32)] * 3,
        scratch_shapes=[pltpu.VMEM((2, B, P), F32)],
        compiler_params=_cparams(("arbitrary", "arbitrary")),
    )(d_o, qn, kn, vb, lsum, done)


def _regroup_in_rows(wt):
    cut = 2 * GLA_DK + 2 * GLA_DV
    pad = jnp.zeros((IN_PAD - IN_WIDTH, wt.shape[1]), wt.dtype)
    return jnp.concatenate([wt[:cut], wt[cut + GLA_GATE_RANK:], wt[cut:cut + GLA_GATE_RANK], pad], axis=0)


def _ungroup_in_rows(gt):
    cut = 2 * GLA_DK + 2 * GLA_DV
    return jnp.concatenate([gt[:cut], gt[3072:3072 + GLA_GATE_RANK], gt[cut:3072]], axis=0)


def _colsum(v, *, name, tm=512):
    T, C = v.shape

    def body(v_ref, o_ref):
        i = pl.program_id(0)
        part = jnp.sum(v_ref[...], axis=0, keepdims=True)

        @pl.when(i == 0)
        def _():
            o_ref[...] = part

        @pl.when(i > 0)
        def _():
            o_ref[...] += part

    return pl.pallas_call(
        body, name=name, grid=(T // tm,),
        in_specs=[pl.BlockSpec((tm, C), lambda i: (i, 0))], out_specs=pl.BlockSpec((1, C), lambda i: (0, 0)),
        out_shape=jax.ShapeDtypeStruct((1, C), F32),
        compiler_params=_cparams(("arbitrary",)),
    )(v)


def _ffn_fwd(h, g_norm, wgu_t, wd, tag):
    hf = _rms_fwd(h, g_norm, name=f"ffn{tag}_norm")
    ab, s = _swiglu_up(hf, *wgu_t, name=f"ffn{tag}_up")
    h_out = _matmul(s, wd, mode='nn', out_dtype=F32, name=f"ffn{tag}_down", tm=512, tn=D_MODEL, tk=D_FF, residual=h)
    return h_out, (hf, ab, s)


def _ffn_bwd(dh, dh_c, h_in, g_norm, wgu_t, wd, saved, tag):
    hf, ab, s = saved
    dwd = _matmul(s, dh_c, mode='tn', out_dtype=F32, name=f"ffn{tag}_dwd", tm=D_FF // 2, tn=D_MODEL, tk=TK_TOKENS)
    dab = _swiglu_dact(dh_c, wd, ab, name=f"ffn{tag}_dact")
    dwgu_t = _matmul(dab, hf, mode='tn', out_dtype=F32, name=f"ffn{tag}_dwgu", tm=D_FF // 2, tn=D_MODEL, tk=TK_TOKENS)
    dhf = _swiglu_dhf(dab, *wgu_t, name=f"ffn{tag}_dhf")
    dh_in, dh_in_c, dg = _rms_bwd(dhf, h_in, g_norm, dh, name=f"ffn{tag}_dnorm")
    return dh_in, dh_in_c, dwgu_t, dwd, dg


def _late_weights(gathered):
    g_out, g_pw1, g_pw2, g_gate, g_up, g_down = gathered
    Dm = D_MODEL
    return {
        'hy_w_out': g_out.reshape(Dm, Dm),
        'cv_w_pw1_t': g_pw1.reshape(2 * Dm, Dm),
        'cv_w_pw2': g_pw2.reshape(Dm, Dm),
        'ffn_wgu_t': [(g_gate[:, l].reshape(D_FF, Dm), g_up[:, l].reshape(D_FF, Dm)) for l in range(2)],
        'ffn_w_down': [g_down[:, l].reshape(D_FF, Dm) for l in range(2)],
    }


def _local_step(x, tgt, W, late_blocks):
    row = lambda v: v.reshape(1, -1)
    win_p = _regroup_in_rows(W['hy_w_in_t'])
    wg2p = jnp.pad(W['hy_w_gate2'], ((0, 128 - GLA_GATE_RANK), (0, 0)))
    b_gate = row(W['hy_b_gate'])
    g_gla = row(W['hy_gla_norm'])
    gq = jnp.tile(W['hy_sb_q_norm'].reshape(-1), SB_D // SB_HEAD_DIM).reshape(1, SB_D)
    gk = jnp.tile(W['hy_sb_k_norm'].reshape(-1), SB_D // SB_HEAD_DIM).reshape(1, SB_D)
    w_dw = jnp.pad(W['cv_w_dw'], ((0, CONV_HALO - CONV_WIDTH), (0, 0)))
    mixn = [row(W['mix_norm'][l]) for l in range(2)]
    ffnn = [row(W['ffn_norm'][l]) for l in range(2)]

    hn0 = _rms_fwd(x, mixn[0], name="mix0_norm")
    proj = _matmul(hn0, win_p, mode='nt', out_dtype=F32, name="hy_in", tm=256, tn=IN_PAD, tk=D_MODEL)
    o_gla, o_raw, states = _gla_fwd(proj, wg2p, b_gate, g_gla)
    qn, kn, vb = _sb_prep(proj, gq, gk)
    o_sb, lsum, sb_done, gathered = _sb_fwd(qn, kn, vb, late_blocks)
    W = {**W, **_late_weights(gathered)}
    w_out, wgu, wd = W['hy_w_out'], W['ffn_wgu_t'], W['ffn_w_down']
    o_mix = jnp.concatenate([o_gla, o_sb], axis=1)
    h1 = _matmul(o_mix, w_out, mode='nn', out_dtype=F32, name="hy_out", tm=512, tn=D_MODEL, tk=D_MODEL, residual=x)
    h2, ffn0_saved = _ffn_fwd(h1, ffnn[0], wgu[0], wd[0], 0)
    hn1 = _rms_fwd(h2, mixn[1], name="mix1_norm")
    a_cv = _matmul(hn1, W['cv_w_pw1_t'], mode='nt', out_dtype=F32, name="cv_pw1", tm=512, tn=2 * D_MODEL, tk=D_MODEL,
                   bias=row(W['cv_b_pw1']))
    s_cv, u_cv, c_cv = _conv_fwd(a_cv, w_dw, row(W['cv_b_dw']), row(W['cv_ln_g']), row(W['cv_ln_b']))
    h3 = _matmul(s_cv, W['cv_w_pw2'], mode='nn', out_dtype=F32, name="cv_pw2", tm=512, tn=D_MODEL, tk=D_MODEL,
                 bias=row(W['cv_b_pw2']), residual=h2)
    h4, ffn1_saved = _ffn_fwd(h3, ffnn[1], wgu[1], wd[1], 1)
    sq_err, dy, dy_c = _loss_head(h4, tgt)

    G = {}
    dh3, dh3_c, dwgu1, dwd1, dg_ffn1 = _ffn_bwd(dy, dy_c, h3, ffnn[1], wgu[1], wd[1], ffn1_saved, 1)
    G['cv_b_pw2'] = _colsum(dh3, name="cv_db2")
    G['cv_w_pw2'] = _matmul(s_cv, dh3_c, mode='tn', out_dtype=F32, name="cv_dw2", tm=D_MODEL, tn=D_MODEL, tk=TK_TOKENS)
    ds_cv = _matmul(dh3_c, W['cv_w_pw2'], mode='nt', out_dtype=F32, name="cv_ds", tm=512, tn=D_MODEL, tk=D_MODEL)
    F8 = D_FF // N_DEV
    early_own = [G['cv_w_pw2'].reshape(N_DEV, D_MODEL // N_DEV, D_MODEL),
                 dwgu1.reshape(2, N_DEV, F8, D_MODEL).transpose(1, 0, 2, 3).reshape(N_DEV, 2 * F8, D_MODEL),
                 dwd1.reshape(N_DEV, F8, D_MODEL)]
    (da_cv, db1, dwdw, dbdw, dlng, dlnb), early_recv = _conv_bwd(
        ds_cv, c_cv, u_cv, a_cv, w_dw, row(W['cv_ln_g']), row(W['cv_ln_b']), [a.astype(BF16) for a in early_own])
    G['cv_b_pw1'] = db1
    G['cv_w_dw'] = dwdw[:CONV_WIDTH]
    G['cv_b_dw'], G['cv_ln_g'], G['cv_ln_b'] = dbdw, dlng, dlnb
    G['cv_w_pw1_t'] = _matmul(da_cv, hn1, mode='tn', out_dtype=F32, name="cv_dw1", tm=D_MODEL, tn=D_MODEL, tk=TK_TOKENS)
    dhn1 = _matmul(da_cv, W['cv_w_pw1_t'], mode='nn', out_dtype=F32, name="cv_dhn", tm=512, tn=D_MODEL, tk=2 * D_MODEL)
    dh2, dh2_c, dg_mix1 = _rms_bwd(dhn1, h2, mixn[1], dh3, name="mix1_dnorm")
    dh1, dh1_c, dwgu0, dwd0, dg_ffn0 = _ffn_bwd(dh2, dh2_c, h1, ffnn[0], wgu[0], wd[0], ffn0_saved, 0)
    G['hy_w_out'] = _matmul(o_mix, dh1_c, mode='tn', out_dtype=F32, name="hy_dwout", tm=D_MODEL, tn=D_MODEL, tk=TK_TOKENS)
    d_omix = _matmul(dh1_c, w_out, mode='nt', out_dtype=F32, name="hy_domix", tm=512, tn=D_MODEL, tk=D_MODEL)
    dgla, dglr, dwg2, dbg, dgg = _gla_bwd(d_omix, proj, o_raw, states, wg2p, b_gate, g_gla)
    dqn, dkn, dvs = _sb_bwd(d_omix, qn, kn, vb, lsum, sb_done)
    dsb, dgq, dgk = _sb_prep_bwd(dqn, dkn, dvs, proj, gq, gk)
    dproj = jnp.concatenate([dgla, dsb, dglr], axis=1)
    dwin_p = _matmul(dproj, hn0, mode='tn', out_dtype=F32, name="hy_dwin", tm=IN_PAD // 5, tn=D_MODEL, tk=TK_TOKENS)
    dhn0 = _matmul(dproj, win_p, mode='nn', out_dtype=F32, name="hy_dhn", tm=256, tn=D_MODEL, tk=IN_PAD)
    dx, _, dg_mix0 = _rms_bwd(dhn0, x, mixn[0], dh1, name="mix0_dnorm")

    G['hy_w_in_t'] = _ungroup_in_rows(dwin_p)
    G['hy_w_gate2'] = dwg2[:GLA_GATE_RANK]
    G['hy_b_gate'] = dbg
    G['hy_gla_norm'] = dgg
    G['hy_sb_q_norm'] = dgq.reshape(SB_D // SB_HEAD_DIM, SB_HEAD_DIM).sum(axis=0, keepdims=True)
    G['hy_sb_k_norm'] = dgk.reshape(SB_D // SB_HEAD_DIM, SB_HEAD_DIM).sum(axis=0, keepdims=True)
    G['mix_norm'] = jnp.concatenate([dg_mix0, dg_mix1], axis=0)
    G['ffn_norm'] = jnp.concatenate([dg_ffn0, dg_ffn1], axis=0)
    G['ffn_wgu_t0'] = dwgu0
    G['ffn_w_down0'] = dwd0
    return sq_err, dx, G, (early_own, early_recv)


MESH_IDS = pl.DeviceIdType.MESH
N_PEER = N_DEV - 1


def _exchange_sems(n):
    return [pltpu.SemaphoreType.DMA((n * N_PEER,)), pltpu.SemaphoreType.DMA((n * N_PEER,)),
            pltpu.SemaphoreType.DMA((n,))]


def _gather_phases(x_refs, out_refs, send_sems, recv_sems, local_sems):
    n = len(x_refs)
    x, y, c = lax.axis_index("x"), lax.axis_index("y"), lax.axis_index("c")
    me, sibling = (x, y, c), (x, y, 1 - c)
    chips = [(1 - x, y), (x, 1 - y), (1 - x, 1 - y)]

    def slot(a, px, py, pc):
        return out_refs[a].at[4 * px + 2 * py + pc]

    def copy(a, k, blk, to, src=None):
        return pltpu.make_async_remote_copy(
            src_ref=slot(a, *blk) if src is None else src, dst_ref=slot(a, *blk),
            send_sem=send_sems.at[a * N_PEER + k], recv_sem=recv_sems.at[a * N_PEER + k],
            device_id=to, device_id_type=MESH_IDS)

    def local(a):
        return pltpu.make_async_copy(x_refs[a], slot(a, *me), local_sems.at[a])

    def first(a):
        return [copy(a, 0, me, sibling, src=x_refs[a])] + [copy(a, 1 + j, me, (*chip, c), src=x_refs[a])
                                                           for j, chip in enumerate(chips)]

    def passed(a):
        return [copy(a, 4 + j, (*chip, c), sibling) for j, chip in enumerate(chips)]

    def start():
        for a in range(n):
            local(a).start()
        for a in range(n):
            for cp in first(a):
                cp.start()

    def forward():
        for a in range(n):
            for j, chip in enumerate(chips):
                copy(a, 1 + j, (*chip, c), me).wait_recv()
                copy(a, 4 + j, (*chip, c), sibling).start()

    def finish():
        for a in range(n):
            copy(a, 0, sibling, me).wait_recv()
            for j, chip in enumerate(chips):
                copy(a, 4 + j, (*chip, 1 - c), me).wait_recv()
        for a in range(n):
            for cp in first(a) + passed(a):
                cp.wait_send()
            local(a).wait()

    return start, forward, finish


def _all_gather(blocks):
    n = len(blocks)

    def body(*refs):
        start, forward, finish = _gather_phases(refs[:n], refs[n:2 * n], *refs[2 * n:])
        start()
        forward()
        finish()

    anyspec = pl.BlockSpec(memory_space=pl.ANY)
    return pl.pallas_call(
        body, name="fsdp_all_gather",
        out_shape=[jax.ShapeDtypeStruct((N_DEV,) + b.shape, b.dtype) for b in blocks],
        in_specs=[anyspec] * n, out_specs=[anyspec] * n,
        scratch_shapes=_exchange_sems(n),
    )(*blocks)


def _scatter_phases(s_refs, r_refs, send_sems, recv_sems, local_sems):
    n = len(s_refs)
    x, y, c = lax.axis_index("x"), lax.axis_index("y"), lax.axis_index("c")
    me = 4 * x + 2 * y + c

    def local(a):
        return pltpu.make_async_copy(s_refs[a].at[me], r_refs[a].at[me], local_sems.at[a])

    def copy(a, k):
        px, py, pc = x ^ ((k >> 2) & 1), y ^ ((k >> 1) & 1), c ^ (k & 1)
        return pltpu.make_async_remote_copy(
            src_ref=s_refs[a].at[4 * px + 2 * py + pc], dst_ref=r_refs[a].at[me],
            send_sem=send_sems.at[a * N_PEER + k - 1], recv_sem=recv_sems.at[a * N_PEER + k - 1],
            device_id=(px, py, pc), device_id_type=MESH_IDS)

    def start():
        for a in range(n):
            local(a).start()
        for a in range(n):
            for k in range(1, N_DEV):
                copy(a, k).start()

    def finish():
        for a in range(n):
            for k in range(1, N_DEV):
                copy(a, k).wait()
            local(a).wait()

    return start, finish


def _scatter_exchange(sends):
    n = len(sends)

    def body(*refs):
        start, finish = _scatter_phases(refs[:n], refs[n:2 * n], *refs[2 * n:])
        start()
        finish()

    anyspec = pl.BlockSpec(memory_space=pl.ANY)
    return pl.pallas_call(
        body, name="fsdp_scatter_exchange",
        out_shape=[jax.ShapeDtypeStruct(s.shape, s.dtype) for s in sends],
        in_specs=[anyspec] * n, out_specs=[anyspec] * n,
        scratch_shapes=_exchange_sems(n),
    )(*sends)


def _sum_contrib(recv, own, *, name, tr):
    _, R, C = recv.shape
    assert R % tr == 0

    def body(r_ref, own_ref, g_ref):
        me = 4 * lax.axis_index("x") + 2 * lax.axis_index("y") + lax.axis_index("c")
        g = jnp.zeros((tr, C), F32)
        for s in range(N_DEV):
            g = g + jnp.where(me == s, own_ref[...], r_ref[s].astype(F32))
        g_ref[...] = g

    row = pl.BlockSpec((tr, C), lambda i: (i, 0))
    return pl.pallas_call(
        body, name=name, grid=(R // tr,),
        in_specs=[pl.BlockSpec((N_DEV, tr, C), lambda i: (0, i, 0)), row], out_specs=row,
        out_shape=jax.ShapeDtypeStruct((R, C), F32),
        compiler_params=_cparams(("parallel",)),
    )(recv, own)


def _adamw(g, w, m, v, *, name, tr):
    R, C = g.shape
    assert R % tr == 0

    def body(g_ref, w_ref, m_ref, v_ref, d_ref, mo_ref, vo_ref):
        gv = g_ref[...]
        mn = ADAM_B1 * m_ref[...] + (1.0 - ADAM_B1) * gv
        vn = ADAM_B2 * v_ref[...] + (1.0 - ADAM_B2) * (gv * gv)
        m_hat = mn / (1.0 - ADAM_B1 ** ADAM_STEP)
        v_hat = vn / (1.0 - ADAM_B2 ** ADAM_STEP)
        d_ref[...] = -ADAM_LR * (m_hat / (jnp.sqrt(v_hat) + ADAM_EPS) + ADAM_WD * w_ref[...])
        mo_ref[...] = mn
        vo_ref[...] = vn

    row = pl.BlockSpec((tr, C), lambda i: (i, 0))
    return pl.pallas_call(
        body, name=name, grid=(R // tr,),
        in_specs=[row] * 4, out_specs=[row] * 3,
        out_shape=[jax.ShapeDtypeStruct((R, C), F32)] * 3,
        compiler_params=_cparams(("parallel",)),
    )(g, w, m, v)


SMALL_SHARDED = ('hy_w_gate2', 'cv_b_pw1', 'cv_w_dw', 'cv_b_dw', 'cv_ln_g', 'cv_ln_b', 'cv_b_pw2')
SMALL_REPLICATED = ('mix_norm', 'ffn_norm', 'hy_b_gate', 'hy_gla_norm', 'hy_sb_q_norm', 'hy_sb_k_norm')
LANES = 128


def _small_rows(n):
    return -(-n // (8 * LANES)) * 8


def _pack_small(parts, lead=()):
    out = []
    for p in parts:
        n = p.shape[-1]
        p = jnp.pad(p, [(0, 0)] * len(lead) + [(0, _small_rows(n) * LANES - n)])
        out.append(p.reshape(*lead, _small_rows(n), LANES))
    return jnp.concatenate(out, axis=len(lead))


def _unpack_small(packed, sizes, lead=()):
    out, r0 = [], 0
    for n in sizes:
        r = _small_rows(n)
        out.append(packed[..., r0:r0 + r, :].reshape(*lead, r * LANES)[..., :n])
        r0 += r
    return out


def _to_blocks(full, axis):
    shp = full.shape
    t = full.reshape(shp[:axis] + (N_DEV, shp[axis] // N_DEV) + shp[axis + 1:])
    return jnp.moveaxis(t, axis, 0)


def _from_blocks(blocks, axis):
    t = jnp.moveaxis(blocks, 0, axis)
    shp = t.shape
    return t.reshape(shp[:axis] + (shp[axis] * shp[axis + 1],) + shp[axis + 2:])


def kernel(x, mix_norm, ffn_norm, hy_w_in, hy_w_gate2, hy_b_gate, hy_gla_norm, hy_sb_q_norm, hy_sb_k_norm, hy_w_out, cv_w_pw1, cv_b_pw1, cv_w_dw, cv_b_dw, cv_ln_g, cv_ln_b, cv_w_pw2, cv_b_pw2, ffn_w_gate, ffn_w_up, ffn_w_down, loss_target, m_mix_norm, m_ffn_norm, m_hy_w_in, m_hy_w_gate2, m_hy_b_gate, m_hy_gla_norm, m_hy_sb_q_norm, m_hy_sb_k_norm, m_hy_w_out, m_cv_w_pw1, m_cv_b_pw1, m_cv_w_dw, m_cv_b_dw, m_cv_ln_g, m_cv_ln_b, m_cv_w_pw2, m_cv_b_pw2, m_ffn_w_gate, m_ffn_w_up, m_ffn_w_down, v_mix_norm, v_ffn_norm, v_hy_w_in, v_hy_w_gate2, v_hy_b_gate, v_hy_gla_norm, v_hy_sb_q_norm, v_hy_sb_k_norm, v_hy_w_out, v_cv_w_pw1, v_cv_b_pw1, v_cv_w_dw, v_cv_b_dw, v_cv_ln_g, v_cv_ln_b, v_cv_w_pw2, v_cv_b_pw2, v_ffn_w_gate, v_ffn_w_up, v_ffn_w_down):
    w_loc = dict(zip(WEIGHT_NAMES, (mix_norm, ffn_norm, hy_w_in, hy_w_gate2, hy_b_gate, hy_gla_norm, hy_sb_q_norm, hy_sb_k_norm, hy_w_out, cv_w_pw1, cv_b_pw1, cv_w_dw, cv_b_dw, cv_ln_g, cv_ln_b, cv_w_pw2, cv_b_pw2, ffn_w_gate, ffn_w_up, ffn_w_down)))
    m_loc = dict(zip(WEIGHT_NAMES, (m_mix_norm, m_ffn_norm, m_hy_w_in, m_hy_w_gate2, m_hy_b_gate, m_hy_gla_norm, m_hy_sb_q_norm, m_hy_sb_k_norm, m_hy_w_out, m_cv_w_pw1, m_cv_b_pw1, m_cv_w_dw, m_cv_b_dw, m_cv_ln_g, m_cv_ln_b, m_cv_w_pw2, m_cv_b_pw2, m_ffn_w_gate, m_ffn_w_up, m_ffn_w_down)))
    v_loc = dict(zip(WEIGHT_NAMES, (v_mix_norm, v_ffn_norm, v_hy_w_in, v_hy_w_gate2, v_hy_b_gate, v_hy_gla_norm, v_hy_sb_q_norm, v_hy_sb_k_norm, v_hy_w_out, v_cv_w_pw1, v_cv_b_pw1, v_cv_w_dw, v_cv_b_dw, v_cv_ln_g, v_cv_ln_b, v_cv_w_pw2, v_cv_b_pw2, v_ffn_w_gate, v_ffn_w_up, v_ffn_w_down)))

    Dm, F8 = D_MODEL, D_FF // N_DEV
    tr_ = lambda a: jnp.swapaxes(a, -1, -2)

    small_local = _pack_small([w_loc[n].reshape(-1) for n in SMALL_SHARDED])
    g_in, g_small = _all_gather([tr_(hy_w_in[0]).astype(BF16), small_local])
    late_blocks = [hy_w_out[0].astype(BF16),
                   tr_(cv_w_pw1[0]).astype(BF16),
                   cv_w_pw2[0].astype(BF16),
                   tr_(ffn_w_gate).astype(BF16),
                   tr_(ffn_w_up).astype(BF16),
                   ffn_w_down.astype(BF16)]
    small_sizes = [w_loc[n].size for n in SMALL_SHARDED]
    small_full = dict(zip(SMALL_SHARDED, _unpack_small(g_small, small_sizes, lead=(N_DEV,))))
    W = {n: w_loc[n] for n in SMALL_REPLICATED}
    W['hy_w_in_t'] = g_in.reshape(IN_WIDTH, Dm)
    W['hy_w_gate2'] = _from_blocks(small_full['hy_w_gate2'].reshape(N_DEV, GLA_GATE_RANK, GLA_DK // N_DEV), 1).astype(BF16)
    W['cv_w_dw'] = _from_blocks(small_full['cv_w_dw'].reshape(N_DEV, CONV_WIDTH, Dm // N_DEV), 1)
    for n in ('cv_b_pw1', 'cv_b_dw', 'cv_ln_g', 'cv_ln_b', 'cv_b_pw2'):
        W[n] = small_full[n].reshape(-1)

    sq_err, dx, G, (early_own, early_recv) = _local_step(x[0], loss_target[0], W, late_blocks)

    own_f32 = [
        G['hy_w_in_t'].reshape(N_DEV, IN_WIDTH // N_DEV, Dm),
        G['hy_w_out'].reshape(N_DEV, Dm // N_DEV, Dm),
        G['cv_w_pw1_t'].reshape(N_DEV, 2 * Dm // N_DEV, Dm),
        G['ffn_wgu_t0'].reshape(2, N_DEV, F8, Dm).transpose(1, 0, 2, 3).reshape(N_DEV, 2 * F8, Dm),
        G['ffn_w_down0'].reshape(N_DEV, F8, Dm),
    ]
    small_parts = []
    for n in SMALL_SHARDED:
        axis = SHARD_AXIS[n] - 1
        shard = w_loc[n].shape[1:]
        full = shard[:axis] + (shard[axis] * N_DEV,) + shard[axis + 1:]
        small_parts.append(_to_blocks(G[n].reshape(full), axis).reshape(N_DEV, -1))
    for n in SMALL_REPLICATED:
        small_parts.append(jnp.broadcast_to(G[n].reshape(1, -1), (N_DEV, G[n].size)))
    small_parts.append(jnp.broadcast_to(sq_err.reshape(1, 1), (N_DEV, 1)))
    send_small = _pack_small(small_parts, lead=(N_DEV,))
    recv = _scatter_exchange([a.astype(BF16) for a in own_f32] + [send_small])
    me = 4 * lax.axis_index("x") + 2 * lax.axis_index("y") + lax.axis_index("c")
    tags = ['hy_w_in', 'hy_w_out', 'cv_w_pw1', 'ffn_wgu0', 'ffn_w_down0', 'small', 'cv_w_pw2', 'ffn_wgu1', 'ffn_w_down1']
    own_all = own_f32 + [send_small] + list(early_own)
    recv_all = list(recv) + list(early_recv)
    gsum = dict((t, _sum_contrib(r, lax.dynamic_index_in_dim(o, me, 0, keepdims=False), name=f"sum_{t}", tr=r.shape[1]))
                for t, r, o in zip(tags, recv_all, own_all))

    grad = {}
    grad['hy_w_in'] = tr_(gsum['hy_w_in'])[None]
    grad['hy_w_out'] = gsum['hy_w_out'][None]
    grad['cv_w_pw1'] = tr_(gsum['cv_w_pw1'])[None]
    grad['cv_w_pw2'] = gsum['cv_w_pw2'][None]
    gu = jnp.stack([gsum['ffn_wgu0'], gsum['ffn_wgu1']]).reshape(2, 2, F8, Dm)
    grad['ffn_w_gate'] = tr_(gu[:, 0])
    grad['ffn_w_up'] = tr_(gu[:, 1])
    grad['ffn_w_down'] = jnp.stack([gsum['ffn_w_down0'], gsum['ffn_w_down1']])
    small_names = SMALL_SHARDED + SMALL_REPLICATED
    small_all = [w_loc[n].size for n in small_names]
    *small_grads, sq_sum = _unpack_small(gsum['small'], small_all + [1])
    for n, a in zip(small_names, small_grads):
        grad[n] = a.reshape(w_loc[n].shape)
    loss = 0.5 / Dm * sq_sum[0]

    delta, new_m, new_v = {}, {}, {}
    view = {'hy_w_in': (Dm, 256), 'hy_w_out': (Dm // N_DEV, Dm // N_DEV), 'cv_w_pw1': (Dm, 256),
            'cv_w_pw2': (Dm // N_DEV, Dm // N_DEV), 'ffn_w_gate': (2 * Dm, 256), 'ffn_w_up': (2 * Dm, 256),
            'ffn_w_down': (2 * F8, F8)}
    for n, (rows, tr) in view.items():
        shp = w_loc[n].shape
        outs = _adamw(grad[n].reshape(rows, -1), w_loc[n].reshape(rows, -1), m_loc[n].reshape(rows, -1),
                      v_loc[n].reshape(rows, -1), name=f"adamw_{n}", tr=tr)
        delta[n], new_m[n], new_v[n] = (o.reshape(shp) for o in outs)
    packed = [_pack_small([d[n].reshape(-1) for n in small_names] + [jnp.zeros((1,), F32)]) for d in (w_loc, m_loc, v_loc)]
    outs = _adamw(gsum['small'], *packed, name="adamw_small", tr=gsum['small'].shape[0])
    for dst, o in zip((delta, new_m, new_v), outs):
        for n, a in zip(small_names, _unpack_small(o, small_all)):
            dst[n] = a.reshape(w_loc[n].shape)

    return (loss, dx[None], *[grad[n] for n in WEIGHT_NAMES], *[delta[n] for n in WEIGHT_NAMES],
            *[new_m[n] for n in WEIGHT_NAMES], *[new_v[n] for n in WEIGHT_NAMES])
```

```python
import jax
import jax.numpy as jnp
from jax import lax
from jax.experimental import pallas as pl
from jax.experimental.pallas import tpu as pltpu

F32 = jnp.float32
BF16 = jnp.bfloat16
CDT = jnp.bfloat16

D_MODEL = 1024
EPS = 1e-6
CHUNK = 64
GLA_HEADS = 4
GLA_HEAD_K = 64
GLA_HEAD_V = 128
GLA_DK = GLA_HEADS * GLA_HEAD_K
GLA_DV = GLA_HEADS * GLA_HEAD_V
GLA_GATE_RANK = 16
GLA_GATE_NORMALIZER = 16.0
SB_HEAD_DIM = 64
SB_D = 512
SB_TILE = 256
SB_PAIR = 128
SB_SPLIT_LK = 2
SB_SPLIT_G = 1
SB_SPLIT_NORM = 2
SB_DEAD = 120.0
IN_WIDTH = 3088
IN_PAD = 3200
CONV_WIDTH = 31
CONV_HALO = 32
D_FF = 2816
N_DEV = 8

ADAM_LR = 0.001
ADAM_B1 = 0.9
ADAM_B2 = 0.999
ADAM_EPS = 1e-08
ADAM_WD = 0.01
ADAM_STEP = 10

VMEM_LIMIT = 56 * 1024 * 1024
TK_TOKENS = 2048

WEIGHT_NAMES = ['mix_norm', 'ffn_norm', 'hy_w_in', 'hy_w_gate2', 'hy_b_gate', 'hy_gla_norm', 'hy_sb_q_norm',
                'hy_sb_k_norm', 'hy_w_out', 'cv_w_pw1', 'cv_b_pw1', 'cv_w_dw', 'cv_b_dw', 'cv_ln_g', 'cv_ln_b',
                'cv_w_pw2', 'cv_b_pw2', 'ffn_w_gate', 'ffn_w_up', 'ffn_w_down']
SHARD_AXIS = {'mix_norm': None, 'ffn_norm': None, 'hy_w_in': 2, 'hy_w_gate2': 2, 'hy_b_gate': None,
              'hy_gla_norm': None, 'hy_sb_q_norm': None, 'hy_sb_k_norm': None, 'hy_w_out': 1, 'cv_w_pw1': 2,
              'cv_b_pw1': 1, 'cv_w_dw': 2, 'cv_b_dw': 1, 'cv_ln_g': 1, 'cv_ln_b': 1, 'cv_w_pw2': 1, 'cv_b_pw2': 1,
              'ffn_w_gate': 2, 'ffn_w_up': 2, 'ffn_w_down': 1}


def _cparams(sem=None, vmem=VMEM_LIMIT):
    return pltpu.CompilerParams(dimension_semantics=sem, vmem_limit_bytes=vmem)


def _log_sigmoid(x):
    return jnp.minimum(x, 0.0) - jnp.log1p(jnp.exp(-jnp.abs(x)))


def _sigmoid(x):
    return 1.0 / (1.0 + jnp.exp(-x))


def _softplus(x):
    return jnp.maximum(x, 0.0) + jnp.log(1.0 + jnp.exp(-jnp.abs(x)))


def _split_bf16(x, n):
    parts = []
    rem = x
    for _ in range(n):
        p = rem.astype(BF16)
        parts.append(p)
        rem = rem - p.astype(F32)
    return parts


def _dot_exact_rhs(x, m, n):
    return sum(jnp.dot(p, m, preferred_element_type=F32) for p in _split_bf16(x, n))


def _dot_exact_lhs(m, x, n):
    return sum(jnp.dot(m, p, preferred_element_type=F32) for p in _split_bf16(x, n))


_NN = (((1,), (0,)), ((), ()))
_NT = (((1,), (1,)), ((), ()))
_TN = (((0,), (0,)), ((), ()))


def _dg(a, b, dn):
    return lax.dot_general(a.astype(CDT), b.astype(CDT), dn, preferred_element_type=F32)


def _matmul(a, b, *, mode, out_dtype, name, tm, tn, tk, bias=None, residual=None):
    if mode == 'nn':
        (M, K), (K2, N) = a.shape, b.shape
    elif mode == 'nt':
        (M, K), (N, K2) = a.shape, b.shape
    else:
        (K, M), (K2, N) = a.shape, b.shape
    assert K == K2 and M % tm == 0 and N % tn == 0 and K % tk == 0, (name, a.shape, b.shape, tm, tn, tk)
    nk = K // tk
    a_spec = pl.BlockSpec((tk, tm), lambda i, j, k: (k, i)) if mode == 'tn' else pl.BlockSpec((tm, tk), lambda i, j, k: (i, k))
    b_spec = pl.BlockSpec((tn, tk), lambda i, j, k: (j, k)) if mode == 'nt' else pl.BlockSpec((tk, tn), lambda i, j, k: (k, j))
    dn = {'nn': _NN, 'nt': _NT, 'tn': _TN}[mode]
    has_bias, has_res = bias is not None, residual is not None

    def body(*refs):
        a_ref, b_ref = refs[0], refs[1]
        pos = 2
        bias_ref = res_ref = None
        if has_bias:
            bias_ref = refs[pos]
            pos += 1
        if has_res:
            res_ref = refs[pos]
            pos += 1
        o_ref = refs[pos]
        acc_ref = refs[pos + 1] if nk > 1 else None
        p = _dg(a_ref[...], b_ref[...], dn)

        def finish(acc):
            if has_bias:
                acc = acc + bias_ref[...]
            if has_res:
                acc = res_ref[...] + acc
            o_ref[...] = acc.astype(o_ref.dtype)

        if nk == 1:
            finish(p)
        else:
            k = pl.program_id(2)

            @pl.when(k == 0)
            def _():
                acc_ref[...] = p

            @pl.when(k > 0)
            def _():
                acc_ref[...] += p

            @pl.when(k == nk - 1)
            def _():
                finish(acc_ref[...])

    in_specs = [a_spec, b_spec]
    args = [a, b]
    if has_bias:
        in_specs.append(pl.BlockSpec((1, tn), lambda i, j, k: (0, j)))
        args.append(bias)
    if has_res:
        in_specs.append(pl.BlockSpec((tm, tn), lambda i, j, k: (i, j)))
        args.append(residual)
    return pl.pallas_call(
        body, name=name, grid=(M // tm, N // tn, nk),
        in_specs=in_specs, out_specs=pl.BlockSpec((tm, tn), lambda i, j, k: (i, j)),
        out_shape=jax.ShapeDtypeStruct((M, N), out_dtype),
        scratch_shapes=[pltpu.VMEM((tm, tn), F32)] if nk > 1 else [],
        compiler_params=_cparams(("parallel", "parallel", "arbitrary")),
    )(*args)


def _rms_fwd(x, g, *, name, tm=512):
    T, Dm = x.shape

    def body(x_ref, g_ref, o_ref):
        xv = x_ref[...]
        r = lax.rsqrt(jnp.mean(xv * xv, axis=-1, keepdims=True) + EPS)
        o_ref[...] = (xv * r * g_ref[...]).astype(o_ref.dtype)

    return pl.pallas_call(
        body, name=name, grid=(T // tm,),
        in_specs=[pl.BlockSpec((tm, Dm), lambda i: (i, 0)), pl.BlockSpec((1, Dm), lambda i: (0, 0))],
        out_specs=pl.BlockSpec((tm, Dm), lambda i: (i, 0)),
        out_shape=jax.ShapeDtypeStruct((T, Dm), CDT),
        compiler_params=_cparams(("parallel",)),
    )(x, g)


def _rms_bwd(dy, x, g, resid, *, name, tm=512):
    T, Dm = x.shape

    def body(dy_ref, x_ref, g_ref, res_ref, dx_ref, dxb_ref, dg_ref):
        i = pl.program_id(0)
        xv, dyv = x_ref[...], dy_ref[...]
        r = lax.rsqrt(jnp.mean(xv * xv, axis=-1, keepdims=True) + EPS)
        u = dyv * g_ref[...]
        dot = jnp.mean(u * xv, axis=-1, keepdims=True)
        dx = res_ref[...] + (r * u - xv * (r * r * r * dot))
        dx_ref[...] = dx
        dxb_ref[...] = dx.astype(dxb_ref.dtype)
        part = jnp.sum(dyv * xv * r, axis=0, keepdims=True)

        @pl.when(i == 0)
        def _():
            dg_ref[...] = part

        @pl.when(i > 0)
        def _():
            dg_ref[...] += part

    row = pl.BlockSpec((tm, Dm), lambda i: (i, 0))
    vec = pl.BlockSpec((1, Dm), lambda i: (0, 0))
    return pl.pallas_call(
        body, name=name, grid=(T // tm,),
        in_specs=[row, row, vec, row], out_specs=[row, row, vec],
        out_shape=[jax.ShapeDtypeStruct((T, Dm), F32), jax.ShapeDtypeStruct((T, Dm), CDT),
                   jax.ShapeDtypeStruct((1, Dm), F32)],
        compiler_params=_cparams(("arbitrary",)),
    )(dy, x, g, resid)


def _loss_head(y, tgt, *, tm=512):
    T, Dm = y.shape

    def body(y_ref, t_ref, s_ref, dy_ref, dyb_ref):
        i = pl.program_id(0)
        e = y_ref[...] - t_ref[...]
        dy = e * (1.0 / Dm)
        dy_ref[...] = dy
        dyb_ref[...] = dy.astype(dyb_ref.dtype)
        part = jnp.sum(jnp.sum(e * e, axis=1, keepdims=True), axis=0, keepdims=True)

        @pl.when(i == 0)
        def _():
            s_ref[...] = part

        @pl.when(i > 0)
        def _():
            s_ref[...] += part

    row = pl.BlockSpec((tm, Dm), lambda i: (i, 0))
    return pl.pallas_call(
        body, name="loss_head", grid=(T // tm,),
        in_specs=[row, row], out_specs=[pl.BlockSpec((1, 1), lambda i: (0, 0)), row, row],
        out_shape=[jax.ShapeDtypeStruct((1, 1), F32), jax.ShapeDtypeStruct((T, Dm), F32),
                   jax.ShapeDtypeStruct((T, Dm), CDT)],
        compiler_params=_cparams(("arbitrary",)),
    )(y, tgt)


def _swiglu_up(hf, wg_t, wu_t, *, name, tm=256):
    T, Dm = hf.shape
    F = wg_t.shape[0]

    def body(h_ref, wg_ref, wu_ref, ab_ref, s_ref):
        h = h_ref[...]
        a = _dg(h, wg_ref[...], _NT)
        b = _dg(h, wu_ref[...], _NT)
        ab_ref[:, :F] = a.astype(ab_ref.dtype)
        ab_ref[:, F:] = b.astype(ab_ref.dtype)
        s_ref[...] = (a * _sigmoid(a) * b).astype(s_ref.dtype)

    wspec = pl.BlockSpec((F, Dm), lambda i: (0, 0))
    return pl.pallas_call(
        body, name=name, grid=(T // tm,),
        in_specs=[pl.BlockSpec((tm, Dm), lambda i: (i, 0)), wspec, wspec],
        out_specs=[pl.BlockSpec((tm, 2 * F), lambda i: (i, 0)), pl.BlockSpec((tm, F), lambda i: (i, 0))],
        out_shape=[jax.ShapeDtypeStruct((T, 2 * F), CDT), jax.ShapeDtypeStruct((T, F), CDT)],
        compiler_params=_cparams(("parallel",)),
    )(hf, wg_t, wu_t)


def _swiglu_dhf(dab, wg_t, wu_t, *, name, tm=256):
    T = dab.shape[0]
    F, Dm = wg_t.shape

    def body(d_ref, wg_ref, wu_ref, o_ref):
        o_ref[...] = _dg(d_ref[:, :F], wg_ref[...], _NN) + _dg(d_ref[:, F:], wu_ref[...], _NN)

    wspec = pl.BlockSpec((F, Dm), lambda i: (0, 0))
    return pl.pallas_call(
        body, name=name, grid=(T // tm,),
        in_specs=[pl.BlockSpec((tm, 2 * F), lambda i: (i, 0)), wspec, wspec],
        out_specs=pl.BlockSpec((tm, Dm), lambda i: (i, 0)),
        out_shape=jax.ShapeDtypeStruct((T, Dm), F32),
        compiler_params=_cparams(("parallel",)),
    )(dab, wg_t, wu_t)


def _swiglu_dact(dh_c, wd, ab, *, name, tm=256):
    T, Dm = dh_c.shape
    F2 = ab.shape[1]
    F = F2 // 2

    def body(dh_ref, w_ref, ab_ref, o_ref):
        dsv = _dg(dh_ref[...], w_ref[...], _NT)
        a = ab_ref[:, :F].astype(F32)
        b = ab_ref[:, F:].astype(F32)
        sg = _sigmoid(a)
        o_ref[:, :F] = (dsv * b * (sg * (1.0 + a * (1.0 - sg)))).astype(o_ref.dtype)
        o_ref[:, F:] = (dsv * (a * sg)).astype(o_ref.dtype)

    return pl.pallas_call(
        body, name=name, grid=(T // tm,),
        in_specs=[pl.BlockSpec((tm, Dm), lambda i: (i, 0)), pl.BlockSpec((F, Dm), lambda i: (0, 0)),
                  pl.BlockSpec((tm, F2), lambda i: (i, 0))],
        out_specs=pl.BlockSpec((tm, F2), lambda i: (i, 0)),
        out_shape=jax.ShapeDtypeStruct((T, F2), CDT),
        compiler_params=_cparams(("parallel",)),
    )(dh_c, wd, ab)


SUBLANES = 8


def _shifted_copies(buf, shifted, tm):
    n = tm + CONV_HALO - SUBLANES
    for b in range(1, SUBLANES):
        shifted[b - 1] = buf[pl.ds(b, n), :]


CONV_ROWS = 64
CONV_LANES = 128


def _rows_from(buf, shifted, offset, rows, r0, cols):
    a, b = divmod(offset, SUBLANES)
    if b == 0:
        return buf[pl.ds(r0 + SUBLANES * a, rows), cols]
    return shifted[b - 1, pl.ds(r0 + SUBLANES * a, rows), cols]


def _conv_fwd(a, w_dw, b_dw, ln_g, ln_b, *, tm=256):
    T = a.shape[0]
    Dm = D_MODEL

    def body(a_ref, w_ref, bdw_ref, g_ref, b_ref, s_ref, u_ref, c_ref, ubuf, shifted):
        i = pl.program_id(0)

        @pl.when(i == 0)
        def _():
            ubuf[0:CONV_HALO, :] = jnp.zeros((CONV_HALO, Dm), F32)

        @pl.when(i > 0)
        def _():
            ubuf[0:CONV_HALO, :] = ubuf[tm:tm + CONV_HALO, :]

        u = a_ref[:, :Dm] * _sigmoid(a_ref[:, Dm:])
        ubuf[CONV_HALO:CONV_HALO + tm, :] = u
        u_ref[...] = u
        _shifted_copies(ubuf, shifted, tm)
        acc = jnp.zeros((tm, Dm), F32) + bdw_ref[...]
        for k in range(CONV_WIDTH):
            acc = acc + w_ref[k:k + 1, :] * _rows_from(ubuf, shifted, CONV_HALO - (CONV_WIDTH - 1) + k, tm, 0, slice(None))
        c_ref[...] = acc
        mu = jnp.mean(acc, axis=-1, keepdims=True)
        cen = acc - mu
        var = jnp.mean(cen * cen, axis=-1, keepdims=True)
        l = cen * lax.rsqrt(var + EPS) * g_ref[...] + b_ref[...]
        s_ref[...] = (l * _sigmoid(l)).astype(s_ref.dtype)

    row = pl.BlockSpec((tm, Dm), lambda i: (i, 0))
    vec = pl.BlockSpec((1, Dm), lambda i: (0, 0))
    return pl.pallas_call(
        body, name="conv_fwd", grid=(T // tm,),
        in_specs=[pl.BlockSpec((tm, 2 * Dm), lambda i: (i, 0)), pl.BlockSpec((CONV_HALO, Dm), lambda i: (0, 0)), vec, vec, vec],
        out_specs=[row, row, row],
        out_shape=[jax.ShapeDtypeStruct((T, Dm), CDT), jax.ShapeDtypeStruct((T, Dm), F32), jax.ShapeDtypeStruct((T, Dm), F32)],
        scratch_shapes=[pltpu.VMEM((tm + CONV_HALO, Dm), F32), pltpu.VMEM((SUBLANES - 1, tm + CONV_HALO - SUBLANES, Dm), F32)],
        compiler_params=_cparams(("arbitrary",)),
    )(a, w_dw, b_dw, ln_g, ln_b)


def _conv_bwd(ds, c, u, a, w_dw, ln_g, ln_b, sends, *, tm=256):
    T = a.shape[0]
    Dm = D_MODEL
    nt = T // tm
    ns = len(sends)

    def body(*refs):
        ds_ref, c_ref, u_ref, a_ref, w_ref, g_ref, b_ref = refs[:7]
        da_ref, db1_ref, dw_ref, dbdw_ref, dg_ref, dbln_ref = refs[7 + ns:13 + ns]
        dcbuf, shifted, du_scr = refs[13 + 2 * ns:16 + 2 * ns]
        x_start, x_finish = _scatter_phases(refs[7:7 + ns], refs[13 + ns:13 + 2 * ns], *refs[16 + 2 * ns:])
        i = pl.program_id(0)
        pl.when(i == 0)(x_start)

        @pl.when(i == 0)
        def _():
            dcbuf[tm:tm + CONV_HALO, :] = jnp.zeros((CONV_HALO, Dm), F32)
            db1_ref[...] = jnp.zeros_like(db1_ref)
            dw_ref[...] = jnp.zeros_like(dw_ref)
            dbdw_ref[...] = jnp.zeros_like(dbdw_ref)
            dg_ref[...] = jnp.zeros_like(dg_ref)
            dbln_ref[...] = jnp.zeros_like(dbln_ref)

        @pl.when(i > 0)
        def _():
            dcbuf[tm:tm + CONV_HALO, :] = dcbuf[0:CONV_HALO, :]

        cv = c_ref[...]
        mu = jnp.mean(cv, axis=-1, keepdims=True)
        cen = cv - mu
        var = jnp.mean(cen * cen, axis=-1, keepdims=True)
        rstd = lax.rsqrt(var + EPS)
        n = cen * rstd
        l = n * g_ref[...] + b_ref[...]
        sg = _sigmoid(l)
        dl = ds_ref[...] * (sg * (1.0 + l * (1.0 - sg)))
        dg_ref[...] += jnp.sum(dl * n, axis=0, keepdims=True)
        dbln_ref[...] += jnp.sum(dl, axis=0, keepdims=True)
        dn = dl * g_ref[...]
        dc = rstd * (dn - jnp.mean(dn, axis=-1, keepdims=True) - n * jnp.mean(dn * n, axis=-1, keepdims=True))
        dbdw_ref[...] += jnp.sum(dc, axis=0, keepdims=True)
        dcbuf[0:tm, :] = dc
        _shifted_copies(dcbuf, shifted, tm)
        groups = CONV_ROWS // SUBLANES
        for cb in range(Dm // CONV_LANES):
            cols = slice(cb * CONV_LANES, (cb + 1) * CONV_LANES)

            def block(r, dw_part, cols=cols):
                r0 = pl.multiple_of(r * CONV_ROWS, CONV_ROWS)
                uv = u_ref[pl.ds(r0, CONV_ROWS), cols]
                du_b = jnp.zeros((CONV_ROWS, CONV_LANES), F32)
                out = []
                for k in range(CONV_WIDTH):
                    slab = _rows_from(dcbuf, shifted, CONV_WIDTH - 1 - k, CONV_ROWS, r0, cols)
                    du_b = du_b + w_ref[k:k + 1, cols] * slab
                    out.append(dw_part[k] + jnp.sum((slab * uv).reshape(groups, SUBLANES, CONV_LANES), axis=0))
                du_scr[pl.ds(r0, CONV_ROWS), cols] = du_b
                return tuple(out)

            zero = jnp.zeros((SUBLANES, CONV_LANES), F32)
            dw_part = lax.fori_loop(0, tm // CONV_ROWS, block, (zero,) * CONV_WIDTH)
            for k in range(CONV_WIDTH):
                dw_ref[k:k + 1, cols] += jnp.sum(dw_part[k], axis=0, keepdims=True)
        du = du_scr[...]
        a1 = a_ref[:, :Dm]
        s2 = _sigmoid(a_ref[:, Dm:])
        da1 = du * s2
        da2 = du * a1 * (s2 * (1.0 - s2))
        da_ref[:, :Dm] = da1.astype(da_ref.dtype)
        da_ref[:, Dm:] = da2.astype(da_ref.dtype)
        db1_ref[:, :Dm] += jnp.sum(da1, axis=0, keepdims=True)
        db1_ref[:, Dm:] += jnp.sum(da2, axis=0, keepdims=True)
        pl.when(i == nt - 1)(x_finish)

    rev = lambda i: (nt - 1 - i, 0)
    row = pl.BlockSpec((tm, Dm), rev)
    row2 = pl.BlockSpec((tm, 2 * Dm), rev)
    vec = pl.BlockSpec((1, Dm), lambda i: (0, 0))
    vec2 = pl.BlockSpec((1, 2 * Dm), lambda i: (0, 0))
    taps = pl.BlockSpec((CONV_HALO, Dm), lambda i: (0, 0))
    anyspec = pl.BlockSpec(memory_space=pl.ANY)
    outs = pl.pallas_call(
        body, name="conv_bwd", grid=(nt,),
        in_specs=[row, row, row, row2, taps, vec, vec] + [anyspec] * ns,
        out_specs=[row2, vec2, taps, vec, vec, vec] + [anyspec] * ns,
        out_shape=[jax.ShapeDtypeStruct((T, 2 * Dm), CDT), jax.ShapeDtypeStruct((1, 2 * Dm), F32),
                   jax.ShapeDtypeStruct((CONV_HALO, Dm), F32), jax.ShapeDtypeStruct((1, Dm), F32),
                   jax.ShapeDtypeStruct((1, Dm), F32), jax.ShapeDtypeStruct((1, Dm), F32)]
        + [jax.ShapeDtypeStruct(s.shape, s.dtype) for s in sends],
        scratch_shapes=[pltpu.VMEM((tm + CONV_HALO, Dm), F32), pltpu.VMEM((SUBLANES - 1, tm + CONV_HALO - SUBLANES, Dm), F32),
                        pltpu.VMEM((tm, Dm), F32)]
        + _exchange_sems(ns),
        compiler_params=_cparams(("arbitrary",)),
    )(ds, c, u, a, w_dw, ln_g, ln_b, *sends)
    return outs[:6], outs[6:]


def _gla_head_masks(width, per_head):
    lane = lax.broadcasted_iota(jnp.int32, (1, width), 1)
    return [((lane >= h * per_head) & (lane < (h + 1) * per_head)).astype(F32) for h in range(GLA_HEADS)]


def _gla_specs(tm, order):
    return [pl.BlockSpec((tm, GLA_DK), lambda i: (order(i), 0)),
            pl.BlockSpec((tm, GLA_DK), lambda i: (order(i), 1)),
            pl.BlockSpec((tm, GLA_DV), lambda i: (order(i), 1)),
            pl.BlockSpec((tm, GLA_DV), lambda i: (order(i), 2)),
            pl.BlockSpec((tm, 128), lambda i: (order(i), 3072 // 128))]


def _gla_chunk_decay(la_c, tri):
    bc = _dot_exact_lhs(tri, la_c, 3)
    b_end = bc[CHUNK - 1:CHUNK, :]
    return b_end, jnp.exp(b_end - bc)


def _gla_fwd(proj, wg2p, b_gate, g_gla, *, tm=256):
    T = proj.shape[0]
    ncs = tm // CHUNK
    scale = GLA_HEAD_K ** -0.5

    def body(q_ref, k_ref, v_ref, r_ref, glr_ref, wg_ref, bg_ref, gg_ref, o_ref, oraw_ref, st_ref, s_scr):
        i = pl.program_id(0)

        @pl.when(i == 0)
        def _():
            s_scr[...] = jnp.zeros_like(s_scr)

        mk = _gla_head_masks(GLA_DK, GLA_HEAD_K)
        rr = lax.broadcasted_iota(jnp.int32, (CHUNK, CHUNK), 0)
        cc = lax.broadcasted_iota(jnp.int32, (CHUNK, CHUNK), 1)
        tri = (cc <= rr).astype(BF16)
        y = _dg(glr_ref[...], wg_ref[...], _NN) + bg_ref[...]
        la = _log_sigmoid(y) / GLA_GATE_NORMALIZER
        qs = q_ref[...] * scale
        for ci in range(ncs):
            rows = slice(ci * CHUNK, (ci + 1) * CHUNK)
            b_end, dec = _gla_chunk_decay(la[rows], tri)
            kend = (k_ref[rows, :] * dec).astype(CDT)
            upd = jnp.zeros((GLA_HEAD_V, GLA_DK), F32)
            for h in range(GLA_HEADS):
                vh = v_ref[rows, h * GLA_HEAD_V:(h + 1) * GLA_HEAD_V]
                upd = upd + mk[h] * _dg(vh, kend, _TN)
            s_new = jnp.exp(b_end) * s_scr[...] + upd
            s_scr[...] = s_new
            st_ref[ci] = s_new
            s_c = s_new.astype(CDT)
            for h in range(GLA_HEADS):
                o_h = _dg(qs[rows] * mk[h], s_c, _NT)
                oraw_ref[rows, h * GLA_HEAD_V:(h + 1) * GLA_HEAD_V] = o_h
        for h in range(GLA_HEADS):
            cols = slice(h * GLA_HEAD_V, (h + 1) * GLA_HEAD_V)
            o_h = oraw_ref[:, cols]
            rs = lax.rsqrt(jnp.mean(o_h * o_h, axis=-1, keepdims=True) + EPS)
            rg = r_ref[:, cols]
            o_ref[:, cols] = (o_h * rs * gg_ref[...] * (rg * _sigmoid(rg))).astype(o_ref.dtype)

    full = lambda shape: pl.BlockSpec(shape, lambda i: tuple(0 for _ in shape))
    return pl.pallas_call(
        body, name="gla_fwd", grid=(T // tm,),
        in_specs=_gla_specs(tm, lambda i: i) + [full((128, GLA_DK)), full((1, GLA_DK)), full((1, GLA_HEAD_V))],
        out_specs=[pl.BlockSpec((tm, GLA_DV), lambda i: (i, 0)), pl.BlockSpec((tm, GLA_DV), lambda i: (i, 0)),
                   pl.BlockSpec((ncs, GLA_HEAD_V, GLA_DK), lambda i: (i, 0, 0))],
        out_shape=[jax.ShapeDtypeStruct((T, GLA_DV), CDT), jax.ShapeDtypeStruct((T, GLA_DV), F32),
                   jax.ShapeDtypeStruct((T // CHUNK, GLA_HEAD_V, GLA_DK), F32)],
        scratch_shapes=[pltpu.VMEM((GLA_HEAD_V, GLA_DK), F32)],
        compiler_params=_cparams(("arbitrary",)),
    )(proj, proj, proj, proj, proj, wg2p, b_gate, g_gla)


def _gla_bwd(d_o, proj, oraw, states, wg2p, b_gate, g_gla, *, tm=256):
    T = proj.shape[0]
    nt = T // tm
    ncs = tm // CHUNK
    scale = GLA_HEAD_K ** -0.5

    def body(do_ref, q_ref, k_ref, v_ref, r_ref, glr_ref, oraw_ref, st_ref, stp_ref, wg_ref, bg_ref, gg_ref,
             dgla_ref, dglr_ref, dwg_ref, dbg_ref, dgg_ref, ds_scr, dy_scr, dor_scr):
        i = pl.program_id(0)
        tile = nt - 1 - i

        @pl.when(i == 0)
        def _():
            ds_scr[...] = jnp.zeros_like(ds_scr)
            dwg_ref[...] = jnp.zeros_like(dwg_ref)
            dbg_ref[...] = jnp.zeros_like(dbg_ref)
            dgg_ref[...] = jnp.zeros_like(dgg_ref)

        mk = _gla_head_masks(GLA_DK, GLA_HEAD_K)
        rr = lax.broadcasted_iota(jnp.int32, (CHUNK, CHUNK), 0)
        cc = lax.broadcasted_iota(jnp.int32, (CHUNK, CHUNK), 1)
        tri = (cc <= rr).astype(BF16)
        tri_t = (cc >= rr).astype(BF16)
        last_row = (lax.broadcasted_iota(jnp.int32, (CHUNK, 1), 0) == CHUNK - 1).astype(F32)

        dgg = jnp.zeros((1, GLA_HEAD_V), F32)
        for h in range(GLA_HEADS):
            cols = slice(h * GLA_HEAD_V, (h + 1) * GLA_HEAD_V)
            o_h = oraw_ref[:, cols]
            rs = lax.rsqrt(jnp.mean(o_h * o_h, axis=-1, keepdims=True) + EPS)
            rg = r_ref[:, cols]
            sg = _sigmoid(rg)
            dov = do_ref[:, cols]
            on = o_h * rs * gg_ref[...]
            d_on = dov * (rg * sg)
            dgla_ref[:, 2 * GLA_DK + GLA_DV + h * GLA_HEAD_V:2 * GLA_DK + GLA_DV + (h + 1) * GLA_HEAD_V] = (
                dov * on * (sg * (1.0 + rg * (1.0 - sg)))).astype(dgla_ref.dtype)
            dgg = dgg + jnp.sum(d_on * o_h * rs, axis=0, keepdims=True)
            uu = d_on * gg_ref[...]
            dor_scr[:, cols] = rs * uu - o_h * (rs * rs * rs * jnp.mean(uu * o_h, axis=-1, keepdims=True))
        dgg_ref[...] += dgg

        y = _dg(glr_ref[...], wg_ref[...], _NN) + bg_ref[...]
        la = _log_sigmoid(y) / GLA_GATE_NORMALIZER
        qs = q_ref[...] * scale
        for ci in reversed(range(ncs)):
            rows = slice(ci * CHUNK, (ci + 1) * CHUNK)
            b_end, dec = _gla_chunk_decay(la[rows], tri)
            decay = jnp.exp(b_end)
            kend = k_ref[rows, :] * dec
            kend_c = kend.astype(CDT)
            s_c = st_ref[ci].astype(CDT)
            if ci > 0:
                s_prev = st_ref[ci - 1]
            else:
                s_prev = jnp.where(tile > 0, stp_ref[0], 0.0)
            dqs = jnp.zeros((CHUNK, GLA_DK), F32)
            dst = ds_scr[...]
            for h in range(GLA_HEADS):
                do_h = dor_scr[rows, h * GLA_HEAD_V:(h + 1) * GLA_HEAD_V].astype(CDT)
                dqs = dqs + mk[h] * _dg(do_h, s_c, _NN)
                dst = dst + mk[h] * _dg(do_h, qs[rows], _TN)
            d_decay = jnp.sum(dst * s_prev, axis=0, keepdims=True)
            ds_scr[...] = decay * dst
            dst_c = dst.astype(CDT)
            dkend = jnp.zeros((CHUNK, GLA_DK), F32)
            for h in range(GLA_HEADS):
                cols = slice(h * GLA_HEAD_V, (h + 1) * GLA_HEAD_V)
                dv_h = _dg(kend * mk[h], dst_c, _NT)
                dgla_ref[rows, 2 * GLA_DK + h * GLA_HEAD_V:2 * GLA_DK + (h + 1) * GLA_HEAD_V] = dv_h.astype(dgla_ref.dtype)
                dkend = dkend + mk[h] * _dg(v_ref[rows, cols], dst_c, _NN)
            dgla_ref[rows, 0:GLA_DK] = (dqs * scale).astype(dgla_ref.dtype)
            dgla_ref[rows, GLA_DK:2 * GLA_DK] = (dkend * dec).astype(dgla_ref.dtype)
            mm = dkend * kend
            db_end = jnp.sum(mm, axis=0, keepdims=True) + d_decay * decay
            dbc = last_row * db_end - mm
            dla = _dot_exact_lhs(tri_t, dbc, 3)
            dy_scr[rows, :] = dla * (1.0 / GLA_GATE_NORMALIZER) * _sigmoid(-y[rows])
        dy = dy_scr[...]
        dbg_ref[...] += jnp.sum(dy, axis=0, keepdims=True)
        dwg_ref[...] += _dg(glr_ref[...], dy, _TN)
        dglr_ref[...] = _dg(dy, wg_ref[...], _NT).astype(dglr_ref.dtype)

    rev = lambda i: nt - 1 - i
    full = lambda shape: pl.BlockSpec(shape, lambda i: tuple(0 for _ in shape))
    st_spec = pl.BlockSpec((ncs, GLA_HEAD_V, GLA_DK), lambda i: (rev(i), 0, 0))
    stp_spec = pl.BlockSpec((1, GLA_HEAD_V, GLA_DK), lambda i: (jnp.maximum(rev(i) * ncs - 1, 0), 0, 0))
    return pl.pallas_call(
        body, name="gla_bwd", grid=(nt,),
        in_specs=[pl.BlockSpec((tm, GLA_DV), lambda i: (rev(i), 0))] + _gla_specs(tm, rev)
        + [pl.BlockSpec((tm, GLA_DV), lambda i: (rev(i), 0)), st_spec, stp_spec,
           full((128, GLA_DK)), full((1, GLA_DK)), full((1, GLA_HEAD_V))],
        out_specs=[pl.BlockSpec((tm, 2 * GLA_DK + 2 * GLA_DV), lambda i: (rev(i), 0)),
                   pl.BlockSpec((tm, 128), lambda i: (rev(i), 0)),
                   full((128, GLA_DK)), full((1, GLA_DK)), full((1, GLA_HEAD_V))],
        out_shape=[jax.ShapeDtypeStruct((T, 2 * GLA_DK + 2 * GLA_DV), CDT), jax.ShapeDtypeStruct((T, 128), CDT),
                   jax.ShapeDtypeStruct((128, GLA_DK), F32), jax.ShapeDtypeStruct((1, GLA_DK), F32),
                   jax.ShapeDtypeStruct((1, GLA_HEAD_V), F32)],
        scratch_shapes=[pltpu.VMEM((GLA_HEAD_V, GLA_DK), F32), pltpu.VMEM((tm, GLA_DK), F32),
                        pltpu.VMEM((tm, GLA_DV), F32)],
        compiler_params=_cparams(("arbitrary",)),
    )(d_o, proj, proj, proj, proj, proj, oraw, states, states, wg2p, b_gate, g_gla)


def _head_mean_matrix():
    r = lax.broadcasted_iota(jnp.int32, (SB_D, SB_D), 0) // SB_HEAD_DIM
    c = lax.broadcasted_iota(jnp.int32, (SB_D, SB_D), 1) // SB_HEAD_DIM
    return jnp.where(r == c, 1.0 / SB_HEAD_DIM, 0.0).astype(BF16)


def _sb_prep(proj, gq, gk, *, tm=256):
    T = proj.shape[0]
    scale = SB_HEAD_DIM ** -0.5

    def body(q_ref, k_ref, v_ref, gq_ref, gk_ref, qn_ref, kn_ref, vb_ref):
        hm = _head_mean_matrix()
        qv, kv = q_ref[...], k_ref[...]
        rq = lax.rsqrt(_dot_exact_rhs(qv * qv, hm, SB_SPLIT_NORM) + EPS)
        rk = lax.rsqrt(_dot_exact_rhs(kv * kv, hm, SB_SPLIT_NORM) + EPS)
        qn_ref[...] = (qv * rq * gq_ref[...] * scale).astype(qn_ref.dtype)
        kn_ref[...] = (kv * rk * gk_ref[...]).astype(kn_ref.dtype)
        vb_ref[...] = v_ref[...].astype(vb_ref.dtype)

    col = lambda j: pl.BlockSpec((tm, SB_D), lambda i: (i, j))
    vec = pl.BlockSpec((1, SB_D), lambda i: (0, 0))
    out = pl.BlockSpec((tm, SB_D), lambda i: (i, 0))
    return pl.pallas_call(
        body, name="sb_prep", grid=(T // tm,),
        in_specs=[col(3), col(4), col(5), vec, vec], out_specs=[out, out, out],
        out_shape=[jax.ShapeDtypeStruct((T, SB_D), CDT)] * 3,
        compiler_params=_cparams(("parallel",)),
    )(proj, proj, proj, gq, gk)


def _sb_prep_bwd(dqn, dkn, dv, proj, gq, gk, *, tm=256):
    T = proj.shape[0]
    scale = SB_HEAD_DIM ** -0.5

    def body(dqn_ref, dkn_ref, dv_ref, q_ref, k_ref, gq_ref, gk_ref, dsb_ref, dgq_ref, dgk_ref):
        i = pl.program_id(0)

        @pl.when(i == 0)
        def _():
            dgq_ref[...] = jnp.zeros_like(dgq_ref)
            dgk_ref[...] = jnp.zeros_like(dgk_ref)

        hm = _head_mean_matrix()

        def one(dn_ref, x_ref, g_ref, dg_ref, sc, lo):
            xv = x_ref[...]
            dnv = dn_ref[...] * sc
            r = lax.rsqrt(_dot_exact_rhs(xv * xv, hm, SB_SPLIT_NORM) + EPS)
            u = dnv * g_ref[...]
            dot = _dot_exact_rhs(u * xv, hm, SB_SPLIT_NORM)
            dsb_ref[:, lo:lo + SB_D] = (r * u - xv * (r * r * r * dot)).astype(dsb_ref.dtype)
            dg_ref[...] += jnp.sum(dnv * xv * r, axis=0, keepdims=True)

        one(dqn_ref, q_ref, gq_ref, dgq_ref, scale, 0)
        one(dkn_ref, k_ref, gk_ref, dgk_ref, 1.0, SB_D)
        dsb_ref[:, 2 * SB_D:3 * SB_D] = dv_ref[...].astype(dsb_ref.dtype)

    col = lambda j: pl.BlockSpec((tm, SB_D), lambda i: (i, j))
    vec = pl.BlockSpec((1, SB_D), lambda i: (0, 0))
    row = pl.BlockSpec((tm, SB_D), lambda i: (i, 0))
    return pl.pallas_call(
        body, name="sb_prep_bwd", grid=(T // tm,),
        in_specs=[row, row, row, col(3), col(4), vec, vec],
        out_specs=[pl.BlockSpec((tm, 3 * SB_D), lambda i: (i, 0)), vec, vec],
        out_shape=[jax.ShapeDtypeStruct((T, 3 * SB_D), CDT), jax.ShapeDtypeStruct((1, SB_D), F32),
                   jax.ShapeDtypeStruct((1, SB_D), F32)],
        compiler_params=_cparams(("arbitrary",)),
    )(dqn, dkn, dv, proj, proj, gq, gk)


def _sb_masks():
    lane = lax.broadcasted_iota(jnp.int32, (1, 128), 1)
    m = [lane < SB_HEAD_DIM, lane >= SB_HEAD_DIM]
    return m, [x.astype(F32) for x in m]


def _sb_fwd(qn, kn, vb, blocks):
    T = qn.shape[0]
    nq = T // SB_TILE
    B, P = SB_TILE, SB_PAIR
    hs = range(2)
    n = len(blocks)
    nhp = SB_D // P

    def body(*refs):
        q_ref, k_ref, v_ref = refs[:3]
        o_ref, l_ref, done_ref = refs[3 + n:6 + n]
        acc_ref = refs[6 + 2 * n]
        hp, qb = pl.program_id(0), pl.program_id(1)
        g_start, g_forward, g_finish = _gather_phases(refs[3:3 + n], refs[6 + n:6 + 2 * n], *refs[7 + 2 * n:])
        pl.when((hp == 0) & (qb == 0))(g_start)
        pl.when((hp == nhp - 1) & (qb == 0))(g_forward)
        m, mf = _sb_masks()
        row = lax.broadcasted_iota(jnp.int32, (B, B), 0)
        col = lax.broadcasted_iota(jnp.int32, (B, B), 1)
        later = (row > col).astype(BF16)
        past = col < row
        q2 = q_ref[...]
        qm = [jnp.where(m[h], q2, jnp.zeros_like(q2)) for h in hs]
        acc_ref[...] = jnp.zeros_like(acc_ref)

        def keys(kb):
            return k_ref[pl.ds(pl.multiple_of(kb * B, B), B), :]

        def values(kb):
            return v_ref[pl.ds(pl.multiple_of(kb * B, B), B), :]

        def scores(kb):
            k2 = keys(kb)
            return [_dg(qm[h], k2, _NT) for h in hs]

        def run(tiles, R):
            zs, cum, rsum = {}, {}, {}
            for t, (z, _, diag) in enumerate(tiles):
                for h in hs:
                    sp = _softplus(z[h])
                    lk = jnp.where(past, sp, 0.0) if diag else sp
                    zs[t, h] = z[h] - sp
                    cum[t, h] = _dot_exact_rhs(lk, later, SB_SPLIT_LK)
                    rsum[t, h] = jnp.sum(lk, axis=1, keepdims=True)
            R = list(R)
            for t, (_, kb, diag) in enumerate(tiles):
                v2 = values(kb)
                for h in hs:
                    w = jnp.exp(zs[t, h] - (cum[t, h] + R[h]))
                    if diag:
                        w = jnp.where(past, w, 0.0)
                    acc_ref[h] += _dg(w, v2, _NN)
                R = [R[h] + rsum[t, h] for h in hs]
            return tuple(R)

        def live(r):
            return (jnp.minimum(jnp.min(r[0]), jnp.min(r[1])) < SB_DEAD).astype(jnp.int32)

        zero = jnp.zeros((B, 1), F32)
        R = lax.cond(qb > 0,
                     lambda r: run([(scores(qb), qb, True), (scores(jnp.maximum(qb - 1, 0)), jnp.maximum(qb - 1, 0), False)], r),
                     lambda r: run([(scores(qb), qb, True)], r), (zero, zero))
        rest = jnp.maximum(qb - 1, 0)
        npairs = rest // 2

        def pair(carry):
            i, r, za, zb = carry[0], carry[2:4], carry[4:6], carry[6:8]
            ka = qb - 2 - 2 * i
            nxt = (scores(jnp.maximum(ka - 2, 0)), scores(jnp.maximum(ka - 3, 0)))
            r = run([(za, ka, False), (zb, ka - 1, False)], r)
            return (i + 1, live(r), *r, *nxt[0], *nxt[1])

        out = lax.while_loop(lambda c: (c[0] < npairs) & (c[1] > 0), pair,
                             (jnp.int32(0), live(R), *R, *scores(jnp.maximum(qb - 2, 0)), *scores(jnp.maximum(qb - 3, 0))))
        last = (rest % 2 == 1) & (out[0] == npairs) & (out[1] > 0)
        tile0 = jnp.int32(0)
        R = lax.cond(last, lambda r: run([(scores(tile0), tile0, False)], r), lambda r: r, out[2:4])
        done_ref[hp, qb] = 2 * out[0] + last.astype(jnp.int32)
        o_ref[...] = (acc_ref[0] * mf[0] + acc_ref[1] * mf[1]).astype(o_ref.dtype)
        l_ref[0] = R[0] * mf[0] + R[1] * mf[1]
        pl.when((hp == nhp - 1) & (qb == nq - 1))(g_finish)

    slab = pl.BlockSpec((T, P), lambda hp, qb: (0, hp))
    blk = pl.BlockSpec((B, P), lambda hp, qb: (qb, hp))
    anyspec = pl.BlockSpec(memory_space=pl.ANY)
    outs = pl.pallas_call(
        body, name="sb_fwd", grid=(nhp, nq),
        in_specs=[blk, slab, slab] + [anyspec] * n,
        out_specs=[blk, pl.BlockSpec((1, B, P), lambda hp, qb: (hp, qb, 0)), pl.BlockSpec(memory_space=pltpu.SMEM)]
        + [anyspec] * n,
        out_shape=[jax.ShapeDtypeStruct((T, SB_D), CDT), jax.ShapeDtypeStruct((nhp, T, P), F32),
                   jax.ShapeDtypeStruct((nhp, nq), jnp.int32)]
        + [jax.ShapeDtypeStruct((N_DEV,) + b.shape, b.dtype) for b in blocks],
        scratch_shapes=[pltpu.VMEM((2, B, P), F32)] + _exchange_sems(n),
        compiler_params=_cparams(("arbitrary", "arbitrary")),
    )(qn, kn, vb, *blocks)
    return outs[0], outs[1], outs[2], outs[3:]


def _sb_bwd(d_o, qn, kn, vb, lsum, done):
    T = qn.shape[0]
    nq = T // SB_TILE
    B, P = SB_TILE, SB_PAIR
    hs = range(2)
    do_first = (d_o.shape[1] - SB_D) // P

    def body(do_ref, q_ref, k_ref, v_ref, l_ref, done_ref, dq_ref, dk_ref, dv_ref, dqacc_ref):
        qb = pl.program_id(1)

        @pl.when(qb == 0)
        def _():
            dk_ref[...] = jnp.zeros_like(dk_ref)
            dv_ref[...] = jnp.zeros_like(dv_ref)

        m, mf = _sb_masks()
        row = lax.broadcasted_iota(jnp.int32, (B, B), 0)
        col = lax.broadcasted_iota(jnp.int32, (B, B), 1)
        upto = (row <= col).astype(BF16)
        before = (row < col).astype(BF16)
        past = col < row
        q2 = q_ref[...]
        qm = [jnp.where(m[h], q2, jnp.zeros_like(q2)) for h in hs]
        do2 = do_ref[...]
        dom = [jnp.where(m[h], do2, 0.0).astype(CDT) for h in hs]
        lb = l_ref[0]
        ltot = [lb[:, 0:1], lb[:, SB_HEAD_DIM:SB_HEAD_DIM + 1]]
        dqacc_ref[...] = jnp.zeros_like(dqacc_ref)

        def rows(kb):
            return pl.ds(pl.multiple_of(kb * B, B), B)

        def scores(kb):
            k2, v2 = k_ref[rows(kb), :], v_ref[rows(kb), :]
            return [_dg(qm[h], k2, _NT) for h in hs] + [_dg(dom[h], v2, _NT) for h in hs]

        def run(tiles, carry):
            Ps, Pg = list(carry[0]), list(carry[1])
            zs, sp_, cum, rest = {}, {}, {}, {}
            for t, tl in enumerate(tiles):
                diag = tl[5]
                for h in hs:
                    sp = _softplus(tl[h])
                    lk = jnp.where(past, sp, 0.0) if diag else sp
                    zs[t, h], sp_[t, h] = tl[h] - sp, sp
                    cum[t, h] = _dot_exact_rhs(lk, upto, SB_SPLIT_LK)
                    rest[t, h] = ltot[h] - Ps[h]
                    Ps[h] = Ps[h] + jnp.sum(lk, axis=1, keepdims=True)
            w, g, gx = {}, {}, {}
            for t, tl in enumerate(tiles):
                diag = tl[5]
                for h in hs:
                    wt = jnp.exp(zs[t, h] - (rest[t, h] - cum[t, h]))
                    if diag:
                        wt = jnp.where(past, wt, 0.0)
                    w[t, h] = wt
                    g[t, h] = wt * tl[2 + h]
                    gx[t, h] = _dot_exact_rhs(g[t, h], before, SB_SPLIT_G) + Pg[h]
                    Pg[h] = Pg[h] + jnp.sum(g[t, h], axis=1, keepdims=True)
            for t, tl in enumerate(tiles):
                kb, diag = tl[4], tl[5]
                k2 = k_ref[rows(kb), :]
                for h in hs:
                    sneg = jnp.exp(-sp_[t, h])
                    dz = g[t, h] * sneg - (1.0 - sneg) * gx[t, h]
                    if diag:
                        dz = jnp.where(past, dz, 0.0)
                    dz_c = dz.astype(CDT)
                    dv_ref[rows(kb), :] += _dg(w[t, h], dom[h], _TN)
                    dk_ref[rows(kb), :] += _dg(dz_c, qm[h], _TN)
                    dqacc_ref[h] += _dg(dz_c, k2, _NN)
            return tuple(Ps), tuple(Pg)

        zero = jnp.zeros((B, 1), F32)
        rest_tiles = jnp.maximum(qb - 1, 0)
        done = jnp.clip(done_ref[pl.program_id(0), qb], 0, rest_tiles)
        npairs = done // 2
        tile0 = jnp.int32(0)
        sums = lax.cond(done % 2 == 1, lambda s: run([(*scores(tile0), tile0, False)], s), lambda s: s,
                        ((zero, zero), (zero, zero)))
        left0 = qb - 1 - 2 * npairs

        def pair(i, carry):
            sums, ta, tb = (carry[0:2], carry[2:4]), carry[4:8], carry[8:12]
            ka = left0 + 2 * i
            nxt = scores(jnp.minimum(ka + 2, qb)) + scores(jnp.minimum(ka + 3, qb))
            Ps, Pg = run([(*ta, ka, False), (*tb, ka + 1, False)], sums)
            return (*Ps, *Pg, *nxt)

        first = jnp.clip(left0, 0, qb)
        out = lax.fori_loop(0, npairs, pair,
                            (*sums[0], *sums[1], *scores(first), *scores(jnp.minimum(first + 1, qb))))
        sums = (out[0:2], out[2:4])
        near = jnp.maximum(qb - 1, 0)
        lax.cond(qb > 0,
                 lambda s: run([(*scores(near), near, False), (*scores(qb), qb, True)], s),
                 lambda s: run([(*scores(qb), qb, True)], s), sums)
        dq_ref[...] = dqacc_ref[0] * mf[0] + dqacc_ref[1] * mf[1]

    slab = pl.BlockSpec((T, P), lambda hp, qb: (0, hp))
    blk = pl.BlockSpec((B, P), lambda hp, qb: (qb, hp))
    return pl.pallas_call(
        body, name="sb_bwd", grid=(SB_D // P, nq),
        in_specs=[pl.BlockSpec((B, P), lambda hp, qb: (qb, hp + do_first)), blk, slab, slab,
                  pl.BlockSpec((1, B, P), lambda hp, qb: (hp, qb, 0)),
                  pl.BlockSpec(memory_space=pltpu.SMEM)],
        out_specs=[blk, slab, slab],
        out_shape=[jax.ShapeDtypeStruct((T, SB_D), F32)] * 3,
        scratch_shapes=[pltpu.VMEM((2, B, P), F32)],
        compiler_params=_cparams(("arbitrary", "arbitrary")),
    )(d_o, qn, kn, vb, lsum, done)


def _regroup_in_rows(wt):
    cut = 2 * GLA_DK + 2 * GLA_DV
    pad = jnp.zeros((IN_PAD - IN_WIDTH, wt.shape[1]), wt.dtype)
    return jnp.concatenate([wt[:cut], wt[cut + GLA_GATE_RANK:], wt[cut:cut + GLA_GATE_RANK], pad], axis=0)


def _ungroup_in_rows(gt):
    cut = 2 * GLA_DK + 2 * GLA_DV
    return jnp.concatenate([gt[:cut], gt[3072:3072 + GLA_GATE_RANK], gt[cut:3072]], axis=0)


def _colsum(v, *, name, tm=512):
    T, C = v.shape

    def body(v_ref, o_ref):
        i = pl.program_id(0)
        part = jnp.sum(v_ref[...], axis=0, keepdims=True)

        @pl.when(i == 0)
        def _():
            o_ref[...] = part

        @pl.when(i > 0)
        def _():
            o_ref[...] += part

    return pl.pallas_call(
        body, name=name, grid=(T // tm,),
        in_specs=[pl.BlockSpec((tm, C), lambda i: (i, 0))], out_specs=pl.BlockSpec((1, C), lambda i: (0, 0)),
        out_shape=jax.ShapeDtypeStruct((1, C), F32),
        compiler_params=_cparams(("arbitrary",)),
    )(v)


def _ffn_fwd(h, g_norm, wgu_t, wd, tag):
    hf = _rms_fwd(h, g_norm, name=f"ffn{tag}_norm")
    ab, s = _swiglu_up(hf, *wgu_t, name=f"ffn{tag}_up")
    h_out = _matmul(s, wd, mode='nn', out_dtype=F32, name=f"ffn{tag}_down", tm=512, tn=D_MODEL, tk=D_FF, residual=h)
    return h_out, (hf, ab, s)


def _ffn_bwd(dh, dh_c, h_in, g_norm, wgu_t, wd, saved, tag):
    hf, ab, s = saved
    dwd = _matmul(s, dh_c, mode='tn', out_dtype=F32, name=f"ffn{tag}_dwd", tm=D_FF // 2, tn=D_MODEL, tk=TK_TOKENS)
    dab = _swiglu_dact(dh_c, wd, ab, name=f"ffn{tag}_dact")
    dwgu_t = _matmul(dab, hf, mode='tn', out_dtype=F32, name=f"ffn{tag}_dwgu", tm=D_FF // 2, tn=D_MODEL, tk=TK_TOKENS)
    dhf = _swiglu_dhf(dab, *wgu_t, name=f"ffn{tag}_dhf")
    dh_in, dh_in_c, dg = _rms_bwd(dhf, h_in, g_norm, dh, name=f"ffn{tag}_dnorm")
    return dh_in, dh_in_c, dwgu_t, dwd, dg


def _late_weights(gathered):
    g_out, g_pw1, g_pw2, g_gate, g_up, g_down = gathered
    Dm = D_MODEL
    return {
        'hy_w_out': g_out.reshape(Dm, Dm),
        'cv_w_pw1_t': g_pw1.reshape(2 * Dm, Dm),
        'cv_w_pw2': g_pw2.reshape(Dm, Dm),
        'ffn_wgu_t': [(g_gate[:, l].reshape(D_FF, Dm), g_up[:, l].reshape(D_FF, Dm)) for l in range(2)],
        'ffn_w_down': [g_down[:, l].reshape(D_FF, Dm) for l in range(2)],
    }


def _local_step(x, tgt, W, late_blocks):
    row = lambda v: v.reshape(1, -1)
    win_p = _regroup_in_rows(W['hy_w_in_t'])
    wg2p = jnp.pad(W['hy_w_gate2'], ((0, 128 - GLA_GATE_RANK), (0, 0)))
    b_gate = row(W['hy_b_gate'])
    g_gla = row(W['hy_gla_norm'])
    gq = jnp.tile(W['hy_sb_q_norm'].reshape(-1), SB_D // SB_HEAD_DIM).reshape(1, SB_D)
    gk = jnp.tile(W['hy_sb_k_norm'].reshape(-1), SB_D // SB_HEAD_DIM).reshape(1, SB_D)
    w_dw = jnp.pad(W['cv_w_dw'], ((0, CONV_HALO - CONV_WIDTH), (0, 0)))
    mixn = [row(W['mix_norm'][l]) for l in range(2)]
    ffnn = [row(W['ffn_norm'][l]) for l in range(2)]

    hn0 = _rms_fwd(x, mixn[0], name="mix0_norm")
    proj = _matmul(hn0, win_p, mode='nt', out_dtype=F32, name="hy_in", tm=256, tn=IN_PAD, tk=D_MODEL)
    o_gla, o_raw, states = _gla_fwd(proj, wg2p, b_gate, g_gla)
    qn, kn, vb = _sb_prep(proj, gq, gk)
    o_sb, lsum, sb_done, gathered = _sb_fwd(qn, kn, vb, late_blocks)
    W = {**W, **_late_weights(gathered)}
    w_out, wgu, wd = W['hy_w_out'], W['ffn_wgu_t'], W['ffn_w_down']
    o_mix = jnp.concatenate([o_gla, o_sb], axis=1)
    h1 = _matmul(o_mix, w_out, mode='nn', out_dtype=F32, name="hy_out", tm=512, tn=D_MODEL, tk=D_MODEL, residual=x)
    h2, ffn0_saved = _ffn_fwd(h1, ffnn[0], wgu[0], wd[0], 0)
    hn1 = _rms_fwd(h2, mixn[1], name="mix1_norm")
    a_cv = _matmul(hn1, W['cv_w_pw1_t'], mode='nt', out_dtype=F32, name="cv_pw1", tm=512, tn=2 * D_MODEL, tk=D_MODEL,
                   bias=row(W['cv_b_pw1']))
    s_cv, u_cv, c_cv = _conv_fwd(a_cv, w_dw, row(W['cv_b_dw']), row(W['cv_ln_g']), row(W['cv_ln_b']))
    h3 = _matmul(s_cv, W['cv_w_pw2'], mode='nn', out_dtype=F32, name="cv_pw2", tm=512, tn=D_MODEL, tk=D_MODEL,
                 bias=row(W['cv_b_pw2']), residual=h2)
    h4, ffn1_saved = _ffn_fwd(h3, ffnn[1], wgu[1], wd[1], 1)
    sq_err, dy, dy_c = _loss_head(h4, tgt)

    G = {}
    dh3, dh3_c, dwgu1, dwd1, dg_ffn1 = _ffn_bwd(dy, dy_c, h3, ffnn[1], wgu[1], wd[1], ffn1_saved, 1)
    G['cv_b_pw2'] = _colsum(dh3, name="cv_db2")
    G['cv_w_pw2'] = _matmul(s_cv, dh3_c, mode='tn', out_dtype=F32, name="cv_dw2", tm=D_MODEL, tn=D_MODEL, tk=TK_TOKENS)
    ds_cv = _matmul(dh3_c, W['cv_w_pw2'], mode='nt', out_dtype=F32, name="cv_ds", tm=512, tn=D_MODEL, tk=D_MODEL)
    F8 = D_FF // N_DEV
    early_own = [G['cv_w_pw2'].reshape(N_DEV, D_MODEL // N_DEV, D_MODEL),
                 dwgu1.reshape(2, N_DEV, F8, D_MODEL).transpose(1, 0, 2, 3).reshape(N_DEV, 2 * F8, D_MODEL),
                 dwd1.reshape(N_DEV, F8, D_MODEL)]
    (da_cv, db1, dwdw, dbdw, dlng, dlnb), early_recv = _conv_bwd(
        ds_cv, c_cv, u_cv, a_cv, w_dw, row(W['cv_ln_g']), row(W['cv_ln_b']), [a.astype(BF16) for a in early_own])
    G['cv_b_pw1'] = db1
    G['cv_w_dw'] = dwdw[:CONV_WIDTH]
    G['cv_b_dw'], G['cv_ln_g'], G['cv_ln_b'] = dbdw, dlng, dlnb
    G['cv_w_pw1_t'] = _matmul(da_cv, hn1, mode='tn', out_dtype=F32, name="cv_dw1", tm=D_MODEL, tn=D_MODEL, tk=TK_TOKENS)
    dhn1 = _matmul(da_cv, W['cv_w_pw1_t'], mode='nn', out_dtype=F32, name="cv_dhn", tm=512, tn=D_MODEL, tk=2 * D_MODEL)
    dh2, dh2_c, dg_mix1 = _rms_bwd(dhn1, h2, mixn[1], dh3, name="mix1_dnorm")
    dh1, dh1_c, dwgu0, dwd0, dg_ffn0 = _ffn_bwd(dh2, dh2_c, h1, ffnn[0], wgu[0], wd[0], ffn0_saved, 0)
    G['hy_w_out'] = _matmul(o_mix, dh1_c, mode='tn', out_dtype=F32, name="hy_dwout", tm=D_MODEL, tn=D_MODEL, tk=TK_TOKENS)
    d_omix = _matmul(dh1_c, w_out, mode='nt', out_dtype=F32, name="hy_domix", tm=512, tn=D_MODEL, tk=D_MODEL)
    dgla, dglr, dwg2, dbg, dgg = _gla_bwd(d_omix, proj, o_raw, states, wg2p, b_gate, g_gla)
    dqn, dkn, dvs = _sb_bwd(d_omix, qn, kn, vb, lsum, sb_done)
    dsb, dgq, dgk = _sb_prep_bwd(dqn, dkn, dvs, proj, gq, gk)
    dproj = jnp.concatenate([dgla, dsb, dglr], axis=1)
    dwin_p = _matmul(dproj, hn0, mode='tn', out_dtype=F32, name="hy_dwin", tm=IN_PAD // 5, tn=D_MODEL, tk=TK_TOKENS)
    dhn0 = _matmul(dproj, win_p, mode='nn', out_dtype=F32, name="hy_dhn", tm=256, tn=D_MODEL, tk=IN_PAD)
    dx, _, dg_mix0 = _rms_bwd(dhn0, x, mixn[0], dh1, name="mix0_dnorm")

    G['hy_w_in_t'] = _ungroup_in_rows(dwin_p)
    G['hy_w_gate2'] = dwg2[:GLA_GATE_RANK]
    G['hy_b_gate'] = dbg
    G['hy_gla_norm'] = dgg
    G['hy_sb_q_norm'] = dgq.reshape(SB_D // SB_HEAD_DIM, SB_HEAD_DIM).sum(axis=0, keepdims=True)
    G['hy_sb_k_norm'] = dgk.reshape(SB_D // SB_HEAD_DIM, SB_HEAD_DIM).sum(axis=0, keepdims=True)
    G['mix_norm'] = jnp.concatenate([dg_mix0, dg_mix1], axis=0)
    G['ffn_norm'] = jnp.concatenate([dg_ffn0, dg_ffn1], axis=0)
    G['ffn_wgu_t0'] = dwgu0
    G['ffn_w_down0'] = dwd0
    return sq_err, dx, G, (early_own, early_recv)


MESH_IDS = pl.DeviceIdType.MESH
N_PEER = N_DEV - 1


def _exchange_sems(n):
    return [pltpu.SemaphoreType.DMA((n * N_PEER,)), pltpu.SemaphoreType.DMA((n * N_PEER,)),
            pltpu.SemaphoreType.DMA((n,))]


def _gather_phases(x_refs, out_refs, send_sems, recv_sems, local_sems):
    n = len(x_refs)
    x, y, c = lax.axis_index("x"), lax.axis_index("y"), lax.axis_index("c")
    me, sibling = (x, y, c), (x, y, 1 - c)
    chips = [(1 - x, y), (x, 1 - y), (1 - x, 1 - y)]

    def slot(a, px, py, pc):
        return out_refs[a].at[4 * px + 2 * py + pc]

    def copy(a, k, blk, to, src=None):
        return pltpu.make_async_remote_copy(
            src_ref=slot(a, *blk) if src is None else src, dst_ref=slot(a, *blk),
            send_sem=send_sems.at[a * N_PEER + k], recv_sem=recv_sems.at[a * N_PEER + k],
            device_id=to, device_id_type=MESH_IDS)

    def local(a):
        return pltpu.make_async_copy(x_refs[a], slot(a, *me), local_sems.at[a])

    def first(a):
        return [copy(a, 0, me, sibling, src=x_refs[a])] + [copy(a, 1 + j, me, (*chip, c), src=x_refs[a])
                                                           for j, chip in enumerate(chips)]

    def passed(a):
        return [copy(a, 4 + j, (*chip, c), sibling) for j, chip in enumerate(chips)]

    def start():
        for a in range(n):
            local(a).start()
        for a in range(n):
            for cp in first(a):
                cp.start()

    def forward():
        for a in range(n):
            for j, chip in enumerate(chips):
                copy(a, 1 + j, (*chip, c), me).wait_recv()
                copy(a, 4 + j, (*chip, c), sibling).start()

    def finish():
        for a in range(n):
            copy(a, 0, sibling, me).wait_recv()
            for j, chip in enumerate(chips):
                copy(a, 4 + j, (*chip, 1 - c), me).wait_recv()
        for a in range(n):
            for cp in first(a) + passed(a):
                cp.wait_send()
            local(a).wait()

    return start, forward, finish


def _all_gather(blocks):
    n = len(blocks)

    def body(*refs):
        start, forward, finish = _gather_phases(refs[:n], refs[n:2 * n], *refs[2 * n:])
        start()
        forward()
        finish()

    anyspec = pl.BlockSpec(memory_space=pl.ANY)
    return pl.pallas_call(
        body, name="fsdp_all_gather",
        out_shape=[jax.ShapeDtypeStruct((N_DEV,) + b.shape, b.dtype) for b in blocks],
        in_specs=[anyspec] * n, out_specs=[anyspec] * n,
        scratch_shapes=_exchange_sems(n),
    )(*blocks)


def _scatter_phases(s_refs, r_refs, send_sems, recv_sems, local_sems):
    n = len(s_refs)
    x, y, c = lax.axis_index("x"), lax.axis_index("y"), lax.axis_index("c")
    me = 4 * x + 2 * y + c

    def local(a):
        return pltpu.make_async_copy(s_refs[a].at[me], r_refs[a].at[me], local_sems.at[a])

    def copy(a, k):
        px, py, pc = x ^ ((k >> 2) & 1), y ^ ((k >> 1) & 1), c ^ (k & 1)
        return pltpu.make_async_remote_copy(
            src_ref=s_refs[a].at[4 * px + 2 * py + pc], dst_ref=r_refs[a].at[me],
            send_sem=send_sems.at[a * N_PEER + k - 1], recv_sem=recv_sems.at[a * N_PEER + k - 1],
            device_id=(px, py, pc), device_id_type=MESH_IDS)

    def start():
        for a in range(n):
            local(a).start()
        for a in range(n):
            for k in range(1, N_DEV):
                copy(a, k).start()

    def finish():
        for a in range(n):
            for k in range(1, N_DEV):
                copy(a, k).wait()
            local(a).wait()

    return start, finish


def _scatter_exchange(sends):
    n = len(sends)

    def body(*refs):
        start, finish = _scatter_phases(refs[:n], refs[n:2 * n], *refs[2 * n:])
        start()
        finish()

    anyspec = pl.BlockSpec(memory_space=pl.ANY)
    return pl.pallas_call(
        body, name="fsdp_scatter_exchange",
        out_shape=[jax.ShapeDtypeStruct(s.shape, s.dtype) for s in sends],
        in_specs=[anyspec] * n, out_specs=[anyspec] * n,
        scratch_shapes=_exchange_sems(n),
    )(*sends)


def _sum_contrib(recv, own, *, name, tr):
    _, R, C = recv.shape
    assert R % tr == 0

    def body(r_ref, own_ref, g_ref):
        me = 4 * lax.axis_index("x") + 2 * lax.axis_index("y") + lax.axis_index("c")
        g = jnp.zeros((tr, C), F32)
        for s in range(N_DEV):
            g = g + jnp.where(me == s, own_ref[...], r_ref[s].astype(F32))
        g_ref[...] = g

    row = pl.BlockSpec((tr, C), lambda i: (i, 0))
    return pl.pallas_call(
        body, name=name, grid=(R // tr,),
        in_specs=[pl.BlockSpec((N_DEV, tr, C), lambda i: (0, i, 0)), row], out_specs=row,
        out_shape=jax.ShapeDtypeStruct((R, C), F32),
        compiler_params=_cparams(("parallel",)),
    )(recv, own)


def _adamw(g, w, m, v, *, name, tr):
    R, C = g.shape
    assert R % tr == 0

    def body(g_ref, w_ref, m_ref, v_ref, d_ref, mo_ref, vo_ref):
        gv = g_ref[...]
        mn = ADAM_B1 * m_ref[...] + (1.0 - ADAM_B1) * gv
        vn = ADAM_B2 * v_ref[...] + (1.0 - ADAM_B2) * (gv * gv)
        m_hat = mn / (1.0 - ADAM_B1 ** ADAM_STEP)
        v_hat = vn / (1.0 - ADAM_B2 ** ADAM_STEP)
        d_ref[...] = -ADAM_LR * (m_hat / (jnp.sqrt(v_hat) + ADAM_EPS) + ADAM_WD * w_ref[...])
        mo_ref[...] = mn
        vo_ref[...] = vn

    row = pl.BlockSpec((tr, C), lambda i: (i, 0))
    return pl.pallas_call(
        body, name=name, grid=(R // tr,),
        in_specs=[row] * 4, out_specs=[row] * 3,
        out_shape=[jax.ShapeDtypeStruct((R, C), F32)] * 3,
        compiler_params=_cparams(("parallel",)),
    )(g, w, m, v)


SMALL_SHARDED = ('hy_w_gate2', 'cv_b_pw1', 'cv_w_dw', 'cv_b_dw', 'cv_ln_g', 'cv_ln_b', 'cv_b_pw2')
SMALL_REPLICATED = ('mix_norm', 'ffn_norm', 'hy_b_gate', 'hy_gla_norm', 'hy_sb_q_norm', 'hy_sb_k_norm')
LANES = 128


def _small_rows(n):
    return -(-n // (8 * LANES)) * 8


def _pack_small(parts, lead=()):
    out = []
    for p in parts:
        n = p.shape[-1]
        p = jnp.pad(p, [(0, 0)] * len(lead) + [(0, _small_rows(n) * LANES - n)])
        out.append(p.reshape(*lead, _small_rows(n), LANES))
    return jnp.concatenate(out, axis=len(lead))


def _unpack_small(packed, sizes, lead=()):
    out, r0 = [], 0
    for n in sizes:
        r = _small_rows(n)
        out.append(packed[..., r0:r0 + r, :].reshape(*lead, r * LANES)[..., :n])
        r0 += r
    return out


def _to_blocks(full, axis):
    shp = full.shape
    t = full.reshape(shp[:axis] + (N_DEV, shp[axis] // N_DEV) + shp[axis + 1:])
    return jnp.moveaxis(t, axis, 0)


def _from_blocks(blocks, axis):
    t = jnp.moveaxis(blocks, 0, axis)
    shp = t.shape
    return t.reshape(shp[:axis] + (shp[axis] * shp[axis + 1],) + shp[axis + 2:])


def kernel(x, mix_norm, ffn_norm, hy_w_in, hy_w_gate2, hy_b_gate, hy_gla_norm, hy_sb_q_norm, hy_sb_k_norm, hy_w_out, cv_w_pw1, cv_b_pw1, cv_w_dw, cv_b_dw, cv_ln_g, cv_ln_b, cv_w_pw2, cv_b_pw2, ffn_w_gate, ffn_w_up, ffn_w_down, loss_target, m_mix_norm, m_ffn_norm, m_hy_w_in, m_hy_w_gate2, m_hy_b_gate, m_hy_gla_norm, m_hy_sb_q_norm, m_hy_sb_k_norm, m_hy_w_out, m_cv_w_pw1, m_cv_b_pw1, m_cv_w_dw, m_cv_b_dw, m_cv_ln_g, m_cv_ln_b, m_cv_w_pw2, m_cv_b_pw2, m_ffn_w_gate, m_ffn_w_up, m_ffn_w_down, v_mix_norm, v_ffn_norm, v_hy_w_in, v_hy_w_gate2, v_hy_b_gate, v_hy_gla_norm, v_hy_sb_q_norm, v_hy_sb_k_norm, v_hy_w_out, v_cv_w_pw1, v_cv_b_pw1, v_cv_w_dw, v_cv_b_dw, v_cv_ln_g, v_cv_ln_b, v_cv_w_pw2, v_cv_b_pw2, v_ffn_w_gate, v_ffn_w_up, v_ffn_w_down):
    w_loc = dict(zip(WEIGHT_NAMES, (mix_norm, ffn_norm, hy_w_in, hy_w_gate2, hy_b_gate, hy_gla_norm, hy_sb_q_norm, hy_sb_k_norm, hy_w_out, cv_w_pw1, cv_b_pw1, cv_w_dw, cv_b_dw, cv_ln_g, cv_ln_b, cv_w_pw2, cv_b_pw2, ffn_w_gate, ffn_w_up, ffn_w_down)))
    m_loc = dict(zip(WEIGHT_NAMES, (m_mix_norm, m_ffn_norm, m_hy_w_in, m_hy_w_gate2, m_hy_b_gate, m_hy_gla_norm, m_hy_sb_q_norm, m_hy_sb_k_norm, m_hy_w_out, m_cv_w_pw1, m_cv_b_pw1, m_cv_w_dw, m_cv_b_dw, m_cv_ln_g, m_cv_ln_b, m_cv_w_pw2, m_cv_b_pw2, m_ffn_w_gate, m_ffn_w_up, m_ffn_w_down)))
    v_loc = dict(zip(WEIGHT_NAMES, (v_mix_norm, v_ffn_norm, v_hy_w_in, v_hy_w_gate2, v_hy_b_gate, v_hy_gla_norm, v_hy_sb_q_norm, v_hy_sb_k_norm, v_hy_w_out, v_cv_w_pw1, v_cv_b_pw1, v_cv_w_dw, v_cv_b_dw, v_cv_ln_g, v_cv_ln_b, v_cv_w_pw2, v_cv_b_pw2, v_ffn_w_gate, v_ffn_w_up, v_ffn_w_down)))

    Dm, F8 = D_MODEL, D_FF // N_DEV
    tr_ = lambda a: jnp.swapaxes(a, -1, -2)

    small_local = _pack_small([w_loc[n].reshape(-1) for n in SMALL_SHARDED])
    g_in, g_small = _all_gather([tr_(hy_w_in[0]).astype(BF16), small_local])
    late_blocks = [hy_w_out[0].astype(BF16),
                   tr_(cv_w_pw1[0]).astype(BF16),
                   cv_w_pw2[0].astype(BF16),
                   tr_(ffn_w_gate).astype(BF16),
                   tr_(ffn_w_up).astype(BF16),
                   ffn_w_down.astype(BF16)]
    small_sizes = [w_loc[n].size for n in SMALL_SHARDED]
    small_full = dict(zip(SMALL_SHARDED, _unpack_small(g_small, small_sizes, lead=(N_DEV,))))
    W = {n: w_loc[n] for n in SMALL_REPLICATED}
    W['hy_w_in_t'] = g_in.reshape(IN_WIDTH, Dm)
    W['hy_w_gate2'] = _from_blocks(small_full['hy_w_gate2'].reshape(N_DEV, GLA_GATE_RANK, GLA_DK // N_DEV), 1).astype(BF16)
    W['cv_w_dw'] = _from_blocks(small_full['cv_w_dw'].reshape(N_DEV, CONV_WIDTH, Dm // N_DEV), 1)
    for n in ('cv_b_pw1', 'cv_b_dw', 'cv_ln_g', 'cv_ln_b', 'cv_b_pw2'):
        W[n] = small_full[n].reshape(-1)

    sq_err, dx, G, (early_own, early_recv) = _local_step(x[0], loss_target[0], W, late_blocks)

    own_f32 = [
        G['hy_w_in_t'].reshape(N_DEV, IN_WIDTH // N_DEV, Dm),
        G['hy_w_out'].reshape(N_DEV, Dm // N_DEV, Dm),
        G['cv_w_pw1_t'].reshape(N_DEV, 2 * Dm // N_DEV, Dm),
        G['ffn_wgu_t0'].reshape(2, N_DEV, F8, Dm).transpose(1, 0, 2, 3).reshape(N_DEV, 2 * F8, Dm),
        G['ffn_w_down0'].reshape(N_DEV, F8, Dm),
    ]
    small_parts = []
    for n in SMALL_SHARDED:
        axis = SHARD_AXIS[n] - 1
        shard = w_loc[n].shape[1:]
        full = shard[:axis] + (shard[axis] * N_DEV,) + shard[axis + 1:]
        small_parts.append(_to_blocks(G[n].reshape(full), axis).reshape(N_DEV, -1))
    for n in SMALL_REPLICATED:
        small_parts.append(jnp.broadcast_to(G[n].reshape(1, -1), (N_DEV, G[n].size)))
    small_parts.append(jnp.broadcast_to(sq_err.reshape(1, 1), (N_DEV, 1)))
    send_small = _pack_small(small_parts, lead=(N_DEV,))
    recv = _scatter_exchange([a.astype(BF16) for a in own_f32] + [send_small])
    me = 4 * lax.axis_index("x") + 2 * lax.axis_index("y") + lax.axis_index("c")
    tags = ['hy_w_in', 'hy_w_out', 'cv_w_pw1', 'ffn_wgu0', 'ffn_w_down0', 'small', 'cv_w_pw2', 'ffn_wgu1', 'ffn_w_down1']
    own_all = own_f32 + [send_small] + list(early_own)
    recv_all = list(recv) + list(early_recv)
    gsum = dict((t, _sum_contrib(r, lax.dynamic_index_in_dim(o, me, 0, keepdims=False), name=f"sum_{t}", tr=r.shape[1]))
                for t, r, o in zip(tags, recv_all, own_all))

    grad = {}
    grad['hy_w_in'] = tr_(gsum['hy_w_in'])[None]
    grad['hy_w_out'] = gsum['hy_w_out'][None]
    grad['cv_w_pw1'] = tr_(gsum['cv_w_pw1'])[None]
    grad['cv_w_pw2'] = gsum['cv_w_pw2'][None]
    gu = jnp.stack([gsum['ffn_wgu0'], gsum['ffn_wgu1']]).reshape(2, 2, F8, Dm)
    grad['ffn_w_gate'] = tr_(gu[:, 0])
    grad['ffn_w_up'] = tr_(gu[:, 1])
    grad['ffn_w_down'] = jnp.stack([gsum['ffn_w_down0'], gsum['ffn_w_down1']])
    small_names = SMALL_SHARDED + SMALL_REPLICATED
    small_all = [w_loc[n].size for n in small_names]
    *small_grads, sq_sum = _unpack_small(gsum['small'], small_all + [1])
    for n, a in zip(small_names, small_grads):
        grad[n] = a.reshape(w_loc[n].shape)
    loss = 0.5 / Dm * sq_sum[0]

    delta, new_m, new_v = {}, {}, {}
    view = {'hy_w_in': (Dm, 256), 'hy_w_out': (Dm // N_DEV, Dm // N_DEV), 'cv_w_pw1': (Dm, 256),
            'cv_w_pw2': (Dm // N_DEV, Dm // N_DEV), 'ffn_w_gate': (2 * Dm, 256), 'ffn_w_up': (2 * Dm, 256),
            'ffn_w_down': (2 * F8, F8)}
    for n, (rows, tr) in view.items():
        shp = w_loc[n].shape
        outs = _adamw(grad[n].reshape(rows, -1), w_loc[n].reshape(rows, -1), m_loc[n].reshape(rows, -1),
                      v_loc[n].reshape(rows, -1), name=f"adamw_{n}", tr=tr)
        delta[n], new_m[n], new_v[n] = (o.reshape(shp) for o in outs)
    packed = [_pack_small([d[n].reshape(-1) for n in small_names] + [jnp.zeros((1,), F32)]) for d in (w_loc, m_loc, v_loc)]
    outs = _adamw(gsum['small'], *packed, name="adamw_small", tr=gsum['small'].shape[0])
    for dst, o in zip((delta, new_m, new_v), outs):
        for n, a in zip(small_names, _unpack_small(o, small_all)):
            dst[n] = a.reshape(w_loc[n].shape)

    return (loss, dx[None], *[grad[n] for n in WEIGHT_NAMES], *[delta[n] for n in WEIGHT_NAMES],
            *[new_m[n] for n in WEIGHT_NAMES], *[new_v[n] for n in WEIGHT_NAMES])
```

```python
import jax
import jax.numpy as jnp
from jax import lax
from jax.experimental import pallas as pl
from jax.experimental.pallas import tpu as pltpu

F32 = jnp.float32
BF16 = jnp.bfloat16
CDT = jnp.bfloat16

D_MODEL = 1024
EPS = 1e-6
CHUNK = 64
GLA_HEADS = 4
GLA_HEAD_K = 64
GLA_HEAD_V = 128
GLA_DK = GLA_HEADS * GLA_HEAD_K
GLA_DV = GLA_HEADS * GLA_HEAD_V
GLA_GATE_RANK = 16
GLA_GATE_NORMALIZER = 16.0
SB_HEAD_DIM = 64
SB_D = 512
SB_TILE = 256
SB_PAIR = 128
SB_SPLIT_LK = 2
SB_SPLIT_G = 1
SB_SPLIT_NORM = 2
SB_DEAD = 120.0
IN_WIDTH = 3088
IN_PAD = 3200
CONV_WIDTH = 31
CONV_HALO = 32
D_FF = 2816
N_DEV = 8

ADAM_LR = 0.001
ADAM_B1 = 0.9
ADAM_B2 = 0.999
ADAM_EPS = 1e-08
ADAM_WD = 0.01
ADAM_STEP = 10

VMEM_LIMIT = 56 * 1024 * 1024
TK_TOKENS = 2048

WEIGHT_NAMES = ['mix_norm', 'ffn_norm', 'hy_w_in', 'hy_w_gate2', 'hy_b_gate', 'hy_gla_norm', 'hy_sb_q_norm',
                'hy_sb_k_norm', 'hy_w_out', 'cv_w_pw1', 'cv_b_pw1', 'cv_w_dw', 'cv_b_dw', 'cv_ln_g', 'cv_ln_b',
                'cv_w_pw2', 'cv_b_pw2', 'ffn_w_gate', 'ffn_w_up', 'ffn_w_down']
SHARD_AXIS = {'mix_norm': None, 'ffn_norm': None, 'hy_w_in': 2, 'hy_w_gate2': 2, 'hy_b_gate': None,
              'hy_gla_norm': None, 'hy_sb_q_norm': None, 'hy_sb_k_norm': None, 'hy_w_out': 1, 'cv_w_pw1': 2,
              'cv_b_pw1': 1, 'cv_w_dw': 2, 'cv_b_dw': 1, 'cv_ln_g': 1, 'cv_ln_b': 1, 'cv_w_pw2': 1, 'cv_b_pw2': 1,
              'ffn_w_gate': 2, 'ffn_w_up': 2, 'ffn_w_down': 1}


def _cparams(sem=None, vmem=VMEM_LIMIT):
    return pltpu.CompilerParams(dimension_semantics=sem, vmem_limit_bytes=vmem)


def _log_sigmoid(x):
    return jnp.minimum(x, 0.0) - jnp.log1p(jnp.exp(-jnp.abs(x)))


def _sigmoid(x):
    return 1.0 / (1.0 + jnp.exp(-x))


def _softplus(x):
    return jnp.maximum(x, 0.0) + jnp.log(1.0 + jnp.exp(-jnp.abs(x)))


def _split_bf16(x, n):
    parts = []
    rem = x
    for _ in range(n):
        p = rem.astype(BF16)
        parts.append(p)
        rem = rem - p.astype(F32)
    return parts


def _dot_exact_rhs(x, m, n):
    return sum(jnp.dot(p, m, preferred_element_type=F32) for p in _split_bf16(x, n))


def _dot_exact_lhs(m, x, n):
    return sum(jnp.dot(m, p, preferred_element_type=F32) for p in _split_bf16(x, n))


_NN = (((1,), (0,)), ((), ()))
_NT = (((1,), (1,)), ((), ()))
_TN = (((0,), (0,)), ((), ()))


def _dg(a, b, dn):
    return lax.dot_general(a.astype(CDT), b.astype(CDT), dn, preferred_element_type=F32)


def _matmul(a, b, *, mode, out_dtype, name, tm, tn, tk, bias=None, residual=None):
    if mode == 'nn':
        (M, K), (K2, N) = a.shape, b.shape
    elif mode == 'nt':
        (M, K), (N, K2) = a.shape, b.shape
    else:
        (K, M), (K2, N) = a.shape, b.shape
    assert K == K2 and M % tm == 0 and N % tn == 0 and K % tk == 0, (name, a.shape, b.shape, tm, tn, tk)
    nk = K // tk
    a_spec = pl.BlockSpec((tk, tm), lambda i, j, k: (k, i)) if mode == 'tn' else pl.BlockSpec((tm, tk), lambda i, j, k: (i, k))
    b_spec = pl.BlockSpec((tn, tk), lambda i, j, k: (j, k)) if mode == 'nt' else pl.BlockSpec((tk, tn), lambda i, j, k: (k, j))
    dn = {'nn': _NN, 'nt': _NT, 'tn': _TN}[mode]
    has_bias, has_res = bias is not None, residual is not None

    def body(*refs):
        a_ref, b_ref = refs[0], refs[1]
        pos = 2
        bias_ref = res_ref = None
        if has_bias:
            bias_ref = refs[pos]
            pos += 1
        if has_res:
            res_ref = refs[pos]
            pos += 1
        o_ref = refs[pos]
        acc_ref = refs[pos + 1] if nk > 1 else None
        p = _dg(a_ref[...], b_ref[...], dn)

        def finish(acc):
            if has_bias:
                acc = acc + bias_ref[...]
            if has_res:
                acc = res_ref[...] + acc
            o_ref[...] = acc.astype(o_ref.dtype)

        if nk == 1:
            finish(p)
        else:
            k = pl.program_id(2)

            @pl.when(k == 0)
            def _():
                acc_ref[...] = p

            @pl.when(k > 0)
            def _():
                acc_ref[...] += p

            @pl.when(k == nk - 1)
            def _():
                finish(acc_ref[...])

    in_specs = [a_spec, b_spec]
    args = [a, b]
    if has_bias:
        in_specs.append(pl.BlockSpec((1, tn), lambda i, j, k: (0, j)))
        args.append(bias)
    if has_res:
        in_specs.append(pl.BlockSpec((tm, tn), lambda i, j, k: (i, j)))
        args.append(residual)
    return pl.pallas_call(
        body, name=name, grid=(M // tm, N // tn, nk),
        in_specs=in_specs, out_specs=pl.BlockSpec((tm, tn), lambda i, j, k: (i, j)),
        out_shape=jax.ShapeDtypeStruct((M, N), out_dtype),
        scratch_shapes=[pltpu.VMEM((tm, tn), F32)] if nk > 1 else [],
        compiler_params=_cparams(("parallel", "parallel", "arbitrary")),
    )(*args)


def _rms_fwd(x, g, *, name, tm=512):
    T, Dm = x.shape

    def body(x_ref, g_ref, o_ref):
        xv = x_ref[...]
        r = lax.rsqrt(jnp.mean(xv * xv, axis=-1, keepdims=True) + EPS)
        o_ref[...] = (xv * r * g_ref[...]).astype(o_ref.dtype)

    return pl.pallas_call(
        body, name=name, grid=(T // tm,),
        in_specs=[pl.BlockSpec((tm, Dm), lambda i: (i, 0)), pl.BlockSpec((1, Dm), lambda i: (0, 0))],
        out_specs=pl.BlockSpec((tm, Dm), lambda i: (i, 0)),
        out_shape=jax.ShapeDtypeStruct((T, Dm), CDT),
        compiler_params=_cparams(("parallel",)),
    )(x, g)


def _rms_bwd(dy, x, g, resid, *, name, tm=512):
    T, Dm = x.shape

    def body(dy_ref, x_ref, g_ref, res_ref, dx_ref, dxb_ref, dg_ref, dxs_ref):
        i = pl.program_id(0)
        xv, dyv = x_ref[...], dy_ref[...]
        r = lax.rsqrt(jnp.mean(xv * xv, axis=-1, keepdims=True) + EPS)
        u = dyv * g_ref[...]
        dot = jnp.mean(u * xv, axis=-1, keepdims=True)
        dx = res_ref[...] + (r * u - xv * (r * r * r * dot))
        dx_ref[...] = dx
        dxb_ref[...] = dx.astype(dxb_ref.dtype)
        part = jnp.sum(dyv * xv * r, axis=0, keepdims=True)
        part_dx = jnp.sum(dx, axis=0, keepdims=True)

        @pl.when(i == 0)
        def _():
            dg_ref[...] = part
            dxs_ref[...] = part_dx

        @pl.when(i > 0)
        def _():
            dg_ref[...] += part
            dxs_ref[...] += part_dx

    row = pl.BlockSpec((tm, Dm), lambda i: (i, 0))
    vec = pl.BlockSpec((1, Dm), lambda i: (0, 0))
    return pl.pallas_call(
        body, name=name, grid=(T // tm,),
        in_specs=[row, row, vec, row], out_specs=[row, row, vec, vec],
        out_shape=[jax.ShapeDtypeStruct((T, Dm), F32), jax.ShapeDtypeStruct((T, Dm), CDT),
                   jax.ShapeDtypeStruct((1, Dm), F32), jax.ShapeDtypeStruct((1, Dm), F32)],
        compiler_params=_cparams(("arbitrary",)),
    )(dy, x, g, resid)


def _loss_head(y, tgt, *, tm=512):
    T, Dm = y.shape

    def body(y_ref, t_ref, s_ref, dy_ref, dyb_ref):
        i = pl.program_id(0)
        e = y_ref[...] - t_ref[...]
        dy = e * (1.0 / Dm)
        dy_ref[...] = dy
        dyb_ref[...] = dy.astype(dyb_ref.dtype)
        part = jnp.sum(jnp.sum(e * e, axis=1, keepdims=True), axis=0, keepdims=True)

        @pl.when(i == 0)
        def _():
            s_ref[...] = part

        @pl.when(i > 0)
        def _():
            s_ref[...] += part

    row = pl.BlockSpec((tm, Dm), lambda i: (i, 0))
    return pl.pallas_call(
        body, name="loss_head", grid=(T // tm,),
        in_specs=[row, row], out_specs=[pl.BlockSpec((1, 1), lambda i: (0, 0)), row, row],
        out_shape=[jax.ShapeDtypeStruct((1, 1), F32), jax.ShapeDtypeStruct((T, Dm), F32),
                   jax.ShapeDtypeStruct((T, Dm), CDT)],
        compiler_params=_cparams(("arbitrary",)),
    )(y, tgt)


def _swiglu_up(hf, wg_t, wu_t, *, name, tm=256):
    T, Dm = hf.shape
    F = wg_t.shape[0]

    def body(h_ref, wg_ref, wu_ref, ab_ref, s_ref):
        h = h_ref[...]
        a = _dg(h, wg_ref[...], _NT)
        b = _dg(h, wu_ref[...], _NT)
        ab_ref[:, :F] = a.astype(ab_ref.dtype)
        ab_ref[:, F:] = b.astype(ab_ref.dtype)
        s_ref[...] = (a * _sigmoid(a) * b).astype(s_ref.dtype)

    wspec = pl.BlockSpec((F, Dm), lambda i: (0, 0))
    return pl.pallas_call(
        body, name=name, grid=(T // tm,),
        in_specs=[pl.BlockSpec((tm, Dm), lambda i: (i, 0)), wspec, wspec],
        out_specs=[pl.BlockSpec((tm, 2 * F), lambda i: (i, 0)), pl.BlockSpec((tm, F), lambda i: (i, 0))],
        out_shape=[jax.ShapeDtypeStruct((T, 2 * F), CDT), jax.ShapeDtypeStruct((T, F), CDT)],
        compiler_params=_cparams(("parallel",)),
    )(hf, wg_t, wu_t)


def _swiglu_dhf(dab, wg_t, wu_t, *, name, tm=256):
    T = dab.shape[0]
    F, Dm = wg_t.shape

    def body(d_ref, wg_ref, wu_ref, o_ref):
        o_ref[...] = _dg(d_ref[:, :F], wg_ref[...], _NN) + _dg(d_ref[:, F:], wu_ref[...], _NN)

    wspec = pl.BlockSpec((F, Dm), lambda i: (0, 0))
    return pl.pallas_call(
        body, name=name, grid=(T // tm,),
        in_specs=[pl.BlockSpec((tm, 2 * F), lambda i: (i, 0)), wspec, wspec],
        out_specs=pl.BlockSpec((tm, Dm), lambda i: (i, 0)),
        out_shape=jax.ShapeDtypeStruct((T, Dm), F32),
        compiler_params=_cparams(("parallel",)),
    )(dab, wg_t, wu_t)


def _swiglu_dact(dh_c, wd, ab, *, name, tm=256):
    T, Dm = dh_c.shape
    F2 = ab.shape[1]
    F = F2 // 2

    def body(dh_ref, w_ref, ab_ref, o_ref):
        dsv = _dg(dh_ref[...], w_ref[...], _NT)
        a = ab_ref[:, :F].astype(F32)
        b = ab_ref[:, F:].astype(F32)
        sg = _sigmoid(a)
        o_ref[:, :F] = (dsv * b * (sg * (1.0 + a * (1.0 - sg)))).astype(o_ref.dtype)
        o_ref[:, F:] = (dsv * (a * sg)).astype(o_ref.dtype)

    return pl.pallas_call(
        body, name=name, grid=(T // tm,),
        in_specs=[pl.BlockSpec((tm, Dm), lambda i: (i, 0)), pl.BlockSpec((F, Dm), lambda i: (0, 0)),
                  pl.BlockSpec((tm, F2), lambda i: (i, 0))],
        out_specs=pl.BlockSpec((tm, F2), lambda i: (i, 0)),
        out_shape=jax.ShapeDtypeStruct((T, F2), CDT),
        compiler_params=_cparams(("parallel",)),
    )(dh_c, wd, ab)


SUBLANES = 8


def _shifted_copies(buf, shifted, tm):
    n = tm + CONV_HALO - SUBLANES
    for b in range(1, SUBLANES):
        shifted[b - 1] = buf[pl.ds(b, n), :]


CONV_ROWS = 64
CONV_LANES = 128


def _rows_from(buf, shifted, offset, rows, r0, cols):
    a, b = divmod(offset, SUBLANES)
    if b == 0:
        return buf[pl.ds(r0 + SUBLANES * a, rows), cols]
    return shifted[b - 1, pl.ds(r0 + SUBLANES * a, rows), cols]


def _conv_fwd(a, w_dw, b_dw, ln_g, ln_b, *, tm=256):
    T = a.shape[0]
    Dm = D_MODEL

    def body(a_ref, w_ref, bdw_ref, g_ref, b_ref, s_ref, u_ref, c_ref, ubuf, shifted):
        i = pl.program_id(0)

        @pl.when(i == 0)
        def _():
            ubuf[0:CONV_HALO, :] = jnp.zeros((CONV_HALO, Dm), F32)

        @pl.when(i > 0)
        def _():
            ubuf[0:CONV_HALO, :] = ubuf[tm:tm + CONV_HALO, :]

        u = a_ref[:, :Dm] * _sigmoid(a_ref[:, Dm:])
        ubuf[CONV_HALO:CONV_HALO + tm, :] = u
        u_ref[...] = u
        _shifted_copies(ubuf, shifted, tm)
        acc = jnp.zeros((tm, Dm), F32) + bdw_ref[...]
        for k in range(CONV_WIDTH):
            acc = acc + w_ref[k:k + 1, :] * _rows_from(ubuf, shifted, CONV_HALO - (CONV_WIDTH - 1) + k, tm, 0, slice(None))
        c_ref[...] = acc
        mu = jnp.mean(acc, axis=-1, keepdims=True)
        cen = acc - mu
        var = jnp.mean(cen * cen, axis=-1, keepdims=True)
        l = cen * lax.rsqrt(var + EPS) * g_ref[...] + b_ref[...]
        s_ref[...] = (l * _sigmoid(l)).astype(s_ref.dtype)

    row = pl.BlockSpec((tm, Dm), lambda i: (i, 0))
    vec = pl.BlockSpec((1, Dm), lambda i: (0, 0))
    return pl.pallas_call(
        body, name="conv_fwd", grid=(T // tm,),
        in_specs=[pl.BlockSpec((tm, 2 * Dm), lambda i: (i, 0)), pl.BlockSpec((CONV_HALO, Dm), lambda i: (0, 0)), vec, vec, vec],
        out_specs=[row, row, row],
        out_shape=[jax.ShapeDtypeStruct((T, Dm), CDT), jax.ShapeDtypeStruct((T, Dm), F32), jax.ShapeDtypeStruct((T, Dm), F32)],
        scratch_shapes=[pltpu.VMEM((tm + CONV_HALO, Dm), F32), pltpu.VMEM((SUBLANES - 1, tm + CONV_HALO - SUBLANES, Dm), F32)],
        compiler_params=_cparams(("arbitrary",)),
    )(a, w_dw, b_dw, ln_g, ln_b)


def _conv_bwd(ds, c, u, a, w_dw, ln_g, ln_b, sends, *, tm=256):
    T = a.shape[0]
    Dm = D_MODEL
    nt = T // tm
    ns = len(sends)

    def body(*refs):
        ds_ref, c_ref, u_ref, a_ref, w_ref, g_ref, b_ref = refs[:7]
        da_ref, db1_ref, dw_ref, dbdw_ref, dg_ref, dbln_ref = refs[7 + ns:13 + ns]
        dcbuf, shifted, du_scr = refs[13 + 2 * ns:16 + 2 * ns]
        x_start, x_finish = _scatter_phases(refs[7:7 + ns], refs[13 + ns:13 + 2 * ns], *refs[16 + 2 * ns:])
        i = pl.program_id(0)
        pl.when(i == 0)(x_start)

        @pl.when(i == 0)
        def _():
            dcbuf[tm:tm + CONV_HALO, :] = jnp.zeros((CONV_HALO, Dm), F32)
            db1_ref[...] = jnp.zeros_like(db1_ref)
            dw_ref[...] = jnp.zeros_like(dw_ref)
            dbdw_ref[...] = jnp.zeros_like(dbdw_ref)
            dg_ref[...] = jnp.zeros_like(dg_ref)
            dbln_ref[...] = jnp.zeros_like(dbln_ref)

        @pl.when(i > 0)
        def _():
            dcbuf[tm:tm + CONV_HALO, :] = dcbuf[0:CONV_HALO, :]

        cv = c_ref[...]
        mu = jnp.mean(cv, axis=-1, keepdims=True)
        cen = cv - mu
        var = jnp.mean(cen * cen, axis=-1, keepdims=True)
        rstd = lax.rsqrt(var + EPS)
        n = cen * rstd
        l = n * g_ref[...] + b_ref[...]
        sg = _sigmoid(l)
        dl = ds_ref[...] * (sg * (1.0 + l * (1.0 - sg)))
        dg_ref[...] += jnp.sum(dl * n, axis=0, keepdims=True)
        dbln_ref[...] += jnp.sum(dl, axis=0, keepdims=True)
        dn = dl * g_ref[...]
        dc = rstd * (dn - jnp.mean(dn, axis=-1, keepdims=True) - n * jnp.mean(dn * n, axis=-1, keepdims=True))
        dbdw_ref[...] += jnp.sum(dc, axis=0, keepdims=True)
        dcbuf[0:tm, :] = dc
        _shifted_copies(dcbuf, shifted, tm)
        groups = CONV_ROWS // SUBLANES
        for cb in range(Dm // CONV_LANES):
            cols = slice(cb * CONV_LANES, (cb + 1) * CONV_LANES)

            def block(r, dw_part, cols=cols):
                r0 = pl.multiple_of(r * CONV_ROWS, CONV_ROWS)
                uv = u_ref[pl.ds(r0, CONV_ROWS), cols]
                du_b = jnp.zeros((CONV_ROWS, CONV_LANES), F32)
                out = []
                for k in range(CONV_WIDTH):
                    slab = _rows_from(dcbuf, shifted, CONV_WIDTH - 1 - k, CONV_ROWS, r0, cols)
                    du_b = du_b + w_ref[k:k + 1, cols] * slab
                    out.append(dw_part[k] + jnp.sum((slab * uv).reshape(groups, SUBLANES, CONV_LANES), axis=0))
                du_scr[pl.ds(r0, CONV_ROWS), cols] = du_b
                return tuple(out)

            zero = jnp.zeros((SUBLANES, CONV_LANES), F32)
            dw_part = lax.fori_loop(0, tm // CONV_ROWS, block, (zero,) * CONV_WIDTH)
            for k in range(CONV_WIDTH):
                dw_ref[k:k + 1, cols] += jnp.sum(dw_part[k], axis=0, keepdims=True)
        du = du_scr[...]
        a1 = a_ref[:, :Dm]
        s2 = _sigmoid(a_ref[:, Dm:])
        da1 = du * s2
        da2 = du * a1 * (s2 * (1.0 - s2))
        da_ref[:, :Dm] = da1.astype(da_ref.dtype)
        da_ref[:, Dm:] = da2.astype(da_ref.dtype)
        db1_ref[:, :Dm] += jnp.sum(da1, axis=0, keepdims=True)
        db1_ref[:, Dm:] += jnp.sum(da2, axis=0, keepdims=True)
        pl.when(i == nt - 1)(x_finish)

    rev = lambda i: (nt - 1 - i, 0)
    row = pl.BlockSpec((tm, Dm), rev)
    row2 = pl.BlockSpec((tm, 2 * Dm), rev)
    vec = pl.BlockSpec((1, Dm), lambda i: (0, 0))
    vec2 = pl.BlockSpec((1, 2 * Dm), lambda i: (0, 0))
    taps = pl.BlockSpec((CONV_HALO, Dm), lambda i: (0, 0))
    anyspec = pl.BlockSpec(memory_space=pl.ANY)
    outs = pl.pallas_call(
        body, name="conv_bwd", grid=(nt,),
        in_specs=[row, row, row, row2, taps, vec, vec] + [anyspec] * ns,
        out_specs=[row2, vec2, taps, vec, vec, vec] + [anyspec] * ns,
        out_shape=[jax.ShapeDtypeStruct((T, 2 * Dm), CDT), jax.ShapeDtypeStruct((1, 2 * Dm), F32),
                   jax.ShapeDtypeStruct((CONV_HALO, Dm), F32), jax.ShapeDtypeStruct((1, Dm), F32),
                   jax.ShapeDtypeStruct((1, Dm), F32), jax.ShapeDtypeStruct((1, Dm), F32)]
        + [jax.ShapeDtypeStruct(s.shape, s.dtype) for s in sends],
        scratch_shapes=[pltpu.VMEM((tm + CONV_HALO, Dm), F32), pltpu.VMEM((SUBLANES - 1, tm + CONV_HALO - SUBLANES, Dm), F32),
                        pltpu.VMEM((tm, Dm), F32)]
        + _exchange_sems(ns),
        compiler_params=_cparams(("arbitrary",)),
    )(ds, c, u, a, w_dw, ln_g, ln_b, *sends)
    return outs[:6], outs[6:]


def _gla_head_masks(width, per_head):
    lane = lax.broadcasted_iota(jnp.int32, (1, width), 1)
    return [((lane >= h * per_head) & (lane < (h + 1) * per_head)).astype(F32) for h in range(GLA_HEADS)]


def _gla_specs(tm, order):
    return [pl.BlockSpec((tm, GLA_DK), lambda i: (order(i), 0)),
            pl.BlockSpec((tm, GLA_DK), lambda i: (order(i), 1)),
            pl.BlockSpec((tm, GLA_DV), lambda i: (order(i), 1)),
            pl.BlockSpec((tm, GLA_DV), lambda i: (order(i), 2)),
            pl.BlockSpec((tm, 128), lambda i: (order(i), 3072 // 128))]


def _gla_chunk_decay(la_c, tri):
    bc = _dot_exact_lhs(tri, la_c, 3)
    b_end = bc[CHUNK - 1:CHUNK, :]
    return b_end, jnp.exp(b_end - bc)


def _gla_fwd(proj, wg2p, b_gate, g_gla, *, tm=256):
    T = proj.shape[0]
    ncs = tm // CHUNK
    scale = GLA_HEAD_K ** -0.5

    def body(q_ref, k_ref, v_ref, r_ref, glr_ref, wg_ref, bg_ref, gg_ref, o_ref, oraw_ref, st_ref, s_scr):
        i = pl.program_id(0)

        @pl.when(i == 0)
        def _():
            s_scr[...] = jnp.zeros_like(s_scr)

        mk = _gla_head_masks(GLA_DK, GLA_HEAD_K)
        rr = lax.broadcasted_iota(jnp.int32, (CHUNK, CHUNK), 0)
        cc = lax.broadcasted_iota(jnp.int32, (CHUNK, CHUNK), 1)
        tri = (cc <= rr).astype(BF16)
        y = _dg(glr_ref[...], wg_ref[...], _NN) + bg_ref[...]
        la = _log_sigmoid(y) / GLA_GATE_NORMALIZER
        qs = q_ref[...] * scale
        for ci in range(ncs):
            rows = slice(ci * CHUNK, (ci + 1) * CHUNK)
            b_end, dec = _gla_chunk_decay(la[rows], tri)
            kend = (k_ref[rows, :] * dec).astype(CDT)
            upd = jnp.zeros((GLA_HEAD_V, GLA_DK), F32)
            for h in range(GLA_HEADS):
                vh = v_ref[rows, h * GLA_HEAD_V:(h + 1) * GLA_HEAD_V]
                upd = upd + mk[h] * _dg(vh, kend, _TN)
            s_new = jnp.exp(b_end) * s_scr[...] + upd
            s_scr[...] = s_new
            st_ref[ci] = s_new
            s_c = s_new.astype(CDT)
            for h in range(GLA_HEADS):
                o_h = _dg(qs[rows] * mk[h], s_c, _NT)
                oraw_ref[rows, h * GLA_HEAD_V:(h + 1) * GLA_HEAD_V] = o_h
        for h in range(GLA_HEADS):
            cols = slice(h * GLA_HEAD_V, (h + 1) * GLA_HEAD_V)
            o_h = oraw_ref[:, cols]
            rs = lax.rsqrt(jnp.mean(o_h * o_h, axis=-1, keepdims=True) + EPS)
            rg = r_ref[:, cols]
            o_ref[:, cols] = (o_h * rs * gg_ref[...] * (rg * _sigmoid(rg))).astype(o_ref.dtype)

    full = lambda shape: pl.BlockSpec(shape, lambda i: tuple(0 for _ in shape))
    return pl.pallas_call(
        body, name="gla_fwd", grid=(T // tm,),
        in_specs=_gla_specs(tm, lambda i: i) + [full((128, GLA_DK)), full((1, GLA_DK)), full((1, GLA_HEAD_V))],
        out_specs=[pl.BlockSpec((tm, GLA_DV), lambda i: (i, 0)), pl.BlockSpec((tm, GLA_DV), lambda i: (i, 0)),
                   pl.BlockSpec((ncs, GLA_HEAD_V, GLA_DK), lambda i: (i, 0, 0))],
        out_shape=[jax.ShapeDtypeStruct((T, GLA_DV), CDT), jax.ShapeDtypeStruct((T, GLA_DV), F32),
                   jax.ShapeDtypeStruct((T // CHUNK, GLA_HEAD_V, GLA_DK), F32)],
        scratch_shapes=[pltpu.VMEM((GLA_HEAD_V, GLA_DK), F32)],
        compiler_params=_cparams(("arbitrary",)),
    )(proj, proj, proj, proj, proj, wg2p, b_gate, g_gla)


def _gla_bwd(d_o, proj, oraw, states, wg2p, b_gate, g_gla, *, tm=256):
    T = proj.shape[0]
    nt = T // tm
    ncs = tm // CHUNK
    scale = GLA_HEAD_K ** -0.5

    def body(do_ref, q_ref, k_ref, v_ref, r_ref, glr_ref, oraw_ref, st_ref, stp_ref, wg_ref, bg_ref, gg_ref,
             dgla_ref, dglr_ref, dwg_ref, dbg_ref, dgg_ref, ds_scr, dy_scr, dor_scr):
        i = pl.program_id(0)
        tile = nt - 1 - i

        @pl.when(i == 0)
        def _():
            ds_scr[...] = jnp.zeros_like(ds_scr)
            dwg_ref[...] = jnp.zeros_like(dwg_ref)
            dbg_ref[...] = jnp.zeros_like(dbg_ref)
            dgg_ref[...] = jnp.zeros_like(dgg_ref)

        mk = _gla_head_masks(GLA_DK, GLA_HEAD_K)
        rr = lax.broadcasted_iota(jnp.int32, (CHUNK, CHUNK), 0)
        cc = lax.broadcasted_iota(jnp.int32, (CHUNK, CHUNK), 1)
        tri = (cc <= rr).astype(BF16)
        tri_t = (cc >= rr).astype(BF16)
        last_row = (lax.broadcasted_iota(jnp.int32, (CHUNK, 1), 0) == CHUNK - 1).astype(F32)

        dgg = jnp.zeros((1, GLA_HEAD_V), F32)
        for h in range(GLA_HEADS):
            cols = slice(h * GLA_HEAD_V, (h + 1) * GLA_HEAD_V)
            o_h = oraw_ref[:, cols]
            rs = lax.rsqrt(jnp.mean(o_h * o_h, axis=-1, keepdims=True) + EPS)
            rg = r_ref[:, cols]
            sg = _sigmoid(rg)
            dov = do_ref[:, cols]
            on = o_h * rs * gg_ref[...]
            d_on = dov * (rg * sg)
            dgla_ref[:, 2 * GLA_DK + GLA_DV + h * GLA_HEAD_V:2 * GLA_DK + GLA_DV + (h + 1) * GLA_HEAD_V] = (
                dov * on * (sg * (1.0 + rg * (1.0 - sg)))).astype(dgla_ref.dtype)
            dgg = dgg + jnp.sum(d_on * o_h * rs, axis=0, keepdims=True)
            uu = d_on * gg_ref[...]
            dor_scr[:, cols] = rs * uu - o_h * (rs * rs * rs * jnp.mean(uu * o_h, axis=-1, keepdims=True))
        dgg_ref[...] += dgg

        y = _dg(glr_ref[...], wg_ref[...], _NN) + bg_ref[...]
        la = _log_sigmoid(y) / GLA_GATE_NORMALIZER
        qs = q_ref[...] * scale
        for ci in reversed(range(ncs)):
            rows = slice(ci * CHUNK, (ci + 1) * CHUNK)
            b_end, dec = _gla_chunk_decay(la[rows], tri)
            decay = jnp.exp(b_end)
            kend = k_ref[rows, :] * dec
            kend_c = kend.astype(CDT)
            s_c = st_ref[ci].astype(CDT)
            if ci > 0:
                s_prev = st_ref[ci - 1]
            else:
                s_prev = jnp.where(tile > 0, stp_ref[0], 0.0)
            dqs = jnp.zeros((CHUNK, GLA_DK), F32)
            dst = ds_scr[...]
            for h in range(GLA_HEADS):
                do_h = dor_scr[rows, h * GLA_HEAD_V:(h + 1) * GLA_HEAD_V].astype(CDT)
                dqs = dqs + mk[h] * _dg(do_h, s_c, _NN)
                dst = dst + mk[h] * _dg(do_h, qs[rows], _TN)
            d_decay = jnp.sum(dst * s_prev, axis=0, keepdims=True)
            ds_scr[...] = decay * dst
            dst_c = dst.astype(CDT)
            dkend = jnp.zeros((CHUNK, GLA_DK), F32)
            for h in range(GLA_HEADS):
                cols = slice(h * GLA_HEAD_V, (h + 1) * GLA_HEAD_V)
                dv_h = _dg(kend * mk[h], dst_c, _NT)
                dgla_ref[rows, 2 * GLA_DK + h * GLA_HEAD_V:2 * GLA_DK + (h + 1) * GLA_HEAD_V] = dv_h.astype(dgla_ref.dtype)
                dkend = dkend + mk[h] * _dg(v_ref[rows, cols], dst_c, _NN)
            dgla_ref[rows, 0:GLA_DK] = (dqs * scale).astype(dgla_ref.dtype)
            dgla_ref[rows, GLA_DK:2 * GLA_DK] = (dkend * dec).astype(dgla_ref.dtype)
            mm = dkend * kend
            db_end = jnp.sum(mm, axis=0, keepdims=True) + d_decay * decay
            dbc = last_row * db_end - mm
            dla = _dot_exact_lhs(tri_t, dbc, 3)
            dy_scr[rows, :] = dla * (1.0 / GLA_GATE_NORMALIZER) * _sigmoid(-y[rows])
        dy = dy_scr[...]
        dbg_ref[...] += jnp.sum(dy, axis=0, keepdims=True)
        dwg_ref[...] += _dg(glr_ref[...], dy, _TN)
        dglr_ref[...] = _dg(dy, wg_ref[...], _NT).astype(dglr_ref.dtype)

    rev = lambda i: nt - 1 - i
    full = lambda shape: pl.BlockSpec(shape, lambda i: tuple(0 for _ in shape))
    st_spec = pl.BlockSpec((ncs, GLA_HEAD_V, GLA_DK), lambda i: (rev(i), 0, 0))
    stp_spec = pl.BlockSpec((1, GLA_HEAD_V, GLA_DK), lambda i: (jnp.maximum(rev(i) * ncs - 1, 0), 0, 0))
    return pl.pallas_call(
        body, name="gla_bwd", grid=(nt,),
        in_specs=[pl.BlockSpec((tm, GLA_DV), lambda i: (rev(i), 0))] + _gla_specs(tm, rev)
        + [pl.BlockSpec((tm, GLA_DV), lambda i: (rev(i), 0)), st_spec, stp_spec,
           full((128, GLA_DK)), full((1, GLA_DK)), full((1, GLA_HEAD_V))],
        out_specs=[pl.BlockSpec((tm, 2 * GLA_DK + 2 * GLA_DV), lambda i: (rev(i), 0)),
                   pl.BlockSpec((tm, 128), lambda i: (rev(i), 0)),
                   full((128, GLA_DK)), full((1, GLA_DK)), full((1, GLA_HEAD_V))],
        out_shape=[jax.ShapeDtypeStruct((T, 2 * GLA_DK + 2 * GLA_DV), CDT), jax.ShapeDtypeStruct((T, 128), CDT),
                   jax.ShapeDtypeStruct((128, GLA_DK), F32), jax.ShapeDtypeStruct((1, GLA_DK), F32),
                   jax.ShapeDtypeStruct((1, GLA_HEAD_V), F32)],
        scratch_shapes=[pltpu.VMEM((GLA_HEAD_V, GLA_DK), F32), pltpu.VMEM((tm, GLA_DK), F32),
                        pltpu.VMEM((tm, GLA_DV), F32)],
        compiler_params=_cparams(("arbitrary",)),
    )(d_o, proj, proj, proj, proj, proj, oraw, states, states, wg2p, b_gate, g_gla)


def _head_mean_matrix():
    r = lax.broadcasted_iota(jnp.int32, (SB_D, SB_D), 0) // SB_HEAD_DIM
    c = lax.broadcasted_iota(jnp.int32, (SB_D, SB_D), 1) // SB_HEAD_DIM
    return jnp.where(r == c, 1.0 / SB_HEAD_DIM, 0.0).astype(BF16)


def _sb_prep(proj, gq, gk, *, tm=256):
    T = proj.shape[0]
    scale = SB_HEAD_DIM ** -0.5

    def body(q_ref, k_ref, v_ref, gq_ref, gk_ref, qn_ref, kn_ref, vb_ref):
        hm = _head_mean_matrix()
        qv, kv = q_ref[...], k_ref[...]
        rq = lax.rsqrt(_dot_exact_rhs(qv * qv, hm, SB_SPLIT_NORM) + EPS)
        rk = lax.rsqrt(_dot_exact_rhs(kv * kv, hm, SB_SPLIT_NORM) + EPS)
        qn_ref[...] = (qv * rq * gq_ref[...] * scale).astype(qn_ref.dtype)
        kn_ref[...] = (kv * rk * gk_ref[...]).astype(kn_ref.dtype)
        vb_ref[...] = v_ref[...].astype(vb_ref.dtype)

    col = lambda j: pl.BlockSpec((tm, SB_D), lambda i: (i, j))
    vec = pl.BlockSpec((1, SB_D), lambda i: (0, 0))
    out = pl.BlockSpec((tm, SB_D), lambda i: (i, 0))
    return pl.pallas_call(
        body, name="sb_prep", grid=(T // tm,),
        in_specs=[col(3), col(4), col(5), vec, vec], out_specs=[out, out, out],
        out_shape=[jax.ShapeDtypeStruct((T, SB_D), CDT)] * 3,
        compiler_params=_cparams(("parallel",)),
    )(proj, proj, proj, gq, gk)


def _sb_prep_bwd(dqn, dkn, dv, proj, gq, gk, *, tm=256):
    T = proj.shape[0]
    scale = SB_HEAD_DIM ** -0.5

    def body(dqn_ref, dkn_ref, dv_ref, q_ref, k_ref, gq_ref, gk_ref, dsb_ref, dgq_ref, dgk_ref):
        i = pl.program_id(0)

        @pl.when(i == 0)
        def _():
            dgq_ref[...] = jnp.zeros_like(dgq_ref)
            dgk_ref[...] = jnp.zeros_like(dgk_ref)

        hm = _head_mean_matrix()

        def one(dn_ref, x_ref, g_ref, dg_ref, sc, lo):
            xv = x_ref[...]
            dnv = dn_ref[...] * sc
            r = lax.rsqrt(_dot_exact_rhs(xv * xv, hm, SB_SPLIT_NORM) + EPS)
            u = dnv * g_ref[...]
            dot = _dot_exact_rhs(u * xv, hm, SB_SPLIT_NORM)
            dsb_ref[:, lo:lo + SB_D] = (r * u - xv * (r * r * r * dot)).astype(dsb_ref.dtype)
            dg_ref[...] += jnp.sum(dnv * xv * r, axis=0, keepdims=True)

        one(dqn_ref, q_ref, gq_ref, dgq_ref, scale, 0)
        one(dkn_ref, k_ref, gk_ref, dgk_ref, 1.0, SB_D)
        dsb_ref[:, 2 * SB_D:3 * SB_D] = dv_ref[...].astype(dsb_ref.dtype)

    col = lambda j: pl.BlockSpec((tm, SB_D), lambda i: (i, j))
    vec = pl.BlockSpec((1, SB_D), lambda i: (0, 0))
    row = pl.BlockSpec((tm, SB_D), lambda i: (i, 0))
    return pl.pallas_call(
        body, name="sb_prep_bwd", grid=(T // tm,),
        in_specs=[row, row, row, col(3), col(4), vec, vec],
        out_specs=[pl.BlockSpec((tm, 3 * SB_D), lambda i: (i, 0)), vec, vec],
        out_shape=[jax.ShapeDtypeStruct((T, 3 * SB_D), CDT), jax.ShapeDtypeStruct((1, SB_D), F32),
                   jax.ShapeDtypeStruct((1, SB_D), F32)],
        compiler_params=_cparams(("arbitrary",)),
    )(dqn, dkn, dv, proj, proj, gq, gk)


def _sb_masks():
    lane = lax.broadcasted_iota(jnp.int32, (1, 128), 1)
    m = [lane < SB_HEAD_DIM, lane >= SB_HEAD_DIM]
    return m, [x.astype(F32) for x in m]


def _sb_fwd(qn, kn, vb, blocks):
    T = qn.shape[0]
    nq = T // SB_TILE
    B, P = SB_TILE, SB_PAIR
    hs = range(2)
    n = len(blocks)
    nhp = SB_D // P

    def body(*refs):
        q_ref, k_ref, v_ref = refs[:3]
        o_ref, l_ref, done_ref = refs[3 + n:6 + n]
        acc_ref = refs[6 + 2 * n]
        hp, qb = pl.program_id(0), pl.program_id(1)
        g_start, g_forward, g_finish = _gather_phases(refs[3:3 + n], refs[6 + n:6 + 2 * n], *refs[7 + 2 * n:])
        pl.when((hp == 0) & (qb == 0))(g_start)
        pl.when((hp == nhp - 1) & (qb == 0))(g_forward)
        m, mf = _sb_masks()
        row = lax.broadcasted_iota(jnp.int32, (B, B), 0)
        col = lax.broadcasted_iota(jnp.int32, (B, B), 1)
        later = (row > col).astype(BF16)
        past = col < row
        q2 = q_ref[...]
        qm = [jnp.where(m[h], q2, jnp.zeros_like(q2)) for h in hs]
        acc_ref[...] = jnp.zeros_like(acc_ref)

        def keys(kb):
            return k_ref[pl.ds(pl.multiple_of(kb * B, B), B), :]

        def values(kb):
            return v_ref[pl.ds(pl.multiple_of(kb * B, B), B), :]

        def scores(kb):
            k2 = keys(kb)
            return [_dg(qm[h], k2, _NT) for h in hs]

        def run(tiles, R):
            zs, cum, rsum = {}, {}, {}
            for t, (z, _, diag) in enumerate(tiles):
                for h in hs:
                    sp = _softplus(z[h])
                    lk = jnp.where(past, sp, 0.0) if diag else sp
                    zs[t, h] = z[h] - sp
                    cum[t, h] = _dot_exact_rhs(lk, later, SB_SPLIT_LK)
                    rsum[t, h] = jnp.sum(lk, axis=1, keepdims=True)
            R = list(R)
            for t, (_, kb, diag) in enumerate(tiles):
                v2 = values(kb)
                for h in hs:
                    w = jnp.exp(zs[t, h] - (cum[t, h] + R[h]))
                    if diag:
                        w = jnp.where(past, w, 0.0)
                    acc_ref[h] += _dg(w, v2, _NN)
                R = [R[h] + rsum[t, h] for h in hs]
            return tuple(R)

        def live(r):
            return (jnp.minimum(jnp.min(r[0]), jnp.min(r[1])) < SB_DEAD).astype(jnp.int32)

        zero = jnp.zeros((B, 1), F32)
        R = lax.cond(qb > 0,
                     lambda r: run([(scores(qb), qb, True), (scores(jnp.maximum(qb - 1, 0)), jnp.maximum(qb - 1, 0), False)], r),
                     lambda r: run([(scores(qb), qb, True)], r), (zero, zero))
        rest = jnp.maximum(qb - 1, 0)
        npairs = rest // 2

        def pair(carry):
            i, r, za, zb = carry[0], carry[2:4], carry[4:6], carry[6:8]
            ka = qb - 2 - 2 * i
            nxt = (scores(jnp.maximum(ka - 2, 0)), scores(jnp.maximum(ka - 3, 0)))
            r = run([(za, ka, False), (zb, ka - 1, False)], r)
            return (i + 1, live(r), *r, *nxt[0], *nxt[1])

        out = lax.while_loop(lambda c: (c[0] < npairs) & (c[1] > 0), pair,
                             (jnp.int32(0), live(R), *R, *scores(jnp.maximum(qb - 2, 0)), *scores(jnp.maximum(qb - 3, 0))))
        last = (rest % 2 == 1) & (out[0] == npairs) & (out[1] > 0)
        tile0 = jnp.int32(0)
        R = lax.cond(last, lambda r: run([(scores(tile0), tile0, False)], r), lambda r: r, out[2:4])
        done_ref[hp, qb] = 2 * out[0] + last.astype(jnp.int32)
        o_ref[...] = (acc_ref[0] * mf[0] + acc_ref[1] * mf[1]).astype(o_ref.dtype)
        l_ref[0] = R[0] * mf[0] + R[1] * mf[1]
        pl.when((hp == nhp - 1) & (qb == nq - 1))(g_finish)

    slab = pl.BlockSpec((T, P), lambda hp, qb: (0, hp))
    blk = pl.BlockSpec((B, P), lambda hp, qb: (qb, hp))
    anyspec = pl.BlockSpec(memory_space=pl.ANY)
    outs = pl.pallas_call(
        body, name="sb_fwd", grid=(nhp, nq),
        in_specs=[blk, slab, slab] + [anyspec] * n,
        out_specs=[blk, pl.BlockSpec((1, B, P), lambda hp, qb: (hp, qb, 0)), pl.BlockSpec(memory_space=pltpu.SMEM)]
        + [anyspec] * n,
        out_shape=[jax.ShapeDtypeStruct((T, SB_D), CDT), jax.ShapeDtypeStruct((nhp, T, P), F32),
                   jax.ShapeDtypeStruct((nhp, nq), jnp.int32)]
        + [jax.ShapeDtypeStruct((N_DEV,) + b.shape, b.dtype) for b in blocks],
        scratch_shapes=[pltpu.VMEM((2, B, P), F32)] + _exchange_sems(n),
        compiler_params=_cparams(("arbitrary", "arbitrary")),
    )(qn, kn, vb, *blocks)
    return outs[0], outs[1], outs[2], outs[3:]


def _sb_bwd(d_o, qn, kn, vb, lsum, done):
    T = qn.shape[0]
    nq = T // SB_TILE
    B, P = SB_TILE, SB_PAIR
    hs = range(2)
    do_first = (d_o.shape[1] - SB_D) // P

    def body(do_ref, q_ref, k_ref, v_ref, l_ref, done_ref, dq_ref, dk_ref, dv_ref, dqacc_ref):
        qb = pl.program_id(1)

        @pl.when(qb == 0)
        def _():
            dk_ref[...] = jnp.zeros_like(dk_ref)
            dv_ref[...] = jnp.zeros_like(dv_ref)

        m, mf = _sb_masks()
        row = lax.broadcasted_iota(jnp.int32, (B, B), 0)
        col = lax.broadcasted_iota(jnp.int32, (B, B), 1)
        upto = (row <= col).astype(BF16)
        before = (row < col).astype(BF16)
        past = col < row
        q2 = q_ref[...]
        qm = [jnp.where(m[h], q2, jnp.zeros_like(q2)) for h in hs]
        do2 = do_ref[...]
        dom = [jnp.where(m[h], do2, 0.0).astype(CDT) for h in hs]
        lb = l_ref[0]
        ltot = [lb[:, 0:1], lb[:, SB_HEAD_DIM:SB_HEAD_DIM + 1]]
        dqacc_ref[...] = jnp.zeros_like(dqacc_ref)

        def rows(kb):
            return pl.ds(pl.multiple_of(kb * B, B), B)

        def scores(kb):
            k2, v2 = k_ref[rows(kb), :], v_ref[rows(kb), :]
            return [_dg(qm[h], k2, _NT) for h in hs] + [_dg(dom[h], v2, _NT) for h in hs]

        def run(tiles, carry):
            Ps, Pg = list(carry[0]), list(carry[1])
            zs, sp_, cum, rest = {}, {}, {}, {}
            for t, tl in enumerate(tiles):
                diag = tl[5]
                for h in hs:
                    sp = _softplus(tl[h])
                    lk = jnp.where(past, sp, 0.0) if diag else sp
                    zs[t, h], sp_[t, h] = tl[h] - sp, sp
                    cum[t, h] = _dot_exact_rhs(lk, upto, SB_SPLIT_LK)
                    rest[t, h] = ltot[h] - Ps[h]
                    Ps[h] = Ps[h] + jnp.sum(lk, axis=1, keepdims=True)
            w, g, gx = {}, {}, {}
            for t, tl in enumerate(tiles):
                diag = tl[5]
                for h in hs:
                    wt = jnp.exp(zs[t, h] - (rest[t, h] - cum[t, h]))
                    if diag:
                        wt = jnp.where(past, wt, 0.0)
                    w[t, h] = wt
                    g[t, h] = wt * tl[2 + h]
                    gx[t, h] = _dot_exact_rhs(g[t, h], before, SB_SPLIT_G) + Pg[h]
                    Pg[h] = Pg[h] + jnp.sum(g[t, h], axis=1, keepdims=True)
            for t, tl in enumerate(tiles):
                kb, diag = tl[4], tl[5]
                k2 = k_ref[rows(kb), :]
                for h in hs:
                    sneg = jnp.exp(-sp_[t, h])
                    dz = g[t, h] * sneg - (1.0 - sneg) * gx[t, h]
                    if diag:
                        dz = jnp.where(past, dz, 0.0)
                    dz_c = dz.astype(CDT)
                    dv_ref[rows(kb), :] += _dg(w[t, h], dom[h], _TN)
                    dk_ref[rows(kb), :] += _dg(dz_c, qm[h], _TN)
                    dqacc_ref[h] += _dg(dz_c, k2, _NN)
            return tuple(Ps), tuple(Pg)

        zero = jnp.zeros((B, 1), F32)
        rest_tiles = jnp.maximum(qb - 1, 0)
        done = jnp.clip(done_ref[pl.program_id(0), qb], 0, rest_tiles)
        npairs = done // 2
        tile0 = jnp.int32(0)
        sums = lax.cond(done % 2 == 1, lambda s: run([(*scores(tile0), tile0, False)], s), lambda s: s,
                        ((zero, zero), (zero, zero)))
        left0 = qb - 1 - 2 * npairs

        def pair(i, carry):
            sums, ta, tb = (carry[0:2], carry[2:4]), carry[4:8], carry[8:12]
            ka = left0 + 2 * i
            nxt = scores(jnp.minimum(ka + 2, qb)) + scores(jnp.minimum(ka + 3, qb))
            Ps, Pg = run([(*ta, ka, False), (*tb, ka + 1, False)], sums)
            return (*Ps, *Pg, *nxt)

        first = jnp.clip(left0, 0, qb)
        out = lax.fori_loop(0, npairs, pair,
                            (*sums[0], *sums[1], *scores(first), *scores(jnp.minimum(first + 1, qb))))
        sums = (out[0:2], out[2:4])
        near = jnp.maximum(qb - 1, 0)
        lax.cond(qb > 0,
                 lambda s: run([(*scores(near), near, False), (*scores(qb), qb, True)], s),
                 lambda s: run([(*scores(qb), qb, True)], s), sums)
        dq_ref[...] = dqacc_ref[0] * mf[0] + dqacc_ref[1] * mf[1]

    slab = pl.BlockSpec((T, P), lambda hp, qb: (0, hp))
    blk = pl.BlockSpec((B, P), lambda hp, qb: (qb, hp))
    return pl.pallas_call(
        body, name="sb_bwd", grid=(SB_D // P, nq),
        in_specs=[pl.BlockSpec((B, P), lambda hp, qb: (qb, hp + do_first)), blk, slab, slab,
                  pl.BlockSpec((1, B, P), lambda hp, qb: (hp, qb, 0)),
                  pl.BlockSpec(memory_space=pltpu.SMEM)],
        out_specs=[blk, slab, slab],
        out_shape=[jax.ShapeDtypeStruct((T, SB_D), F32)] * 3,
        scratch_shapes=[pltpu.VMEM((2, B, P), F32)],
        compiler_params=_cparams(("arbitrary", "arbitrary")),
    )(d_o, qn, kn, vb, lsum, done)


def _regroup_in_rows(wt):
    cut = 2 * GLA_DK + 2 * GLA_DV
    pad = jnp.zeros((IN_PAD - IN_WIDTH, wt.shape[1]), wt.dtype)
    return jnp.concatenate([wt[:cut], wt[cut + GLA_GATE_RANK:], wt[cut:cut + GLA_GATE_RANK], pad], axis=0)


def _ungroup_in_rows(gt):
    cut = 2 * GLA_DK + 2 * GLA_DV
    return jnp.concatenate([gt[:cut], gt[3072:3072 + GLA_GATE_RANK], gt[cut:3072]], axis=0)


def _colsum(v, *, name, tm=512):
    T, C = v.shape

    def body(v_ref, o_ref):
        i = pl.program_id(0)
        part = jnp.sum(v_ref[...], axis=0, keepdims=True)

        @pl.when(i == 0)
        def _():
            o_ref[...] = part

        @pl.when(i > 0)
        def _():
            o_ref[...] += part

    return pl.pallas_call(
        body, name=name, grid=(T // tm,),
        in_specs=[pl.BlockSpec((tm, C), lambda i: (i, 0))], out_specs=pl.BlockSpec((1, C), lambda i: (0, 0)),
        out_shape=jax.ShapeDtypeStruct((1, C), F32),
        compiler_params=_cparams(("arbitrary",)),
    )(v)


def _ffn_fwd(h, g_norm, wgu_t, wd, tag):
    hf = _rms_fwd(h, g_norm, name=f"ffn{tag}_norm")
    ab, s = _swiglu_up(hf, *wgu_t, name=f"ffn{tag}_up")
    h_out = _matmul(s, wd, mode='nn', out_dtype=F32, name=f"ffn{tag}_down", tm=512, tn=D_MODEL, tk=D_FF, residual=h)
    return h_out, (hf, ab, s)


def _ffn_bwd(dh, dh_c, h_in, g_norm, wgu_t, wd, saved, tag):
    hf, ab, s = saved
    dwd = _matmul(s, dh_c, mode='tn', out_dtype=F32, name=f"ffn{tag}_dwd", tm=D_FF // 2, tn=D_MODEL, tk=TK_TOKENS)
    dab = _swiglu_dact(dh_c, wd, ab, name=f"ffn{tag}_dact")
    dwgu_t = _matmul(dab, hf, mode='tn', out_dtype=F32, name=f"ffn{tag}_dwgu", tm=D_FF // 2, tn=D_MODEL, tk=TK_TOKENS)
    dhf = _swiglu_dhf(dab, *wgu_t, name=f"ffn{tag}_dhf")
    dh_in, dh_in_c, dg, dh_in_sum = _rms_bwd(dhf, h_in, g_norm, dh, name=f"ffn{tag}_dnorm")
    return dh_in, dh_in_c, dwgu_t, dwd, dg, dh_in_sum


def _late_weights(gathered):
    g_out, g_pw1, g_pw2, g_gate, g_up, g_down = gathered
    Dm = D_MODEL
    return {
        'hy_w_out': g_out.reshape(Dm, Dm),
        'cv_w_pw1_t': g_pw1.reshape(2 * Dm, Dm),
        'cv_w_pw2': g_pw2.reshape(Dm, Dm),
        'ffn_wgu_t': [(g_gate[:, l].reshape(D_FF, Dm), g_up[:, l].reshape(D_FF, Dm)) for l in range(2)],
        'ffn_w_down': [g_down[:, l].reshape(D_FF, Dm) for l in range(2)],
    }


def _local_step(x, tgt, W, late_blocks):
    row = lambda v: v.reshape(1, -1)
    win_p = _regroup_in_rows(W['hy_w_in_t'])
    wg2p = jnp.pad(W['hy_w_gate2'], ((0, 128 - GLA_GATE_RANK), (0, 0)))
    b_gate = row(W['hy_b_gate'])
    g_gla = row(W['hy_gla_norm'])
    gq = jnp.tile(W['hy_sb_q_norm'].reshape(-1), SB_D // SB_HEAD_DIM).reshape(1, SB_D)
    gk = jnp.tile(W['hy_sb_k_norm'].reshape(-1), SB_D // SB_HEAD_DIM).reshape(1, SB_D)
    w_dw = jnp.pad(W['cv_w_dw'], ((0, CONV_HALO - CONV_WIDTH), (0, 0)))
    mixn = [row(W['mix_norm'][l]) for l in range(2)]
    ffnn = [row(W['ffn_norm'][l]) for l in range(2)]

    hn0 = _rms_fwd(x, mixn[0], name="mix0_norm")
    proj = _matmul(hn0, win_p, mode='nt', out_dtype=F32, name="hy_in", tm=256, tn=IN_PAD, tk=D_MODEL)
    o_gla, o_raw, states = _gla_fwd(proj, wg2p, b_gate, g_gla)
    qn, kn, vb = _sb_prep(proj, gq, gk)
    o_sb, lsum, sb_done, gathered = _sb_fwd(qn, kn, vb, late_blocks)
    W = {**W, **_late_weights(gathered)}
    w_out, wgu, wd = W['hy_w_out'], W['ffn_wgu_t'], W['ffn_w_down']
    o_mix = jnp.concatenate([o_gla, o_sb], axis=1)
    h1 = _matmul(o_mix, w_out, mode='nn', out_dtype=F32, name="hy_out", tm=512, tn=D_MODEL, tk=D_MODEL, residual=x)
    h2, ffn0_saved = _ffn_fwd(h1, ffnn[0], wgu[0], wd[0], 0)
    hn1 = _rms_fwd(h2, mixn[1], name="mix1_norm")
    a_cv = _matmul(hn1, W['cv_w_pw1_t'], mode='nt', out_dtype=F32, name="cv_pw1", tm=512, tn=2 * D_MODEL, tk=D_MODEL,
                   bias=row(W['cv_b_pw1']))
    s_cv, u_cv, c_cv = _conv_fwd(a_cv, w_dw, row(W['cv_b_dw']), row(W['cv_ln_g']), row(W['cv_ln_b']))
    h3 = _matmul(s_cv, W['cv_w_pw2'], mode='nn', out_dtype=F32, name="cv_pw2", tm=512, tn=D_MODEL, tk=D_MODEL,
                 bias=row(W['cv_b_pw2']), residual=h2)
    h4, ffn1_saved = _ffn_fwd(h3, ffnn[1], wgu[1], wd[1], 1)
    sq_err, dy, dy_c = _loss_head(h4, tgt)

    G = {}
    dh3, dh3_c, dwgu1, dwd1, dg_ffn1, G['cv_b_pw2'] = _ffn_bwd(dy, dy_c, h3, ffnn[1], wgu[1], wd[1], ffn1_saved, 1)
    G['cv_w_pw2'] = _matmul(s_cv, dh3_c, mode='tn', out_dtype=F32, name="cv_dw2", tm=D_MODEL, tn=D_MODEL, tk=TK_TOKENS)
    ds_cv = _matmul(dh3_c, W['cv_w_pw2'], mode='nt', out_dtype=F32, name="cv_ds", tm=512, tn=D_MODEL, tk=D_MODEL)
    F8 = D_FF // N_DEV
    early_own = [G['cv_w_pw2'].reshape(N_DEV, D_MODEL // N_DEV, D_MODEL),
                 dwgu1.reshape(2, N_DEV, F8, D_MODEL).transpose(1, 0, 2, 3).reshape(N_DEV, 2 * F8, D_MODEL),
                 dwd1.reshape(N_DEV, F8, D_MODEL)]
    (da_cv, db1, dwdw, dbdw, dlng, dlnb), early_recv = _conv_bwd(
        ds_cv, c_cv, u_cv, a_cv, w_dw, row(W['cv_ln_g']), row(W['cv_ln_b']), [a.astype(BF16) for a in early_own])
    G['cv_b_pw1'] = db1
    G['cv_w_dw'] = dwdw[:CONV_WIDTH]
    G['cv_b_dw'], G['cv_ln_g'], G['cv_ln_b'] = dbdw, dlng, dlnb
    G['cv_w_pw1_t'] = _matmul(da_cv, hn1, mode='tn', out_dtype=F32, name="cv_dw1", tm=D_MODEL, tn=D_MODEL, tk=TK_TOKENS)
    dhn1 = _matmul(da_cv, W['cv_w_pw1_t'], mode='nn', out_dtype=F32, name="cv_dhn", tm=512, tn=D_MODEL, tk=2 * D_MODEL)
    dh2, dh2_c, dg_mix1, _ = _rms_bwd(dhn1, h2, mixn[1], dh3, name="mix1_dnorm")
    dh1, dh1_c, dwgu0, dwd0, dg_ffn0, _ = _ffn_bwd(dh2, dh2_c, h1, ffnn[0], wgu[0], wd[0], ffn0_saved, 0)
    G['hy_w_out'] = _matmul(o_mix, dh1_c, mode='tn', out_dtype=F32, name="hy_dwout", tm=D_MODEL, tn=D_MODEL, tk=TK_TOKENS)
    d_omix = _matmul(dh1_c, w_out, mode='nt', out_dtype=F32, name="hy_domix", tm=512, tn=D_MODEL, tk=D_MODEL)
    dgla, dglr, dwg2, dbg, dgg = _gla_bwd(d_omix, proj, o_raw, states, wg2p, b_gate, g_gla)
    dqn, dkn, dvs = _sb_bwd(d_omix, qn, kn, vb, lsum, sb_done)
    dsb, dgq, dgk = _sb_prep_bwd(dqn, dkn, dvs, proj, gq, gk)
    dproj = jnp.concatenate([dgla, dsb, dglr], axis=1)
    dwin_p = _matmul(dproj, hn0, mode='tn', out_dtype=F32, name="hy_dwin", tm=IN_PAD // 5, tn=D_MODEL, tk=TK_TOKENS)
    dhn0 = _matmul(dproj, win_p, mode='nn', out_dtype=F32, name="hy_dhn", tm=256, tn=D_MODEL, tk=IN_PAD)
    dx, _, dg_mix0, _ = _rms_bwd(dhn0, x, mixn[0], dh1, name="mix0_dnorm")

    G['hy_w_in_t'] = _ungroup_in_rows(dwin_p)
    G['hy_w_gate2'] = dwg2[:GLA_GATE_RANK]
    G['hy_b_gate'] = dbg
    G['hy_gla_norm'] = dgg
    G['hy_sb_q_norm'] = dgq.reshape(SB_D // SB_HEAD_DIM, SB_HEAD_DIM).sum(axis=0, keepdims=True)
    G['hy_sb_k_norm'] = dgk.reshape(SB_D // SB_HEAD_DIM, SB_HEAD_DIM).sum(axis=0, keepdims=True)
    G['mix_norm'] = jnp.concatenate([dg_mix0, dg_mix1], axis=0)
    G['ffn_norm'] = jnp.concatenate([dg_ffn0, dg_ffn1], axis=0)
    G['ffn_wgu_t0'] = dwgu0
    G['ffn_w_down0'] = dwd0
    return sq_err, dx, G, (early_own, early_recv)


MESH_IDS = pl.DeviceIdType.MESH
N_PEER = N_DEV - 1


def _exchange_sems(n):
    return [pltpu.SemaphoreType.DMA((n * N_PEER,)), pltpu.SemaphoreType.DMA((n * N_PEER,)),
            pltpu.SemaphoreType.DMA((n,))]


def _gather_phases(x_refs, out_refs, send_sems, recv_sems, local_sems):
    n = len(x_refs)
    x, y, c = lax.axis_index("x"), lax.axis_index("y"), lax.axis_index("c")
    me, sibling = (x, y, c), (x, y, 1 - c)
    chips = [(1 - x, y), (x, 1 - y), (1 - x, 1 - y)]

    def slot(a, px, py, pc):
        return out_refs[a].at[4 * px + 2 * py + pc]

    def copy(a, k, blk, to, src=None):
        return pltpu.make_async_remote_copy(
            src_ref=slot(a, *blk) if src is None else src, dst_ref=slot(a, *blk),
            send_sem=send_sems.at[a * N_PEER + k], recv_sem=recv_sems.at[a * N_PEER + k],
            device_id=to, device_id_type=MESH_IDS)

    def local(a):
        return pltpu.make_async_copy(x_refs[a], slot(a, *me), local_sems.at[a])

    def first(a):
        return [copy(a, 0, me, sibling, src=x_refs[a])] + [copy(a, 1 + j, me, (*chip, c), src=x_refs[a])
                                                           for j, chip in enumerate(chips)]

    def passed(a):
        return [copy(a, 4 + j, (*chip, c), sibling) for j, chip in enumerate(chips)]

    def start():
        for a in range(n):
            local(a).start()
        for a in range(n):
            for cp in first(a):
                cp.start()

    def forward():
        for a in range(n):
            for j, chip in enumerate(chips):
                copy(a, 1 + j, (*chip, c), me).wait_recv()
                copy(a, 4 + j, (*chip, c), sibling).start()

    def finish():
        for a in range(n):
            copy(a, 0, sibling, me).wait_recv()
            for j, chip in enumerate(chips):
                copy(a, 4 + j, (*chip, 1 - c), me).wait_recv()
        for a in range(n):
            for cp in first(a) + passed(a):
                cp.wait_send()
            local(a).wait()

    return start, forward, finish


def _all_gather(blocks):
    n = len(blocks)

    def body(*refs):
        start, forward, finish = _gather_phases(refs[:n], refs[n:2 * n], *refs[2 * n:])
        start()
        forward()
        finish()

    anyspec = pl.BlockSpec(memory_space=pl.ANY)
    return pl.pallas_call(
        body, name="fsdp_all_gather",
        out_shape=[jax.ShapeDtypeStruct((N_DEV,) + b.shape, b.dtype) for b in blocks],
        in_specs=[anyspec] * n, out_specs=[anyspec] * n,
        scratch_shapes=_exchange_sems(n),
    )(*blocks)


def _scatter_phases(s_refs, r_refs, send_sems, recv_sems, local_sems):
    n = len(s_refs)
    x, y, c = lax.axis_index("x"), lax.axis_index("y"), lax.axis_index("c")
    me = 4 * x + 2 * y + c

    def local(a):
        return pltpu.make_async_copy(s_refs[a].at[me], r_refs[a].at[me], local_sems.at[a])

    def copy(a, k):
        px, py, pc = x ^ ((k >> 2) & 1), y ^ ((k >> 1) & 1), c ^ (k & 1)
        return pltpu.make_async_remote_copy(
            src_ref=s_refs[a].at[4 * px + 2 * py + pc], dst_ref=r_refs[a].at[me],
            send_sem=send_sems.at[a * N_PEER + k - 1], recv_sem=recv_sems.at[a * N_PEER + k - 1],
            device_id=(px, py, pc), device_id_type=MESH_IDS)

    def start():
        for a in range(n):
            local(a).start()
        for a in range(n):
            for k in range(1, N_DEV):
                copy(a, k).start()

    def finish():
        for a in range(n):
            for k in range(1, N_DEV):
                copy(a, k).wait()
            local(a).wait()

    return start, finish


def _scatter_exchange(sends):
    n = len(sends)

    def body(*refs):
        start, finish = _scatter_phases(refs[:n], refs[n:2 * n], *refs[2 * n:])
        start()
        finish()

    anyspec = pl.BlockSpec(memory_space=pl.ANY)
    return pl.pallas_call(
        body, name="fsdp_scatter_exchange",
        out_shape=[jax.ShapeDtypeStruct(s.shape, s.dtype) for s in sends],
        in_specs=[anyspec] * n, out_specs=[anyspec] * n,
        scratch_shapes=_exchange_sems(n),
    )(*sends)


def _sum_contrib(recv, own, *, name, tr):
    _, R, C = recv.shape
    assert R % tr == 0

    def body(r_ref, own_ref, g_ref):
        me = 4 * lax.axis_index("x") + 2 * lax.axis_index("y") + lax.axis_index("c")
        g = jnp.zeros((tr, C), F32)
        for s in range(N_DEV):
            g = g + jnp.where(me == s, own_ref[...], r_ref[s].astype(F32))
        g_ref[...] = g

    row = pl.BlockSpec((tr, C), lambda i: (i, 0))
    return pl.pallas_call(
        body, name=name, grid=(R // tr,),
        in_specs=[pl.BlockSpec((N_DEV, tr, C), lambda i: (0, i, 0)), row], out_specs=row,
        out_shape=jax.ShapeDtypeStruct((R, C), F32),
        compiler_params=_cparams(("parallel",)),
    )(recv, own)


def _adamw(g, w, m, v, *, name, tr):
    R, C = g.shape
    assert R % tr == 0

    def body(g_ref, w_ref, m_ref, v_ref, d_ref, mo_ref, vo_ref):
        gv = g_ref[...]
        mn = ADAM_B1 * m_ref[...] + (1.0 - ADAM_B1) * gv
        vn = ADAM_B2 * v_ref[...] + (1.0 - ADAM_B2) * (gv * gv)
        m_hat = mn / (1.0 - ADAM_B1 ** ADAM_STEP)
        v_hat = vn / (1.0 - ADAM_B2 ** ADAM_STEP)
        d_ref[...] = -ADAM_LR * (m_hat / (jnp.sqrt(v_hat) + ADAM_EPS) + ADAM_WD * w_ref[...])
        mo_ref[...] = mn
        vo_ref[...] = vn

    row = pl.BlockSpec((tr, C), lambda i: (i, 0))
    return pl.pallas_call(
        body, name=name, grid=(R // tr,),
        in_specs=[row] * 4, out_specs=[row] * 3,
        out_shape=[jax.ShapeDtypeStruct((R, C), F32)] * 3,
        compiler_params=_cparams(("parallel",)),
    )(g, w, m, v)


SMALL_SHARDED = ('hy_w_gate2', 'cv_b_pw1', 'cv_w_dw', 'cv_b_dw', 'cv_ln_g', 'cv_ln_b', 'cv_b_pw2')
SMALL_REPLICATED = ('mix_norm', 'ffn_norm', 'hy_b_gate', 'hy_gla_norm', 'hy_sb_q_norm', 'hy_sb_k_norm')
LANES = 128


def _small_rows(n):
    return -(-n // (8 * LANES)) * 8


def _pack_small(parts, lead=()):
    out = []
    for p in parts:
        n = p.shape[-1]
        p = jnp.pad(p, [(0, 0)] * len(lead) + [(0, _small_rows(n) * LANES - n)])
        out.append(p.reshape(*lead, _small_rows(n), LANES))
    return jnp.concatenate(out, axis=len(lead))


def _unpack_small(packed, sizes, lead=()):
    out, r0 = [], 0
    for n in sizes:
        r = _small_rows(n)
        out.append(packed[..., r0:r0 + r, :].reshape(*lead, r * LANES)[..., :n])
        r0 += r
    return out


def _to_blocks(full, axis):
    shp = full.shape
    t = full.reshape(shp[:axis] + (N_DEV, shp[axis] // N_DEV) + shp[axis + 1:])
    return jnp.moveaxis(t, axis, 0)


def _from_blocks(blocks, axis):
    t = jnp.moveaxis(blocks, 0, axis)
    shp = t.shape
    return t.reshape(shp[:axis] + (shp[axis] * shp[axis + 1],) + shp[axis + 2:])


def kernel(x, mix_norm, ffn_norm, hy_w_in, hy_w_gate2, hy_b_gate, hy_gla_norm, hy_sb_q_norm, hy_sb_k_norm, hy_w_out, cv_w_pw1, cv_b_pw1, cv_w_dw, cv_b_dw, cv_ln_g, cv_ln_b, cv_w_pw2, cv_b_pw2, ffn_w_gate, ffn_w_up, ffn_w_down, loss_target, m_mix_norm, m_ffn_norm, m_hy_w_in, m_hy_w_gate2, m_hy_b_gate, m_hy_gla_norm, m_hy_sb_q_norm, m_hy_sb_k_norm, m_hy_w_out, m_cv_w_pw1, m_cv_b_pw1, m_cv_w_dw, m_cv_b_dw, m_cv_ln_g, m_cv_ln_b, m_cv_w_pw2, m_cv_b_pw2, m_ffn_w_gate, m_ffn_w_up, m_ffn_w_down, v_mix_norm, v_ffn_norm, v_hy_w_in, v_hy_w_gate2, v_hy_b_gate, v_hy_gla_norm, v_hy_sb_q_norm, v_hy_sb_k_norm, v_hy_w_out, v_cv_w_pw1, v_cv_b_pw1, v_cv_w_dw, v_cv_b_dw, v_cv_ln_g, v_cv_ln_b, v_cv_w_pw2, v_cv_b_pw2, v_ffn_w_gate, v_ffn_w_up, v_ffn_w_down):
    w_loc = dict(zip(WEIGHT_NAMES, (mix_norm, ffn_norm, hy_w_in, hy_w_gate2, hy_b_gate, hy_gla_norm, hy_sb_q_norm, hy_sb_k_norm, hy_w_out, cv_w_pw1, cv_b_pw1, cv_w_dw, cv_b_dw, cv_ln_g, cv_ln_b, cv_w_pw2, cv_b_pw2, ffn_w_gate, ffn_w_up, ffn_w_down)))
    m_loc = dict(zip(WEIGHT_NAMES, (m_mix_norm, m_ffn_norm, m_hy_w_in, m_hy_w_gate2, m_hy_b_gate, m_hy_gla_norm, m_hy_sb_q_norm, m_hy_sb_k_norm, m_hy_w_out, m_cv_w_pw1, m_cv_b_pw1, m_cv_w_dw, m_cv_b_dw, m_cv_ln_g, m_cv_ln_b, m_cv_w_pw2, m_cv_b_pw2, m_ffn_w_gate, m_ffn_w_up, m_ffn_w_down)))
    v_loc = dict(zip(WEIGHT_NAMES, (v_mix_norm, v_ffn_norm, v_hy_w_in, v_hy_w_gate2, v_hy_b_gate, v_hy_gla_norm, v_hy_sb_q_norm, v_hy_sb_k_norm, v_hy_w_out, v_cv_w_pw1, v_cv_b_pw1, v_cv_w_dw, v_cv_b_dw, v_cv_ln_g, v_cv_ln_b, v_cv_w_pw2, v_cv_b_pw2, v_ffn_w_gate, v_ffn_w_up, v_ffn_w_down)))

    Dm, F8 = D_MODEL, D_FF // N_DEV
    tr_ = lambda a: jnp.swapaxes(a, -1, -2)

    small_local = _pack_small([w_loc[n].reshape(-1) for n in SMALL_SHARDED])
    g_in, g_small = _all_gather([tr_(hy_w_in[0]).astype(BF16), small_local])
    late_blocks = [hy_w_out[0].astype(BF16),
                   tr_(cv_w_pw1[0]).astype(BF16),
                   cv_w_pw2[0].astype(BF16),
                   tr_(ffn_w_gate).astype(BF16),
                   tr_(ffn_w_up).astype(BF16),
                   ffn_w_down.astype(BF16)]
    small_sizes = [w_loc[n].size for n in SMALL_SHARDED]
    small_full = dict(zip(SMALL_SHARDED, _unpack_small(g_small, small_sizes, lead=(N_DEV,))))
    W = {n: w_loc[n] for n in SMALL_REPLICATED}
    W['hy_w_in_t'] = g_in.reshape(IN_WIDTH, Dm)
    W['hy_w_gate2'] = _from_blocks(small_full['hy_w_gate2'].reshape(N_DEV, GLA_GATE_RANK, GLA_DK // N_DEV), 1).astype(BF16)
    W['cv_w_dw'] = _from_blocks(small_full['cv_w_dw'].reshape(N_DEV, CONV_WIDTH, Dm // N_DEV), 1)
    for n in ('cv_b_pw1', 'cv_b_dw', 'cv_ln_g', 'cv_ln_b', 'cv_b_pw2'):
        W[n] = small_full[n].reshape(-1)

    sq_err, dx, G, (early_own, early_recv) = _local_step(x[0], loss_target[0], W, late_blocks)

    own_f32 = [
        G['hy_w_in_t'].reshape(N_DEV, IN_WIDTH // N_DEV, Dm),
        G['hy_w_out'].reshape(N_DEV, Dm // N_DEV, Dm),
        G['cv_w_pw1_t'].reshape(N_DEV, 2 * Dm // N_DEV, Dm),
        G['ffn_wgu_t0'].reshape(2, N_DEV, F8, Dm).transpose(1, 0, 2, 3).reshape(N_DEV, 2 * F8, Dm),
        G['ffn_w_down0'].reshape(N_DEV, F8, Dm),
    ]
    small_parts = []
    for n in SMALL_SHARDED:
        axis = SHARD_AXIS[n] - 1
        shard = w_loc[n].shape[1:]
        full = shard[:axis] + (shard[axis] * N_DEV,) + shard[axis + 1:]
        small_parts.append(_to_blocks(G[n].reshape(full), axis).reshape(N_DEV, -1))
    for n in SMALL_REPLICATED:
        small_parts.append(jnp.broadcast_to(G[n].reshape(1, -1), (N_DEV, G[n].size)))
    small_parts.append(jnp.broadcast_to(sq_err.reshape(1, 1), (N_DEV, 1)))
    send_small = _pack_small(small_parts, lead=(N_DEV,))
    recv = _scatter_exchange([a.astype(BF16) for a in own_f32] + [send_small])
    me = 4 * lax.axis_index("x") + 2 * lax.axis_index("y") + lax.axis_index("c")
    tags = ['hy_w_in', 'hy_w_out', 'cv_w_pw1', 'ffn_wgu0', 'ffn_w_down0', 'small', 'cv_w_pw2', 'ffn_wgu1', 'ffn_w_down1']
    own_all = own_f32 + [send_small] + list(early_own)
    recv_all = list(recv) + list(early_recv)
    gsum = dict((t, _sum_contrib(r, lax.dynamic_index_in_dim(o, me, 0, keepdims=False), name=f"sum_{t}", tr=r.shape[1]))
                for t, r, o in zip(tags, recv_all, own_all))

    grad = {}
    grad['hy_w_in'] = tr_(gsum['hy_w_in'])[None]
    grad['hy_w_out'] = gsum['hy_w_out'][None]
    grad['cv_w_pw1'] = tr_(gsum['cv_w_pw1'])[None]
    grad['cv_w_pw2'] = gsum['cv_w_pw2'][None]
    gu = jnp.stack([gsum['ffn_wgu0'], gsum['ffn_wgu1']]).reshape(2, 2, F8, Dm)
    grad['ffn_w_gate'] = tr_(gu[:, 0])
    grad['ffn_w_up'] = tr_(gu[:, 1])
    grad['ffn_w_down'] = jnp.stack([gsum['ffn_w_down0'], gsum['ffn_w_down1']])
    small_names = SMALL_SHARDED + SMALL_REPLICATED
    small_all = [w_loc[n].size for n in small_names]
    *small_grads, sq_sum = _unpack_small(gsum['small'], small_all + [1])
    for n, a in zip(small_names, small_grads):
        grad[n] = a.reshape(w_loc[n].shape)
    loss = 0.5 / Dm * sq_sum[0]

    delta, new_m, new_v = {}, {}, {}
    view = {'hy_w_in': (Dm, 256), 'hy_w_out': (Dm // N_DEV, Dm // N_DEV), 'cv_w_pw1': (Dm, 256),
            'cv_w_pw2': (Dm // N_DEV, Dm // N_DEV), 'ffn_w_gate': (2 * Dm, 256), 'ffn_w_up': (2 * Dm, 256),
            'ffn_w_down': (2 * F8, F8)}
    for n, (rows, tr) in view.items():
        shp = w_loc[n].shape
        outs = _adamw(grad[n].reshape(rows, -1), w_loc[n].reshape(rows, -1), m_loc[n].reshape(rows, -1),
                      v_loc[n].reshape(rows, -1), name=f"adamw_{n}", tr=tr)
        delta[n], new_m[n], new_v[n] = (o.reshape(shp) for o in outs)
    packed = [_pack_small([d[n].reshape(-1) for n in small_names] + [jnp.zeros((1,), F32)]) for d in (w_loc, m_loc, v_loc)]
    outs = _adamw(gsum['small'], *packed, name="adamw_small", tr=gsum['small'].shape[0])
    for dst, o in zip((delta, new_m, new_v), outs):
        for n, a in zip(small_names, _unpack_small(o, small_all)):
            dst[n] = a.reshape(w_loc[n].shape)

    return (loss, dx[None], *[grad[n] for n in WEIGHT_NAMES], *[delta[n] for n in WEIGHT_NAMES],
            *[new_m[n] for n in WEIGHT_NAMES], *[new_v[n] for n in WEIGHT_NAMES])
```
